```python
import math
import jax, jax.numpy as jnp
from jax import lax
import numpy as np

D_MODEL = 1024
BATCH = 8
SEQ = 2048
DEPTH = 1

GRID_W = 64
CTX_LEN = 256

D_MIX = D_MODEL
D_RNN = D_MIX // 2
D_HYENA = D_MIX - D_RNN
RNN_HEADS = 8
RNN_HEAD_DIM = D_RNN // RNN_HEADS
RNN_CONV_W = 4
RNN_CONV_LEFT = 2
LRU_C = 8.0
HYENA_ORDER = 2
HYENA_SHORT_W = 3
FILTER_EMB = 33
FILTER_HIDDEN = 64
DECAY_FAST_PCT = 0.3
DECAY_SLOW_PCT = 1.5
DECAY_TARGET = 1e-2
N_GROUPS = 4
EXPERTS_PER_GROUP = 8
N_EXPERTS = N_GROUPS * EXPERTS_PER_GROUP
TOP_K = 2
D_EXPERT = 512
N_MOD = 6
EPS = 1e-6
D_IN = 2 * D_RNN + (HYENA_ORDER + 1) * D_HYENA

kernel_name = "hybrid_rglru_hyena_hmoe_dit_block"

F32 = jnp.float32


def _rmsnorm(u, g):
    u32 = u.astype(F32)
    y = u32 * lax.rsqrt(jnp.mean(u32 * u32, axis=-1, keepdims=True) + EPS) * g.astype(F32)
    return y.astype(u.dtype)


def _modulate(h, shift, scale):
    return h * (1 + scale) + shift


def _sincos_2d(rows, cols, dim):
    quarter = dim // 4
    omega = 1.0 / (10000.0 ** (jnp.arange(quarter, dtype=F32) / quarter))
    ang_r = jnp.arange(rows, dtype=F32)[:, None] * omega
    ang_c = jnp.arange(cols, dtype=F32)[:, None] * omega
    emb_r = jnp.concatenate([jnp.sin(ang_r), jnp.cos(ang_r)], axis=-1)
    emb_c = jnp.concatenate([jnp.sin(ang_c), jnp.cos(ang_c)], axis=-1)
    pe = jnp.concatenate([jnp.broadcast_to(emb_r[:, None, :], (rows, cols, 2 * quarter)),
                          jnp.broadcast_to(emb_c[None, :, :], (rows, cols, 2 * quarter))], axis=-1)
    return pe.reshape(rows * cols, 4 * quarter)


def _dwconv(u, w, pad_left):
    k_w = w.shape[0]
    n = u.shape[1]
    up = jnp.pad(u, ((0, 0), (pad_left, k_w - 1 - pad_left), (0, 0)))
    y = up[:, 0:n] * w[0]
    for k in range(1, k_w):
        y = y + up[:, k:k + n] * w[k]
    return y


def _block_diag(u, w, b):
    bn, n, _ = u.shape
    uh = u.reshape(bn, n, RNN_HEADS, RNN_HEAD_DIM)
    return jnp.einsum("blhi,hij->blhj", uh, w).reshape(bn, n, D_RNN) + b


def _linear_scan(a, b, h0, reverse):
    if reverse:
        b = b.at[:, -1].add(a[:, -1] * h0)
    else:
        b = b.at[:, 0].add(a[:, 0] * h0)

    def combine(left, right):
        a_l, b_l = left
        a_r, b_r = right
        return a_l * a_r, a_r * b_l + b_r

    _, h = lax.associative_scan(combine, (a, b), axis=1, reverse=reverse)
    return h


def _rglru_scan(u, h0, wa, ba, wx, bx, lam, reverse):
    r = jax.nn.sigmoid(_block_diag(u, wa, ba))
    i = jax.nn.sigmoid(_block_diag(u, wx, bx))
    log_a = -LRU_C * r * jax.nn.softplus(-lam)
    a = jnp.exp(log_a)
    b = jnp.sqrt(-jnp.expm1(2.0 * log_a)) * (i * u)
    return _linear_scan(a, b, h0, reverse)


def _rglru_bidir(u_ctx, u_lat, conv_w, conv_b, wa, ba, wx, bx, lam):
    cw = conv_w.astype(F32)
    cb = conv_b.astype(F32)
    v_ctx = _dwconv(u_ctx.astype(F32), cw, RNN_CONV_LEFT) + cb
    v_lat = _dwconv(u_lat.astype(F32), cw, RNN_CONV_LEFT) + cb
    h0 = jnp.zeros((u_ctx.shape[0], D_RNN), F32)
    fwd = (wa[0].astype(F32), ba[0].astype(F32), wx[0].astype(F32), bx[0].astype(F32), lam[0].astype(F32))
    bwd = (wa[1].astype(F32), ba[1].astype(F32), wx[1].astype(F32), bx[1].astype(F32), lam[1].astype(F32))
    hf_ctx = _rglru_scan(v_ctx, h0, *fwd, reverse=False)
    hb_ctx = _rglru_scan(v_ctx, h0, *bwd, reverse=True)
    hf_lat = _rglru_scan(v_lat, hf_ctx[:, -1], *fwd, reverse=False)
    hb_lat = _rglru_scan(v_lat, hb_ctx[:, 0], *bwd, reverse=True)
    return hf_ctx, hb_ctx, hf_lat + hb_lat


def _hyena_filters(n, w1, b1, f1, w2, b2, f2, w3, b3):
    bands = (FILTER_EMB - 1) // 2
    pos = jnp.arange(n, dtype=F32)
    t = pos / max(n - 1, 1)
    ang = (2.0 * math.pi * pos / n)[:, None] * jnp.linspace(1e-4, bands - 1, bands, dtype=F32)[None, :]
    z = jnp.concatenate([t[:, None], jnp.cos(ang), -jnp.sin(ang)], axis=-1)
    hdn = jnp.sin(f1.astype(F32) * (z @ w1.astype(F32) + b1.astype(F32)))
    hdn = jnp.sin(f2.astype(F32) * (hdn @ w2.astype(F32) + b2.astype(F32)))
    k = (hdn @ w3.astype(F32) + b3.astype(F32)).reshape(n, 2, HYENA_ORDER, D_HYENA)
    min_decay = math.log(DECAY_TARGET) / DECAY_SLOW_PCT
    max_decay = math.log(DECAY_TARGET) / DECAY_FAST_PCT
    deltas = jnp.abs(jnp.linspace(min_decay, max_decay, D_HYENA, dtype=F32))
    k = k * jnp.exp(-t[:, None] * deltas[None, :])[:, None, None, :]
    g = jnp.concatenate([k[:, 0], jnp.zeros((1, HYENA_ORDER, D_HYENA), F32), k[:0:-1, 1]], axis=0)
    g = g / jnp.sum(jnp.abs(g), axis=0, keepdims=True)
    return jnp.fft.rfft(g, axis=0)


def _long_conv(u, spec, bias):
    n = u.shape[1]
    y = jnp.fft.irfft(jnp.fft.rfft(u, n=2 * n, axis=1) * spec[None], n=2 * n, axis=1)[:, :n]
    return y + u * bias


def _hyena(u, conv_w, fbias, w1, b1, f1, w2, b2, f2, w3, b3):
    n = u.shape[1]
    z_all = _dwconv(u.astype(F32), conv_w.astype(F32), HYENA_SHORT_W // 2)
    v, *gates = jnp.split(z_all, HYENA_ORDER + 1, axis=-1)
    spec = _hyena_filters(n, w1, b1, f1, w2, b2, f2, w3, b3)
    fb = fbias.astype(F32)
    z = v
    for o in range(HYENA_ORDER):
        z = gates[o] * _long_conv(z, spec[:, o], fb[o])
    return z


def _merge(ya, yb, g_a, g_b, w_out, dt):
    y = jnp.concatenate([_rmsnorm(ya, g_a), _rmsnorm(yb, g_b)], axis=-1).astype(dt)
    return y @ w_out


def _moe(h, w_rg, b_rg, w_re, b_re, w_gate, w_up, w_down):
    shp = h.shape
    t = h.reshape(-1, shp[-1])
    t32 = t.astype(F32)
    g_prob = jax.nn.softmax(t32 @ w_rg.astype(F32) + b_rg.astype(F32), axis=-1)
    g_p, g_i = lax.top_k(g_prob, 1)
    g_onehot = jax.nn.one_hot(g_i[:, 0], N_GROUPS, dtype=F32)
    e_logits = (t32 @ w_re.astype(F32) + b_re.astype(F32)).reshape(-1, N_GROUPS, EXPERTS_PER_GROUP)
    e_prob = jax.nn.softmax(jnp.einsum("tge,tg->te", e_logits, g_onehot), axis=-1)
    e_p, e_i = lax.top_k(e_prob, TOP_K)
    wts = g_p * e_p / jnp.sum(e_p, axis=-1, keepdims=True)
    idx = g_i * EXPERTS_PER_GROUP + e_i
    combine = jnp.einsum("tk,tke->te", wts, jax.nn.one_hot(idx, N_EXPERTS, dtype=F32)).astype(t.dtype)
    out = jnp.zeros_like(t)
    for e in range(N_EXPERTS):
        ye = (jax.nn.silu(t @ w_gate[e]) * (t @ w_up[e])) @ w_down[e]
        out = out + combine[:, e:e + 1] * ye
    return out.reshape(shp)


def setup_inputs(seed: int = 0) -> dict:
    key = jax.random.key(seed)
    ks = iter(jax.random.split(key, 48))
    D = D_MODEL

    def nrm(shape, scale):
        return scale * jax.random.normal(next(ks), shape, F32)

    lam_u = jax.random.uniform(next(ks), (DEPTH, 2, D_RNN), F32, 0.9, 0.999)
    a0 = lam_u ** (1.0 / LRU_C)
    lam = jnp.log(a0) - jnp.log1p(-a0)
    return {
        "x": nrm((BATCH, SEQ, D), 1.0),
        "c": nrm((BATCH, D), 1.0),
        "ctx": nrm((BATCH, CTX_LEN, D), 1.0),
        "c_ctx": nrm((D,), 1.0),
        "w_ada": nrm((DEPTH, D, N_MOD * D), 0.5 * D ** -0.5),
        "b_ada": nrm((DEPTH, N_MOD * D), 0.01),
        "norm1_g": 1.0 + nrm((DEPTH, D), 0.02),
        "w_in": nrm((DEPTH, D, D_IN), D ** -0.5),
        "conv_a_w": nrm((DEPTH, RNN_CONV_W, D_RNN), RNN_CONV_W ** -0.5),
        "conv_a_b": nrm((DEPTH, D_RNN), 0.01),
        "lru_wa": nrm((DEPTH, 2, RNN_HEADS, RNN_HEAD_DIM, RNN_HEAD_DIM), RNN_HEAD_DIM ** -0.5),
        "lru_ba": nrm((DEPTH, 2, D_RNN), 0.01),
        "lru_wx": nrm((DEPTH, 2, RNN_HEADS, RNN_HEAD_DIM, RNN_HEAD_DIM), RNN_HEAD_DIM ** -0.5),
        "lru_bx": nrm((DEPTH, 2, D_RNN), 0.01),
        "lru_lambda": lam,
        "conv_b_w": nrm((DEPTH, HYENA_SHORT_W, (HYENA_ORDER + 1) * D_HYENA), HYENA_SHORT_W ** -0.5),
        "filt_w1": nrm((DEPTH, FILTER_EMB, FILTER_HIDDEN), FILTER_EMB ** -0.5),
        "filt_b1": nrm((DEPTH, FILTER_HIDDEN), 0.1),
        "filt_freq1": 1.0 + nrm((DEPTH, FILTER_HIDDEN), 0.02),
        "filt_w2": nrm((DEPTH, FILTER_HIDDEN, FILTER_HIDDEN), FILTER_HIDDEN ** -0.5),
        "filt_b2": nrm((DEPTH, FILTER_HIDDEN), 0.1),
        "filt_freq2": 1.0 + nrm((DEPTH, FILTER_HIDDEN), 0.02),
        "filt_w3": nrm((DEPTH, FILTER_HIDDEN, 2 * HYENA_ORDER * D_HYENA), FILTER_HIDDEN ** -0.5),
        "filt_b3": nrm((DEPTH, 2 * HYENA_ORDER * D_HYENA), 0.01),
        "filt_bias": nrm((DEPTH, HYENA_ORDER, D_HYENA), 0.1),
        "out_norm_a": 1.0 + nrm((DEPTH, D_RNN), 0.02),
        "out_norm_b": 1.0 + nrm((DEPTH, D_HYENA), 0.02),
        "w_out": nrm((DEPTH, D_MIX, D), D_MIX ** -0.5),
        "norm2_g": 1.0 + nrm((DEPTH, D), 0.02),
        "w_rg": nrm((DEPTH, D, N_GROUPS), D ** -0.5),
        "b_rg": nrm((DEPTH, N_GROUPS), 0.01),
        "w_re": nrm((DEPTH, D, N_EXPERTS), D ** -0.5),
        "b_re": nrm((DEPTH, N_EXPERTS), 0.01),
        "w_gate": nrm((DEPTH, N_EXPERTS, D, D_EXPERT), D ** -0.5),
        "w_up": nrm((DEPTH, N_EXPERTS, D, D_EXPERT), D ** -0.5),
        "w_down": nrm((DEPTH, N_EXPERTS, D_EXPERT, D), D_EXPERT ** -0.5),
        "final_g": 1.0 + nrm((D,), 0.02),
    }


def reference(x, c, ctx, c_ctx, w_ada, b_ada, norm1_g, w_in, conv_a_w, conv_a_b, lru_wa, lru_ba,
              lru_wx, lru_bx, lru_lambda, conv_b_w, filt_w1, filt_b1, filt_freq1, filt_w2, filt_b2,
              filt_freq2, filt_w3, filt_b3, filt_bias, out_norm_a, out_norm_b, w_out, norm2_g,
              w_rg, b_rg, w_re, b_re, w_gate, w_up, w_down, final_g):
    dt = x.dtype
    n_lat = x.shape[1]
    rows = n_lat // GRID_W
    x = x + _sincos_2d(rows, GRID_W, x.shape[-1]).astype(dt)[None]
    xc = ctx
    for l in range(DEPTH):
        last = l == DEPTH - 1
        mod = jax.nn.silu(c) @ w_ada[l] + b_ada[l]
        mod_c = jax.nn.silu(c_ctx) @ w_ada[l] + b_ada[l]
        sh1, sc1, g1, sh2, sc2, g2 = jnp.split(mod[:, None, :], N_MOD, axis=-1)
        csh1, csc1, cg1, csh2, csc2, cg2 = jnp.split(mod_c, N_MOD, axis=-1)

        p = _modulate(_rmsnorm(x, norm1_g[l]), sh1, sc1) @ w_in[l]
        pc = _modulate(_rmsnorm(xc, norm1_g[l]), csh1, csc1) @ w_in[l]
        filt = (filt_w1[l], filt_b1[l], filt_freq1[l], filt_w2[l], filt_b2[l], filt_freq2[l],
                filt_w3[l], filt_b3[l])

        hf_ctx, hb_ctx, rnn_lat = _rglru_bidir(pc[..., D_RNN:2 * D_RNN], p[..., D_RNN:2 * D_RNN],
                                               conv_a_w[l], conv_a_b[l], lru_wa[l], lru_ba[l],
                                               lru_wx[l], lru_bx[l], lru_lambda[l])
        ya = jax.nn.gelu(p[..., :D_RNN].astype(F32), approximate=True) * rnn_lat
        yb = _hyena(p[..., 2 * D_RNN:], conv_b_w[l], filt_bias[l], *filt)
        y = _merge(ya, yb, out_norm_a[l], out_norm_b[l], w_out[l], dt)

        if not last:
            yac = jax.nn.gelu(pc[..., :D_RNN].astype(F32), approximate=True) * (hf_ctx + hb_ctx)
            ybc = _hyena(pc[..., 2 * D_RNN:], conv_b_w[l], filt_bias[l], *filt)
            xc = xc + cg1 * _merge(yac, ybc, out_norm_a[l], out_norm_b[l], w_out[l], dt)
            hc2 = _modulate(_rmsnorm(xc, norm2_g[l]), csh2, csc2)
            xc = xc + cg2 * _moe(hc2, w_rg[l], b_rg[l], w_re[l], b_re[l], w_gate[l], w_up[l], w_down[l])

        x = x + g1 * y
        h2 = _modulate(_rmsnorm(x, norm2_g[l]), sh2, sc2)
        x = x + g2 * _moe(h2, w_rg[l], b_rg[l], w_re[l], b_re[l], w_gate[l], w_up[l], w_down[l])
    return _rmsnorm(x, final_g)
```

```python
import functools
import math

import numpy as np
import jax
import jax.numpy as jnp
from jax import lax
from jax.experimental import pallas as pl
from jax.experimental.pallas import tpu as pltpu

F32 = jnp.float32
BF16 = jnp.bfloat16
EPS = 1e-6
LRU_C = 8.0
N_MOD = 6
GRID_W = 64
RNN_HEADS = 8
N_GROUPS = 4
EXPERTS_PER_GROUP = 8
N_EXPERTS = N_GROUPS * EXPERTS_PER_GROUP
FILTER_BANDS = 16
DECAY_FAST_PCT = 0.3
DECAY_SLOW_PCT = 1.5
DECAY_TARGET = 1e-2
SUBLANES = 8
LANES = 128
VMEM_LIMIT = 60 * 1024 * 1024


def _cparams(*sem):
    return pltpu.CompilerParams(dimension_semantics=sem, vmem_limit_bytes=VMEM_LIMIT)


def _dot(a, b):
    return jnp.dot(a, b, preferred_element_type=F32)


def _split(a):
    hi = a.astype(BF16)
    return hi, (a - hi.astype(F32)).astype(BF16)


def _dot3(a, b):
    a_hi, a_lo = _split(a)
    b_hi, b_lo = _split(b)
    return _dot(a_hi, b_hi) + _dot(a_lo, b_hi) + _dot(a_hi, b_lo)


def _rms(v, g):
    return v * lax.rsqrt(jnp.mean(v * v, axis=-1, keepdims=True) + EPS) * g


def _sigmoid(z):
    return 1.0 / (1.0 + jnp.exp(-z))


def _shift_rows(v, d):
    n = v.shape[0]
    rolled = pltpu.roll(v, (-d) % n, 0)
    row = lax.broadcasted_iota(jnp.int32, v.shape, 0)
    ok = (row + d >= 0) & (row + d < n)
    return jnp.where(ok, rolled, 0.0)


def _ada_body(c_ref, w_ref, b_ref, o_ref):
    c = c_ref[...]
    o_ref[...] = _dot3(c * _sigmoid(c), w_ref[...]) + b_ref[...]


def _ada(cc, w_ada, b_ada):
    rows, d = cc.shape
    n = w_ada.shape[1]
    tn = 1024
    return pl.pallas_call(
        _ada_body,
        grid=(n // tn,),
        in_specs=[pl.BlockSpec((rows, d), lambda j: (0, 0)),
                  pl.BlockSpec((d, tn), lambda j: (0, j)),
                  pl.BlockSpec((1, tn), lambda j: (0, j))],
        out_specs=pl.BlockSpec((rows, tn), lambda j: (0, j)),
        out_shape=jax.ShapeDtypeStruct((rows, n), F32),
        compiler_params=_cparams("arbitrary"),
        name="ada",
    )(cc, w_ada, b_ada.reshape(1, n))


def _inproj_body(*refs, with_pe):
    if with_pe:
        x_ref, pe_ref, mod_ref, g_ref, w_ref, o_ref, wb_ref = refs
    else:
        x_ref, mod_ref, g_ref, w_ref, o_ref, wb_ref = refs

    @pl.when((pl.program_id(0) == 0) & (pl.program_id(1) == 0))
    def _():
        wb_ref[...] = w_ref[...].astype(BF16)

    x = x_ref[...]
    if with_pe:
        x = x + pe_ref[...]
    h = _rms(x, g_ref[...]) * (1.0 + mod_ref[1:2, :]) + mod_ref[0:1, :]
    o_ref[...] = _dot(h.astype(BF16), wb_ref[...])


def _inproj(x3, pe, mod3, mod_row, g, w, col_block, n_out, tm):
    b, l, d = x3.shape
    with_pe = pe is not None
    in_specs = [pl.BlockSpec((None, tm, d), lambda bi, i: (bi, i, 0))]
    args = [x3]
    if with_pe:
        in_specs.append(pl.BlockSpec((tm, d), lambda bi, i: (i, 0)))
        args.append(pe)
    in_specs += [pl.BlockSpec((None, N_MOD, d), lambda bi, i: (mod_row(bi), 0, 0)),
                 pl.BlockSpec((1, d), lambda bi, i: (0, 0)),
                 pl.BlockSpec((d, n_out), lambda bi, i: (0, col_block), pipeline_mode=pl.Buffered(1))]
    args += [mod3, g.reshape(1, d), w]
    return pl.pallas_call(
        functools.partial(_inproj_body, with_pe=with_pe),
        grid=(b, l // tm),
        in_specs=in_specs,
        out_specs=pl.BlockSpec((None, tm, n_out), lambda bi, i: (bi, i, 0)),
        out_shape=jax.ShapeDtypeStruct((b, l, n_out), F32),
        scratch_shapes=[pltpu.VMEM((d, n_out), BF16)],
        compiler_params=_cparams("arbitrary", "arbitrary"),
        name="inproj_pe" if with_pe else "inproj_ctx",
    )(*args)


def _rglru_body(*refs, with_gate):
    if with_gate:
        (u_ref, gate_ref, cw_ref, cb_ref, w_ref, bias_ref, sp_ref, h0f_ref, h0b_ref,
         y_ref, hf_end_ref, hb_end_ref, af_ref, bf_ref, ab_ref, bb_ref) = refs
    else:
        (u_ref, cw_ref, cb_ref, w_ref, bias_ref, sp_ref, h0f_ref, h0b_ref,
         hf_end_ref, hb_end_ref, af_ref, bf_ref, ab_ref, bb_ref) = refs
    n, c = u_ref.shape
    u = u_ref[...]
    v = cb_ref[...] + cw_ref[1:2, :] * _shift_rows(u, -1) + cw_ref[0:1, :] * _shift_rows(u, -2)
    v = v + cw_ref[2:3, :] * u + cw_ref[3:4, :] * _shift_rows(u, 1)
    z = _dot(v.astype(BF16), w_ref[...]) + bias_ref[...]
    sub = lax.broadcasted_iota(jnp.int32, (n, c), 0) % SUBLANES

    def local_scan(k, reverse, a_ref, b_ref):
        r = _sigmoid(z[:, (2 * k) * c:(2 * k + 1) * c])
        i = _sigmoid(z[:, (2 * k + 1) * c:(2 * k + 2) * c])
        log_a = (-LRU_C) * r * sp_ref[k:k + 1, :]
        a = jnp.exp(log_a)
        b = jnp.sqrt(1.0 - a * a) * (i * v)
        for s in (1, 2, 4):
            if reverse:
                ok = sub < SUBLANES - s
                a_sh = jnp.where(ok, pltpu.roll(a, n - s, 0), 1.0)
                b_sh = jnp.where(ok, pltpu.roll(b, n - s, 0), 0.0)
            else:
                ok = sub >= s
                a_sh = jnp.where(ok, pltpu.roll(a, s, 0), 1.0)
                b_sh = jnp.where(ok, pltpu.roll(b, s, 0), 0.0)
            b = a * b_sh + b
            a = a * a_sh
        a_ref[...] = a
        b_ref[...] = b

    local_scan(0, False, af_ref, bf_ref)
    local_scan(1, True, ab_ref, bb_ref)

    n_tiles = n // SUBLANES

    def carry_step(q, carry):
        cf, cb = carry
        rf = pl.ds(pl.multiple_of(q * SUBLANES, SUBLANES), SUBLANES)
        rb = pl.ds(pl.multiple_of((n_tiles - 1 - q) * SUBLANES, SUBLANES), SUBLANES)
        hf = af_ref[rf, :] * cf + bf_ref[rf, :]
        hb = ab_ref[rb, :] * cb + bb_ref[rb, :]
        bf_ref[rf, :] = hf
        bb_ref[rb, :] = hb
        return (jnp.broadcast_to(hf[SUBLANES - 1:SUBLANES, :], (SUBLANES, c)),
                jnp.broadcast_to(hb[0:1, :], (SUBLANES, c)))

    cf0 = jnp.broadcast_to(h0f_ref[...], (SUBLANES, c))
    cb0 = jnp.broadcast_to(h0b_ref[...], (SUBLANES, c))
    cf, cb = lax.fori_loop(0, n_tiles, carry_step, (cf0, cb0), unroll=4)
    hf_end_ref[...] = cf[0:1, :]
    hb_end_ref[...] = cb[0:1, :]
    if with_gate:
        y_ref[...] = jax.nn.gelu(gate_ref[...], approximate=True) * (bf_ref[...] + bb_ref[...])


def _rglru(p3, u_blk0, gate_blk0, cw, cb, w_gates, bias, sp, h0f, h0b, with_gate, ch):
    b, l, _ = p3.shape
    d_rnn = cw.shape[1]
    nh = d_rnn // ch
    in_specs = [pl.BlockSpec((None, l, ch), lambda bi, h: (bi, 0, u_blk0 + h))]
    args = [p3]
    if with_gate:
        in_specs.append(pl.BlockSpec((None, l, ch), lambda bi, h: (bi, 0, gate_blk0 + h)))
        args.append(p3)
    in_specs += [pl.BlockSpec((4, ch), lambda bi, h: (0, h)),
                 pl.BlockSpec((1, ch), lambda bi, h: (0, h)),
                 pl.BlockSpec((None, ch, 4 * ch), lambda bi, h: (h, 0, 0)),
                 pl.BlockSpec((None, 1, 4 * ch), lambda bi, h: (h, 0, 0)),
                 pl.BlockSpec((2, ch), lambda bi, h: (0, h)),
                 pl.BlockSpec((None, 1, ch), lambda bi, h: (bi, 0, h)),
                 pl.BlockSpec((None, 1, ch), lambda bi, h: (bi, 0, h))]
    args += [cw, cb, w_gates, bias, sp, h0f, h0b]
    end_spec = pl.BlockSpec((None, 1, ch), lambda bi, h: (bi, 0, h))
    end_shape = jax.ShapeDtypeStruct((b, 1, d_rnn), F32)
    out_specs = [end_spec, end_spec]
    out_shape = [end_shape, end_shape]
    if with_gate:
        out_specs = [pl.BlockSpec((None, l, ch), lambda bi, h: (bi, 0, h))] + out_specs
        out_shape = [jax.ShapeDtypeStruct((b, l, d_rnn), F32)] + out_shape
    return pl.pallas_call(
        functools.partial(_rglru_body, with_gate=with_gate),
        grid=(b, nh),
        in_specs=in_specs,
        out_specs=out_specs,
        out_shape=out_shape,
        scratch_shapes=[pltpu.VMEM((l, ch), F32)] * 4,
        compiler_params=_cparams("arbitrary", "arbitrary"),
        name="rglru_lat" if with_gate else "rglru_ctx",
    )(*args)


def _filt_body(z_ref, w1_ref, b1_ref, f1_ref, w2_ref, b2_ref, f2_ref, w3f_ref, w3b_ref, b3f_ref, b3b_ref,
               dec_ref, c3_ref, s3_ref, cphi_ref, sphi_ref, gr_ref, gi_ref):
    n = z_ref.shape[0]
    ch = gr_ref.shape[1]
    hdn = jnp.sin(f1_ref[...] * (_dot3(z_ref[...], w1_ref[...]) + b1_ref[...]))
    hdn = jnp.sin(f2_ref[...] * (_dot3(hdn, w2_ref[...]) + b2_ref[...]))
    decay = jnp.exp(-z_ref[:, 0:1] * dec_ref[...])
    kf = (_dot3(hdn, w3f_ref[...]) + b3f_ref[...]) * decay
    kb = (_dot3(hdn, w3b_ref[...]) + b3b_ref[...]) * decay
    row = lax.broadcasted_iota(jnp.int32, (n, ch), 0)
    kb = jnp.where(row == 0, 0.0, kb)
    norm = jnp.sum(jnp.abs(kf) + jnp.abs(kb), axis=0, keepdims=True)
    taps = jnp.concatenate([kf, kb], axis=1).astype(BF16)
    p = _dot(c3_ref[...], taps)
    q = _dot(s3_ref[...], taps)
    pf, pb = p[:, :ch], p[:, ch:]
    qf, qb = q[:, :ch], q[:, ch:]
    scale = (2.0 / (2 * n)) / norm
    cphi = cphi_ref[...]
    sphi = sphi_ref[...]
    gr_ref[...] = (cphi * (pf + pb) + sphi * (qf + qb)) * scale
    gi_ref[...] = (sphi * (pf - pb) - cphi * (qf - qb)) * scale


def _filters(z, w1, b1, f1, w2, b2, f2, w3, b3, deltas, c3, s3, cphi, sphi, d_h, order, ch):
    n, fe = z.shape
    hid = w2.shape[0]
    per_order = d_h // ch
    nblk = order * per_order
    const = lambda shape: pl.BlockSpec(shape, lambda g: (0, 0))
    once = lambda shape: pl.BlockSpec(shape, lambda g: (0, 0), pipeline_mode=pl.Buffered(1))
    return pl.pallas_call(
        _filt_body,
        grid=(nblk,),
        in_specs=[const((n, fe)), const((fe, hid)), const((1, hid)), const((1, hid)),
                  const((hid, hid)), const((1, hid)), const((1, hid)),
                  pl.BlockSpec((hid, ch), lambda g: (0, g)),
                  pl.BlockSpec((hid, ch), lambda g: (0, nblk + g)),
                  pl.BlockSpec((1, ch), lambda g: (0, g)),
                  pl.BlockSpec((1, ch), lambda g: (0, nblk + g)),
                  pl.BlockSpec((1, ch), lambda g: (0, g % per_order)),
                  once((n, n)), once((n, n)), const((n, 1)), const((n, 1))],
        out_specs=[pl.BlockSpec((n, ch), lambda g: (0, g)), pl.BlockSpec((n, ch), lambda g: (0, g))],
        out_shape=[jax.ShapeDtypeStruct((n, order * d_h), F32)] * 2,
        compiler_params=_cparams("arbitrary"),
        name="hyena_filters",
    )(z, w1, b1, f1, w2, b2, f2, w3, w3, b3, b3, deltas, c3, s3, cphi, sphi)


def _hyena_body(v_ref, x1_ref, x2_ref, wv_ref, w1_ref, w2_ref, fb_ref, c3_ref, s3_ref,
                g0r_ref, g0i_ref, g1r_ref, g1i_ref, o_ref, z_ref, zb_ref, yr_ref, yi_ref):
    n = o_ref.shape[0]
    rows_per = 512
    halo = SUBLANES

    def conv3(src_ref, w_ref, r0):
        lo, hi = max(r0 - halo, 0), min(r0 + rows_per + halo, n)
        win = src_ref[lo:hi, :]
        size = hi - lo
        row = lax.broadcasted_iota(jnp.int32, win.shape, 0)
        prev = pltpu.roll(win, 1, 0)
        nxt = pltpu.roll(win, size - 1, 0)
        if lo == 0:
            prev = jnp.where(row == 0, 0.0, prev)
        if hi == n:
            nxt = jnp.where(row == size - 1, 0.0, nxt)
        out = w_ref[0:1, :] * prev + w_ref[1:2, :] * win + w_ref[2:3, :] * nxt
        return out[r0 - lo:r0 - lo + rows_per, :]

    def long_conv(gr_ref, gi_ref, fb, gate_ref, gate_w_ref, dst_ref):
        for r0 in range(0, n, rows_per):
            rows = slice(r0, r0 + rows_per)
            p = _dot(c3_ref[rows, :], zb_ref[...])
            q = _dot(s3_ref[rows, :], zb_ref[...])
            gr = gr_ref[rows, :]
            gi = gi_ref[rows, :]
            yr_ref[rows, :] = (p * gr + q * gi).astype(BF16)
            yi_ref[rows, :] = (q * gr - p * gi).astype(BF16)
        for r0 in range(0, n, rows_per):
            rows = slice(r0, r0 + rows_per)
            y = _dot(c3_ref[rows, :], yr_ref[...]) + _dot(s3_ref[rows, :], yi_ref[...])
            dst_ref[rows, :] = conv3(gate_ref, gate_w_ref, r0) * (y + z_ref[rows, :] * fb)

    for r0 in range(0, n, rows_per):
        z_ref[r0:r0 + rows_per, :] = conv3(v_ref, wv_ref, r0)
    zb_ref[...] = z_ref[...].astype(BF16)
    long_conv(g0r_ref, g0i_ref, fb_ref[0:1, :], x1_ref, w1_ref, z_ref)
    zb_ref[...] = z_ref[...].astype(BF16)
    long_conv(g1r_ref, g1i_ref, fb_ref[1:2, :], x2_ref, w2_ref, o_ref)


def _hyena(p3, col_blk0, conv_w, fbias, c3, s3, gr, gi, d_h, ch):
    b, n, _ = p3.shape
    per = d_h // ch
    zspec = lambda k: pl.BlockSpec((None, n, ch), lambda h, bi: (bi, 0, col_blk0 + k * per + h))
    wspec = lambda k: pl.BlockSpec((3, ch), lambda h, bi: (0, k * per + h))
    gspec = lambda o: pl.BlockSpec((n, ch), lambda h, bi: (0, o * per + h), pipeline_mode=pl.Buffered(1))
    const = pl.BlockSpec((n, n), lambda h, bi: (0, 0), pipeline_mode=pl.Buffered(1))
    return pl.pallas_call(
        _hyena_body,
        grid=(per, b),
        in_specs=[zspec(0), zspec(1), zspec(2), wspec(0), wspec(1), wspec(2),
                  pl.BlockSpec((2, ch), lambda h, bi: (0, h)), const, const,
                  gspec(0), gspec(0), gspec(1), gspec(1)],
        out_specs=pl.BlockSpec((None, n, ch), lambda h, bi: (bi, 0, h)),
        out_shape=jax.ShapeDtypeStruct((b, n, d_h), F32),
        scratch_shapes=[pltpu.VMEM((n, ch), F32), pltpu.VMEM((n, ch), BF16),
                        pltpu.VMEM((n, ch), BF16), pltpu.VMEM((n, ch), BF16)],
        compiler_params=_cparams("arbitrary", "arbitrary"),
        name="hyena_mix",
    )(p3, p3, p3, conv_w, conv_w, conv_w, fbias, c3, s3, gr, gi, gr, gi)


def _merge_body(ya_ref, yb_ref, x_ref, pe_ref, mod_ref, ga_ref, gb_ref, wo_ref, g2_ref, wr_ref, br_ref,
                x1_ref, h2_ref, comb_ref, wob_ref):
    @pl.when((pl.program_id(0) == 0) & (pl.program_id(1) == 0))
    def _():
        wob_ref[...] = wo_ref[...].astype(BF16)

    da = ya_ref.shape[1]
    na = _rms(ya_ref[...], ga_ref[...]).astype(BF16)
    nb = _rms(yb_ref[...], gb_ref[...]).astype(BF16)
    y = _dot(na, wob_ref[0:da, :]) + _dot(nb, wob_ref[da:, :])
    x1 = x_ref[...] + pe_ref[...] + mod_ref[2:3, :] * y
    x1_ref[...] = x1
    h2 = _rms(x1, g2_ref[...]) * (1.0 + mod_ref[4:5, :]) + mod_ref[3:4, :]
    h2_ref[...] = h2.astype(BF16)

    logits = _dot3(h2, wr_ref[...]) + br_ref[...]
    lane = lax.broadcasted_iota(jnp.int32, logits.shape, 1).astype(F32)
    neg = -jnp.inf
    big = jnp.float32(1 << 20)
    gl = jnp.where((lane >= N_EXPERTS) & (lane < N_EXPERTS + N_GROUPS), logits, neg)
    gmax = jnp.max(gl, axis=1, keepdims=True)
    g_p = 1.0 / jnp.sum(jnp.exp(gl - gmax), axis=1, keepdims=True)
    g_i = jnp.min(jnp.where(gl == gmax, lane, big), axis=1, keepdims=True) - N_EXPERTS
    lo = g_i * EXPERTS_PER_GROUP
    el = jnp.where((lane >= lo) & (lane < lo + EXPERTS_PER_GROUP), logits, neg)
    m1 = jnp.max(el, axis=1, keepdims=True)
    i1 = jnp.min(jnp.where(el == m1, lane, big), axis=1, keepdims=True)
    el2 = jnp.where(lane == i1, neg, el)
    m2 = jnp.max(el2, axis=1, keepdims=True)
    i2 = jnp.min(jnp.where(el2 == m2, lane, big), axis=1, keepdims=True)
    e2 = jnp.exp(m2 - m1)
    w1 = g_p / (1.0 + e2)
    w2 = g_p * e2 / (1.0 + e2)
    comb_ref[...] = jnp.where(lane == i1, w1, 0.0) + jnp.where(lane == i2, w2, 0.0)


def _merge(ya, yb, x3, pe, mod3, ga, gb, w_out, g2n, w_r, b_r, tm):
    b, n, d = x3.shape
    da, db = ya.shape[2], yb.shape[2]
    tok = lambda w: pl.BlockSpec((None, tm, w), lambda bi, i: (bi, i, 0))
    const = lambda shape: pl.BlockSpec(shape, lambda bi, i: (0, 0))
    return pl.pallas_call(
        _merge_body,
        grid=(b, n // tm),
        in_specs=[tok(da), tok(db), tok(d), pl.BlockSpec((tm, d), lambda bi, i: (i, 0)),
                  pl.BlockSpec((None, N_MOD, d), lambda bi, i: (bi, 0, 0)),
                  const((1, da)), const((1, db)), const((da + db, d)), const((1, d)),
                  const((d, LANES)), const((1, LANES))],
        out_specs=[tok(d), tok(d), tok(LANES)],
        out_shape=[jax.ShapeDtypeStruct((b, n, d), F32), jax.ShapeDtypeStruct((b, n, d), BF16),
                   jax.ShapeDtypeStruct((b, n, LANES), F32)],
        scratch_shapes=[pltpu.VMEM((da + db, d), BF16)],
        compiler_params=_cparams("arbitrary", "arbitrary"),
        name="merge_route",
    )(ya, yb, x3, pe, mod3, ga.reshape(1, da), gb.reshape(1, db), w_out, g2n.reshape(1, d), w_r, b_r)


def _moe_body(h2_ref, comb_ref, x1_ref, mod_ref, fg_ref, wg_ref, wu_ref, wd_ref, o_ref, acc_ref):
    e = pl.program_id(2)

    @pl.when(e == 0)
    def _():
        acc_ref[...] = jnp.zeros_like(acc_ref)

    h = h2_ref[...]
    act = _dot(h, wg_ref[...].astype(BF16))
    act = act * _sigmoid(act) * _dot(h, wu_ref[...].astype(BF16))
    ye = _dot(act.astype(BF16), wd_ref[...].astype(BF16))
    lane = lax.broadcasted_iota(jnp.int32, comb_ref.shape, 1)
    cw = jnp.sum(jnp.where(lane == e, comb_ref[...], 0.0), axis=1, keepdims=True)
    acc_ref[...] += cw * ye

    @pl.when(e == pl.num_programs(2) - 1)
    def _():
        o_ref[...] = _rms(x1_ref[...] + mod_ref[5:6, :] * acc_ref[...], fg_ref[...])


def _moe(h2, comb, x1, mod3, final_g, w_gate, w_up, w_down, tm):
    b, n, d = x1.shape
    ne, _, de = w_gate.shape
    tok = lambda w: pl.BlockSpec((None, tm, w), lambda bi, i, e: (bi, i, 0))
    return pl.pallas_call(
        _moe_body,
        grid=(b, n // tm, ne),
        in_specs=[tok(d), tok(LANES), tok(d),
                  pl.BlockSpec((None, N_MOD, d), lambda bi, i, e: (bi, 0, 0)),
                  pl.BlockSpec((1, d), lambda bi, i, e: (0, 0)),
                  pl.BlockSpec((None, d, de), lambda bi, i, e: (e, 0, 0)),
                  pl.BlockSpec((None, d, de), lambda bi, i, e: (e, 0, 0)),
                  pl.BlockSpec((None, de, d), lambda bi, i, e: (e, 0, 0))],
        out_specs=tok(d),
        out_shape=jax.ShapeDtypeStruct((b, n, d), F32),
        scratch_shapes=[pltpu.VMEM((tm, d), F32)],
        compiler_params=_cparams("arbitrary", "arbitrary", "arbitrary"),
        name="moe_dense",
    )(h2, comb, x1, mod3, final_g.reshape(1, d), w_gate, w_up, w_down)


def _sincos_table(rows, cols, dim):
    quarter = dim // 4
    omega = 1.0 / (10000.0 ** (np.arange(quarter, dtype=np.float64) / quarter))
    ang_r = np.arange(rows, dtype=np.float64)[:, None] * omega
    ang_c = np.arange(cols, dtype=np.float64)[:, None] * omega
    emb_r = np.concatenate([np.sin(ang_r), np.cos(ang_r)], axis=-1)
    emb_c = np.concatenate([np.sin(ang_c), np.cos(ang_c)], axis=-1)
    pe = np.concatenate([np.broadcast_to(emb_r[:, None, :], (rows, cols, 2 * quarter)),
                         np.broadcast_to(emb_c[None, :, :], (rows, cols, 2 * quarter))], axis=-1)
    return pe.reshape(rows * cols, 4 * quarter).astype(np.float32)


def _filter_features(n, width):
    pos = np.arange(n, dtype=np.float64)
    t = pos / max(n - 1, 1)
    ang = (2.0 * math.pi * pos / n)[:, None] * np.linspace(1e-4, FILTER_BANDS - 1, FILTER_BANDS)[None, :]
    z = np.concatenate([t[:, None], np.cos(ang), -np.sin(ang)], axis=-1)
    return np.pad(z, ((0, 0), (0, width - z.shape[1]))).astype(np.float32)


def _decay_rates(d_h):
    min_decay = math.log(DECAY_TARGET) / DECAY_SLOW_PCT
    max_decay = math.log(DECAY_TARGET) / DECAY_FAST_PCT
    return np.abs(np.linspace(min_decay, max_decay, d_h)).astype(np.float32).reshape(1, d_h)


def _dft_tables(n):
    idx = 2 * np.arange(n, dtype=np.int64) + 1
    j = (idx[:, None] * idx[None, :]) % (8 * n)
    ang = j.astype(np.float64) * (2.0 * math.pi / (8 * n))
    half = (np.arange(n, dtype=np.float64) + 0.5) * (math.pi / (2 * n))
    return (np.cos(ang).astype(np.float32), np.sin(ang).astype(np.float32),
            np.cos(half).astype(np.float32).reshape(n, 1), np.sin(half).astype(np.float32).reshape(n, 1))


def kernel(x, c, ctx, c_ctx, w_ada, b_ada, norm1_g, w_in, conv_a_w, conv_a_b, lru_wa, lru_ba, lru_wx, lru_bx, lru_lambda, conv_b_w, filt_w1, filt_b1, filt_freq1, filt_w2, filt_b2, filt_freq2, filt_w3, filt_b3, filt_bias, out_norm_a, out_norm_b, w_out, norm2_g, w_rg, b_rg, w_re, b_re, w_gate, w_up, w_down, final_g):
    bsz, n, d = x.shape
    n_ctx = ctx.shape[1]
    d_rnn = conv_a_w.shape[2]
    d_h = filt_bias.shape[2]
    order = filt_bias.shape[1]
    assert w_ada.shape[0] == 1, "single-layer block"
    l = 0
    ch = 256

    mod_rows = 16
    cc = jnp.concatenate([c, c_ctx[None, :], jnp.zeros((mod_rows - bsz - 1, d), F32)], axis=0)
    mod3 = _ada(cc, w_ada[l], b_ada[l]).reshape(mod_rows, N_MOD, d)

    pe = jnp.asarray(_sincos_table(n // GRID_W, GRID_W, d))
    d_in = w_in.shape[2]
    p3 = _inproj(x, pe, mod3, lambda bi: bi, norm1_g[l], w_in[l], 0, d_in, 512)
    pc3 = _inproj(ctx, None, mod3, lambda bi: bsz, norm1_g[l], w_in[l], 1, d_rnn, n_ctx)

    heads_per_blk = ch // (d_rnn // RNN_HEADS)
    nblk = d_rnn // ch
    eye = jnp.eye(heads_per_blk, dtype=F32)

    def blockdiag(w):
        w5 = w.reshape(2, nblk, heads_per_blk, w.shape[2], w.shape[3])
        return jnp.einsum("dnkij,kl->dnkilj", w5, eye).reshape(2, nblk, ch, ch)

    wa_bd, wx_bd = blockdiag(lru_wa[l]), blockdiag(lru_wx[l])
    w_gates = jnp.concatenate([wa_bd[0], wx_bd[0], wa_bd[1], wx_bd[1]], axis=-1).astype(BF16)
    ba, bx = lru_ba[l].reshape(2, nblk, ch), lru_bx[l].reshape(2, nblk, ch)
    bias = jnp.concatenate([ba[0], bx[0], ba[1], bx[1]], axis=-1).reshape(nblk, 1, 4 * ch)
    sp = jax.nn.softplus(-lru_lambda[l])
    cb = conv_a_b[l].reshape(1, d_rnn)
    zeros_state = jnp.zeros((bsz, 1, d_rnn), F32)
    hf_ctx, hb_ctx = _rglru(pc3, 0, 0, conv_a_w[l], cb, w_gates, bias, sp, zeros_state, zeros_state, False, ch)
    ya, _, _ = _rglru(p3, d_rnn // ch, 0, conv_a_w[l], cb, w_gates, bias, sp, hf_ctx, hb_ctx, True, ch)

    c3_np, s3_np, cphi_np, sphi_np = _dft_tables(n)
    c3 = jnp.asarray(c3_np).astype(BF16)
    s3 = jnp.asarray(s3_np).astype(BF16)
    fe = 64
    z = jnp.asarray(_filter_features(n, fe))
    w1 = jnp.pad(filt_w1[l], ((0, fe - filt_w1.shape[1]), (0, 0)))
    hid = filt_w2.shape[1]
    gr, gi = _filters(z, w1, filt_b1[l].reshape(1, hid), filt_freq1[l].reshape(1, hid), filt_w2[l],
                      filt_b2[l].reshape(1, hid), filt_freq2[l].reshape(1, hid), filt_w3[l],
                      filt_b3[l].reshape(1, -1), jnp.asarray(_decay_rates(d_h)), c3, s3,
                      jnp.asarray(cphi_np), jnp.asarray(sphi_np), d_h, order, ch)
    yb = _hyena(p3, 2 * d_rnn // ch, conv_b_w[l], filt_bias[l], c3, s3, gr, gi, d_h, ch)

    w_r = jnp.concatenate([w_re[l], w_rg[l], jnp.zeros((d, LANES - N_EXPERTS - N_GROUPS), F32)], axis=1)
    b_r = jnp.concatenate([b_re[l], b_rg[l], jnp.zeros((LANES - N_EXPERTS - N_GROUPS,), F32)]).reshape(1, LANES)
    x1, h2, comb = _merge(ya, yb, x, pe, mod3, out_norm_a[l], out_norm_b[l], w_out[l], norm2_g[l], w_r, b_r, 512)

    return _moe(h2, comb, x1, mod3, final_g, w_gate[l], w_up[l], w_down[l], 1024)
```

```python
import functools
import math

import numpy as np
import jax
import jax.numpy as jnp
from jax import lax
from jax.experimental import pallas as pl
from jax.experimental.pallas import tpu as pltpu

F32 = jnp.float32
BF16 = jnp.bfloat16
EPS = 1e-6
LRU_C = 8.0
N_MOD = 6
GRID_W = 64
RNN_HEADS = 8
N_GROUPS = 4
EXPERTS_PER_GROUP = 8
N_EXPERTS = N_GROUPS * EXPERTS_PER_GROUP
FILTER_BANDS = 16
DECAY_FAST_PCT = 0.3
DECAY_SLOW_PCT = 1.5
DECAY_TARGET = 1e-2
SUBLANES = 8
LANES = 128
VMEM_LIMIT = 60 * 1024 * 1024


def _cparams(*sem):
    return pltpu.CompilerParams(dimension_semantics=sem, vmem_limit_bytes=VMEM_LIMIT)


def _dot(a, b):
    return jnp.dot(a, b, preferred_element_type=F32)


def _split(a):
    hi = a.astype(BF16)
    return hi, (a - hi.astype(F32)).astype(BF16)


def _dot3(a, b):
    a_hi, a_lo = _split(a)
    b_hi, b_lo = _split(b)
    return _dot(a_hi, b_hi) + _dot(a_lo, b_hi) + _dot(a_hi, b_lo)


def _rms(v, g):
    return v * lax.rsqrt(jnp.mean(v * v, axis=-1, keepdims=True) + EPS) * g


def _sigmoid(z):
    return 1.0 / (1.0 + jnp.exp(-z))


def _shift_rows(v, d):
    n = v.shape[0]
    rolled = pltpu.roll(v, (-d) % n, 0)
    row = lax.broadcasted_iota(jnp.int32, v.shape, 0)
    ok = (row + d >= 0) & (row + d < n)
    return jnp.where(ok, rolled, 0.0)


def _to_token_tiles(src_ref, dst_ref):
    rows, width = src_ref.shape
    groups = width // LANES

    def body(g, carry):
        r0 = pl.multiple_of(g * SUBLANES, SUBLANES)
        for j in range(groups):
            dst_ref[pl.ds(g * SUBLANES * groups + j, SUBLANES, stride=groups), :] = (
                src_ref[pl.ds(r0, SUBLANES), j * LANES:(j + 1) * LANES])
        return carry

    lax.fori_loop(0, rows // SUBLANES, body, 0, unroll=2)


def _from_token_tiles(src_ref, dst_ref):
    rows, width = dst_ref.shape
    groups = width // LANES

    def body(g, carry):
        r0 = pl.multiple_of(g * SUBLANES, SUBLANES)
        for j in range(groups):
            dst_ref[pl.ds(r0, SUBLANES), j * LANES:(j + 1) * LANES] = (
                src_ref[pl.ds(g * SUBLANES * groups + j, SUBLANES, stride=groups), :])
        return carry

    lax.fori_loop(0, rows // SUBLANES, body, 0, unroll=2)


def _token_rows(ref, token):
    return ref.at[pl.ds(pl.multiple_of(token * SUBLANES, SUBLANES), SUBLANES), :]


def _ada_body(c_ref, w_ref, b_ref, o_ref):
    c = c_ref[...]
    o_ref[...] = _dot3(c * _sigmoid(c), w_ref[...]) + b_ref[...]


def _ada(cc, w_ada, b_ada):
    rows, d = cc.shape
    n = w_ada.shape[1]
    tn = 1024
    return pl.pallas_call(
        _ada_body,
        grid=(n // tn,),
        in_specs=[pl.BlockSpec((rows, d), lambda j: (0, 0)),
                  pl.BlockSpec((d, tn), lambda j: (0, j)),
                  pl.BlockSpec((1, tn), lambda j: (0, j))],
        out_specs=pl.BlockSpec((rows, tn), lambda j: (0, j)),
        out_shape=jax.ShapeDtypeStruct((rows, n), F32),
        compiler_params=_cparams("arbitrary"),
        name="ada",
    )(cc, w_ada, b_ada.reshape(1, n))


def _inproj_body(*refs, with_pe):
    if with_pe:
        x_ref, pe_ref, mod_ref, g_ref, w_ref, o_ref, wb_ref = refs
    else:
        x_ref, mod_ref, g_ref, w_ref, o_ref, wb_ref = refs

    @pl.when((pl.program_id(0) == 0) & (pl.program_id(1) == 0))
    def _():
        wb_ref[...] = w_ref[...].astype(BF16)

    x = x_ref[...]
    if with_pe:
        x = x + pe_ref[...]
    h = _rms(x, g_ref[...]) * (1.0 + mod_ref[1:2, :]) + mod_ref[0:1, :]
    o_ref[...] = _dot(h.astype(BF16), wb_ref[...])


def _inproj(x3, pe, mod3, mod_row, g, w, col_block, n_out, tm):
    b, l, d = x3.shape
    with_pe = pe is not None
    in_specs = [pl.BlockSpec((None, tm, d), lambda bi, i: (bi, i, 0))]
    args = [x3]
    if with_pe:
        in_specs.append(pl.BlockSpec((tm, d), lambda bi, i: (i, 0)))
        args.append(pe)
    in_specs += [pl.BlockSpec((None, N_MOD, d), lambda bi, i: (mod_row(bi), 0, 0)),
                 pl.BlockSpec((1, d), lambda bi, i: (0, 0)),
                 pl.BlockSpec((d, n_out), lambda bi, i: (0, col_block), pipeline_mode=pl.Buffered(1))]
    args += [mod3, g.reshape(1, d), w]
    return pl.pallas_call(
        functools.partial(_inproj_body, with_pe=with_pe),
        grid=(b, l // tm),
        in_specs=in_specs,
        out_specs=pl.BlockSpec((None, tm, n_out), lambda bi, i: (bi, i, 0)),
        out_shape=jax.ShapeDtypeStruct((b, l, n_out), F32),
        scratch_shapes=[pltpu.VMEM((d, n_out), BF16)],
        compiler_params=_cparams("arbitrary", "arbitrary"),
        name="inproj_pe" if with_pe else "inproj_ctx",
    )(*args)


def _rglru_body(*refs, with_gate):
    if with_gate:
        (u_ref, gate_ref, cw_ref, cb_ref, w_ref, bias_ref, sp_ref, h0f_ref, h0b_ref,
         y_ref, hf_end_ref, hb_end_ref, af_ref, bf_ref, ab_ref, bb_ref) = refs
    else:
        (u_ref, cw_ref, cb_ref, w_ref, bias_ref, sp_ref, h0f_ref, h0b_ref,
         hf_end_ref, hb_end_ref, af_ref, bf_ref, ab_ref, bb_ref) = refs
    n, c = u_ref.shape
    u = u_ref[...]
    v = cb_ref[...] + cw_ref[1:2, :] * _shift_rows(u, -1) + cw_ref[0:1, :] * _shift_rows(u, -2)
    v = v + cw_ref[2:3, :] * u + cw_ref[3:4, :] * _shift_rows(u, 1)
    z = _dot(v.astype(BF16), w_ref[...]) + bias_ref[...]
    sub = lax.broadcasted_iota(jnp.int32, (n, c), 0) % SUBLANES

    def local_scan(k, reverse, a_ref, b_ref):
        r = _sigmoid(z[:, (2 * k) * c:(2 * k + 1) * c])
        i = _sigmoid(z[:, (2 * k + 1) * c:(2 * k + 2) * c])
        log_a = (-LRU_C) * r * sp_ref[k:k + 1, :]
        a = jnp.exp(log_a)
        b = jnp.sqrt(1.0 - a * a) * (i * v)
        for s in (1, 2, 4):
            if reverse:
                ok = sub < SUBLANES - s
                a_sh = jnp.where(ok, pltpu.roll(a, n - s, 0), 1.0)
                b_sh = jnp.where(ok, pltpu.roll(b, n - s, 0), 0.0)
            else:
                ok = sub >= s
                a_sh = jnp.where(ok, pltpu.roll(a, s, 0), 1.0)
                b_sh = jnp.where(ok, pltpu.roll(b, s, 0), 0.0)
            b = a * b_sh + b
            a = a * a_sh
        a_ref[...] = a
        b_ref[...] = b

    local_scan(0, False, af_ref, bf_ref)
    local_scan(1, True, ab_ref, bb_ref)

    n_tiles = n // SUBLANES

    def carry_step(q, carry):
        cf, cb = carry
        rf = pl.ds(pl.multiple_of(q * SUBLANES, SUBLANES), SUBLANES)
        rb = pl.ds(pl.multiple_of((n_tiles - 1 - q) * SUBLANES, SUBLANES), SUBLANES)
        hf = af_ref[rf, :] * cf + bf_ref[rf, :]
        hb = ab_ref[rb, :] * cb + bb_ref[rb, :]
        bf_ref[rf, :] = hf
        bb_ref[rb, :] = hb
        return (jnp.broadcast_to(hf[SUBLANES - 1:SUBLANES, :], (SUBLANES, c)),
                jnp.broadcast_to(hb[0:1, :], (SUBLANES, c)))

    cf0 = jnp.broadcast_to(h0f_ref[...], (SUBLANES, c))
    cb0 = jnp.broadcast_to(h0b_ref[...], (SUBLANES, c))
    cf, cb = lax.fori_loop(0, n_tiles, carry_step, (cf0, cb0), unroll=4)
    hf_end_ref[...] = cf[0:1, :]
    hb_end_ref[...] = cb[0:1, :]
    if with_gate:
        y_ref[...] = jax.nn.gelu(gate_ref[...], approximate=True) * (bf_ref[...] + bb_ref[...])


def _rglru(p3, u_blk0, gate_blk0, cw, cb, w_gates, bias, sp, h0f, h0b, with_gate, ch):
    b, l, _ = p3.shape
    d_rnn = cw.shape[1]
    nh = d_rnn // ch
    in_specs = [pl.BlockSpec((None, l, ch), lambda bi, h: (bi, 0, u_blk0 + h))]
    args = [p3]
    if with_gate:
        in_specs.append(pl.BlockSpec((None, l, ch), lambda bi, h: (bi, 0, gate_blk0 + h)))
        args.append(p3)
    in_specs += [pl.BlockSpec((4, ch), lambda bi, h: (0, h)),
                 pl.BlockSpec((1, ch), lambda bi, h: (0, h)),
                 pl.BlockSpec((None, ch, 4 * ch), lambda bi, h: (h, 0, 0)),
                 pl.BlockSpec((None, 1, 4 * ch), lambda bi, h: (h, 0, 0)),
                 pl.BlockSpec((2, ch), lambda bi, h: (0, h)),
                 pl.BlockSpec((None, 1, ch), lambda bi, h: (bi, 0, h)),
                 pl.BlockSpec((None, 1, ch), lambda bi, h: (bi, 0, h))]
    args += [cw, cb, w_gates, bias, sp, h0f, h0b]
    end_spec = pl.BlockSpec((None, 1, ch), lambda bi, h: (bi, 0, h))
    end_shape = jax.ShapeDtypeStruct((b, 1, d_rnn), F32)
    out_specs = [end_spec, end_spec]
    out_shape = [end_shape, end_shape]
    if with_gate:
        out_specs = [pl.BlockSpec((None, l, ch), lambda bi, h: (bi, 0, h))] + out_specs
        out_shape = [jax.ShapeDtypeStruct((b, l, d_rnn), F32)] + out_shape
    return pl.pallas_call(
        functools.partial(_rglru_body, with_gate=with_gate),
        grid=(b, nh),
        in_specs=in_specs,
        out_specs=out_specs,
        out_shape=out_shape,
        scratch_shapes=[pltpu.VMEM((l, ch), F32)] * 4,
        compiler_params=_cparams("arbitrary", "arbitrary"),
        name="rglru_lat" if with_gate else "rglru_ctx",
    )(*args)


def _filt_body(z_ref, w1_ref, b1_ref, f1_ref, w2_ref, b2_ref, f2_ref, w3f_ref, w3b_ref, b3f_ref, b3b_ref,
               dec_ref, c3_ref, s3_ref, cphi_ref, sphi_ref, gr_ref, gi_ref):
    n = z_ref.shape[0]
    ch = gr_ref.shape[1]
    hdn = jnp.sin(f1_ref[...] * (_dot3(z_ref[...], w1_ref[...]) + b1_ref[...]))
    hdn = jnp.sin(f2_ref[...] * (_dot3(hdn, w2_ref[...]) + b2_ref[...]))
    decay = jnp.exp(-z_ref[:, 0:1] * dec_ref[...])
    kf = (_dot3(hdn, w3f_ref[...]) + b3f_ref[...]) * decay
    kb = (_dot3(hdn, w3b_ref[...]) + b3b_ref[...]) * decay
    row = lax.broadcasted_iota(jnp.int32, (n, ch), 0)
    kb = jnp.where(row == 0, 0.0, kb)
    norm = jnp.sum(jnp.abs(kf) + jnp.abs(kb), axis=0, keepdims=True)
    taps = jnp.concatenate([kf, kb], axis=1).astype(BF16)
    p = _dot(c3_ref[...], taps)
    q = _dot(s3_ref[...], taps)
    pf, pb = p[:, :ch], p[:, ch:]
    qf, qb = q[:, :ch], q[:, ch:]
    scale = (2.0 / (2 * n)) / norm
    cphi = cphi_ref[...]
    sphi = sphi_ref[...]
    gr_ref[...] = (cphi * (pf + pb) + sphi * (qf + qb)) * scale
    gi_ref[...] = (sphi * (pf - pb) - cphi * (qf - qb)) * scale


def _filters(z, w1, b1, f1, w2, b2, f2, w3, b3, deltas, c3, s3, cphi, sphi, d_h, order, ch):
    n, fe = z.shape
    hid = w2.shape[0]
    per_order = d_h // ch
    nblk = order * per_order
    const = lambda shape: pl.BlockSpec(shape, lambda g: (0, 0))
    once = lambda shape: pl.BlockSpec(shape, lambda g: (0, 0), pipeline_mode=pl.Buffered(1))
    return pl.pallas_call(
        _filt_body,
        grid=(nblk,),
        in_specs=[const((n, fe)), const((fe, hid)), const((1, hid)), const((1, hid)),
                  const((hid, hid)), const((1, hid)), const((1, hid)),
                  pl.BlockSpec((hid, ch), lambda g: (0, g)),
                  pl.BlockSpec((hid, ch), lambda g: (0, nblk + g)),
                  pl.BlockSpec((1, ch), lambda g: (0, g)),
                  pl.BlockSpec((1, ch), lambda g: (0, nblk + g)),
                  pl.BlockSpec((1, ch), lambda g: (0, g % per_order)),
                  once((n, n)), once((n, n)), const((n, 1)), const((n, 1))],
        out_specs=[pl.BlockSpec((n, ch), lambda g: (0, g)), pl.BlockSpec((n, ch), lambda g: (0, g))],
        out_shape=[jax.ShapeDtypeStruct((n, order * d_h), F32)] * 2,
        compiler_params=_cparams("arbitrary"),
        name="hyena_filters",
    )(z, w1, b1, f1, w2, b2, f2, w3, w3, b3, b3, deltas, c3, s3, cphi, sphi)


def _hyena_body(v_ref, x1_ref, x2_ref, wv_ref, w1_ref, w2_ref, fb_ref, c3_ref, s3_ref,
                g0r_ref, g0i_ref, g1r_ref, g1i_ref, o_ref, z_ref, zb_ref, yr_ref, yi_ref):
    n = o_ref.shape[0]
    rows_per = 512
    halo = SUBLANES

    def conv3(src_ref, w_ref, r0):
        lo, hi = max(r0 - halo, 0), min(r0 + rows_per + halo, n)
        win = src_ref[lo:hi, :]
        size = hi - lo
        row = lax.broadcasted_iota(jnp.int32, win.shape, 0)
        prev = pltpu.roll(win, 1, 0)
        nxt = pltpu.roll(win, size - 1, 0)
        if lo == 0:
            prev = jnp.where(row == 0, 0.0, prev)
        if hi == n:
            nxt = jnp.where(row == size - 1, 0.0, nxt)
        out = w_ref[0:1, :] * prev + w_ref[1:2, :] * win + w_ref[2:3, :] * nxt
        return out[r0 - lo:r0 - lo + rows_per, :]

    def long_conv(gr_ref, gi_ref, fb, gate_ref, gate_w_ref, dst_ref):
        for r0 in range(0, n, rows_per):
            rows = slice(r0, r0 + rows_per)
            p = _dot(c3_ref[rows, :], zb_ref[...])
            q = _dot(s3_ref[rows, :], zb_ref[...])
            gr = gr_ref[rows, :]
            gi = gi_ref[rows, :]
            yr_ref[rows, :] = (p * gr + q * gi).astype(BF16)
            yi_ref[rows, :] = (q * gr - p * gi).astype(BF16)
        for r0 in range(0, n, rows_per):
            rows = slice(r0, r0 + rows_per)
            y = _dot(c3_ref[rows, :], yr_ref[...]) + _dot(s3_ref[rows, :], yi_ref[...])
            dst_ref[rows, :] = conv3(gate_ref, gate_w_ref, r0) * (y + z_ref[rows, :] * fb)

    for r0 in range(0, n, rows_per):
        z_ref[r0:r0 + rows_per, :] = conv3(v_ref, wv_ref, r0)
    zb_ref[...] = z_ref[...].astype(BF16)
    long_conv(g0r_ref, g0i_ref, fb_ref[0:1, :], x1_ref, w1_ref, z_ref)
    zb_ref[...] = z_ref[...].astype(BF16)
    long_conv(g1r_ref, g1i_ref, fb_ref[1:2, :], x2_ref, w2_ref, o_ref)


def _hyena(p3, col_blk0, conv_w, fbias, c3, s3, gr, gi, d_h, ch):
    b, n, _ = p3.shape
    per = d_h // ch
    zspec = lambda k: pl.BlockSpec((None, n, ch), lambda h, bi: (bi, 0, col_blk0 + k * per + h))
    wspec = lambda k: pl.BlockSpec((3, ch), lambda h, bi: (0, k * per + h))
    gspec = lambda o: pl.BlockSpec((n, ch), lambda h, bi: (0, o * per + h), pipeline_mode=pl.Buffered(1))
    const = pl.BlockSpec((n, n), lambda h, bi: (0, 0), pipeline_mode=pl.Buffered(1))
    return pl.pallas_call(
        _hyena_body,
        grid=(per, b),
        in_specs=[zspec(0), zspec(1), zspec(2), wspec(0), wspec(1), wspec(2),
                  pl.BlockSpec((2, ch), lambda h, bi: (0, h)), const, const,
                  gspec(0), gspec(0), gspec(1), gspec(1)],
        out_specs=pl.BlockSpec((None, n, ch), lambda h, bi: (bi, 0, h)),
        out_shape=jax.ShapeDtypeStruct((b, n, d_h), F32),
        scratch_shapes=[pltpu.VMEM((n, ch), F32), pltpu.VMEM((n, ch), BF16),
                        pltpu.VMEM((n, ch), BF16), pltpu.VMEM((n, ch), BF16)],
        compiler_params=_cparams("arbitrary", "arbitrary"),
        name="hyena_mix",
    )(p3, p3, p3, conv_w, conv_w, conv_w, fbias, c3, s3, gr, gi, gr, gi)


def _merge_body(ya_ref, yb_ref, x_ref, pe_ref, mod_ref, ga_ref, gb_ref, wo_ref, g2_ref, wr_ref, br_ref,
                x1_ref, h2t_ref, rt_ref, cnt_ref, wob_ref, h2_ref):
    first = (pl.program_id(0) == 0) & (pl.program_id(1) == 0)

    @pl.when(first)
    def _():
        wob_ref[...] = wo_ref[...].astype(BF16)
        cnt_ref[...] = jnp.zeros_like(cnt_ref)

    da = ya_ref.shape[1]
    na = _rms(ya_ref[...], ga_ref[...]).astype(BF16)
    nb = _rms(yb_ref[...], gb_ref[...]).astype(BF16)
    y = _dot(na, wob_ref[0:da, :]) + _dot(nb, wob_ref[da:, :])
    x1 = x_ref[...] + pe_ref[...] + mod_ref[2:3, :] * y
    x1_ref[...] = x1
    h2 = _rms(x1, g2_ref[...]) * (1.0 + mod_ref[4:5, :]) + mod_ref[3:4, :]
    h2_ref[...] = h2
    _to_token_tiles(h2_ref, h2t_ref)

    logits = _dot3(h2, wr_ref[...]) + br_ref[...]
    lane = lax.broadcasted_iota(jnp.int32, logits.shape, 1).astype(F32)
    neg = -jnp.inf
    big = jnp.float32(1 << 20)
    gl = jnp.where((lane >= N_EXPERTS) & (lane < N_EXPERTS + N_GROUPS), logits, neg)
    gmax = jnp.max(gl, axis=1, keepdims=True)
    g_p = 1.0 / jnp.sum(jnp.exp(gl - gmax), axis=1, keepdims=True)
    g_i = jnp.min(jnp.where(gl == gmax, lane, big), axis=1, keepdims=True) - N_EXPERTS
    lo = g_i * EXPERTS_PER_GROUP
    el = jnp.where((lane >= lo) & (lane < lo + EXPERTS_PER_GROUP), logits, neg)
    m1 = jnp.max(el, axis=1, keepdims=True)
    i1 = jnp.min(jnp.where(el == m1, lane, big), axis=1, keepdims=True)
    el2 = jnp.where(lane == i1, neg, el)
    m2 = jnp.max(el2, axis=1, keepdims=True)
    i2 = jnp.min(jnp.where(el2 == m2, lane, big), axis=1, keepdims=True)
    e2 = jnp.exp(m2 - m1)
    w1 = g_p / (1.0 + e2)
    w2 = g_p * e2 / (1.0 + e2)
    rt_ref[...] = (jnp.where(lane == 0.0, i1, 0.0) + jnp.where(lane == 1.0, i2, 0.0)
                   + jnp.where(lane == 2.0, w1, 0.0) + jnp.where(lane == 3.0, w2, 0.0))
    picked = jnp.where((lane == i1) | (lane == i2), 1.0, 0.0)
    cnt_ref[...] += jnp.sum(picked, axis=0, keepdims=True)


def _merge(ya, yb, x3, pe, mod3, ga, gb, w_out, g2n, w_r, b_r, tm):
    b, n, d = x3.shape
    da, db = ya.shape[2], yb.shape[2]
    tok = lambda w: pl.BlockSpec((None, tm, w), lambda bi, i: (bi, i, 0))
    const = lambda shape: pl.BlockSpec(shape, lambda bi, i: (0, 0))
    return pl.pallas_call(
        _merge_body,
        grid=(b, n // tm),
        in_specs=[tok(da), tok(db), tok(d), pl.BlockSpec((tm, d), lambda bi, i: (i, 0)),
                  pl.BlockSpec((None, N_MOD, d), lambda bi, i: (bi, 0, 0)),
                  const((1, da)), const((1, db)), const((da + db, d)), const((1, d)),
                  const((d, LANES)), const((1, LANES))],
        out_specs=[tok(d), pl.BlockSpec((tm * d // LANES, LANES), lambda bi, i: (bi * (n // tm) + i, 0)),
                   tok(LANES), const((SUBLANES, LANES))],
        out_shape=[jax.ShapeDtypeStruct((b, n, d), F32), jax.ShapeDtypeStruct((b * n * d // LANES, LANES), F32),
                   jax.ShapeDtypeStruct((b, n, LANES), F32), jax.ShapeDtypeStruct((SUBLANES, LANES), F32)],
        scratch_shapes=[pltpu.VMEM((da + db, d), BF16), pltpu.VMEM((tm, d), F32)],
        compiler_params=_cparams("arbitrary", "arbitrary"),
        name="merge_route",
    )(ya, yb, x3, pe, mod3, ga.reshape(1, da), gb.reshape(1, db), w_out, g2n.reshape(1, d), w_r, b_r)


def _rank_body(rt_ref, off_ref, pos_ref, base_ref):
    @pl.when(pl.program_id(0) == 0)
    def _():
        base_ref[...] = off_ref[...]

    tb = rt_ref.shape[0]
    rt = rt_ref[...]
    lane = lax.broadcasted_iota(jnp.int32, rt.shape, 1).astype(F32)
    i1 = jnp.sum(jnp.where(lane == 0.0, rt, 0.0), axis=1, keepdims=True)
    i2 = jnp.sum(jnp.where(lane == 1.0, rt, 0.0), axis=1, keepdims=True)
    o1 = jnp.where(lane == i1, 1.0, 0.0)
    o2 = jnp.where(lane == i2, 1.0, 0.0)
    both = o1 + o2
    r = lax.broadcasted_iota(jnp.int32, (tb, tb), 0)
    c = lax.broadcasted_iota(jnp.int32, (tb, tb), 1)
    earlier = jnp.where(c < r, 1.0, 0.0).astype(BF16)
    before = _dot(earlier, both.astype(BF16)) + base_ref[0:1, :]
    p1 = jnp.sum(o1 * before, axis=1, keepdims=True)
    p2 = jnp.sum(o2 * before, axis=1, keepdims=True)
    pos_ref[...] = (jnp.where(lane == 0.0, p1, 0.0) + jnp.where(lane == 1.0, p2, 0.0)).astype(jnp.int32)
    base_ref[...] += jnp.sum(both, axis=0, keepdims=True)


def _rank(rt2, off, tb):
    t = rt2.shape[0]
    return pl.pallas_call(
        _rank_body,
        grid=(t // tb,),
        in_specs=[pl.BlockSpec((tb, LANES), lambda i: (i, 0)),
                  pl.BlockSpec((SUBLANES, LANES), lambda i: (0, 0))],
        out_specs=pl.BlockSpec((tb, LANES), lambda i: (i, 0)),
        out_shape=jax.ShapeDtypeStruct((t, LANES), jnp.int32),
        scratch_shapes=[pltpu.VMEM((SUBLANES, LANES), F32)],
        compiler_params=_cparams("arbitrary"),
        name="moe_rank",
    )(rt2, off)


def _token_copy(src, src_token, dst, dst_token, sem):
    return pltpu.make_async_copy(_token_rows(src, src_token), _token_rows(dst, dst_token), sem)


def _dispatch_body(pos_ref, ends_ref, h2t_ref, xs_ref, zero_ref, sem):
    i = pl.program_id(0)
    tb = h2t_ref.shape[0] // SUBLANES
    pad_tokens = zero_ref.shape[0] // SUBLANES

    def zero_fill(first_token):
        rows = pl.ds(pl.multiple_of(first_token * SUBLANES, SUBLANES), pad_tokens * SUBLANES)
        cp = pltpu.make_async_copy(zero_ref, xs_ref.at[rows, :], sem)
        cp.start()
        cp.wait()

    @pl.when(i == 0)
    def _():
        zero_ref[...] = jnp.zeros_like(zero_ref)

        def fill(e, start):
            end = ends_ref[e]

            @pl.when(end > start)
            def _():
                zero_fill(end - pad_tokens)
            return end

        used = lax.fori_loop(0, N_EXPERTS, fill, 0)

        def fill_unused(j, carry):
            zero_fill(used + j * pad_tokens)
            return carry

        lax.fori_loop(0, (xs_ref.shape[0] // SUBLANES - used) // pad_tokens, fill_unused, 0)

    def issue(t, carry):
        slot = 2 * (i * tb + t)
        _token_copy(h2t_ref, t, xs_ref, pos_ref[slot], sem).start()
        _token_copy(h2t_ref, t, xs_ref, pos_ref[slot + 1], sem).start()
        return carry

    lax.fori_loop(0, tb, issue, 0, unroll=8)

    for _ in range(2):
        pltpu.make_async_copy(h2t_ref, xs_ref.at[pl.ds(0, tb * SUBLANES), :], sem).wait()


def _dispatch(pos_flat, ends, h2t, slots, tb, tile):
    rows = h2t.shape[0]
    return pl.pallas_call(
        _dispatch_body,
        grid_spec=pltpu.PrefetchScalarGridSpec(
            num_scalar_prefetch=2,
            grid=(rows // (tb * SUBLANES),),
            in_specs=[pl.BlockSpec((tb * SUBLANES, LANES), lambda i, pos, ends: (i, 0))],
            out_specs=pl.BlockSpec(memory_space=pl.ANY),
            scratch_shapes=[pltpu.VMEM((tile * SUBLANES, LANES), F32), pltpu.SemaphoreType.DMA]),
        out_shape=jax.ShapeDtypeStruct((slots * SUBLANES, LANES), F32),
        compiler_params=_cparams("arbitrary"),
        name="moe_dispatch",
    )(pos_flat, ends, h2t)


def _ffn_body(te_ref, nu_ref, xs_ref, wg_ref, wu_ref, wd_ref, ys_ref, wgb_ref, wub_ref, wdb_ref, x2_ref, y2_ref):
    i = pl.program_id(0)
    prev = te_ref[jnp.maximum(i - 1, 0)]

    @pl.when((i == 0) | (te_ref[i] != prev))
    def _():
        wgb_ref[...] = wg_ref[...].astype(BF16)
        wub_ref[...] = wu_ref[...].astype(BF16)
        wdb_ref[...] = wd_ref[...].astype(BF16)

    @pl.when(i < nu_ref[0])
    def _():
        _from_token_tiles(xs_ref, x2_ref)
        xb = x2_ref[...].astype(BF16)
        act = _dot(xb, wgb_ref[...])
        act = act * _sigmoid(act) * _dot(xb, wub_ref[...])
        y2_ref[...] = _dot(act.astype(BF16), wdb_ref[...])
        _to_token_tiles(y2_ref, ys_ref)

    @pl.when(i >= nu_ref[0])
    def _():
        ys_ref[...] = jnp.zeros_like(ys_ref)


def _ffn(tile_expert, n_used, xs, w_gate, w_up, w_down, tile):
    _, d, de = w_gate.shape
    slots = xs.shape[0] * LANES // d
    blk = (tile * d // LANES, LANES)
    last = lambda i, te, nu: jnp.minimum(i, nu[0] - 1)
    return pl.pallas_call(
        _ffn_body,
        grid_spec=pltpu.PrefetchScalarGridSpec(
            num_scalar_prefetch=2,
            grid=(slots // tile,),
            in_specs=[pl.BlockSpec(blk, lambda i, te, nu: (last(i, te, nu), 0)),
                      pl.BlockSpec((None, d, de), lambda i, te, nu: (te[i], 0, 0)),
                      pl.BlockSpec((None, d, de), lambda i, te, nu: (te[i], 0, 0)),
                      pl.BlockSpec((None, de, d), lambda i, te, nu: (te[i], 0, 0))],
            out_specs=pl.BlockSpec(blk, lambda i, te, nu: (i, 0)),
            scratch_shapes=[pltpu.VMEM((d, de), BF16), pltpu.VMEM((d, de), BF16), pltpu.VMEM((de, d), BF16),
                            pltpu.VMEM((tile, d), F32), pltpu.VMEM((tile, d), F32)]),
        out_shape=jax.ShapeDtypeStruct(xs.shape, F32),
        compiler_params=_cparams("arbitrary"),
        name="moe_ffn",
    )(tile_expert, n_used, xs, w_gate, w_up, w_down)


def _combine_body(pos_ref, ys_ref, rt_ref, x1_ref, mod_ref, fg_ref, o_ref, y1t_ref, y2t_ref, y1_ref, y2_ref, sem):
    i = pl.program_id(0)
    tb = x1_ref.shape[0]

    def issue(t, carry):
        slot = 2 * (i * tb + t)
        _token_copy(ys_ref, pos_ref[slot], y1t_ref, t, sem).start()
        _token_copy(ys_ref, pos_ref[slot + 1], y2t_ref, t, sem).start()
        return carry

    lax.fori_loop(0, tb, issue, 0, unroll=8)

    pltpu.make_async_copy(ys_ref.at[pl.ds(0, tb * SUBLANES), :], y1t_ref, sem).wait()
    pltpu.make_async_copy(ys_ref.at[pl.ds(0, tb * SUBLANES), :], y2t_ref, sem).wait()
    _from_token_tiles(y1t_ref, y1_ref)
    _from_token_tiles(y2t_ref, y2_ref)

    rt = rt_ref[...]
    lane = lax.broadcasted_iota(jnp.int32, rt.shape, 1)
    w1 = jnp.sum(jnp.where(lane == 2, rt, 0.0), axis=1, keepdims=True)
    w2 = jnp.sum(jnp.where(lane == 3, rt, 0.0), axis=1, keepdims=True)
    moe = w1 * y1_ref[...] + w2 * y2_ref[...]
    o_ref[...] = _rms(x1_ref[...] + mod_ref[5:6, :] * moe, fg_ref[...])


def _combine(pos_flat, ys, rt2, x1, mod3, final_g, tb, per_batch):
    t, d = x1.shape
    return pl.pallas_call(
        _combine_body,
        grid_spec=pltpu.PrefetchScalarGridSpec(
            num_scalar_prefetch=1,
            grid=(t // tb,),
            in_specs=[pl.BlockSpec(memory_space=pl.ANY),
                      pl.BlockSpec((tb, LANES), lambda i, pos: (i, 0)),
                      pl.BlockSpec((tb, d), lambda i, pos: (i, 0)),
                      pl.BlockSpec((None, N_MOD, d), lambda i, pos: (i // per_batch, 0, 0)),
                      pl.BlockSpec((1, d), lambda i, pos: (0, 0))],
            out_specs=pl.BlockSpec((tb, d), lambda i, pos: (i, 0)),
            scratch_shapes=[pltpu.VMEM((tb * d // LANES, LANES), F32), pltpu.VMEM((tb * d // LANES, LANES), F32),
                            pltpu.VMEM((tb, d), F32), pltpu.VMEM((tb, d), F32), pltpu.SemaphoreType.DMA]),
        out_shape=jax.ShapeDtypeStruct((t, d), F32),
        compiler_params=_cparams("arbitrary"),
        name="moe_combine",
    )(pos_flat, ys, rt2, x1, mod3, final_g.reshape(1, d))


def _sincos_table(rows, cols, dim):
    quarter = dim // 4
    omega = 1.0 / (10000.0 ** (np.arange(quarter, dtype=np.float64) / quarter))
    ang_r = np.arange(rows, dtype=np.float64)[:, None] * omega
    ang_c = np.arange(cols, dtype=np.float64)[:, None] * omega
    emb_r = np.concatenate([np.sin(ang_r), np.cos(ang_r)], axis=-1)
    emb_c = np.concatenate([np.sin(ang_c), np.cos(ang_c)], axis=-1)
    pe = np.concatenate([np.broadcast_to(emb_r[:, None, :], (rows, cols, 2 * quarter)),
                         np.broadcast_to(emb_c[None, :, :], (rows, cols, 2 * quarter))], axis=-1)
    return pe.reshape(rows * cols, 4 * quarter).astype(np.float32)


def _filter_features(n, width):
    pos = np.arange(n, dtype=np.float64)
    t = pos / max(n - 1, 1)
    ang = (2.0 * math.pi * pos / n)[:, None] * np.linspace(1e-4, FILTER_BANDS - 1, FILTER_BANDS)[None, :]
    z = np.concatenate([t[:, None], np.cos(ang), -np.sin(ang)], axis=-1)
    return np.pad(z, ((0, 0), (0, width - z.shape[1]))).astype(np.float32)


def _decay_rates(d_h):
    min_decay = math.log(DECAY_TARGET) / DECAY_SLOW_PCT
    max_decay = math.log(DECAY_TARGET) / DECAY_FAST_PCT
    return np.abs(np.linspace(min_decay, max_decay, d_h)).astype(np.float32).reshape(1, d_h)


def _dft_tables(n):
    idx = 2 * np.arange(n, dtype=np.int64) + 1
    j = (idx[:, None] * idx[None, :]) % (8 * n)
    ang = j.astype(np.float64) * (2.0 * math.pi / (8 * n))
    half = (np.arange(n, dtype=np.float64) + 0.5) * (math.pi / (2 * n))
    return (np.cos(ang).astype(np.float32), np.sin(ang).astype(np.float32),
            np.cos(half).astype(np.float32).reshape(n, 1), np.sin(half).astype(np.float32).reshape(n, 1))


def kernel(x, c, ctx, c_ctx, w_ada, b_ada, norm1_g, w_in, conv_a_w, conv_a_b, lru_wa, lru_ba, lru_wx, lru_bx, lru_lambda, conv_b_w, filt_w1, filt_b1, filt_freq1, filt_w2, filt_b2, filt_freq2, filt_w3, filt_b3, filt_bias, out_norm_a, out_norm_b, w_out, norm2_g, w_rg, b_rg, w_re, b_re, w_gate, w_up, w_down, final_g):
    bsz, n, d = x.shape
    n_ctx = ctx.shape[1]
    d_rnn = conv_a_w.shape[2]
    d_h = filt_bias.shape[2]
    order = filt_bias.shape[1]
    assert w_ada.shape[0] == 1, "single-layer block"
    l = 0
    ch = 256

    mod_rows = 16
    cc = jnp.concatenate([c, c_ctx[None, :], jnp.zeros((mod_rows - bsz - 1, d), F32)], axis=0)
    mod3 = _ada(cc, w_ada[l], b_ada[l]).reshape(mod_rows, N_MOD, d)

    pe = jnp.asarray(_sincos_table(n // GRID_W, GRID_W, d))
    d_in = w_in.shape[2]
    p3 = _inproj(x, pe, mod3, lambda bi: bi, norm1_g[l], w_in[l], 0, d_in, 512)
    pc3 = _inproj(ctx, None, mod3, lambda bi: bsz, norm1_g[l], w_in[l], 1, d_rnn, n_ctx)

    heads_per_blk = ch // (d_rnn // RNN_HEADS)
    nblk = d_rnn // ch
    eye = jnp.eye(heads_per_blk, dtype=F32)

    def blockdiag(w):
        w5 = w.reshape(2, nblk, heads_per_blk, w.shape[2], w.shape[3])
        return jnp.einsum("dnkij,kl->dnkilj", w5, eye).reshape(2, nblk, ch, ch)

    wa_bd, wx_bd = blockdiag(lru_wa[l]), blockdiag(lru_wx[l])
    w_gates = jnp.concatenate([wa_bd[0], wx_bd[0], wa_bd[1], wx_bd[1]], axis=-1).astype(BF16)
    ba, bx = lru_ba[l].reshape(2, nblk, ch), lru_bx[l].reshape(2, nblk, ch)
    bias = jnp.concatenate([ba[0], bx[0], ba[1], bx[1]], axis=-1).reshape(nblk, 1, 4 * ch)
    sp = jax.nn.softplus(-lru_lambda[l])
    cb = conv_a_b[l].reshape(1, d_rnn)
    zeros_state = jnp.zeros((bsz, 1, d_rnn), F32)
    hf_ctx, hb_ctx = _rglru(pc3, 0, 0, conv_a_w[l], cb, w_gates, bias, sp, zeros_state, zeros_state, False, ch)
    ya, _, _ = _rglru(p3, d_rnn // ch, 0, conv_a_w[l], cb, w_gates, bias, sp, hf_ctx, hb_ctx, True, ch)

    c3_np, s3_np, cphi_np, sphi_np = _dft_tables(n)
    c3 = jnp.asarray(c3_np).astype(BF16)
    s3 = jnp.asarray(s3_np).astype(BF16)
    fe = 64
    z = jnp.asarray(_filter_features(n, fe))
    w1 = jnp.pad(filt_w1[l], ((0, fe - filt_w1.shape[1]), (0, 0)))
    hid = filt_w2.shape[1]
    gr, gi = _filters(z, w1, filt_b1[l].reshape(1, hid), filt_freq1[l].reshape(1, hid), filt_w2[l],
                      filt_b2[l].reshape(1, hid), filt_freq2[l].reshape(1, hid), filt_w3[l],
                      filt_b3[l].reshape(1, -1), jnp.asarray(_decay_rates(d_h)), c3, s3,
                      jnp.asarray(cphi_np), jnp.asarray(sphi_np), d_h, order, ch)
    yb = _hyena(p3, 2 * d_rnn // ch, conv_b_w[l], filt_bias[l], c3, s3, gr, gi, d_h, ch)

    w_r = jnp.concatenate([w_re[l], w_rg[l], jnp.zeros((d, LANES - N_EXPERTS - N_GROUPS), F32)], axis=1)
    b_r = jnp.concatenate([b_re[l], b_rg[l], jnp.zeros((LANES - N_EXPERTS - N_GROUPS,), F32)]).reshape(1, LANES)
    x1, h2, rt, cnt = _merge(ya, yb, x, pe, mod3, out_norm_a[l], out_norm_b[l], w_out[l], norm2_g[l], w_r, b_r, 512)

    tile = 256
    t_all = bsz * n
    counts = cnt[0, :N_EXPERTS].astype(jnp.int32)
    tiles_per = (counts + tile - 1) // tile
    tile_ends = jnp.cumsum(tiles_per)
    n_tiles = (2 * t_all + N_EXPERTS * (tile - 1)) // tile
    n_used = tile_ends[-1:]
    tile_expert = jnp.minimum(jnp.searchsorted(tile_ends, jnp.arange(n_tiles, dtype=jnp.int32), side="right"),
                              N_EXPERTS - 1).astype(jnp.int32)
    tile_expert = jnp.where(jnp.arange(n_tiles) < n_used, tile_expert, tile_expert[jnp.maximum(n_used - 1, 0)])
    row_ends = tile_ends * tile
    row_starts = (row_ends - tiles_per * tile).astype(F32)
    off = jnp.broadcast_to(jnp.pad(row_starts, (0, LANES - N_EXPERTS))[None, :], (SUBLANES, LANES))

    rt2 = rt.reshape(t_all, LANES)
    pos = _rank(rt2, off, 512)
    pos_flat = pos[:, :2].reshape(-1)
    assert d == SUBLANES * LANES, "token-tile layout: one token is one (8, 128) tile"
    xs = _dispatch(pos_flat, row_ends.astype(jnp.int32), h2, n_tiles * tile, 256, tile)
    ys = _ffn(tile_expert, n_used.astype(jnp.int32), xs, w_gate[l], w_up[l], w_down[l], tile)
    out = _combine(pos_flat, ys, rt2, x1.reshape(t_all, d), mod3, final_g, 256, n // 256)
    return out.reshape(bsz, n, d)
```

```python
import functools
import math

import numpy as np
import jax
import jax.numpy as jnp
from jax import lax
from jax.experimental import pallas as pl
from jax.experimental.pallas import tpu as pltpu

F32 = jnp.float32
BF16 = jnp.bfloat16
EPS = 1e-6
LRU_C = 8.0
N_MOD = 6
GRID_W = 64
RNN_HEADS = 8
N_GROUPS = 4
EXPERTS_PER_GROUP = 8
N_EXPERTS = N_GROUPS * EXPERTS_PER_GROUP
FILTER_BANDS = 16
DECAY_FAST_PCT = 0.3
DECAY_SLOW_PCT = 1.5
DECAY_TARGET = 1e-2
SUBLANES = 8
LANES = 128
VMEM_LIMIT = 60 * 1024 * 1024


def _cparams(*sem):
    return pltpu.CompilerParams(dimension_semantics=sem, vmem_limit_bytes=VMEM_LIMIT)


def _dot(a, b):
    return jnp.dot(a, b, preferred_element_type=F32)


def _split(a):
    hi = a.astype(BF16)
    return hi, (a - hi.astype(F32)).astype(BF16)


def _dot3(a, b):
    a_hi, a_lo = _split(a)
    b_hi, b_lo = _split(b)
    return _dot(a_hi, b_hi) + _dot(a_lo, b_hi) + _dot(a_hi, b_lo)


def _rms(v, g):
    return v * lax.rsqrt(jnp.mean(v * v, axis=-1, keepdims=True) + EPS) * g


def _sigmoid(z):
    return 1.0 / (1.0 + jnp.exp(-z))


def _shift_rows(v, d):
    n = v.shape[0]
    rolled = pltpu.roll(v, (-d) % n, 0)
    row = lax.broadcasted_iota(jnp.int32, v.shape, 0)
    ok = (row + d >= 0) & (row + d < n)
    return jnp.where(ok, rolled, 0.0)


def _ada_body(c_ref, w_ref, b_ref, o_ref):
    c = c_ref[...]
    o_ref[...] = _dot3(c * _sigmoid(c), w_ref[...]) + b_ref[...]


def _ada(cc, w_ada, b_ada):
    rows, d = cc.shape
    n = w_ada.shape[1]
    tn = 1024
    return pl.pallas_call(
        _ada_body,
        grid=(n // tn,),
        in_specs=[pl.BlockSpec((rows, d), lambda j: (0, 0)),
                  pl.BlockSpec((d, tn), lambda j: (0, j)),
                  pl.BlockSpec((1, tn), lambda j: (0, j))],
        out_specs=pl.BlockSpec((rows, tn), lambda j: (0, j)),
        out_shape=jax.ShapeDtypeStruct((rows, n), F32),
        compiler_params=_cparams("arbitrary"),
        name="ada",
    )(cc, w_ada, b_ada.reshape(1, n))


def _inproj_body(*refs, with_pe):
    if with_pe:
        x_ref, pe_ref, mod_ref, g_ref, w_ref, o_ref, wb_ref = refs
    else:
        x_ref, mod_ref, g_ref, w_ref, o_ref, wb_ref = refs

    @pl.when((pl.program_id(0) == 0) & (pl.program_id(1) == 0))
    def _():
        wb_ref[...] = w_ref[...].astype(BF16)

    x = x_ref[...]
    if with_pe:
        x = x + pe_ref[...]
    h = _rms(x, g_ref[...]) * (1.0 + mod_ref[1:2, :]) + mod_ref[0:1, :]
    o_ref[...] = _dot(h.astype(BF16), wb_ref[...])


def _inproj(x3, pe, mod3, mod_row, g, w, col_block, n_out, tm):
    b, l, d = x3.shape
    with_pe = pe is not None
    in_specs = [pl.BlockSpec((None, tm, d), lambda bi, i: (bi, i, 0))]
    args = [x3]
    if with_pe:
        in_specs.append(pl.BlockSpec((tm, d), lambda bi, i: (i, 0)))
        args.append(pe)
    in_specs += [pl.BlockSpec((None, N_MOD, d), lambda bi, i: (mod_row(bi), 0, 0)),
                 pl.BlockSpec((1, d), lambda bi, i: (0, 0)),
                 pl.BlockSpec((d, n_out), lambda bi, i: (0, col_block), pipeline_mode=pl.Buffered(1))]
    args += [mod3, g.reshape(1, d), w]
    return pl.pallas_call(
        functools.partial(_inproj_body, with_pe=with_pe),
        grid=(b, l // tm),
        in_specs=in_specs,
        out_specs=pl.BlockSpec((None, tm, n_out), lambda bi, i: (bi, i, 0)),
        out_shape=jax.ShapeDtypeStruct((b, l, n_out), F32),
        scratch_shapes=[pltpu.VMEM((d, n_out), BF16)],
        compiler_params=_cparams("arbitrary", "arbitrary"),
        name="inproj_pe" if with_pe else "inproj_ctx",
    )(*args)


def _rglru_body(*refs, with_gate):
    if with_gate:
        (u_ref, gate_ref, cw_ref, cb_ref, w_ref, bias_ref, sp_ref, h0f_ref, h0b_ref,
         y_ref, hf_end_ref, hb_end_ref, af_ref, bf_ref, ab_ref, bb_ref) = refs
    else:
        (u_ref, cw_ref, cb_ref, w_ref, bias_ref, sp_ref, h0f_ref, h0b_ref,
         hf_end_ref, hb_end_ref, af_ref, bf_ref, ab_ref, bb_ref) = refs
    n, c = u_ref.shape
    u = u_ref[...]
    v = cb_ref[...] + cw_ref[1:2, :] * _shift_rows(u, -1) + cw_ref[0:1, :] * _shift_rows(u, -2)
    v = v + cw_ref[2:3, :] * u + cw_ref[3:4, :] * _shift_rows(u, 1)
    z = _dot(v.astype(BF16), w_ref[...]) + bias_ref[...]
    sub = lax.broadcasted_iota(jnp.int32, (n, c), 0) % SUBLANES

    def local_scan(k, reverse, a_ref, b_ref):
        r = _sigmoid(z[:, (2 * k) * c:(2 * k + 1) * c])
        i = _sigmoid(z[:, (2 * k + 1) * c:(2 * k + 2) * c])
        log_a = (-LRU_C) * r * sp_ref[k:k + 1, :]
        a = jnp.exp(log_a)
        b = jnp.sqrt(1.0 - a * a) * (i * v)
        for s in (1, 2, 4):
            if reverse:
                ok = sub < SUBLANES - s
                a_sh = jnp.where(ok, pltpu.roll(a, n - s, 0), 1.0)
                b_sh = jnp.where(ok, pltpu.roll(b, n - s, 0), 0.0)
            else:
                ok = sub >= s
                a_sh = jnp.where(ok, pltpu.roll(a, s, 0), 1.0)
                b_sh = jnp.where(ok, pltpu.roll(b, s, 0), 0.0)
            b = a * b_sh + b
            a = a * a_sh
        a_ref[...] = a
        b_ref[...] = b

    local_scan(0, False, af_ref, bf_ref)
    local_scan(1, True, ab_ref, bb_ref)

    n_tiles = n // SUBLANES

    def carry_step(q, carry):
        cf, cb = carry
        rf = pl.ds(pl.multiple_of(q * SUBLANES, SUBLANES), SUBLANES)
        rb = pl.ds(pl.multiple_of((n_tiles - 1 - q) * SUBLANES, SUBLANES), SUBLANES)
        hf = af_ref[rf, :] * cf + bf_ref[rf, :]
        hb = ab_ref[rb, :] * cb + bb_ref[rb, :]
        bf_ref[rf, :] = hf
        bb_ref[rb, :] = hb
        return (jnp.broadcast_to(hf[SUBLANES - 1:SUBLANES, :], (SUBLANES, c)),
                jnp.broadcast_to(hb[0:1, :], (SUBLANES, c)))

    cf0 = jnp.broadcast_to(h0f_ref[...], (SUBLANES, c))
    cb0 = jnp.broadcast_to(h0b_ref[...], (SUBLANES, c))
    cf, cb = lax.fori_loop(0, n_tiles, carry_step, (cf0, cb0), unroll=4)
    hf_end_ref[...] = cf[0:1, :]
    hb_end_ref[...] = cb[0:1, :]
    if with_gate:
        y_ref[...] = jax.nn.gelu(gate_ref[...], approximate=True) * (bf_ref[...] + bb_ref[...])


def _rglru(p3, u_blk0, gate_blk0, cw, cb, w_gates, bias, sp, h0f, h0b, with_gate, ch):
    b, l, _ = p3.shape
    d_rnn = cw.shape[1]
    nh = d_rnn // ch
    in_specs = [pl.BlockSpec((None, l, ch), lambda bi, h: (bi, 0, u_blk0 + h))]
    args = [p3]
    if with_gate:
        in_specs.append(pl.BlockSpec((None, l, ch), lambda bi, h: (bi, 0, gate_blk0 + h)))
        args.append(p3)
    in_specs += [pl.BlockSpec((4, ch), lambda bi, h: (0, h)),
                 pl.BlockSpec((1, ch), lambda bi, h: (0, h)),
                 pl.BlockSpec((None, ch, 4 * ch), lambda bi, h: (h, 0, 0)),
                 pl.BlockSpec((None, 1, 4 * ch), lambda bi, h: (h, 0, 0)),
                 pl.BlockSpec((2, ch), lambda bi, h: (0, h)),
                 pl.BlockSpec((None, 1, ch), lambda bi, h: (bi, 0, h)),
                 pl.BlockSpec((None, 1, ch), lambda bi, h: (bi, 0, h))]
    args += [cw, cb, w_gates, bias, sp, h0f, h0b]
    end_spec = pl.BlockSpec((None, 1, ch), lambda bi, h: (bi, 0, h))
    end_shape = jax.ShapeDtypeStruct((b, 1, d_rnn), F32)
    out_specs = [end_spec, end_spec]
    out_shape = [end_shape, end_shape]
    if with_gate:
        out_specs = [pl.BlockSpec((None, l, ch), lambda bi, h: (bi, 0, h))] + out_specs
        out_shape = [jax.ShapeDtypeStruct((b, l, d_rnn), F32)] + out_shape
    return pl.pallas_call(
        functools.partial(_rglru_body, with_gate=with_gate),
        grid=(b, nh),
        in_specs=in_specs,
        out_specs=out_specs,
        out_shape=out_shape,
        scratch_shapes=[pltpu.VMEM((l, ch), F32)] * 4,
        compiler_params=_cparams("arbitrary", "arbitrary"),
        name="rglru_lat" if with_gate else "rglru_ctx",
    )(*args)


def _filt_body(z_ref, w1_ref, b1_ref, f1_ref, w2_ref, b2_ref, f2_ref, w3f_ref, w3b_ref, b3f_ref, b3b_ref,
               dec_ref, c3_ref, s3_ref, cphi_ref, sphi_ref, gr_ref, gi_ref):
    n = z_ref.shape[0]
    ch = gr_ref.shape[1]
    hdn = jnp.sin(f1_ref[...] * (_dot3(z_ref[...], w1_ref[...]) + b1_ref[...]))
    hdn = jnp.sin(f2_ref[...] * (_dot3(hdn, w2_ref[...]) + b2_ref[...]))
    decay = jnp.exp(-z_ref[:, 0:1] * dec_ref[...])
    kf = (_dot3(hdn, w3f_ref[...]) + b3f_ref[...]) * decay
    kb = (_dot3(hdn, w3b_ref[...]) + b3b_ref[...]) * decay
    row = lax.broadcasted_iota(jnp.int32, (n, ch), 0)
    kb = jnp.where(row == 0, 0.0, kb)
    norm = jnp.sum(jnp.abs(kf) + jnp.abs(kb), axis=0, keepdims=True)
    taps = jnp.concatenate([kf, kb], axis=1).astype(BF16)
    p = _dot(c3_ref[...], taps)
    q = _dot(s3_ref[...], taps)
    pf, pb = p[:, :ch], p[:, ch:]
    qf, qb = q[:, :ch], q[:, ch:]
    scale = (2.0 / (2 * n)) / norm
    cphi = cphi_ref[...]
    sphi = sphi_ref[...]
    gr_ref[...] = (cphi * (pf + pb) + sphi * (qf + qb)) * scale
    gi_ref[...] = (sphi * (pf - pb) - cphi * (qf - qb)) * scale


def _filters(z, w1, b1, f1, w2, b2, f2, w3, b3, deltas, c3, s3, cphi, sphi, d_h, order, ch):
    n, fe = z.shape
    hid = w2.shape[0]
    per_order = d_h // ch
    nblk = order * per_order
    const = lambda shape: pl.BlockSpec(shape, lambda g: (0, 0))
    once = lambda shape: pl.BlockSpec(shape, lambda g: (0, 0), pipeline_mode=pl.Buffered(1))
    return pl.pallas_call(
        _filt_body,
        grid=(nblk,),
        in_specs=[const((n, fe)), const((fe, hid)), const((1, hid)), const((1, hid)),
                  const((hid, hid)), const((1, hid)), const((1, hid)),
                  pl.BlockSpec((hid, ch), lambda g: (0, g)),
                  pl.BlockSpec((hid, ch), lambda g: (0, nblk + g)),
                  pl.BlockSpec((1, ch), lambda g: (0, g)),
                  pl.BlockSpec((1, ch), lambda g: (0, nblk + g)),
                  pl.BlockSpec((1, ch), lambda g: (0, g % per_order)),
                  once((n, n)), once((n, n)), const((n, 1)), const((n, 1))],
        out_specs=[pl.BlockSpec((n, ch), lambda g: (0, g)), pl.BlockSpec((n, ch), lambda g: (0, g))],
        out_shape=[jax.ShapeDtypeStruct((n, order * d_h), F32)] * 2,
        compiler_params=_cparams("arbitrary"),
        name="hyena_filters",
    )(z, w1, b1, f1, w2, b2, f2, w3, w3, b3, b3, deltas, c3, s3, cphi, sphi)


def _hyena_body(v_ref, x1_ref, x2_ref, wv_ref, w1_ref, w2_ref, fb_ref, c3_ref, s3_ref,
                g0r_ref, g0i_ref, g1r_ref, g1i_ref, o_ref, z_ref, zb_ref, yr_ref, yi_ref):
    n = o_ref.shape[0]
    rows_per = 512
    halo = SUBLANES

    def conv3(src_ref, w_ref, r0):
        lo, hi = max(r0 - halo, 0), min(r0 + rows_per + halo, n)
        win = src_ref[lo:hi, :]
        size = hi - lo
        row = lax.broadcasted_iota(jnp.int32, win.shape, 0)
        prev = pltpu.roll(win, 1, 0)
        nxt = pltpu.roll(win, size - 1, 0)
        if lo == 0:
            prev = jnp.where(row == 0, 0.0, prev)
        if hi == n:
            nxt = jnp.where(row == size - 1, 0.0, nxt)
        out = w_ref[0:1, :] * prev + w_ref[1:2, :] * win + w_ref[2:3, :] * nxt
        return out[r0 - lo:r0 - lo + rows_per, :]

    def long_conv(gr_ref, gi_ref, fb, gate_ref, gate_w_ref, dst_ref):
        for r0 in range(0, n, rows_per):
            rows = slice(r0, r0 + rows_per)
            p = _dot(c3_ref[rows, :], zb_ref[...])
            q = _dot(s3_ref[rows, :], zb_ref[...])
            gr = gr_ref[rows, :]
            gi = gi_ref[rows, :]
            yr_ref[rows, :] = (p * gr + q * gi).astype(BF16)
            yi_ref[rows, :] = (q * gr - p * gi).astype(BF16)
        for r0 in range(0, n, rows_per):
            rows = slice(r0, r0 + rows_per)
            y = _dot(c3_ref[rows, :], yr_ref[...]) + _dot(s3_ref[rows, :], yi_ref[...])
            dst_ref[rows, :] = conv3(gate_ref, gate_w_ref, r0) * (y + z_ref[rows, :] * fb)

    for r0 in range(0, n, rows_per):
        z_ref[r0:r0 + rows_per, :] = conv3(v_ref, wv_ref, r0)
    zb_ref[...] = z_ref[...].astype(BF16)
    long_conv(g0r_ref, g0i_ref, fb_ref[0:1, :], x1_ref, w1_ref, z_ref)
    zb_ref[...] = z_ref[...].astype(BF16)
    long_conv(g1r_ref, g1i_ref, fb_ref[1:2, :], x2_ref, w2_ref, o_ref)


def _hyena(p3, col_blk0, conv_w, fbias, c3, s3, gr, gi, d_h, ch):
    b, n, _ = p3.shape
    per = d_h // ch
    zspec = lambda k: pl.BlockSpec((None, n, ch), lambda h, bi: (bi, 0, col_blk0 + k * per + h))
    wspec = lambda k: pl.BlockSpec((3, ch), lambda h, bi: (0, k * per + h))
    gspec = lambda o: pl.BlockSpec((n, ch), lambda h, bi: (0, o * per + h), pipeline_mode=pl.Buffered(1))
    const = pl.BlockSpec((n, n), lambda h, bi: (0, 0), pipeline_mode=pl.Buffered(1))
    return pl.pallas_call(
        _hyena_body,
        grid=(per, b),
        in_specs=[zspec(0), zspec(1), zspec(2), wspec(0), wspec(1), wspec(2),
                  pl.BlockSpec((2, ch), lambda h, bi: (0, h)), const, const,
                  gspec(0), gspec(0), gspec(1), gspec(1)],
        out_specs=pl.BlockSpec((None, n, ch), lambda h, bi: (bi, 0, h)),
        out_shape=jax.ShapeDtypeStruct((b, n, d_h), F32),
        scratch_shapes=[pltpu.VMEM((n, ch), F32), pltpu.VMEM((n, ch), BF16),
                        pltpu.VMEM((n, ch), BF16), pltpu.VMEM((n, ch), BF16)],
        compiler_params=_cparams("arbitrary", "arbitrary"),
        name="hyena_mix",
    )(p3, p3, p3, conv_w, conv_w, conv_w, fbias, c3, s3, gr, gi, gr, gi)


def _merge_body(ya_ref, yb_ref, x_ref, pe_ref, mod_ref, ga_ref, gb_ref, wo_ref, g2_ref, wr_ref, br_ref,
                x1_ref, h2_ref, rt_ref, cnt_ref, wob_ref):
    first = (pl.program_id(0) == 0) & (pl.program_id(1) == 0)

    @pl.when(first)
    def _():
        wob_ref[...] = wo_ref[...].astype(BF16)

    da = ya_ref.shape[1]
    na = _rms(ya_ref[...], ga_ref[...]).astype(BF16)
    nb = _rms(yb_ref[...], gb_ref[...]).astype(BF16)
    y = _dot(na, wob_ref[0:da, :]) + _dot(nb, wob_ref[da:, :])
    x1 = x_ref[...] + pe_ref[...] + mod_ref[2:3, :] * y
    x1_ref[...] = x1
    h2 = _rms(x1, g2_ref[...]) * (1.0 + mod_ref[4:5, :]) + mod_ref[3:4, :]
    h2_ref[...] = h2.astype(BF16)

    logits = _dot3(h2, wr_ref[...]) + br_ref[...]
    lane = lax.broadcasted_iota(jnp.int32, logits.shape, 1).astype(F32)
    neg = -jnp.inf
    big = jnp.float32(1 << 20)
    gl = jnp.where((lane >= N_EXPERTS) & (lane < N_EXPERTS + N_GROUPS), logits, neg)
    gmax = jnp.max(gl, axis=1, keepdims=True)
    g_p = 1.0 / jnp.sum(jnp.exp(gl - gmax), axis=1, keepdims=True)
    g_i = jnp.min(jnp.where(gl == gmax, lane, big), axis=1, keepdims=True) - N_EXPERTS
    lo = g_i * EXPERTS_PER_GROUP
    el = jnp.where((lane >= lo) & (lane < lo + EXPERTS_PER_GROUP), logits, neg)
    m1 = jnp.max(el, axis=1, keepdims=True)
    i1 = jnp.min(jnp.where(el == m1, lane, big), axis=1, keepdims=True)
    el2 = jnp.where(lane == i1, neg, el)
    m2 = jnp.max(el2, axis=1, keepdims=True)
    i2 = jnp.min(jnp.where(el2 == m2, lane, big), axis=1, keepdims=True)
    e2 = jnp.exp(m2 - m1)
    w1 = g_p / (1.0 + e2)
    w2 = g_p * e2 / (1.0 + e2)
    rt_ref[...] = (jnp.where(lane == 0.0, i1, 0.0) + jnp.where(lane == 1.0, i2, 0.0)
                   + jnp.where(lane == 2.0, w1, 0.0) + jnp.where(lane == 3.0, w2, 0.0))
    picked = jnp.where((lane == i1) | (lane == i2), 1.0, 0.0)
    nsub = cnt_ref.shape[0]
    sub = picked.shape[0] // nsub
    for k in range(nsub):
        cnt_ref[k] = jnp.broadcast_to(jnp.sum(picked[k * sub:(k + 1) * sub], axis=0, keepdims=True), cnt_ref.shape[1:])


def _merge(ya, yb, x3, pe, mod3, ga, gb, w_out, g2n, w_r, b_r, tm, nsub):
    b, n, d = x3.shape
    da, db = ya.shape[2], yb.shape[2]
    tok = lambda w: pl.BlockSpec((None, tm, w), lambda bi, i: (bi, i, 0))
    const = lambda shape: pl.BlockSpec(shape, lambda bi, i: (0, 0))
    return pl.pallas_call(
        _merge_body,
        grid=(b, n // tm),
        in_specs=[tok(da), tok(db), tok(d), pl.BlockSpec((tm, d), lambda bi, i: (i, 0)),
                  pl.BlockSpec((None, N_MOD, d), lambda bi, i: (bi, 0, 0)),
                  const((1, da)), const((1, db)), const((da + db, d)), const((1, d)),
                  const((d, LANES)), const((1, LANES))],
        out_specs=[tok(d), tok(d), tok(LANES),
                   pl.BlockSpec((nsub, SUBLANES, LANES), lambda bi, i: (bi * (n // tm) + i, 0, 0))],
        out_shape=[jax.ShapeDtypeStruct((b, n, d), F32), jax.ShapeDtypeStruct((b, n, d), BF16),
                   jax.ShapeDtypeStruct((b, n, LANES), F32),
                   jax.ShapeDtypeStruct((b * (n // tm) * nsub, SUBLANES, LANES), F32)],
        scratch_shapes=[pltpu.VMEM((da + db, d), BF16)],
        compiler_params=_cparams("arbitrary", "arbitrary"),
        name="merge_route",
    )(ya, yb, x3, pe, mod3, ga.reshape(1, da), gb.reshape(1, db), w_out, g2n.reshape(1, d), w_r, b_r)


SEG = SUBLANES


def _local_slots(rt_ref, lstart_ref):
    tb = rt_ref.shape[0]
    rt = rt_ref[...]
    lane = lax.broadcasted_iota(jnp.int32, rt.shape, 1).astype(F32)
    col = lambda k: jnp.sum(jnp.where(lane == float(k), rt, 0.0), axis=1, keepdims=True)
    i1, i2, w1, w2 = col(0), col(1), col(2), col(3)
    o1 = jnp.where(lane == i1, 1.0, 0.0)
    o2 = jnp.where(lane == i2, 1.0, 0.0)
    r = lax.broadcasted_iota(jnp.int32, (tb, tb), 0)
    c = lax.broadcasted_iota(jnp.int32, (tb, tb), 1)
    earlier = jnp.where(c < r, 1.0, 0.0).astype(BF16)
    before = _dot(earlier, (o1 + o2).astype(BF16)) + lstart_ref[0:1, :]
    p1 = jnp.sum(o1 * before, axis=1, keepdims=True)
    p2 = jnp.sum(o2 * before, axis=1, keepdims=True)
    return p1, p2, w1, w2


def _segment_copies(tab_ref, blk, local_ref, global_ref, sem, to_global, wait):
    def per_expert(e, carry):
        k = (blk * N_EXPERTS + e) * 3
        ls, gs, nchunk = tab_ref[k], tab_ref[k + 1], tab_ref[k + 2]

        def per_chunk(ci, carry2):
            lrows = pl.ds(pl.multiple_of(ls + ci * SEG, SEG), SEG)
            grows = pl.ds(pl.multiple_of(gs + ci * SEG, SEG), SEG)
            if to_global:
                cp = pltpu.make_async_copy(local_ref.at[lrows, :], global_ref.at[grows, :], sem)
            else:
                cp = pltpu.make_async_copy(global_ref.at[grows, :], local_ref.at[lrows, :], sem)
            if wait:
                cp.wait()
            else:
                cp.start()
            return carry2

        return lax.fori_loop(0, nchunk, per_chunk, carry)

    lax.fori_loop(0, N_EXPERTS, per_expert, 0)


def _dispatch_body(tab_ref, ends_ref, rt_ref, lstart_ref, h2_ref, xs_ref, loc_ref, zero_ref, sem):
    blk = pl.program_id(0)
    tb = rt_ref.shape[0]
    lp = loc_ref.shape[0]
    tile = zero_ref.shape[0]

    @pl.when(blk == 0)
    def _():
        zero_ref[...] = jnp.zeros_like(zero_ref)

        def zero_fill(first_row):
            cp = pltpu.make_async_copy(zero_ref, xs_ref.at[pl.ds(pl.multiple_of(first_row, SEG), tile), :], sem)
            cp.start()
            cp.wait()

        def fill(e, start):
            end = ends_ref[e]

            @pl.when(end > start)
            def _():
                zero_fill(end - tile)
            return end

        used = lax.fori_loop(0, N_EXPERTS, fill, 0)

        def fill_unused(j, carry):
            zero_fill(used + j * tile)
            return carry

        lax.fori_loop(0, (xs_ref.shape[0] - used) // tile, fill_unused, 0)

    p1, p2, _, _ = _local_slots(rt_ref, lstart_ref)
    lane = lax.broadcasted_iota(jnp.int32, (tb, LANES), 1)
    parts = (jnp.where(lane == 0, jnp.floor(p1 / 64.0), 0.0) + jnp.where(lane == 1, p1 - 64.0 * jnp.floor(p1 / 64.0), 0.0)
             + jnp.where(lane == 2, jnp.floor(p2 / 64.0), 0.0) + jnp.where(lane == 3, p2 - 64.0 * jnp.floor(p2 / 64.0), 0.0))
    sel_r = lax.broadcasted_iota(jnp.int32, (SUBLANES, LANES), 0)
    sel_c = lax.broadcasted_iota(jnp.int32, (SUBLANES, LANES), 1)
    sel = jnp.where(sel_r == sel_c, 1.0, 0.0).astype(BF16)
    rows = lax.dot_general(sel, parts.astype(BF16), (((1,), (1,)), ((), ())), preferred_element_type=F32)
    q1 = rows[0:1, :] * 64.0 + rows[1:2, :]
    q2 = rows[2:3, :] * 64.0 + rows[3:4, :]
    slot = lax.broadcasted_iota(jnp.int32, (lp, tb), 0).astype(F32)
    onehot = jnp.where((slot == q1) | (slot == q2), 1.0, 0.0).astype(BF16)
    loc_ref[...] = _dot(onehot, h2_ref[...])

    _segment_copies(tab_ref, blk, loc_ref, xs_ref, sem, to_global=True, wait=False)
    _segment_copies(tab_ref, blk, loc_ref, xs_ref, sem, to_global=True, wait=True)


def _dispatch(tab, ends, rt2, lstart, h2, slots, tb, lp, tile):
    t, d = h2.shape
    return pl.pallas_call(
        _dispatch_body,
        grid_spec=pltpu.PrefetchScalarGridSpec(
            num_scalar_prefetch=2,
            grid=(t // tb,),
            in_specs=[pl.BlockSpec((tb, LANES), lambda i, tab, ends: (i, 0)),
                      pl.BlockSpec((None, SUBLANES, LANES), lambda i, tab, ends: (i, 0, 0)),
                      pl.BlockSpec((tb, d), lambda i, tab, ends: (i, 0))],
            out_specs=pl.BlockSpec(memory_space=pl.ANY),
            scratch_shapes=[pltpu.VMEM((lp, d), F32), pltpu.VMEM((tile, d), F32), pltpu.SemaphoreType.DMA]),
        out_shape=jax.ShapeDtypeStruct((slots, d), F32),
        compiler_params=_cparams("arbitrary"),
        name="moe_dispatch",
    )(tab, ends, rt2, lstart, h2)


def _ffn_body(te_ref, nu_ref, xs_ref, wg_ref, wu_ref, wd_ref, ys_ref, wgb_ref, wub_ref, wdb_ref):
    i = pl.program_id(0)
    prev = te_ref[jnp.maximum(i - 1, 0)]

    @pl.when((i == 0) | (te_ref[i] != prev))
    def _():
        wgb_ref[...] = wg_ref[...].astype(BF16)
        wub_ref[...] = wu_ref[...].astype(BF16)
        wdb_ref[...] = wd_ref[...].astype(BF16)

    @pl.when(i < nu_ref[0])
    def _():
        xb = xs_ref[...].astype(BF16)
        act = _dot(xb, wgb_ref[...])
        act = act * _sigmoid(act) * _dot(xb, wub_ref[...])
        ys_ref[...] = _dot(act.astype(BF16), wdb_ref[...])

    @pl.when(i >= nu_ref[0])
    def _():
        ys_ref[...] = jnp.zeros_like(ys_ref)


def _ffn(tile_expert, n_used, xs, w_gate, w_up, w_down, tile):
    slots, d = xs.shape
    _, _, de = w_gate.shape
    last = lambda i, te, nu: jnp.minimum(i, nu[0] - 1)
    return pl.pallas_call(
        _ffn_body,
        grid_spec=pltpu.PrefetchScalarGridSpec(
            num_scalar_prefetch=2,
            grid=(slots // tile,),
            in_specs=[pl.BlockSpec((tile, d), lambda i, te, nu: (last(i, te, nu), 0)),
                      pl.BlockSpec((None, d, de), lambda i, te, nu: (te[i], 0, 0)),
                      pl.BlockSpec((None, d, de), lambda i, te, nu: (te[i], 0, 0)),
                      pl.BlockSpec((None, de, d), lambda i, te, nu: (te[i], 0, 0))],
            out_specs=pl.BlockSpec((tile, d), lambda i, te, nu: (i, 0)),
            scratch_shapes=[pltpu.VMEM((d, de), BF16), pltpu.VMEM((d, de), BF16), pltpu.VMEM((de, d), BF16)]),
        out_shape=jax.ShapeDtypeStruct((slots, d), F32),
        compiler_params=_cparams("arbitrary"),
        name="moe_ffn",
    )(tile_expert, n_used, xs, w_gate, w_up, w_down)


def _combine_body(tab_ref, ltot_ref, ys_ref, rt_ref, lstart_ref, x1_ref, mod_ref, fg_ref, o_ref, loc_ref, sem):
    blk = pl.program_id(0)
    lp, d = loc_ref.shape
    _segment_copies(tab_ref, blk, loc_ref, ys_ref, sem, to_global=False, wait=False)

    def clear(ci, carry):
        loc_ref[pl.ds(pl.multiple_of(ci * SEG, SEG), SEG), :] = jnp.zeros((SEG, d), F32)
        return carry

    lax.fori_loop(ltot_ref[blk] // SEG, lp // SEG, clear, 0)
    p1, p2, w1, w2 = _local_slots(rt_ref, lstart_ref)
    slot = lax.broadcasted_iota(jnp.int32, (rt_ref.shape[0], lp), 1).astype(F32)
    weights = (jnp.where(slot == p1, w1, 0.0) + jnp.where(slot == p2, w2, 0.0)).astype(BF16)
    _segment_copies(tab_ref, blk, loc_ref, ys_ref, sem, to_global=False, wait=True)
    moe = _dot(weights, loc_ref[...].astype(BF16))
    o_ref[...] = _rms(x1_ref[...] + mod_ref[5:6, :] * moe, fg_ref[...])


def _combine(tab, ltot, ys, rt2, lstart, x1, mod3, final_g, tb, lp, per_batch):
    t, d = x1.shape
    return pl.pallas_call(
        _combine_body,
        grid_spec=pltpu.PrefetchScalarGridSpec(
            num_scalar_prefetch=2,
            grid=(t // tb,),
            in_specs=[pl.BlockSpec(memory_space=pl.ANY),
                      pl.BlockSpec((tb, LANES), lambda i, tab, ltot: (i, 0)),
                      pl.BlockSpec((None, SUBLANES, LANES), lambda i, tab, ltot: (i, 0, 0)),
                      pl.BlockSpec((tb, d), lambda i, tab, ltot: (i, 0)),
                      pl.BlockSpec((None, N_MOD, d), lambda i, tab, ltot: (i // per_batch, 0, 0)),
                      pl.BlockSpec((1, d), lambda i, tab, ltot: (0, 0))],
            out_specs=pl.BlockSpec((tb, d), lambda i, tab, ltot: (i, 0)),
            scratch_shapes=[pltpu.VMEM((lp, d), F32), pltpu.SemaphoreType.DMA]),
        out_shape=jax.ShapeDtypeStruct((t, d), F32),
        compiler_params=_cparams("arbitrary"),
        name="moe_combine",
    )(tab, ltot, ys, rt2, lstart, x1, mod3, final_g.reshape(1, d))


def _moe_layout(cnt_blocks, tile, n_tiles):
    nb = cnt_blocks.shape[0]
    lcnt = (cnt_blocks + SEG - 1) // SEG * SEG
    lstart = jnp.cumsum(lcnt, axis=1) - lcnt
    ltot = jnp.sum(lcnt, axis=1)
    per_expert = jnp.sum(lcnt, axis=0)
    tiles_per = (per_expert + tile - 1) // tile
    tile_ends = jnp.cumsum(tiles_per)
    row_ends = tile_ends * tile
    goff = row_ends - tiles_per * tile
    gstart = goff[None, :] + jnp.cumsum(lcnt, axis=0) - lcnt
    tab = jnp.stack([lstart, gstart, lcnt // SEG], axis=-1).reshape(-1).astype(jnp.int32)
    n_used = tile_ends[-1:]
    tile_ids = jnp.arange(n_tiles, dtype=jnp.int32)
    tile_expert = jnp.sum((tile_ends[None, :] <= jnp.minimum(tile_ids, n_used - 1)[:, None]).astype(jnp.int32), axis=1)
    lstart_rows = jnp.broadcast_to(jnp.pad(lstart.astype(F32), ((0, 0), (0, LANES - N_EXPERTS)))[:, None, :],
                                   (nb, SUBLANES, LANES))
    return (tab, row_ends.astype(jnp.int32), ltot.astype(jnp.int32), tile_expert.astype(jnp.int32),
            n_used.astype(jnp.int32), lstart_rows)


def _sincos_table(rows, cols, dim):
    quarter = dim // 4
    omega = 1.0 / (10000.0 ** (np.arange(quarter, dtype=np.float64) / quarter))
    ang_r = np.arange(rows, dtype=np.float64)[:, None] * omega
    ang_c = np.arange(cols, dtype=np.float64)[:, None] * omega
    emb_r = np.concatenate([np.sin(ang_r), np.cos(ang_r)], axis=-1)
    emb_c = np.concatenate([np.sin(ang_c), np.cos(ang_c)], axis=-1)
    pe = np.concatenate([np.broadcast_to(emb_r[:, None, :], (rows, cols, 2 * quarter)),
                         np.broadcast_to(emb_c[None, :, :], (rows, cols, 2 * quarter))], axis=-1)
    return pe.reshape(rows * cols, 4 * quarter).astype(np.float32)


def _filter_features(n, width):
    pos = np.arange(n, dtype=np.float64)
    t = pos / max(n - 1, 1)
    ang = (2.0 * math.pi * pos / n)[:, None] * np.linspace(1e-4, FILTER_BANDS - 1, FILTER_BANDS)[None, :]
    z = np.concatenate([t[:, None], np.cos(ang), -np.sin(ang)], axis=-1)
    return np.pad(z, ((0, 0), (0, width - z.shape[1]))).astype(np.float32)


def _decay_rates(d_h):
    min_decay = math.log(DECAY_TARGET) / DECAY_SLOW_PCT
    max_decay = math.log(DECAY_TARGET) / DECAY_FAST_PCT
    return np.abs(np.linspace(min_decay, max_decay, d_h)).astype(np.float32).reshape(1, d_h)


def _dft_tables(n):
    idx = 2 * np.arange(n, dtype=np.int64) + 1
    j = (idx[:, None] * idx[None, :]) % (8 * n)
    ang = j.astype(np.float64) * (2.0 * math.pi / (8 * n))
    half = (np.arange(n, dtype=np.float64) + 0.5) * (math.pi / (2 * n))
    return (np.cos(ang).astype(np.float32), np.sin(ang).astype(np.float32),
            np.cos(half).astype(np.float32).reshape(n, 1), np.sin(half).astype(np.float32).reshape(n, 1))


def kernel(x, c, ctx, c_ctx, w_ada, b_ada, norm1_g, w_in, conv_a_w, conv_a_b, lru_wa, lru_ba, lru_wx, lru_bx, lru_lambda, conv_b_w, filt_w1, filt_b1, filt_freq1, filt_w2, filt_b2, filt_freq2, filt_w3, filt_b3, filt_bias, out_norm_a, out_norm_b, w_out, norm2_g, w_rg, b_rg, w_re, b_re, w_gate, w_up, w_down, final_g):
    bsz, n, d = x.shape
    n_ctx = ctx.shape[1]
    d_rnn = conv_a_w.shape[2]
    d_h = filt_bias.shape[2]
    order = filt_bias.shape[1]
    assert w_ada.shape[0] == 1, "single-layer block"
    l = 0
    ch = 256

    mod_rows = 16
    cc = jnp.concatenate([c, c_ctx[None, :], jnp.zeros((mod_rows - bsz - 1, d), F32)], axis=0)
    mod3 = _ada(cc, w_ada[l], b_ada[l]).reshape(mod_rows, N_MOD, d)

    pe = jnp.asarray(_sincos_table(n // GRID_W, GRID_W, d))
    d_in = w_in.shape[2]
    p3 = _inproj(x, pe, mod3, lambda bi: bi, norm1_g[l], w_in[l], 0, d_in, 512)
    pc3 = _inproj(ctx, None, mod3, lambda bi: bsz, norm1_g[l], w_in[l], 1, d_rnn, n_ctx)

    heads_per_blk = ch // (d_rnn // RNN_HEADS)
    nblk = d_rnn // ch
    eye = jnp.eye(heads_per_blk, dtype=F32)

    def blockdiag(w):
        w5 = w.reshape(2, nblk, heads_per_blk, w.shape[2], w.shape[3])
        return jnp.einsum("dnkij,kl->dnkilj", w5, eye).reshape(2, nblk, ch, ch)

    wa_bd, wx_bd = blockdiag(lru_wa[l]), blockdiag(lru_wx[l])
    w_gates = jnp.concatenate([wa_bd[0], wx_bd[0], wa_bd[1], wx_bd[1]], axis=-1).astype(BF16)
    ba, bx = lru_ba[l].reshape(2, nblk, ch), lru_bx[l].reshape(2, nblk, ch)
    bias = jnp.concatenate([ba[0], bx[0], ba[1], bx[1]], axis=-1).reshape(nblk, 1, 4 * ch)
    sp = jax.nn.softplus(-lru_lambda[l])
    cb = conv_a_b[l].reshape(1, d_rnn)
    zeros_state = jnp.zeros((bsz, 1, d_rnn), F32)
    hf_ctx, hb_ctx = _rglru(pc3, 0, 0, conv_a_w[l], cb, w_gates, bias, sp, zeros_state, zeros_state, False, ch)
    ya, _, _ = _rglru(p3, d_rnn // ch, 0, conv_a_w[l], cb, w_gates, bias, sp, hf_ctx, hb_ctx, True, ch)

    c3_np, s3_np, cphi_np, sphi_np = _dft_tables(n)
    c3 = jnp.asarray(c3_np).astype(BF16)
    s3 = jnp.asarray(s3_np).astype(BF16)
    fe = 64
    z = jnp.asarray(_filter_features(n, fe))
    w1 = jnp.pad(filt_w1[l], ((0, fe - filt_w1.shape[1]), (0, 0)))
    hid = filt_w2.shape[1]
    gr, gi = _filters(z, w1, filt_b1[l].reshape(1, hid), filt_freq1[l].reshape(1, hid), filt_w2[l],
                      filt_b2[l].reshape(1, hid), filt_freq2[l].reshape(1, hid), filt_w3[l],
                      filt_b3[l].reshape(1, -1), jnp.asarray(_decay_rates(d_h)), c3, s3,
                      jnp.asarray(cphi_np), jnp.asarray(sphi_np), d_h, order, ch)
    yb = _hyena(p3, 2 * d_rnn // ch, conv_b_w[l], filt_bias[l], c3, s3, gr, gi, d_h, ch)

    w_r = jnp.concatenate([w_re[l], w_rg[l], jnp.zeros((d, LANES - N_EXPERTS - N_GROUPS), F32)], axis=1)
    b_r = jnp.concatenate([b_re[l], b_rg[l], jnp.zeros((LANES - N_EXPERTS - N_GROUPS,), F32)]).reshape(1, LANES)
    tb = 256
    tm = 512
    x1, h2, rt, cnt = _merge(ya, yb, x, pe, mod3, out_norm_a[l], out_norm_b[l], w_out[l], norm2_g[l], w_r, b_r,
                             tm, tm // tb)

    tile = 256
    t_all = bsz * n
    nb = t_all // tb
    lp = 2 * tb + N_EXPERTS * SEG
    n_tiles = (2 * t_all + nb * N_EXPERTS * (SEG - 1) + N_EXPERTS * (tile - 1)) // tile
    cnt_blocks = cnt[:, 0, :N_EXPERTS].astype(jnp.int32)
    tab, row_ends, ltot, tile_expert, n_used, lstart_rows = _moe_layout(cnt_blocks, tile, n_tiles)
    rt2 = rt.reshape(t_all, LANES)
    xs = _dispatch(tab, row_ends, rt2, lstart_rows, h2.reshape(t_all, d), n_tiles * tile, tb, lp, tile)
    ys = _ffn(tile_expert, n_used, xs, w_gate[l], w_up[l], w_down[l], tile)
    out = _combine(tab, ltot, ys, rt2, lstart_rows, x1.reshape(t_all, d), mod3, final_g, tb, lp, n // tb)
    return out.reshape(bsz, n, d)
```

```python
import functools
import math

import numpy as np
import jax
import jax.numpy as jnp
from jax import lax
from jax.experimental import pallas as pl
from jax.experimental.pallas import tpu as pltpu

F32 = jnp.float32
BF16 = jnp.bfloat16
EPS = 1e-6
LRU_C = 8.0
N_MOD = 6
GRID_W = 64
RNN_HEADS = 8
N_GROUPS = 4
EXPERTS_PER_GROUP = 8
N_EXPERTS = N_GROUPS * EXPERTS_PER_GROUP
FILTER_BANDS = 16
DECAY_FAST_PCT = 0.3
DECAY_SLOW_PCT = 1.5
DECAY_TARGET = 1e-2
SUBLANES = 8
LANES = 128
VMEM_LIMIT = 60 * 1024 * 1024


def _cparams(*sem):
    return pltpu.CompilerParams(dimension_semantics=sem, vmem_limit_bytes=VMEM_LIMIT)


def _dot(a, b):
    return jnp.dot(a, b, preferred_element_type=F32)


def _split(a):
    hi = a.astype(BF16)
    return hi, (a - hi.astype(F32)).astype(BF16)


def _dot3(a, b):
    a_hi, a_lo = _split(a)
    b_hi, b_lo = _split(b)
    return _dot(a_hi, b_hi) + _dot(a_lo, b_hi) + _dot(a_hi, b_lo)


def _rms(v, g):
    return v * lax.rsqrt(jnp.mean(v * v, axis=-1, keepdims=True) + EPS) * g


def _sigmoid(z):
    return 1.0 / (1.0 + jnp.exp(-z))


def _shift_rows(v, d):
    n = v.shape[0]
    rolled = pltpu.roll(v, (-d) % n, 0)
    row = lax.broadcasted_iota(jnp.int32, v.shape, 0)
    ok = (row + d >= 0) & (row + d < n)
    return jnp.where(ok, rolled, 0.0)


def _ada_body(c_ref, w_ref, b_ref, o_ref):
    c = c_ref[...]
    o_ref[...] = _dot3(c * _sigmoid(c), w_ref[...]) + b_ref[...]


def _ada(cc, w_ada, b_ada):
    rows, d = cc.shape
    n = w_ada.shape[1]
    tn = 1024
    return pl.pallas_call(
        _ada_body,
        grid=(n // tn,),
        in_specs=[pl.BlockSpec((rows, d), lambda j: (0, 0)),
                  pl.BlockSpec((d, tn), lambda j: (0, j)),
                  pl.BlockSpec((1, tn), lambda j: (0, j))],
        out_specs=pl.BlockSpec((rows, tn), lambda j: (0, j)),
        out_shape=jax.ShapeDtypeStruct((rows, n), F32),
        compiler_params=_cparams("arbitrary"),
        name="ada",
    )(cc, w_ada, b_ada.reshape(1, n))


def _inproj_body(*refs, with_pe):
    if with_pe:
        x_ref, pe_ref, mod_ref, g_ref, w_ref, o_ref, wb_ref = refs
    else:
        x_ref, mod_ref, g_ref, w_ref, o_ref, wb_ref = refs

    @pl.when((pl.program_id(0) == 0) & (pl.program_id(1) == 0))
    def _():
        wb_ref[...] = w_ref[...].astype(BF16)

    x = x_ref[...]
    if with_pe:
        x = x + pe_ref[...]
    h = _rms(x, g_ref[...]) * (1.0 + mod_ref[1:2, :]) + mod_ref[0:1, :]
    o_ref[...] = _dot(h.astype(BF16), wb_ref[...])


def _inproj(x3, pe, mod3, mod_row, g, w, col_block, n_out, tm):
    b, l, d = x3.shape
    with_pe = pe is not None
    in_specs = [pl.BlockSpec((None, tm, d), lambda bi, i: (bi, i, 0))]
    args = [x3]
    if with_pe:
        in_specs.append(pl.BlockSpec((tm, d), lambda bi, i: (i, 0)))
        args.append(pe)
    in_specs += [pl.BlockSpec((None, N_MOD, d), lambda bi, i: (mod_row(bi), 0, 0)),
                 pl.BlockSpec((1, d), lambda bi, i: (0, 0)),
                 pl.BlockSpec((d, n_out), lambda bi, i: (0, col_block), pipeline_mode=pl.Buffered(1))]
    args += [mod3, g.reshape(1, d), w]
    return pl.pallas_call(
        functools.partial(_inproj_body, with_pe=with_pe),
        grid=(b, l // tm),
        in_specs=in_specs,
        out_specs=pl.BlockSpec((None, tm, n_out), lambda bi, i: (bi, i, 0)),
        out_shape=jax.ShapeDtypeStruct((b, l, n_out), F32),
        scratch_shapes=[pltpu.VMEM((d, n_out), BF16)],
        compiler_params=_cparams("arbitrary", "arbitrary"),
        name="inproj_pe" if with_pe else "inproj_ctx",
    )(*args)


def _rglru_body(*refs, with_gate):
    if with_gate:
        (u_ref, gate_ref, cw_ref, cb_ref, w_ref, bias_ref, sp_ref, h0f_ref, h0b_ref,
         y_ref, hf_end_ref, hb_end_ref, af_ref, bf_ref, ab_ref, bb_ref) = refs
    else:
        (u_ref, cw_ref, cb_ref, w_ref, bias_ref, sp_ref, h0f_ref, h0b_ref,
         hf_end_ref, hb_end_ref, af_ref, bf_ref, ab_ref, bb_ref) = refs
    n, c = u_ref.shape
    u = u_ref[...]
    v = cb_ref[...] + cw_ref[1:2, :] * _shift_rows(u, -1) + cw_ref[0:1, :] * _shift_rows(u, -2)
    v = v + cw_ref[2:3, :] * u + cw_ref[3:4, :] * _shift_rows(u, 1)
    z = _dot(v.astype(BF16), w_ref[...]) + bias_ref[...]
    sub = lax.broadcasted_iota(jnp.int32, (n, c), 0) % SUBLANES

    def local_scan(k, reverse, a_ref, b_ref):
        r = _sigmoid(z[:, (2 * k) * c:(2 * k + 1) * c])
        i = _sigmoid(z[:, (2 * k + 1) * c:(2 * k + 2) * c])
        log_a = (-LRU_C) * r * sp_ref[k:k + 1, :]
        a = jnp.exp(log_a)
        b = jnp.sqrt(1.0 - a * a) * (i * v)
        for s in (1, 2, 4):
            if reverse:
                ok = sub < SUBLANES - s
                a_sh = jnp.where(ok, pltpu.roll(a, n - s, 0), 1.0)
                b_sh = jnp.where(ok, pltpu.roll(b, n - s, 0), 0.0)
            else:
                ok = sub >= s
                a_sh = jnp.where(ok, pltpu.roll(a, s, 0), 1.0)
                b_sh = jnp.where(ok, pltpu.roll(b, s, 0), 0.0)
            b = a * b_sh + b
            a = a * a_sh
        a_ref[...] = a
        b_ref[...] = b

    local_scan(0, False, af_ref, bf_ref)
    local_scan(1, True, ab_ref, bb_ref)

    n_tiles = n // SUBLANES

    def carry_step(q, carry):
        cf, cb = carry
        rf = pl.ds(pl.multiple_of(q * SUBLANES, SUBLANES), SUBLANES)
        rb = pl.ds(pl.multiple_of((n_tiles - 1 - q) * SUBLANES, SUBLANES), SUBLANES)
        hf = af_ref[rf, :] * cf + bf_ref[rf, :]
        hb = ab_ref[rb, :] * cb + bb_ref[rb, :]
        bf_ref[rf, :] = hf
        bb_ref[rb, :] = hb
        return (jnp.broadcast_to(hf[SUBLANES - 1:SUBLANES, :], (SUBLANES, c)),
                jnp.broadcast_to(hb[0:1, :], (SUBLANES, c)))

    cf0 = jnp.broadcast_to(h0f_ref[...], (SUBLANES, c))
    cb0 = jnp.broadcast_to(h0b_ref[...], (SUBLANES, c))
    cf, cb = lax.fori_loop(0, n_tiles, carry_step, (cf0, cb0), unroll=4)
    hf_end_ref[...] = cf[0:1, :]
    hb_end_ref[...] = cb[0:1, :]
    if with_gate:
        y_ref[...] = jax.nn.gelu(gate_ref[...], approximate=True) * (bf_ref[...] + bb_ref[...])


def _rglru(p3, u_blk0, gate_blk0, cw, cb, w_gates, bias, sp, h0f, h0b, with_gate, ch):
    b, l, _ = p3.shape
    d_rnn = cw.shape[1]
    nh = d_rnn // ch
    in_specs = [pl.BlockSpec((None, l, ch), lambda bi, h: (bi, 0, u_blk0 + h))]
    args = [p3]
    if with_gate:
        in_specs.append(pl.BlockSpec((None, l, ch), lambda bi, h: (bi, 0, gate_blk0 + h)))
        args.append(p3)
    in_specs += [pl.BlockSpec((4, ch), lambda bi, h: (0, h)),
                 pl.BlockSpec((1, ch), lambda bi, h: (0, h)),
                 pl.BlockSpec((None, ch, 4 * ch), lambda bi, h: (h, 0, 0)),
                 pl.BlockSpec((None, 1, 4 * ch), lambda bi, h: (h, 0, 0)),
                 pl.BlockSpec((2, ch), lambda bi, h: (0, h)),
                 pl.BlockSpec((None, 1, ch), lambda bi, h: (bi, 0, h)),
                 pl.BlockSpec((None, 1, ch), lambda bi, h: (bi, 0, h))]
    args += [cw, cb, w_gates, bias, sp, h0f, h0b]
    end_spec = pl.BlockSpec((None, 1, ch), lambda bi, h: (bi, 0, h))
    end_shape = jax.ShapeDtypeStruct((b, 1, d_rnn), F32)
    out_specs = [end_spec, end_spec]
    out_shape = [end_shape, end_shape]
    if with_gate:
        out_specs = [pl.BlockSpec((None, l, ch), lambda bi, h: (bi, 0, h))] + out_specs
        out_shape = [jax.ShapeDtypeStruct((b, l, d_rnn), F32)] + out_shape
    return pl.pallas_call(
        functools.partial(_rglru_body, with_gate=with_gate),
        grid=(b, nh),
        in_specs=in_specs,
        out_specs=out_specs,
        out_shape=out_shape,
        scratch_shapes=[pltpu.VMEM((l, ch), F32)] * 4,
        compiler_params=_cparams("arbitrary", "arbitrary"),
        name="rglru_lat" if with_gate else "rglru_ctx",
    )(*args)


def _filt_body(z_ref, w1_ref, b1_ref, f1_ref, w2_ref, b2_ref, f2_ref, w3f_ref, w3b_ref, b3f_ref, b3b_ref,
               dec_ref, c3_ref, s3_ref, cphi_ref, sphi_ref, gr_ref, gi_ref):
    n = z_ref.shape[0]
    ch = gr_ref.shape[1]
    hdn = jnp.sin(f1_ref[...] * (_dot3(z_ref[...], w1_ref[...]) + b1_ref[...]))
    hdn = jnp.sin(f2_ref[...] * (_dot3(hdn, w2_ref[...]) + b2_ref[...]))
    decay = jnp.exp(-z_ref[:, 0:1] * dec_ref[...])
    kf = (_dot3(hdn, w3f_ref[...]) + b3f_ref[...]) * decay
    kb = (_dot3(hdn, w3b_ref[...]) + b3b_ref[...]) * decay
    row = lax.broadcasted_iota(jnp.int32, (n, ch), 0)
    kb = jnp.where(row == 0, 0.0, kb)
    norm = jnp.sum(jnp.abs(kf) + jnp.abs(kb), axis=0, keepdims=True)
    taps = jnp.concatenate([kf, kb], axis=1).astype(BF16)
    p = _dot(c3_ref[...], taps)
    q = _dot(s3_ref[...], taps)
    pf, pb = p[:, :ch], p[:, ch:]
    qf, qb = q[:, :ch], q[:, ch:]
    scale = (2.0 / (2 * n)) / norm
    cphi = cphi_ref[...]
    sphi = sphi_ref[...]
    gr_ref[...] = (cphi * (pf + pb) + sphi * (qf + qb)) * scale
    gi_ref[...] = (sphi * (pf - pb) - cphi * (qf - qb)) * scale


def _filters(z, w1, b1, f1, w2, b2, f2, w3, b3, deltas, c3, s3, cphi, sphi, d_h, order, ch):
    n, fe = z.shape
    hid = w2.shape[0]
    per_order = d_h // ch
    nblk = order * per_order
    const = lambda shape: pl.BlockSpec(shape, lambda g: (0, 0))
    once = lambda shape: pl.BlockSpec(shape, lambda g: (0, 0), pipeline_mode=pl.Buffered(1))
    return pl.pallas_call(
        _filt_body,
        grid=(nblk,),
        in_specs=[const((n, fe)), const((fe, hid)), const((1, hid)), const((1, hid)),
                  const((hid, hid)), const((1, hid)), const((1, hid)),
                  pl.BlockSpec((hid, ch), lambda g: (0, g)),
                  pl.BlockSpec((hid, ch), lambda g: (0, nblk + g)),
                  pl.BlockSpec((1, ch), lambda g: (0, g)),
                  pl.BlockSpec((1, ch), lambda g: (0, nblk + g)),
                  pl.BlockSpec((1, ch), lambda g: (0, g % per_order)),
                  once((n, n)), once((n, n)), const((n, 1)), const((n, 1))],
        out_specs=[pl.BlockSpec((n, ch), lambda g: (0, g)), pl.BlockSpec((n, ch), lambda g: (0, g))],
        out_shape=[jax.ShapeDtypeStruct((n, order * d_h), F32)] * 2,
        compiler_params=_cparams("arbitrary"),
        name="hyena_filters",
    )(z, w1, b1, f1, w2, b2, f2, w3, w3, b3, b3, deltas, c3, s3, cphi, sphi)


def _hyena_body(v_ref, x1_ref, x2_ref, wv_ref, w1_ref, w2_ref, fb_ref, c3_ref, s3_ref,
                g0r_ref, g0i_ref, g1r_ref, g1i_ref, o_ref, z_ref, zb_ref, yr_ref, yi_ref):
    n = o_ref.shape[0]
    rows_per = 512
    halo = SUBLANES

    def conv3(src_ref, w_ref, r0):
        lo, hi = max(r0 - halo, 0), min(r0 + rows_per + halo, n)
        win = src_ref[lo:hi, :]
        size = hi - lo
        row = lax.broadcasted_iota(jnp.int32, win.shape, 0)
        prev = pltpu.roll(win, 1, 0)
        nxt = pltpu.roll(win, size - 1, 0)
        if lo == 0:
            prev = jnp.where(row == 0, 0.0, prev)
        if hi == n:
            nxt = jnp.where(row == size - 1, 0.0, nxt)
        out = w_ref[0:1, :] * prev + w_ref[1:2, :] * win + w_ref[2:3, :] * nxt
        return out[r0 - lo:r0 - lo + rows_per, :]

    def long_conv(gr_ref, gi_ref, fb, gate_ref, gate_w_ref, dst_ref):
        for r0 in range(0, n, rows_per):
            rows = slice(r0, r0 + rows_per)
            p = _dot(c3_ref[rows, :], zb_ref[...])
            q = _dot(s3_ref[rows, :], zb_ref[...])
            gr = gr_ref[rows, :]
            gi = gi_ref[rows, :]
            yr_ref[rows, :] = (p * gr + q * gi).astype(BF16)
            yi_ref[rows, :] = (q * gr - p * gi).astype(BF16)
        for r0 in range(0, n, rows_per):
            rows = slice(r0, r0 + rows_per)
            y = _dot(c3_ref[rows, :], yr_ref[...]) + _dot(s3_ref[rows, :], yi_ref[...])
            dst_ref[rows, :] = conv3(gate_ref, gate_w_ref, r0) * (y + z_ref[rows, :] * fb)

    for r0 in range(0, n, rows_per):
        z_ref[r0:r0 + rows_per, :] = conv3(v_ref, wv_ref, r0)
    zb_ref[...] = z_ref[...].astype(BF16)
    long_conv(g0r_ref, g0i_ref, fb_ref[0:1, :], x1_ref, w1_ref, z_ref)
    zb_ref[...] = z_ref[...].astype(BF16)
    long_conv(g1r_ref, g1i_ref, fb_ref[1:2, :], x2_ref, w2_ref, o_ref)


def _hyena(p3, col_blk0, conv_w, fbias, c3, s3, gr, gi, d_h, ch):
    b, n, _ = p3.shape
    per = d_h // ch
    zspec = lambda k: pl.BlockSpec((None, n, ch), lambda h, bi: (bi, 0, col_blk0 + k * per + h))
    wspec = lambda k: pl.BlockSpec((3, ch), lambda h, bi: (0, k * per + h))
    gspec = lambda o: pl.BlockSpec((n, ch), lambda h, bi: (0, o * per + h), pipeline_mode=pl.Buffered(1))
    const = pl.BlockSpec((n, n), lambda h, bi: (0, 0), pipeline_mode=pl.Buffered(1))
    return pl.pallas_call(
        _hyena_body,
        grid=(per, b),
        in_specs=[zspec(0), zspec(1), zspec(2), wspec(0), wspec(1), wspec(2),
                  pl.BlockSpec((2, ch), lambda h, bi: (0, h)), const, const,
                  gspec(0), gspec(0), gspec(1), gspec(1)],
        out_specs=pl.BlockSpec((None, n, ch), lambda h, bi: (bi, 0, h)),
        out_shape=jax.ShapeDtypeStruct((b, n, d_h), F32),
        scratch_shapes=[pltpu.VMEM((n, ch), F32), pltpu.VMEM((n, ch), BF16),
                        pltpu.VMEM((n, ch), BF16), pltpu.VMEM((n, ch), BF16)],
        compiler_params=_cparams("arbitrary", "arbitrary"),
        name="hyena_mix",
    )(p3, p3, p3, conv_w, conv_w, conv_w, fbias, c3, s3, gr, gi, gr, gi)


def _merge_body(ya_ref, yb_ref, x_ref, pe_ref, mod_ref, ga_ref, gb_ref, wo_ref, g2_ref, wr_ref, br_ref,
                x1_ref, h2_ref, rt_ref, cnt_ref, wob_ref):
    first = (pl.program_id(0) == 0) & (pl.program_id(1) == 0)

    @pl.when(first)
    def _():
        wob_ref[...] = wo_ref[...].astype(BF16)

    da = ya_ref.shape[1]
    na = _rms(ya_ref[...], ga_ref[...]).astype(BF16)
    nb = _rms(yb_ref[...], gb_ref[...]).astype(BF16)
    y = _dot(na, wob_ref[0:da, :]) + _dot(nb, wob_ref[da:, :])
    x1 = x_ref[...] + pe_ref[...] + mod_ref[2:3, :] * y
    x1_ref[...] = x1
    h2 = _rms(x1, g2_ref[...]) * (1.0 + mod_ref[4:5, :]) + mod_ref[3:4, :]
    h2_ref[...] = h2.astype(BF16)

    logits = _dot3(h2, wr_ref[...]) + br_ref[...]
    lane = lax.broadcasted_iota(jnp.int32, logits.shape, 1).astype(F32)
    neg = -jnp.inf
    big = jnp.float32(1 << 20)
    gl = jnp.where((lane >= N_EXPERTS) & (lane < N_EXPERTS + N_GROUPS), logits, neg)
    gmax = jnp.max(gl, axis=1, keepdims=True)
    g_p = 1.0 / jnp.sum(jnp.exp(gl - gmax), axis=1, keepdims=True)
    g_i = jnp.min(jnp.where(gl == gmax, lane, big), axis=1, keepdims=True) - N_EXPERTS
    lo = g_i * EXPERTS_PER_GROUP
    el = jnp.where((lane >= lo) & (lane < lo + EXPERTS_PER_GROUP), logits, neg)
    m1 = jnp.max(el, axis=1, keepdims=True)
    i1 = jnp.min(jnp.where(el == m1, lane, big), axis=1, keepdims=True)
    el2 = jnp.where(lane == i1, neg, el)
    m2 = jnp.max(el2, axis=1, keepdims=True)
    i2 = jnp.min(jnp.where(el2 == m2, lane, big), axis=1, keepdims=True)
    e2 = jnp.exp(m2 - m1)
    w1 = g_p / (1.0 + e2)
    w2 = g_p * e2 / (1.0 + e2)
    rt_ref[...] = (jnp.where(lane == 0.0, i1, 0.0) + jnp.where(lane == 1.0, i2, 0.0)
                   + jnp.where(lane == 2.0, w1, 0.0) + jnp.where(lane == 3.0, w2, 0.0))
    picked = jnp.where((lane == i1) | (lane == i2), 1.0, 0.0)
    nsub = cnt_ref.shape[0]
    sub = picked.shape[0] // nsub
    for k in range(nsub):
        cnt_ref[k] = jnp.broadcast_to(jnp.sum(picked[k * sub:(k + 1) * sub], axis=0, keepdims=True), cnt_ref.shape[1:])


def _merge(ya, yb, x3, pe, mod3, ga, gb, w_out, g2n, w_r, b_r, tm, nsub):
    b, n, d = x3.shape
    da, db = ya.shape[2], yb.shape[2]
    tok = lambda w: pl.BlockSpec((None, tm, w), lambda bi, i: (bi, i, 0))
    const = lambda shape: pl.BlockSpec(shape, lambda bi, i: (0, 0))
    return pl.pallas_call(
        _merge_body,
        grid=(b, n // tm),
        in_specs=[tok(da), tok(db), tok(d), pl.BlockSpec((tm, d), lambda bi, i: (i, 0)),
                  pl.BlockSpec((None, N_MOD, d), lambda bi, i: (bi, 0, 0)),
                  const((1, da)), const((1, db)), const((da + db, d)), const((1, d)),
                  const((d, LANES)), const((1, LANES))],
        out_specs=[tok(d), tok(d), tok(LANES),
                   pl.BlockSpec((nsub, SUBLANES, LANES), lambda bi, i: (bi * (n // tm) + i, 0, 0))],
        out_shape=[jax.ShapeDtypeStruct((b, n, d), F32), jax.ShapeDtypeStruct((b, n, d), BF16),
                   jax.ShapeDtypeStruct((b, n, LANES), F32),
                   jax.ShapeDtypeStruct((b * (n // tm) * nsub, SUBLANES, LANES), F32)],
        scratch_shapes=[pltpu.VMEM((da + db, d), BF16)],
        compiler_params=_cparams("arbitrary", "arbitrary"),
        name="merge_route",
    )(ya, yb, x3, pe, mod3, ga.reshape(1, da), gb.reshape(1, db), w_out, g2n.reshape(1, d), w_r, b_r)


SEG = SUBLANES


def _local_slots(rt_ref, lstart_ref):
    tb = rt_ref.shape[0]
    rt = rt_ref[...]
    lane = lax.broadcasted_iota(jnp.int32, rt.shape, 1).astype(F32)
    col = lambda k: jnp.sum(jnp.where(lane == float(k), rt, 0.0), axis=1, keepdims=True)
    i1, i2, w1, w2 = col(0), col(1), col(2), col(3)
    o1 = jnp.where(lane == i1, 1.0, 0.0)
    o2 = jnp.where(lane == i2, 1.0, 0.0)
    r = lax.broadcasted_iota(jnp.int32, (tb, tb), 0)
    c = lax.broadcasted_iota(jnp.int32, (tb, tb), 1)
    earlier = jnp.where(c < r, 1.0, 0.0).astype(BF16)
    before = _dot(earlier, (o1 + o2).astype(BF16)) + lstart_ref[0:1, :]
    p1 = jnp.sum(o1 * before, axis=1, keepdims=True)
    p2 = jnp.sum(o2 * before, axis=1, keepdims=True)
    return p1, p2, w1, w2


def _segment_copy(local_ref, lrow, global_ref, grow, rows, sem, to_global):
    lrows = pl.ds(pl.multiple_of(lrow, SEG), rows)
    grows = pl.ds(pl.multiple_of(grow, SEG), rows)
    if to_global:
        return pltpu.make_async_copy(local_ref.at[lrows, :], global_ref.at[grows, :], sem)
    return pltpu.make_async_copy(global_ref.at[grows, :], local_ref.at[lrows, :], sem)


def _start_segments(tab_ref, blk, local_ref, global_ref, sem, to_global):
    def per_expert(e, carry):
        k = (blk * N_EXPERTS + e) * 3
        ls, gs, nchunk = tab_ref[k], tab_ref[k + 1], tab_ref[k + 2]

        def per_chunk(ci, carry2):
            _segment_copy(local_ref, ls + ci * SEG, global_ref, gs + ci * SEG, SEG, sem, to_global).start()
            return carry2

        return lax.fori_loop(0, nchunk, per_chunk, carry)

    lax.fori_loop(0, N_EXPERTS, per_expert, 0)


def _wait_segments(rows, local_ref, global_ref, sem, to_global):
    chunks = rows // SEG
    nbits = (local_ref.shape[0] // SEG).bit_length()
    for k in range(nbits):
        @pl.when(((chunks >> k) & 1) == 1)
        def _():
            _segment_copy(local_ref, 0, global_ref, 0, SEG << k, sem, to_global).wait()


def _dispatch_body(tab_ref, ends_ref, ltot_ref, rt_ref, lstart_ref, h2_ref, xs_ref, loc_ref, zero_ref, sem, zsem):
    blk = pl.program_id(0)
    nblk = pl.num_programs(0)
    tb = rt_ref.shape[0]
    lp = loc_ref.shape[1]
    tile = zero_ref.shape[0]
    slot = blk % 2

    @pl.when(blk == 0)
    def _():
        zero_ref[...] = jnp.zeros_like(zero_ref)

        def zero_fill(first_row):
            return pltpu.make_async_copy(zero_ref, xs_ref.at[pl.ds(pl.multiple_of(first_row, SEG), tile), :], zsem)

        def fill(e, start, wait):
            end = ends_ref[e]

            @pl.when(end > start)
            def _():
                cp = zero_fill(end - tile)
                cp.wait() if wait else cp.start()
            return end

        used = lax.fori_loop(0, N_EXPERTS, functools.partial(fill, wait=False), 0)
        spare = (xs_ref.shape[0] - used) // tile

        def fill_spare(j, carry, wait):
            cp = zero_fill(used + j * tile)
            cp.wait() if wait else cp.start()
            return carry

        lax.fori_loop(0, spare, functools.partial(fill_spare, wait=False), 0)
        lax.fori_loop(0, N_EXPERTS, functools.partial(fill, wait=True), 0)
        lax.fori_loop(0, spare, functools.partial(fill_spare, wait=True), 0)

    p1, p2, _, _ = _local_slots(rt_ref, lstart_ref)
    lane = lax.broadcasted_iota(jnp.int32, (tb, LANES), 1)
    hi1, hi2 = jnp.floor(p1 / 64.0), jnp.floor(p2 / 64.0)
    parts = (jnp.where(lane == 0, hi1, 0.0) + jnp.where(lane == 1, p1 - 64.0 * hi1, 0.0)
             + jnp.where(lane == 2, hi2, 0.0) + jnp.where(lane == 3, p2 - 64.0 * hi2, 0.0))
    sel_r = lax.broadcasted_iota(jnp.int32, (SUBLANES, LANES), 0)
    sel_c = lax.broadcasted_iota(jnp.int32, (SUBLANES, LANES), 1)
    sel = jnp.where(sel_r == sel_c, 1.0, 0.0).astype(BF16)
    rows = lax.dot_general(sel, parts.astype(BF16), (((1,), (1,)), ((), ())), preferred_element_type=F32)
    q1 = rows[0:1, :] * 64.0 + rows[1:2, :]
    q2 = rows[2:3, :] * 64.0 + rows[3:4, :]
    pos = lax.broadcasted_iota(jnp.int32, (lp, tb), 0).astype(F32)
    onehot = jnp.where((pos == q1) | (pos == q2), 1.0, 0.0).astype(BF16)
    loc_ref[slot] = _dot(onehot, h2_ref[...])

    _start_segments(tab_ref, blk, loc_ref.at[slot], xs_ref, sem.at[slot], True)

    @pl.when(blk > 0)
    def _():
        _wait_segments(ltot_ref[blk - 1], loc_ref.at[1 - slot], xs_ref, sem.at[1 - slot], True)

    @pl.when(blk == nblk - 1)
    def _():
        _wait_segments(ltot_ref[blk], loc_ref.at[slot], xs_ref, sem.at[slot], True)


def _dispatch(tab, ends, ltot, rt2, lstart, h2, slots, tb, lp, tile):
    t, d = h2.shape
    return pl.pallas_call(
        _dispatch_body,
        grid_spec=pltpu.PrefetchScalarGridSpec(
            num_scalar_prefetch=3,
            grid=(t // tb,),
            in_specs=[pl.BlockSpec((tb, LANES), lambda i, *_: (i, 0)),
                      pl.BlockSpec((None, SUBLANES, LANES), lambda i, *_: (i, 0, 0)),
                      pl.BlockSpec((tb, d), lambda i, *_: (i, 0))],
            out_specs=pl.BlockSpec(memory_space=pl.ANY),
            scratch_shapes=[pltpu.VMEM((2, lp, d), F32), pltpu.VMEM((tile, d), F32),
                            pltpu.SemaphoreType.DMA((2,)), pltpu.SemaphoreType.DMA]),
        out_shape=jax.ShapeDtypeStruct((slots, d), F32),
        compiler_params=_cparams("arbitrary"),
        name="moe_dispatch",
    )(tab, ends, ltot, rt2, lstart, h2)


def _ffn_body(te_ref, nu_ref, nxt_ref, par_ref, xs_ref, wg_hbm, wu_hbm, wd_hbm, ys_ref,
              wg_buf, wu_buf, wd_buf, wgb_ref, wub_ref, wdb_ref, sem):
    i = pl.program_id(0)
    e = te_ref[i]
    first = (i == 0) | (e != te_ref[jnp.maximum(i - 1, 0)])
    active = i < nu_ref[0]

    def fetch(expert, slot):
        return [pltpu.make_async_copy(w.at[expert], buf.at[slot], sem.at[slot])
                for w, buf in ((wg_hbm, wg_buf), (wu_hbm, wu_buf), (wd_hbm, wd_buf))]

    @pl.when(i == 0)
    def _():
        for cp in fetch(e, par_ref[e]):
            cp.start()

    @pl.when(first & active)
    def _():
        slot = par_ref[e]
        for cp in fetch(e, slot):
            cp.wait()
        nxt = nxt_ref[e]

        @pl.when(nxt >= 0)
        def _():
            for cp in fetch(nxt, 1 - slot):
                cp.start()

        wgb_ref[...] = wg_buf[slot].astype(BF16)
        wub_ref[...] = wu_buf[slot].astype(BF16)
        wdb_ref[...] = wd_buf[slot].astype(BF16)

    @pl.when(active)
    def _():
        xb = xs_ref[...].astype(BF16)
        act = _dot(xb, wgb_ref[...])
        act = act * _sigmoid(act) * _dot(xb, wub_ref[...])
        ys_ref[...] = _dot(act.astype(BF16), wdb_ref[...])


def _ffn(tile_expert, n_used, nxt, par, xs, w_gate, w_up, w_down, tile):
    slots, d = xs.shape
    _, _, de = w_gate.shape
    last = lambda i, te, nu, *_: (jnp.minimum(i, nu[0] - 1), 0)
    hbm = pl.BlockSpec(memory_space=pl.ANY)
    return pl.pallas_call(
        _ffn_body,
        grid_spec=pltpu.PrefetchScalarGridSpec(
            num_scalar_prefetch=4,
            grid=(slots // tile,),
            in_specs=[pl.BlockSpec((tile, d), last), hbm, hbm, hbm],
            out_specs=pl.BlockSpec((tile, d), last),
            scratch_shapes=[pltpu.VMEM((2, d, de), F32), pltpu.VMEM((2, d, de), F32), pltpu.VMEM((2, de, d), F32),
                            pltpu.VMEM((d, de), BF16), pltpu.VMEM((d, de), BF16), pltpu.VMEM((de, d), BF16),
                            pltpu.SemaphoreType.DMA((2,))]),
        out_shape=jax.ShapeDtypeStruct((slots, d), F32),
        input_output_aliases={4: 0},
        compiler_params=_cparams("arbitrary"),
        name="moe_ffn",
    )(tile_expert, n_used, nxt, par, xs, w_gate, w_up, w_down)


def _combine_body(tab_ref, ltot_ref, ys_ref, rt_ref, lstart_ref, x1_ref, mod_ref, fg_ref, o_ref, loc_ref, sem):
    blk = pl.program_id(0)
    nblk = pl.num_programs(0)
    _, lp, d = loc_ref.shape
    slot = blk % 2

    @pl.when(blk == 0)
    def _():
        _start_segments(tab_ref, blk, loc_ref.at[slot], ys_ref, sem.at[slot], False)

    @pl.when(blk + 1 < nblk)
    def _():
        _start_segments(tab_ref, blk + 1, loc_ref.at[1 - slot], ys_ref, sem.at[1 - slot], False)

    p1, p2, w1, w2 = _local_slots(rt_ref, lstart_ref)
    pos = lax.broadcasted_iota(jnp.int32, (rt_ref.shape[0], lp), 1).astype(F32)
    weights = (jnp.where(pos == p1, w1, 0.0) + jnp.where(pos == p2, w2, 0.0)).astype(BF16)
    _wait_segments(ltot_ref[blk], loc_ref.at[slot], ys_ref, sem.at[slot], False)

    def clear(ci, carry):
        loc_ref[slot, pl.ds(pl.multiple_of(ci * SEG, SEG), SEG), :] = jnp.zeros((SEG, d), F32)
        return carry

    lax.fori_loop(ltot_ref[blk] // SEG, lp // SEG, clear, 0)
    moe = _dot(weights, loc_ref[slot].astype(BF16))
    o_ref[...] = _rms(x1_ref[...] + mod_ref[5:6, :] * moe, fg_ref[...])


def _combine(tab, ltot, ys, rt2, lstart, x1, mod3, final_g, tb, lp, per_batch):
    t, d = x1.shape
    return pl.pallas_call(
        _combine_body,
        grid_spec=pltpu.PrefetchScalarGridSpec(
            num_scalar_prefetch=2,
            grid=(t // tb,),
            in_specs=[pl.BlockSpec(memory_space=pl.ANY),
                      pl.BlockSpec((tb, LANES), lambda i, *_: (i, 0)),
                      pl.BlockSpec((None, SUBLANES, LANES), lambda i, *_: (i, 0, 0)),
                      pl.BlockSpec((tb, d), lambda i, *_: (i, 0)),
                      pl.BlockSpec((None, N_MOD, d), lambda i, *_: (i // per_batch, 0, 0)),
                      pl.BlockSpec((1, d), lambda i, *_: (0, 0))],
            out_specs=pl.BlockSpec((tb, d), lambda i, *_: (i, 0)),
            scratch_shapes=[pltpu.VMEM((2, lp, d), F32), pltpu.SemaphoreType.DMA((2,))]),
        out_shape=jax.ShapeDtypeStruct((t, d), F32),
        compiler_params=_cparams("arbitrary"),
        name="moe_combine",
    )(tab, ltot, ys, rt2, lstart, x1, mod3, final_g.reshape(1, d))


def _moe_layout(cnt_blocks, tile, n_tiles):
    nb = cnt_blocks.shape[0]
    lcnt = (cnt_blocks + SEG - 1) // SEG * SEG
    lstart = jnp.cumsum(lcnt, axis=1) - lcnt
    ltot = jnp.sum(lcnt, axis=1)
    per_expert = jnp.sum(lcnt, axis=0)
    tiles_per = (per_expert + tile - 1) // tile
    tile_ends = jnp.cumsum(tiles_per)
    row_ends = tile_ends * tile
    goff = row_ends - tiles_per * tile
    gstart = goff[None, :] + jnp.cumsum(lcnt, axis=0) - lcnt
    tab = jnp.stack([lstart, gstart, lcnt // SEG], axis=-1).reshape(-1).astype(jnp.int32)
    n_used = tile_ends[-1:]
    tile_ids = jnp.arange(n_tiles, dtype=jnp.int32)
    tile_expert = jnp.sum((tile_ends[None, :] <= jnp.minimum(tile_ids, n_used - 1)[:, None]).astype(jnp.int32), axis=1)
    ids = jnp.arange(N_EXPERTS, dtype=jnp.int32)
    used = tiles_per > 0
    par = (jnp.cumsum(used) - used) % 2
    later = jnp.where(used[None, :] & (ids[None, :] > ids[:, None]), ids[None, :], N_EXPERTS)
    nxt = jnp.min(later, axis=1)
    nxt = jnp.where(nxt == N_EXPERTS, -1, nxt)
    lstart_rows = jnp.broadcast_to(jnp.pad(lstart.astype(F32), ((0, 0), (0, LANES - N_EXPERTS)))[:, None, :],
                                   (nb, SUBLANES, LANES))
    return (tab, row_ends.astype(jnp.int32), ltot.astype(jnp.int32), tile_expert.astype(jnp.int32),
            n_used.astype(jnp.int32), nxt.astype(jnp.int32), par.astype(jnp.int32), lstart_rows)


def _sincos_table(rows, cols, dim):
    quarter = dim // 4
    omega = 1.0 / (10000.0 ** (np.arange(quarter, dtype=np.float64) / quarter))
    ang_r = np.arange(rows, dtype=np.float64)[:, None] * omega
    ang_c = np.arange(cols, dtype=np.float64)[:, None] * omega
    emb_r = np.concatenate([np.sin(ang_r), np.cos(ang_r)], axis=-1)
    emb_c = np.concatenate([np.sin(ang_c), np.cos(ang_c)], axis=-1)
    pe = np.concatenate([np.broadcast_to(emb_r[:, None, :], (rows, cols, 2 * quarter)),
                         np.broadcast_to(emb_c[None, :, :], (rows, cols, 2 * quarter))], axis=-1)
    return pe.reshape(rows * cols, 4 * quarter).astype(np.float32)


def _filter_features(n, width):
    pos = np.arange(n, dtype=np.float64)
    t = pos / max(n - 1, 1)
    ang = (2.0 * math.pi * pos / n)[:, None] * np.linspace(1e-4, FILTER_BANDS - 1, FILTER_BANDS)[None, :]
    z = np.concatenate([t[:, None], np.cos(ang), -np.sin(ang)], axis=-1)
    return np.pad(z, ((0, 0), (0, width - z.shape[1]))).astype(np.float32)


def _decay_rates(d_h):
    min_decay = math.log(DECAY_TARGET) / DECAY_SLOW_PCT
    max_decay = math.log(DECAY_TARGET) / DECAY_FAST_PCT
    return np.abs(np.linspace(min_decay, max_decay, d_h)).astype(np.float32).reshape(1, d_h)


def _dft_tables(n):
    idx = 2 * np.arange(n, dtype=np.int64) + 1
    j = (idx[:, None] * idx[None, :]) % (8 * n)
    ang = j.astype(np.float64) * (2.0 * math.pi / (8 * n))
    half = (np.arange(n, dtype=np.float64) + 0.5) * (math.pi / (2 * n))
    return (np.cos(ang).astype(np.float32), np.sin(ang).astype(np.float32),
            np.cos(half).astype(np.float32).reshape(n, 1), np.sin(half).astype(np.float32).reshape(n, 1))


def kernel(x, c, ctx, c_ctx, w_ada, b_ada, norm1_g, w_in, conv_a_w, conv_a_b, lru_wa, lru_ba, lru_wx, lru_bx, lru_lambda, conv_b_w, filt_w1, filt_b1, filt_freq1, filt_w2, filt_b2, filt_freq2, filt_w3, filt_b3, filt_bias, out_norm_a, out_norm_b, w_out, norm2_g, w_rg, b_rg, w_re, b_re, w_gate, w_up, w_down, final_g):
    bsz, n, d = x.shape
    n_ctx = ctx.shape[1]
    d_rnn = conv_a_w.shape[2]
    d_h = filt_bias.shape[2]
    order = filt_bias.shape[1]
    assert w_ada.shape[0] == 1, "single-layer block"
    l = 0
    ch = 256

    mod_rows = 16
    cc = jnp.concatenate([c, c_ctx[None, :], jnp.zeros((mod_rows - bsz - 1, d), F32)], axis=0)
    mod3 = _ada(cc, w_ada[l], b_ada[l]).reshape(mod_rows, N_MOD, d)

    pe = jnp.asarray(_sincos_table(n // GRID_W, GRID_W, d))
    d_in = w_in.shape[2]
    p3 = _inproj(x, pe, mod3, lambda bi: bi, norm1_g[l], w_in[l], 0, d_in, 512)
    pc3 = _inproj(ctx, None, mod3, lambda bi: bsz, norm1_g[l], w_in[l], 1, d_rnn, n_ctx)

    heads_per_blk = ch // (d_rnn // RNN_HEADS)
    nblk = d_rnn // ch
    eye = jnp.eye(heads_per_blk, dtype=F32)

    def blockdiag(w):
        w5 = w.reshape(2, nblk, heads_per_blk, w.shape[2], w.shape[3])
        return jnp.einsum("dnkij,kl->dnkilj", w5, eye).reshape(2, nblk, ch, ch)

    wa_bd, wx_bd = blockdiag(lru_wa[l]), blockdiag(lru_wx[l])
    w_gates = jnp.concatenate([wa_bd[0], wx_bd[0], wa_bd[1], wx_bd[1]], axis=-1).astype(BF16)
    ba, bx = lru_ba[l].reshape(2, nblk, ch), lru_bx[l].reshape(2, nblk, ch)
    bias = jnp.concatenate([ba[0], bx[0], ba[1], bx[1]], axis=-1).reshape(nblk, 1, 4 * ch)
    sp = jax.nn.softplus(-lru_lambda[l])
    cb = conv_a_b[l].reshape(1, d_rnn)
    zeros_state = jnp.zeros((bsz, 1, d_rnn), F32)
    hf_ctx, hb_ctx = _rglru(pc3, 0, 0, conv_a_w[l], cb, w_gates, bias, sp, zeros_state, zeros_state, False, ch)
    ya, _, _ = _rglru(p3, d_rnn // ch, 0, conv_a_w[l], cb, w_gates, bias, sp, hf_ctx, hb_ctx, True, ch)

    c3_np, s3_np, cphi_np, sphi_np = _dft_tables(n)
    c3 = jnp.asarray(c3_np).astype(BF16)
    s3 = jnp.asarray(s3_np).astype(BF16)
    fe = 64
    z = jnp.asarray(_filter_features(n, fe))
    w1 = jnp.pad(filt_w1[l], ((0, fe - filt_w1.shape[1]), (0, 0)))
    hid = filt_w2.shape[1]
    gr, gi = _filters(z, w1, filt_b1[l].reshape(1, hid), filt_freq1[l].reshape(1, hid), filt_w2[l],
                      filt_b2[l].reshape(1, hid), filt_freq2[l].reshape(1, hid), filt_w3[l],
                      filt_b3[l].reshape(1, -1), jnp.asarray(_decay_rates(d_h)), c3, s3,
                      jnp.asarray(cphi_np), jnp.asarray(sphi_np), d_h, order, ch)
    yb = _hyena(p3, 2 * d_rnn // ch, conv_b_w[l], filt_bias[l], c3, s3, gr, gi, d_h, ch)

    w_r = jnp.concatenate([w_re[l], w_rg[l], jnp.zeros((d, LANES - N_EXPERTS - N_GROUPS), F32)], axis=1)
    b_r = jnp.concatenate([b_re[l], b_rg[l], jnp.zeros((LANES - N_EXPERTS - N_GROUPS,), F32)]).reshape(1, LANES)
    tb = 256
    tm = 512
    x1, h2, rt, cnt = _merge(ya, yb, x, pe, mod3, out_norm_a[l], out_norm_b[l], w_out[l], norm2_g[l], w_r, b_r,
                             tm, tm // tb)

    tile = 512
    t_all = bsz * n
    nb = t_all // tb
    lp = 2 * tb + N_EXPERTS * SEG
    n_tiles = (2 * t_all + nb * N_EXPERTS * (SEG - 1) + N_EXPERTS * (tile - 1)) // tile
    cnt_blocks = cnt[:, 0, :N_EXPERTS].astype(jnp.int32)
    tab, row_ends, ltot, tile_expert, n_used, nxt, par, lstart_rows = _moe_layout(cnt_blocks, tile, n_tiles)
    rt2 = rt.reshape(t_all, LANES)
    xs = _dispatch(tab, row_ends, ltot, rt2, lstart_rows, h2.reshape(t_all, d), n_tiles * tile, tb, lp, tile)
    ys = _ffn(tile_expert, n_used, nxt, par, xs, w_gate[l], w_up[l], w_down[l], tile)
    out = _combine(tab, ltot, ys, rt2, lstart_rows, x1.reshape(t_all, d), mod3, final_g, tb, lp, n // tb)
    return out.reshape(bsz, n, d)
```

```python
import functools
import math

import numpy as np
import jax
import jax.numpy as jnp
from jax import lax
from jax.experimental import pallas as pl
from jax.experimental.pallas import tpu as pltpu

F32 = jnp.float32
BF16 = jnp.bfloat16
EPS = 1e-6
LRU_C = 8.0
N_MOD = 6
GRID_W = 64
RNN_HEADS = 8
N_GROUPS = 4
EXPERTS_PER_GROUP = 8
N_EXPERTS = N_GROUPS * EXPERTS_PER_GROUP
FILTER_BANDS = 16
DECAY_FAST_PCT = 0.3
DECAY_SLOW_PCT = 1.5
DECAY_TARGET = 1e-2
SUBLANES = 8
LANES = 128
VMEM_LIMIT = 60 * 1024 * 1024


def _cparams(*sem):
    return pltpu.CompilerParams(dimension_semantics=sem, vmem_limit_bytes=VMEM_LIMIT)


def _dot(a, b):
    return jnp.dot(a, b, preferred_element_type=F32)


def _split(a):
    hi = a.astype(BF16)
    return hi, (a - hi.astype(F32)).astype(BF16)


def _dot3(a, b):
    a_hi, a_lo = _split(a)
    b_hi, b_lo = _split(b)
    return _dot(a_hi, b_hi) + _dot(a_lo, b_hi) + _dot(a_hi, b_lo)


def _rms(v, g):
    return v * lax.rsqrt(jnp.mean(v * v, axis=-1, keepdims=True) + EPS) * g


def _sigmoid(z):
    return 1.0 / (1.0 + jnp.exp(-z))


def _shift_rows(v, d):
    n = v.shape[0]
    rolled = pltpu.roll(v, (-d) % n, 0)
    row = lax.broadcasted_iota(jnp.int32, v.shape, 0)
    ok = (row + d >= 0) & (row + d < n)
    return jnp.where(ok, rolled, 0.0)


def _ada_body(c_ref, w_ref, b_ref, o_ref):
    c = c_ref[...]
    o_ref[...] = _dot3(c * _sigmoid(c), w_ref[...]) + b_ref[...]


def _ada(cc, w_ada, b_ada):
    rows, d = cc.shape
    n = w_ada.shape[1]
    tn = 1024
    return pl.pallas_call(
        _ada_body,
        grid=(n // tn,),
        in_specs=[pl.BlockSpec((rows, d), lambda j: (0, 0)),
                  pl.BlockSpec((d, tn), lambda j: (0, j)),
                  pl.BlockSpec((1, tn), lambda j: (0, j))],
        out_specs=pl.BlockSpec((rows, tn), lambda j: (0, j)),
        out_shape=jax.ShapeDtypeStruct((rows, n), F32),
        compiler_params=_cparams("arbitrary"),
        name="ada",
    )(cc, w_ada, b_ada.reshape(1, n))


def _inproj_body(*refs, with_pe):
    if with_pe:
        x_ref, pe_ref, mod_ref, g_ref, w_ref, o_ref, wb_ref = refs
    else:
        x_ref, mod_ref, g_ref, w_ref, o_ref, wb_ref = refs

    @pl.when((pl.program_id(0) == 0) & (pl.program_id(1) == 0))
    def _():
        wb_ref[...] = w_ref[...].astype(BF16)

    x = x_ref[...]
    if with_pe:
        x = x + pe_ref[...]
    h = _rms(x, g_ref[...]) * (1.0 + mod_ref[1:2, :]) + mod_ref[0:1, :]
    o_ref[...] = _dot(h.astype(BF16), wb_ref[...])


def _inproj(x3, pe, mod3, mod_row, g, w, col_block, n_out, tm):
    b, l, d = x3.shape
    with_pe = pe is not None
    in_specs = [pl.BlockSpec((None, tm, d), lambda bi, i: (bi, i, 0))]
    args = [x3]
    if with_pe:
        in_specs.append(pl.BlockSpec((tm, d), lambda bi, i: (i, 0)))
        args.append(pe)
    in_specs += [pl.BlockSpec((None, N_MOD, d), lambda bi, i: (mod_row(bi), 0, 0)),
                 pl.BlockSpec((1, d), lambda bi, i: (0, 0)),
                 pl.BlockSpec((d, n_out), lambda bi, i: (0, col_block), pipeline_mode=pl.Buffered(1))]
    args += [mod3, g.reshape(1, d), w]
    return pl.pallas_call(
        functools.partial(_inproj_body, with_pe=with_pe),
        grid=(b, l // tm),
        in_specs=in_specs,
        out_specs=pl.BlockSpec((None, tm, n_out), lambda bi, i: (bi, i, 0)),
        out_shape=jax.ShapeDtypeStruct((b, l, n_out), F32),
        scratch_shapes=[pltpu.VMEM((d, n_out), BF16)],
        compiler_params=_cparams("arbitrary", "arbitrary"),
        name="inproj_pe" if with_pe else "inproj_ctx",
    )(*args)


def _rglru_body(*refs, with_gate):
    if with_gate:
        (u_ref, gate_ref, cw_ref, cb_ref, w_ref, bias_ref, sp_ref, h0f_ref, h0b_ref,
         y_ref, hf_end_ref, hb_end_ref, af_ref, bf_ref, ab_ref, bb_ref) = refs
    else:
        (u_ref, cw_ref, cb_ref, w_ref, bias_ref, sp_ref, h0f_ref, h0b_ref,
         hf_end_ref, hb_end_ref, af_ref, bf_ref, ab_ref, bb_ref) = refs
    n, c = u_ref.shape
    u = u_ref[...]
    v = cb_ref[...] + cw_ref[1:2, :] * _shift_rows(u, -1) + cw_ref[0:1, :] * _shift_rows(u, -2)
    v = v + cw_ref[2:3, :] * u + cw_ref[3:4, :] * _shift_rows(u, 1)
    z = _dot(v.astype(BF16), w_ref[...]) + bias_ref[...]
    sub = lax.broadcasted_iota(jnp.int32, (n, c), 0) % SUBLANES

    def local_scan(k, reverse, a_ref, b_ref):
        r = _sigmoid(z[:, (2 * k) * c:(2 * k + 1) * c])
        i = _sigmoid(z[:, (2 * k + 1) * c:(2 * k + 2) * c])
        log_a = (-LRU_C) * r * sp_ref[k:k + 1, :]
        a = jnp.exp(log_a)
        b = jnp.sqrt(1.0 - a * a) * (i * v)
        for s in (1, 2, 4):
            if reverse:
                ok = sub < SUBLANES - s
                a_sh = jnp.where(ok, pltpu.roll(a, n - s, 0), 1.0)
                b_sh = jnp.where(ok, pltpu.roll(b, n - s, 0), 0.0)
            else:
                ok = sub >= s
                a_sh = jnp.where(ok, pltpu.roll(a, s, 0), 1.0)
                b_sh = jnp.where(ok, pltpu.roll(b, s, 0), 0.0)
            b = a * b_sh + b
            a = a * a_sh
        a_ref[...] = a
        b_ref[...] = b

    local_scan(0, False, af_ref, bf_ref)
    local_scan(1, True, ab_ref, bb_ref)

    n_tiles = n // SUBLANES

    def carry_step(q, carry):
        cf, cb = carry
        rf = pl.ds(pl.multiple_of(q * SUBLANES, SUBLANES), SUBLANES)
        rb = pl.ds(pl.multiple_of((n_tiles - 1 - q) * SUBLANES, SUBLANES), SUBLANES)
        hf = af_ref[rf, :] * cf + bf_ref[rf, :]
        hb = ab_ref[rb, :] * cb + bb_ref[rb, :]
        bf_ref[rf, :] = hf
        bb_ref[rb, :] = hb
        return (jnp.broadcast_to(hf[SUBLANES - 1:SUBLANES, :], (SUBLANES, c)),
                jnp.broadcast_to(hb[0:1, :], (SUBLANES, c)))

    cf0 = jnp.broadcast_to(h0f_ref[...], (SUBLANES, c))
    cb0 = jnp.broadcast_to(h0b_ref[...], (SUBLANES, c))
    cf, cb = lax.fori_loop(0, n_tiles, carry_step, (cf0, cb0), unroll=4)
    hf_end_ref[...] = cf[0:1, :]
    hb_end_ref[...] = cb[0:1, :]
    if with_gate:
        y_ref[...] = jax.nn.gelu(gate_ref[...], approximate=True) * (bf_ref[...] + bb_ref[...])


def _rglru(p3, u_blk0, gate_blk0, cw, cb, w_gates, bias, sp, h0f, h0b, with_gate, ch):
    b, l, _ = p3.shape
    d_rnn = cw.shape[1]
    nh = d_rnn // ch
    in_specs = [pl.BlockSpec((None, l, ch), lambda bi, h: (bi, 0, u_blk0 + h))]
    args = [p3]
    if with_gate:
        in_specs.append(pl.BlockSpec((None, l, ch), lambda bi, h: (bi, 0, gate_blk0 + h)))
        args.append(p3)
    in_specs += [pl.BlockSpec((4, ch), lambda bi, h: (0, h)),
                 pl.BlockSpec((1, ch), lambda bi, h: (0, h)),
                 pl.BlockSpec((None, ch, 4 * ch), lambda bi, h: (h, 0, 0)),
                 pl.BlockSpec((None, 1, 4 * ch), lambda bi, h: (h, 0, 0)),
                 pl.BlockSpec((2, ch), lambda bi, h: (0, h)),
                 pl.BlockSpec((None, 1, ch), lambda bi, h: (bi, 0, h)),
                 pl.BlockSpec((None, 1, ch), lambda bi, h: (bi, 0, h))]
    args += [cw, cb, w_gates, bias, sp, h0f, h0b]
    end_spec = pl.BlockSpec((None, 1, ch), lambda bi, h: (bi, 0, h))
    end_shape = jax.ShapeDtypeStruct((b, 1, d_rnn), F32)
    out_specs = [end_spec, end_spec]
    out_shape = [end_shape, end_shape]
    if with_gate:
        out_specs = [pl.BlockSpec((None, l, ch), lambda bi, h: (bi, 0, h))] + out_specs
        out_shape = [jax.ShapeDtypeStruct((b, l, d_rnn), F32)] + out_shape
    return pl.pallas_call(
        functools.partial(_rglru_body, with_gate=with_gate),
        grid=(b, nh),
        in_specs=in_specs,
        out_specs=out_specs,
        out_shape=out_shape,
        scratch_shapes=[pltpu.VMEM((l, ch), F32)] * 4,
        compiler_params=_cparams("arbitrary", "arbitrary"),
        name="rglru_lat" if with_gate else "rglru_ctx",
    )(*args)


CONV_BLOCK = 512


def _filt_body(z_ref, w1_ref, b1_ref, f1_ref, w2_ref, b2_ref, f2_ref, w3f_ref, w3b_ref, b3f_ref, b3b_ref,
               dec_ref, cu_ref, su_ref, sgn_ref, hr_ref, hi_ref):
    n = z_ref.shape[0]
    blk = cu_ref.shape[0]
    nblk = n // blk
    ch = hr_ref.shape[2]
    hdn = jnp.sin(f1_ref[...] * (_dot3(z_ref[...], w1_ref[...]) + b1_ref[...]))
    hdn = jnp.sin(f2_ref[...] * (_dot3(hdn, w2_ref[...]) + b2_ref[...]))
    decay = jnp.exp(-z_ref[:, 0:1] * dec_ref[...])
    kf = (_dot3(hdn, w3f_ref[...]) + b3f_ref[...]) * decay
    kb = (_dot3(hdn, w3b_ref[...]) + b3b_ref[...]) * decay
    row = lax.broadcasted_iota(jnp.int32, (n, ch), 0)
    kb = jnp.where(row == 0, 0.0, kb)
    norm = jnp.sum(jnp.abs(kf) + jnp.abs(kb), axis=0, keepdims=True)
    scale = (2.0 / (2 * blk)) / norm
    sgn = sgn_ref[...]

    p, q, first = [], [], []
    for c in range(nblk):
        taps = jnp.concatenate([kf[c * blk:(c + 1) * blk], kb[c * blk:(c + 1) * blk]], axis=1).astype(BF16)
        p.append(_dot(cu_ref[...], taps))
        q.append(_dot(su_ref[...], taps))
        first.append(taps[0:1, :].astype(F32))
    fwd = lambda a: a[:, :ch]
    bwd = lambda a: a[:, ch:]
    hr_ref[nblk - 1] = (fwd(p[0]) + bwd(p[0])) * scale
    hi_ref[nblk - 1] = (bwd(q[0]) - fwd(q[0])) * scale
    for c in range(1, nblk):
        hr_ref[nblk - 1 + c] = (fwd(p[c]) + sgn * fwd(q[c - 1])) * scale
        hi_ref[nblk - 1 + c] = (sgn * (fwd(p[c - 1]) - fwd(first[c - 1])) - fwd(q[c])) * scale
        hr_ref[nblk - 1 - c] = (bwd(p[c]) + sgn * bwd(q[c - 1])) * scale
        hi_ref[nblk - 1 - c] = (bwd(q[c]) - sgn * (bwd(p[c - 1]) - bwd(first[c - 1]))) * scale


def _filters(z, w1, b1, f1, w2, b2, f2, w3, b3, deltas, cu, su, sgn, d_h, order, ch):
    n, fe = z.shape
    hid = w2.shape[0]
    blk = cu.shape[0]
    nh = 2 * (n // blk) - 1
    per_order = d_h // ch
    ncol = order * per_order
    const = lambda shape: pl.BlockSpec(shape, lambda g: (0, 0))
    hspec = pl.BlockSpec((nh, blk, ch), lambda g: (0, 0, g))
    return pl.pallas_call(
        _filt_body,
        grid=(ncol,),
        in_specs=[const((n, fe)), const((fe, hid)), const((1, hid)), const((1, hid)),
                  const((hid, hid)), const((1, hid)), const((1, hid)),
                  pl.BlockSpec((hid, ch), lambda g: (0, g)),
                  pl.BlockSpec((hid, ch), lambda g: (0, ncol + g)),
                  pl.BlockSpec((1, ch), lambda g: (0, g)),
                  pl.BlockSpec((1, ch), lambda g: (0, ncol + g)),
                  pl.BlockSpec((1, ch), lambda g: (0, g % per_order)),
                  const((blk, blk)), const((blk, blk)), const((blk, 1))],
        out_specs=[hspec, hspec],
        out_shape=[jax.ShapeDtypeStruct((nh, blk, order * d_h), F32)] * 2,
        compiler_params=_cparams("arbitrary"),
        name="hyena_filters",
    )(z, w1, b1, f1, w2, b2, f2, w3, w3, b3, b3, deltas, cu, su, sgn)


def _hyena_body(v_ref, x1_ref, x2_ref, wv_ref, w1_ref, w2_ref, fb_ref, c3_ref, s3_ref,
                h0r_ref, h0i_ref, h1r_ref, h1i_ref, o_ref, z_ref, zb_ref, p_ref, q_ref, yr_ref, yi_ref):
    n = o_ref.shape[0]
    blk = c3_ref.shape[0]
    nblk = n // blk
    halo = SUBLANES
    sub = 32

    def conv3(src_ref, w_ref, r0):
        lo, hi = max(r0 - halo, 0), min(r0 + blk + halo, n)
        win = src_ref[lo:hi, :]
        size = hi - lo
        row = lax.broadcasted_iota(jnp.int32, win.shape, 0)
        prev = pltpu.roll(win, 1, 0)
        nxt = pltpu.roll(win, size - 1, 0)
        if lo == 0:
            prev = jnp.where(row == 0, 0.0, prev)
        if hi == n:
            nxt = jnp.where(row == size - 1, 0.0, nxt)
        out = w_ref[0:1, :] * prev + w_ref[1:2, :] * win + w_ref[2:3, :] * nxt
        return out[r0 - lo:r0 - lo + blk, :]

    def long_conv(hr_ref, hi_ref, fb, gate_ref, gate_w_ref, dst_ref):
        for i in range(nblk):
            rows = slice(i * blk, (i + 1) * blk)
            p_ref[rows, :] = _dot(c3_ref[...], zb_ref[rows, :])
            q_ref[rows, :] = _dot(s3_ref[...], zb_ref[rows, :])

        def products(rc, carry):
            r0 = pl.multiple_of(rc * sub, sub)
            for j in range(nblk):
                yr = yi = None
                for i in range(nblk):
                    rows = pl.ds(i * blk + r0, sub)
                    p, q = p_ref[rows, :], q_ref[rows, :]
                    gr = hr_ref[nblk - 1 + j - i, pl.ds(r0, sub), :]
                    gi = hi_ref[nblk - 1 + j - i, pl.ds(r0, sub), :]
                    tr, ti = p * gr + q * gi, q * gr - p * gi
                    yr, yi = (tr, ti) if yr is None else (yr + tr, yi + ti)
                yr_ref[pl.ds(j * blk + r0, sub), :] = yr.astype(BF16)
                yi_ref[pl.ds(j * blk + r0, sub), :] = yi.astype(BF16)
            return carry

        lax.fori_loop(0, blk // sub, products, 0)
        for j in range(nblk):
            rows = slice(j * blk, (j + 1) * blk)
            y = _dot(c3_ref[...], yr_ref[rows, :]) + _dot(s3_ref[...], yi_ref[rows, :])
            dst_ref[rows, :] = conv3(gate_ref, gate_w_ref, j * blk) * (y + z_ref[rows, :] * fb)

    for r0 in range(0, n, blk):
        z_ref[r0:r0 + blk, :] = conv3(v_ref, wv_ref, r0)
    zb_ref[...] = z_ref[...].astype(BF16)
    long_conv(h0r_ref, h0i_ref, fb_ref[0:1, :], x1_ref, w1_ref, z_ref)
    zb_ref[...] = z_ref[...].astype(BF16)
    long_conv(h1r_ref, h1i_ref, fb_ref[1:2, :], x2_ref, w2_ref, o_ref)


def _hyena(p3, col_blk0, conv_w, fbias, c3, s3, hr, hi, d_h, ch):
    b, n, _ = p3.shape
    nh, blk, _ = hr.shape
    per = d_h // ch
    zspec = lambda k: pl.BlockSpec((None, n, ch), lambda h, bi: (bi, 0, col_blk0 + k * per + h))
    wspec = lambda k: pl.BlockSpec((3, ch), lambda h, bi: (0, k * per + h))
    hspec = lambda o: pl.BlockSpec((nh, blk, ch), lambda h, bi: (0, 0, o * per + h), pipeline_mode=pl.Buffered(1))
    const = pl.BlockSpec((blk, blk), lambda h, bi: (0, 0))
    return pl.pallas_call(
        _hyena_body,
        grid=(per, b),
        in_specs=[zspec(0), zspec(1), zspec(2), wspec(0), wspec(1), wspec(2),
                  pl.BlockSpec((2, ch), lambda h, bi: (0, h)), const, const,
                  hspec(0), hspec(0), hspec(1), hspec(1)],
        out_specs=pl.BlockSpec((None, n, ch), lambda h, bi: (bi, 0, h)),
        out_shape=jax.ShapeDtypeStruct((b, n, d_h), F32),
        scratch_shapes=[pltpu.VMEM((n, ch), F32), pltpu.VMEM((n, ch), BF16),
                        pltpu.VMEM((n, ch), F32), pltpu.VMEM((n, ch), F32),
                        pltpu.VMEM((n, ch), BF16), pltpu.VMEM((n, ch), BF16)],
        compiler_params=_cparams("arbitrary", "arbitrary"),
        name="hyena_mix",
    )(p3, p3, p3, conv_w, conv_w, conv_w, fbias, c3, s3, hr, hi, hr, hi)


def _merge_body(ya_ref, yb_ref, x_ref, pe_ref, mod_ref, ga_ref, gb_ref, wo_ref, g2_ref, wr_ref, br_ref,
                x1_ref, h2_ref, rt_ref, cnt_ref, wob_ref):
    first = (pl.program_id(0) == 0) & (pl.program_id(1) == 0)

    @pl.when(first)
    def _():
        wob_ref[...] = wo_ref[...].astype(BF16)

    da = ya_ref.shape[1]
    na = _rms(ya_ref[...], ga_ref[...]).astype(BF16)
    nb = _rms(yb_ref[...], gb_ref[...]).astype(BF16)
    y = _dot(na, wob_ref[0:da, :]) + _dot(nb, wob_ref[da:, :])
    x1 = x_ref[...] + pe_ref[...] + mod_ref[2:3, :] * y
    x1_ref[...] = x1
    h2 = _rms(x1, g2_ref[...]) * (1.0 + mod_ref[4:5, :]) + mod_ref[3:4, :]
    h2_ref[...] = h2.astype(BF16)

    logits = _dot3(h2, wr_ref[...]) + br_ref[...]
    lane = lax.broadcasted_iota(jnp.int32, logits.shape, 1).astype(F32)
    neg = -jnp.inf
    big = jnp.float32(1 << 20)
    gl = jnp.where((lane >= N_EXPERTS) & (lane < N_EXPERTS + N_GROUPS), logits, neg)
    gmax = jnp.max(gl, axis=1, keepdims=True)
    g_p = 1.0 / jnp.sum(jnp.exp(gl - gmax), axis=1, keepdims=True)
    g_i = jnp.min(jnp.where(gl == gmax, lane, big), axis=1, keepdims=True) - N_EXPERTS
    lo = g_i * EXPERTS_PER_GROUP
    el = jnp.where((lane >= lo) & (lane < lo + EXPERTS_PER_GROUP), logits, neg)
    m1 = jnp.max(el, axis=1, keepdims=True)
    i1 = jnp.min(jnp.where(el == m1, lane, big), axis=1, keepdims=True)
    el2 = jnp.where(lane == i1, neg, el)
    m2 = jnp.max(el2, axis=1, keepdims=True)
    i2 = jnp.min(jnp.where(el2 == m2, lane, big), axis=1, keepdims=True)
    e2 = jnp.exp(m2 - m1)
    w1 = g_p / (1.0 + e2)
    w2 = g_p * e2 / (1.0 + e2)
    rt_ref[...] = (jnp.where(lane == 0.0, i1, 0.0) + jnp.where(lane == 1.0, i2, 0.0)
                   + jnp.where(lane == 2.0, w1, 0.0) + jnp.where(lane == 3.0, w2, 0.0))
    picked = jnp.where((lane == i1) | (lane == i2), 1.0, 0.0)
    nsub = cnt_ref.shape[0]
    sub = picked.shape[0] // nsub
    for k in range(nsub):
        cnt_ref[k] = jnp.broadcast_to(jnp.sum(picked[k * sub:(k + 1) * sub], axis=0, keepdims=True), cnt_ref.shape[1:])


def _merge(ya, yb, x3, pe, mod3, ga, gb, w_out, g2n, w_r, b_r, tm, nsub):
    b, n, d = x3.shape
    da, db = ya.shape[2], yb.shape[2]
    tok = lambda w: pl.BlockSpec((None, tm, w), lambda bi, i: (bi, i, 0))
    const = lambda shape: pl.BlockSpec(shape, lambda bi, i: (0, 0))
    return pl.pallas_call(
        _merge_body,
        grid=(b, n // tm),
        in_specs=[tok(da), tok(db), tok(d), pl.BlockSpec((tm, d), lambda bi, i: (i, 0)),
                  pl.BlockSpec((None, N_MOD, d), lambda bi, i: (bi, 0, 0)),
                  const((1, da)), const((1, db)), const((da + db, d)), const((1, d)),
                  const((d, LANES)), const((1, LANES))],
        out_specs=[tok(d), tok(d), tok(LANES),
                   pl.BlockSpec((nsub, SUBLANES, LANES), lambda bi, i: (bi * (n // tm) + i, 0, 0))],
        out_shape=[jax.ShapeDtypeStruct((b, n, d), F32), jax.ShapeDtypeStruct((b, n, d), BF16),
                   jax.ShapeDtypeStruct((b, n, LANES), F32),
                   jax.ShapeDtypeStruct((b * (n // tm) * nsub, SUBLANES, LANES), F32)],
        scratch_shapes=[pltpu.VMEM((da + db, d), BF16)],
        compiler_params=_cparams("arbitrary", "arbitrary"),
        name="merge_route",
    )(ya, yb, x3, pe, mod3, ga.reshape(1, da), gb.reshape(1, db), w_out, g2n.reshape(1, d), w_r, b_r)


SEG = SUBLANES


def _local_slots(rt_ref, lstart_ref):
    tb = rt_ref.shape[0]
    rt = rt_ref[...]
    lane = lax.broadcasted_iota(jnp.int32, rt.shape, 1).astype(F32)
    col = lambda k: jnp.sum(jnp.where(lane == float(k), rt, 0.0), axis=1, keepdims=True)
    i1, i2, w1, w2 = col(0), col(1), col(2), col(3)
    o1 = jnp.where(lane == i1, 1.0, 0.0)
    o2 = jnp.where(lane == i2, 1.0, 0.0)
    r = lax.broadcasted_iota(jnp.int32, (tb, tb), 0)
    c = lax.broadcasted_iota(jnp.int32, (tb, tb), 1)
    earlier = jnp.where(c < r, 1.0, 0.0).astype(BF16)
    before = _dot(earlier, (o1 + o2).astype(BF16)) + lstart_ref[0:1, :]
    p1 = jnp.sum(o1 * before, axis=1, keepdims=True)
    p2 = jnp.sum(o2 * before, axis=1, keepdims=True)
    return p1, p2, w1, w2


def _segment_copy(local_ref, lrow, global_ref, grow, rows, sem, to_global):
    lrows = pl.ds(pl.multiple_of(lrow, SEG), rows)
    grows = pl.ds(pl.multiple_of(grow, SEG), rows)
    if to_global:
        return pltpu.make_async_copy(local_ref.at[lrows, :], global_ref.at[grows, :], sem)
    return pltpu.make_async_copy(global_ref.at[grows, :], local_ref.at[lrows, :], sem)


def _start_segments(tab_ref, blk, local_ref, global_ref, sem, to_global):
    def per_expert(e, carry):
        k = (blk * N_EXPERTS + e) * 3
        ls, gs, nchunk = tab_ref[k], tab_ref[k + 1], tab_ref[k + 2]

        def per_chunk(ci, carry2):
            _segment_copy(local_ref, ls + ci * SEG, global_ref, gs + ci * SEG, SEG, sem, to_global).start()
            return carry2

        return lax.fori_loop(0, nchunk, per_chunk, carry)

    lax.fori_loop(0, N_EXPERTS, per_expert, 0)


def _wait_segments(rows, local_ref, global_ref, sem, to_global):
    chunks = rows // SEG
    nbits = (local_ref.shape[0] // SEG).bit_length()
    for k in range(nbits):
        @pl.when(((chunks >> k) & 1) == 1)
        def _():
            _segment_copy(local_ref, 0, global_ref, 0, SEG << k, sem, to_global).wait()


def _dispatch_body(tab_ref, ends_ref, ltot_ref, rt_ref, lstart_ref, h2_ref, xs_ref, loc_ref, zero_ref, sem, zsem):
    blk = pl.program_id(0)
    nblk = pl.num_programs(0)
    tb = rt_ref.shape[0]
    lp = loc_ref.shape[1]
    tile = zero_ref.shape[0]
    slot = blk % 2

    @pl.when(blk == 0)
    def _():
        zero_ref[...] = jnp.zeros_like(zero_ref)

        def zero_fill(first_row):
            return pltpu.make_async_copy(zero_ref, xs_ref.at[pl.ds(pl.multiple_of(first_row, SEG), tile), :], zsem)

        def fill(e, start, wait):
            end = ends_ref[e]

            @pl.when(end > start)
            def _():
                cp = zero_fill(end - tile)
                cp.wait() if wait else cp.start()
            return end

        used = lax.fori_loop(0, N_EXPERTS, functools.partial(fill, wait=False), 0)
        spare = (xs_ref.shape[0] - used) // tile

        def fill_spare(j, carry, wait):
            cp = zero_fill(used + j * tile)
            cp.wait() if wait else cp.start()
            return carry

        lax.fori_loop(0, spare, functools.partial(fill_spare, wait=False), 0)
        lax.fori_loop(0, N_EXPERTS, functools.partial(fill, wait=True), 0)
        lax.fori_loop(0, spare, functools.partial(fill_spare, wait=True), 0)

    p1, p2, _, _ = _local_slots(rt_ref, lstart_ref)
    lane = lax.broadcasted_iota(jnp.int32, (tb, LANES), 1)
    hi1, hi2 = jnp.floor(p1 / 64.0), jnp.floor(p2 / 64.0)
    parts = (jnp.where(lane == 0, hi1, 0.0) + jnp.where(lane == 1, p1 - 64.0 * hi1, 0.0)
             + jnp.where(lane == 2, hi2, 0.0) + jnp.where(lane == 3, p2 - 64.0 * hi2, 0.0))
    sel_r = lax.broadcasted_iota(jnp.int32, (SUBLANES, LANES), 0)
    sel_c = lax.broadcasted_iota(jnp.int32, (SUBLANES, LANES), 1)
    sel = jnp.where(sel_r == sel_c, 1.0, 0.0).astype(BF16)
    rows = lax.dot_general(sel, parts.astype(BF16), (((1,), (1,)), ((), ())), preferred_element_type=F32)
    q1 = rows[0:1, :] * 64.0 + rows[1:2, :]
    q2 = rows[2:3, :] * 64.0 + rows[3:4, :]
    pos = lax.broadcasted_iota(jnp.int32, (lp, tb), 0).astype(F32)
    onehot = jnp.where((pos == q1) | (pos == q2), 1.0, 0.0).astype(BF16)
    loc_ref[slot] = _dot(onehot, h2_ref[...])

    _start_segments(tab_ref, blk, loc_ref.at[slot], xs_ref, sem.at[slot], True)

    @pl.when(blk > 0)
    def _():
        _wait_segments(ltot_ref[blk - 1], loc_ref.at[1 - slot], xs_ref, sem.at[1 - slot], True)

    @pl.when(blk == nblk - 1)
    def _():
        _wait_segments(ltot_ref[blk], loc_ref.at[slot], xs_ref, sem.at[slot], True)


def _dispatch(tab, ends, ltot, rt2, lstart, h2, slots, tb, lp, tile):
    t, d = h2.shape
    return pl.pallas_call(
        _dispatch_body,
        grid_spec=pltpu.PrefetchScalarGridSpec(
            num_scalar_prefetch=3,
            grid=(t // tb,),
            in_specs=[pl.BlockSpec((tb, LANES), lambda i, *_: (i, 0)),
                      pl.BlockSpec((None, SUBLANES, LANES), lambda i, *_: (i, 0, 0)),
                      pl.BlockSpec((tb, d), lambda i, *_: (i, 0))],
            out_specs=pl.BlockSpec(memory_space=pl.ANY),
            scratch_shapes=[pltpu.VMEM((2, lp, d), F32), pltpu.VMEM((tile, d), F32),
                            pltpu.SemaphoreType.DMA((2,)), pltpu.SemaphoreType.DMA]),
        out_shape=jax.ShapeDtypeStruct((slots, d), F32),
        compiler_params=_cparams("arbitrary"),
        name="moe_dispatch",
    )(tab, ends, ltot, rt2, lstart, h2)


def _ffn_body(te_ref, nu_ref, nxt_ref, par_ref, xs_ref, wg_hbm, wu_hbm, wd_hbm, ys_ref,
              wg_buf, wu_buf, wd_buf, wgb_ref, wub_ref, wdb_ref, sem):
    i = pl.program_id(0)
    e = te_ref[i]
    first = (i == 0) | (e != te_ref[jnp.maximum(i - 1, 0)])
    active = i < nu_ref[0]

    def fetch(expert, slot):
        return [pltpu.make_async_copy(w.at[expert], buf.at[slot], sem.at[slot])
                for w, buf in ((wg_hbm, wg_buf), (wu_hbm, wu_buf), (wd_hbm, wd_buf))]

    @pl.when(i == 0)
    def _():
        for cp in fetch(e, par_ref[e]):
            cp.start()

    @pl.when(first & active)
    def _():
        slot = par_ref[e]
        for cp in fetch(e, slot):
            cp.wait()
        nxt = nxt_ref[e]

        @pl.when(nxt >= 0)
        def _():
            for cp in fetch(nxt, 1 - slot):
                cp.start()

        wgb_ref[...] = wg_buf[slot].astype(BF16)
        wub_ref[...] = wu_buf[slot].astype(BF16)
        wdb_ref[...] = wd_buf[slot].astype(BF16)

    @pl.when(active)
    def _():
        xb = xs_ref[...].astype(BF16)
        act = _dot(xb, wgb_ref[...])
        act = act * _sigmoid(act) * _dot(xb, wub_ref[...])
        ys_ref[...] = _dot(act.astype(BF16), wdb_ref[...])


def _ffn(tile_expert, n_used, nxt, par, xs, w_gate, w_up, w_down, tile):
    slots, d = xs.shape
    _, _, de = w_gate.shape
    last = lambda i, te, nu, *_: (jnp.minimum(i, nu[0] - 1), 0)
    hbm = pl.BlockSpec(memory_space=pl.ANY)
    return pl.pallas_call(
        _ffn_body,
        grid_spec=pltpu.PrefetchScalarGridSpec(
            num_scalar_prefetch=4,
            grid=(slots // tile,),
            in_specs=[pl.BlockSpec((tile, d), last), hbm, hbm, hbm],
            out_specs=pl.BlockSpec((tile, d), last),
            scratch_shapes=[pltpu.VMEM((2, d, de), F32), pltpu.VMEM((2, d, de), F32), pltpu.VMEM((2, de, d), F32),
                            pltpu.VMEM((d, de), BF16), pltpu.VMEM((d, de), BF16), pltpu.VMEM((de, d), BF16),
                            pltpu.SemaphoreType.DMA((2,))]),
        out_shape=jax.ShapeDtypeStruct((slots, d), F32),
        input_output_aliases={4: 0},
        compiler_params=_cparams("arbitrary"),
        name="moe_ffn",
    )(tile_expert, n_used, nxt, par, xs, w_gate, w_up, w_down)


def _combine_body(tab_ref, ltot_ref, ys_ref, rt_ref, lstart_ref, x1_ref, mod_ref, fg_ref, o_ref, loc_ref, sem):
    blk = pl.program_id(0)
    nblk = pl.num_programs(0)
    _, lp, d = loc_ref.shape
    slot = blk % 2

    @pl.when(blk == 0)
    def _():
        _start_segments(tab_ref, blk, loc_ref.at[slot], ys_ref, sem.at[slot], False)

    @pl.when(blk + 1 < nblk)
    def _():
        _start_segments(tab_ref, blk + 1, loc_ref.at[1 - slot], ys_ref, sem.at[1 - slot], False)

    p1, p2, w1, w2 = _local_slots(rt_ref, lstart_ref)
    pos = lax.broadcasted_iota(jnp.int32, (rt_ref.shape[0], lp), 1).astype(F32)
    weights = (jnp.where(pos == p1, w1, 0.0) + jnp.where(pos == p2, w2, 0.0)).astype(BF16)
    _wait_segments(ltot_ref[blk], loc_ref.at[slot], ys_ref, sem.at[slot], False)

    def clear(ci, carry):
        loc_ref[slot, pl.ds(pl.multiple_of(ci * SEG, SEG), SEG), :] = jnp.zeros((SEG, d), F32)
        return carry

    lax.fori_loop(ltot_ref[blk] // SEG, lp // SEG, clear, 0)
    moe = _dot(weights, loc_ref[slot].astype(BF16))
    o_ref[...] = _rms(x1_ref[...] + mod_ref[5:6, :] * moe, fg_ref[...])


def _combine(tab, ltot, ys, rt2, lstart, x1, mod3, final_g, tb, lp, per_batch):
    t, d = x1.shape
    return pl.pallas_call(
        _combine_body,
        grid_spec=pltpu.PrefetchScalarGridSpec(
            num_scalar_prefetch=2,
            grid=(t // tb,),
            in_specs=[pl.BlockSpec(memory_space=pl.ANY),
                      pl.BlockSpec((tb, LANES), lambda i, *_: (i, 0)),
                      pl.BlockSpec((None, SUBLANES, LANES), lambda i, *_: (i, 0, 0)),
                      pl.BlockSpec((tb, d), lambda i, *_: (i, 0)),
                      pl.BlockSpec((None, N_MOD, d), lambda i, *_: (i // per_batch, 0, 0)),
                      pl.BlockSpec((1, d), lambda i, *_: (0, 0))],
            out_specs=pl.BlockSpec((tb, d), lambda i, *_: (i, 0)),
            scratch_shapes=[pltpu.VMEM((2, lp, d), F32), pltpu.SemaphoreType.DMA((2,))]),
        out_shape=jax.ShapeDtypeStruct((t, d), F32),
        compiler_params=_cparams("arbitrary"),
        name="moe_combine",
    )(tab, ltot, ys, rt2, lstart, x1, mod3, final_g.reshape(1, d))


def _moe_layout(cnt_blocks, tile, n_tiles):
    nb = cnt_blocks.shape[0]
    lcnt = (cnt_blocks + SEG - 1) // SEG * SEG
    lstart = jnp.cumsum(lcnt, axis=1) - lcnt
    ltot = jnp.sum(lcnt, axis=1)
    per_expert = jnp.sum(lcnt, axis=0)
    tiles_per = (per_expert + tile - 1) // tile
    tile_ends = jnp.cumsum(tiles_per)
    row_ends = tile_ends * tile
    goff = row_ends - tiles_per * tile
    gstart = goff[None, :] + jnp.cumsum(lcnt, axis=0) - lcnt
    tab = jnp.stack([lstart, gstart, lcnt // SEG], axis=-1).reshape(-1).astype(jnp.int32)
    n_used = tile_ends[-1:]
    tile_ids = jnp.arange(n_tiles, dtype=jnp.int32)
    tile_expert = jnp.sum((tile_ends[None, :] <= jnp.minimum(tile_ids, n_used - 1)[:, None]).astype(jnp.int32), axis=1)
    ids = jnp.arange(N_EXPERTS, dtype=jnp.int32)
    used = tiles_per > 0
    par = (jnp.cumsum(used) - used) % 2
    later = jnp.where(used[None, :] & (ids[None, :] > ids[:, None]), ids[None, :], N_EXPERTS)
    nxt = jnp.min(later, axis=1)
    nxt = jnp.where(nxt == N_EXPERTS, -1, nxt)
    lstart_rows = jnp.broadcast_to(jnp.pad(lstart.astype(F32), ((0, 0), (0, LANES - N_EXPERTS)))[:, None, :],
                                   (nb, SUBLANES, LANES))
    return (tab, row_ends.astype(jnp.int32), ltot.astype(jnp.int32), tile_expert.astype(jnp.int32),
            n_used.astype(jnp.int32), nxt.astype(jnp.int32), par.astype(jnp.int32), lstart_rows)


def _sincos_table(rows, cols, dim):
    quarter = dim // 4
    omega = 1.0 / (10000.0 ** (np.arange(quarter, dtype=np.float64) / quarter))
    ang_r = np.arange(rows, dtype=np.float64)[:, None] * omega
    ang_c = np.arange(cols, dtype=np.float64)[:, None] * omega
    emb_r = np.concatenate([np.sin(ang_r), np.cos(ang_r)], axis=-1)
    emb_c = np.concatenate([np.sin(ang_c), np.cos(ang_c)], axis=-1)
    pe = np.concatenate([np.broadcast_to(emb_r[:, None, :], (rows, cols, 2 * quarter)),
                         np.broadcast_to(emb_c[None, :, :], (rows, cols, 2 * quarter))], axis=-1)
    return pe.reshape(rows * cols, 4 * quarter).astype(np.float32)


def _filter_features(n, width):
    pos = np.arange(n, dtype=np.float64)
    t = pos / max(n - 1, 1)
    ang = (2.0 * math.pi * pos / n)[:, None] * np.linspace(1e-4, FILTER_BANDS - 1, FILTER_BANDS)[None, :]
    z = np.concatenate([t[:, None], np.cos(ang), -np.sin(ang)], axis=-1)
    return np.pad(z, ((0, 0), (0, width - z.shape[1]))).astype(np.float32)


def _decay_rates(d_h):
    min_decay = math.log(DECAY_TARGET) / DECAY_SLOW_PCT
    max_decay = math.log(DECAY_TARGET) / DECAY_FAST_PCT
    return np.abs(np.linspace(min_decay, max_decay, d_h)).astype(np.float32).reshape(1, d_h)


def _dft_tables(blk):
    odd = 2 * np.arange(blk, dtype=np.int64) + 1
    shifted = ((odd[:, None] * odd[None, :]) % (8 * blk)).astype(np.float64) * (2.0 * math.pi / (8 * blk))
    plain = ((odd[:, None] * np.arange(blk, dtype=np.int64)[None, :]) % (4 * blk)).astype(np.float64) * (2.0 * math.pi / (4 * blk))
    sgn = (1.0 - 2.0 * (np.arange(blk) % 2)).astype(np.float32).reshape(blk, 1)
    f32 = lambda a: a.astype(np.float32)
    return f32(np.cos(shifted)), f32(np.sin(shifted)), f32(np.cos(plain)), f32(np.sin(plain)), sgn


def kernel(x, c, ctx, c_ctx, w_ada, b_ada, norm1_g, w_in, conv_a_w, conv_a_b, lru_wa, lru_ba, lru_wx, lru_bx, lru_lambda, conv_b_w, filt_w1, filt_b1, filt_freq1, filt_w2, filt_b2, filt_freq2, filt_w3, filt_b3, filt_bias, out_norm_a, out_norm_b, w_out, norm2_g, w_rg, b_rg, w_re, b_re, w_gate, w_up, w_down, final_g):
    bsz, n, d = x.shape
    n_ctx = ctx.shape[1]
    d_rnn = conv_a_w.shape[2]
    d_h = filt_bias.shape[2]
    order = filt_bias.shape[1]
    assert w_ada.shape[0] == 1, "single-layer block"
    l = 0
    ch = 256

    mod_rows = 16
    cc = jnp.concatenate([c, c_ctx[None, :], jnp.zeros((mod_rows - bsz - 1, d), F32)], axis=0)
    mod3 = _ada(cc, w_ada[l], b_ada[l]).reshape(mod_rows, N_MOD, d)

    pe = jnp.asarray(_sincos_table(n // GRID_W, GRID_W, d))
    d_in = w_in.shape[2]
    p3 = _inproj(x, pe, mod3, lambda bi: bi, norm1_g[l], w_in[l], 0, d_in, 512)
    pc3 = _inproj(ctx, None, mod3, lambda bi: bsz, norm1_g[l], w_in[l], 1, d_rnn, n_ctx)

    heads_per_blk = ch // (d_rnn // RNN_HEADS)
    nblk = d_rnn // ch
    eye = jnp.eye(heads_per_blk, dtype=F32)

    def blockdiag(w):
        w5 = w.reshape(2, nblk, heads_per_blk, w.shape[2], w.shape[3])
        return jnp.einsum("dnkij,kl->dnkilj", w5, eye).reshape(2, nblk, ch, ch)

    wa_bd, wx_bd = blockdiag(lru_wa[l]), blockdiag(lru_wx[l])
    w_gates = jnp.concatenate([wa_bd[0], wx_bd[0], wa_bd[1], wx_bd[1]], axis=-1).astype(BF16)
    ba, bx = lru_ba[l].reshape(2, nblk, ch), lru_bx[l].reshape(2, nblk, ch)
    bias = jnp.concatenate([ba[0], bx[0], ba[1], bx[1]], axis=-1).reshape(nblk, 1, 4 * ch)
    sp = jax.nn.softplus(-lru_lambda[l])
    cb = conv_a_b[l].reshape(1, d_rnn)
    zeros_state = jnp.zeros((bsz, 1, d_rnn), F32)
    hf_ctx, hb_ctx = _rglru(pc3, 0, 0, conv_a_w[l], cb, w_gates, bias, sp, zeros_state, zeros_state, False, ch)
    ya, _, _ = _rglru(p3, d_rnn // ch, 0, conv_a_w[l], cb, w_gates, bias, sp, hf_ctx, hb_ctx, True, ch)

    as_bf16 = lambda a: jnp.asarray(a).astype(BF16)
    c3, s3, cu, su, sgn = _dft_tables(CONV_BLOCK)
    fe = 64
    z = jnp.asarray(_filter_features(n, fe))
    w1 = jnp.pad(filt_w1[l], ((0, fe - filt_w1.shape[1]), (0, 0)))
    hid = filt_w2.shape[1]
    hr, hi = _filters(z, w1, filt_b1[l].reshape(1, hid), filt_freq1[l].reshape(1, hid), filt_w2[l],
                      filt_b2[l].reshape(1, hid), filt_freq2[l].reshape(1, hid), filt_w3[l],
                      filt_b3[l].reshape(1, -1), jnp.asarray(_decay_rates(d_h)), as_bf16(cu), as_bf16(su),
                      jnp.asarray(sgn), d_h, order, ch)
    yb = _hyena(p3, 2 * d_rnn // ch, conv_b_w[l], filt_bias[l], as_bf16(c3), as_bf16(s3), hr, hi, d_h, ch)

    w_r = jnp.concatenate([w_re[l], w_rg[l], jnp.zeros((d, LANES - N_EXPERTS - N_GROUPS), F32)], axis=1)
    b_r = jnp.concatenate([b_re[l], b_rg[l], jnp.zeros((LANES - N_EXPERTS - N_GROUPS,), F32)]).reshape(1, LANES)
    tb = 256
    tm = 512
    x1, h2, rt, cnt = _merge(ya, yb, x, pe, mod3, out_norm_a[l], out_norm_b[l], w_out[l], norm2_g[l], w_r, b_r,
                             tm, tm // tb)

    tile = 512
    t_all = bsz * n
    nb = t_all // tb
    lp = 2 * tb + N_EXPERTS * SEG
    n_tiles = (2 * t_all + nb * N_EXPERTS * (SEG - 1) + N_EXPERTS * (tile - 1)) // tile
    cnt_blocks = cnt[:, 0, :N_EXPERTS].astype(jnp.int32)
    tab, row_ends, ltot, tile_expert, n_used, nxt, par, lstart_rows = _moe_layout(cnt_blocks, tile, n_tiles)
    rt2 = rt.reshape(t_all, LANES)
    xs = _dispatch(tab, row_ends, ltot, rt2, lstart_rows, h2.reshape(t_all, d), n_tiles * tile, tb, lp, tile)
    ys = _ffn(tile_expert, n_used, nxt, par, xs, w_gate[l], w_up[l], w_down[l], tile)
    out = _combine(tab, ltot, ys, rt2, lstart_rows, x1.reshape(t_all, d), mod3, final_g, tb, lp, n // tb)
    return out.reshape(bsz, n, d)
```

```python
import functools
import math

import numpy as np
import jax
import jax.numpy as jnp
from jax import lax
from jax.experimental import pallas as pl
from jax.experimental.pallas import tpu as pltpu

F32 = jnp.float32
BF16 = jnp.bfloat16
EPS = 1e-6
LRU_C = 8.0
N_MOD = 6
GRID_W = 64
RNN_HEADS = 8
N_GROUPS = 4
EXPERTS_PER_GROUP = 8
N_EXPERTS = N_GROUPS * EXPERTS_PER_GROUP
FILTER_BANDS = 16
DECAY_FAST_PCT = 0.3
DECAY_SLOW_PCT = 1.5
DECAY_TARGET = 1e-2
SUBLANES = 8
LANES = 128
SEG = SUBLANES
VMEM_LIMIT = 60 * 1024 * 1024


def _cparams(*sem):
    return pltpu.CompilerParams(dimension_semantics=sem, vmem_limit_bytes=VMEM_LIMIT)


def _dot(a, b):
    return jnp.dot(a, b, preferred_element_type=F32)


def _split(a):
    hi = a.astype(BF16)
    return hi, (a - hi.astype(F32)).astype(BF16)


def _dot3(a, b):
    a_hi, a_lo = _split(a)
    b_hi, b_lo = _split(b)
    return _dot(a_hi, b_hi) + _dot(a_lo, b_hi) + _dot(a_hi, b_lo)


def _rms(v, g):
    return v * lax.rsqrt(jnp.mean(v * v, axis=-1, keepdims=True) + EPS) * g


def _sigmoid(z):
    return 1.0 / (1.0 + jnp.exp(-z))


def _shift_rows(v, d):
    n = v.shape[0]
    rolled = pltpu.roll(v, (-d) % n, 0)
    row = lax.broadcasted_iota(jnp.int32, v.shape, 0)
    ok = (row + d >= 0) & (row + d < n)
    return jnp.where(ok, rolled, 0.0)


def _tile_rot(v, s):
    n, c = v.shape
    return pltpu.roll(v.reshape(n // SUBLANES, SUBLANES, c), s, 1).reshape(n, c)


def _ada_body(c_ref, w_ref, b_ref, o_ref):
    c = c_ref[...]
    o_ref[...] = _dot3(c * _sigmoid(c), w_ref[...]) + b_ref[...]


def _ada(cc, w_ada, b_ada):
    rows, d = cc.shape
    n = w_ada.shape[1]
    tn = 1024
    return pl.pallas_call(
        _ada_body,
        grid=(n // tn,),
        in_specs=[pl.BlockSpec((rows, d), lambda j: (0, 0)),
                  pl.BlockSpec((d, tn), lambda j: (0, j)),
                  pl.BlockSpec((1, tn), lambda j: (0, j))],
        out_specs=pl.BlockSpec((rows, tn), lambda j: (0, j)),
        out_shape=jax.ShapeDtypeStruct((rows, n), F32),
        compiler_params=_cparams("arbitrary"),
        name="ada",
    )(cc, w_ada, b_ada.reshape(1, n))


def _inproj_body(*refs, with_pe):
    if with_pe:
        x_ref, pe_ref, mod_ref, g_ref, w_ref, o_ref, wb_ref = refs
    else:
        x_ref, mod_ref, g_ref, w_ref, o_ref, wb_ref = refs

    @pl.when((pl.program_id(0) == 0) & (pl.program_id(1) == 0))
    def _():
        wb_ref[...] = w_ref[...].astype(BF16)

    x = x_ref[...]
    if with_pe:
        x = x + pe_ref[...]
    h = _rms(x, g_ref[...]) * (1.0 + mod_ref[1:2, :]) + mod_ref[0:1, :]
    o_ref[...] = _dot(h.astype(BF16), wb_ref[...])


def _inproj(x3, pe, mod3, mod_row, g, w, col_block, n_out, tm):
    b, l, d = x3.shape
    with_pe = pe is not None
    in_specs = [pl.BlockSpec((None, tm, d), lambda bi, i: (bi, i, 0))]
    args = [x3]
    if with_pe:
        in_specs.append(pl.BlockSpec((tm, d), lambda bi, i: (i, 0)))
        args.append(pe)
    in_specs += [pl.BlockSpec((None, N_MOD, d), lambda bi, i: (mod_row(bi), 0, 0)),
                 pl.BlockSpec((1, d), lambda bi, i: (0, 0)),
                 pl.BlockSpec((d, n_out), lambda bi, i: (0, col_block), pipeline_mode=pl.Buffered(1))]
    args += [mod3, g.reshape(1, d), w]
    return pl.pallas_call(
        functools.partial(_inproj_body, with_pe=with_pe),
        grid=(b, l // tm),
        in_specs=in_specs,
        out_specs=pl.BlockSpec((None, tm, n_out), lambda bi, i: (bi, i, 0)),
        out_shape=jax.ShapeDtypeStruct((b, l, n_out), F32),
        scratch_shapes=[pltpu.VMEM((d, n_out), BF16)],
        compiler_params=_cparams("arbitrary", "arbitrary"),
        name="inproj_pe" if with_pe else "inproj_ctx",
    )(*args)


def _rglru_body(*refs, with_gate):
    if with_gate:
        (u_ref, gate_ref, cw_ref, cb_ref, w_ref, bias_ref, sp_ref, h0f_ref, h0b_ref,
         y_ref, hf_end_ref, hb_end_ref, af_ref, bf_ref, ab_ref, bb_ref) = refs
    else:
        (u_ref, cw_ref, cb_ref, w_ref, bias_ref, sp_ref, h0f_ref, h0b_ref,
         hf_end_ref, hb_end_ref, af_ref, bf_ref, ab_ref, bb_ref) = refs
    n, c = u_ref.shape
    u = u_ref[...]
    v = cb_ref[...] + cw_ref[1:2, :] * _shift_rows(u, -1) + cw_ref[0:1, :] * _shift_rows(u, -2)
    v = v + cw_ref[2:3, :] * u + cw_ref[3:4, :] * _shift_rows(u, 1)
    z = _dot(v.astype(BF16), w_ref[...]) + bias_ref[...]
    sub = lax.broadcasted_iota(jnp.int32, (n, c), 0) % SUBLANES

    def local_scan(k, reverse, a_ref, b_ref):
        r = _sigmoid(z[:, (2 * k) * c:(2 * k + 1) * c])
        i = _sigmoid(z[:, (2 * k + 1) * c:(2 * k + 2) * c])
        log_a = (-LRU_C) * r * sp_ref[k:k + 1, :]
        a = jnp.exp(log_a)
        b = jnp.sqrt(1.0 - a * a) * (i * v)
        for s in (1, 2, 4):
            ok = (sub < SUBLANES - s) if reverse else (sub >= s)
            shift = SUBLANES - s if reverse else s
            b = a * jnp.where(ok, _tile_rot(b, shift), 0.0) + b
            a = a * jnp.where(ok, _tile_rot(a, shift), 1.0)
        a_ref[...] = a
        b_ref[...] = b

    local_scan(0, False, af_ref, bf_ref)
    local_scan(1, True, ab_ref, bb_ref)

    n_tiles = n // SUBLANES

    def carry_step(q, carry):
        cf, cb = carry
        rf = pl.ds(pl.multiple_of(q * SUBLANES, SUBLANES), SUBLANES)
        rb = pl.ds(pl.multiple_of((n_tiles - 1 - q) * SUBLANES, SUBLANES), SUBLANES)
        hf = af_ref[rf, :] * cf + bf_ref[rf, :]
        hb = ab_ref[rb, :] * cb + bb_ref[rb, :]
        bf_ref[rf, :] = hf
        bb_ref[rb, :] = hb
        return (jnp.broadcast_to(hf[SUBLANES - 1:SUBLANES, :], (SUBLANES, c)),
                jnp.broadcast_to(hb[0:1, :], (SUBLANES, c)))

    cf0 = jnp.broadcast_to(h0f_ref[...], (SUBLANES, c))
    cb0 = jnp.broadcast_to(h0b_ref[...], (SUBLANES, c))
    cf, cb = lax.fori_loop(0, n_tiles, carry_step, (cf0, cb0), unroll=4)
    hf_end_ref[...] = cf[0:1, :]
    hb_end_ref[...] = cb[0:1, :]
    if with_gate:
        y_ref[...] = jax.nn.gelu(gate_ref[...], approximate=True) * (bf_ref[...] + bb_ref[...])


def _rglru(p3, u_blk0, gate_blk0, cw, cb, w_gates, bias, sp, h0f, h0b, with_gate, ch):
    b, l, _ = p3.shape
    d_rnn = cw.shape[1]
    nh = d_rnn // ch
    in_specs = [pl.BlockSpec((None, l, ch), lambda bi, h: (bi, 0, u_blk0 + h))]
    args = [p3]
    if with_gate:
        in_specs.append(pl.BlockSpec((None, l, ch), lambda bi, h: (bi, 0, gate_blk0 + h)))
        args.append(p3)
    in_specs += [pl.BlockSpec((4, ch), lambda bi, h: (0, h)),
                 pl.BlockSpec((1, ch), lambda bi, h: (0, h)),
                 pl.BlockSpec((None, ch, 4 * ch), lambda bi, h: (h, 0, 0)),
                 pl.BlockSpec((None, 1, 4 * ch), lambda bi, h: (h, 0, 0)),
                 pl.BlockSpec((2, ch), lambda bi, h: (0, h)),
                 pl.BlockSpec((None, 1, ch), lambda bi, h: (bi, 0, h)),
                 pl.BlockSpec((None, 1, ch), lambda bi, h: (bi, 0, h))]
    args += [cw, cb, w_gates, bias, sp, h0f, h0b]
    end_spec = pl.BlockSpec((None, 1, ch), lambda bi, h: (bi, 0, h))
    end_shape = jax.ShapeDtypeStruct((b, 1, d_rnn), F32)
    out_specs = [end_spec, end_spec]
    out_shape = [end_shape, end_shape]
    if with_gate:
        out_specs = [pl.BlockSpec((None, l, ch), lambda bi, h: (bi, 0, h))] + out_specs
        out_shape = [jax.ShapeDtypeStruct((b, l, d_rnn), F32)] + out_shape
    return pl.pallas_call(
        functools.partial(_rglru_body, with_gate=with_gate),
        grid=(b, nh),
        in_specs=in_specs,
        out_specs=out_specs,
        out_shape=out_shape,
        scratch_shapes=[pltpu.VMEM((l, ch), F32)] * 4,
        compiler_params=_cparams("arbitrary", "arbitrary"),
        name="rglru_lat" if with_gate else "rglru_ctx",
    )(*args)


CONV_BLOCK = 512


def _filt_body(z_ref, w1_ref, b1_ref, f1_ref, w2_ref, b2_ref, f2_ref, w3f_ref, w3b_ref, b3f_ref, b3b_ref,
               dec_ref, cu_ref, su_ref, sgn_ref, hr_ref, hi_ref):
    n = z_ref.shape[0]
    blk = cu_ref.shape[0]
    nblk = n // blk
    ch = hr_ref.shape[2]
    hdn = jnp.sin(f1_ref[...] * (_dot3(z_ref[...], w1_ref[...]) + b1_ref[...]))
    hdn = jnp.sin(f2_ref[...] * (_dot3(hdn, w2_ref[...]) + b2_ref[...]))
    decay = jnp.exp(-z_ref[:, 0:1] * dec_ref[...])
    kf = (_dot3(hdn, w3f_ref[...]) + b3f_ref[...]) * decay
    kb = (_dot3(hdn, w3b_ref[...]) + b3b_ref[...]) * decay
    row = lax.broadcasted_iota(jnp.int32, (n, ch), 0)
    kb = jnp.where(row == 0, 0.0, kb)
    norm = jnp.sum(jnp.abs(kf) + jnp.abs(kb), axis=0, keepdims=True)
    scale = (2.0 / (2 * blk)) / norm
    sgn = sgn_ref[...]

    p, q, first = [], [], []
    for c in range(nblk):
        taps = jnp.concatenate([kf[c * blk:(c + 1) * blk], kb[c * blk:(c + 1) * blk]], axis=1).astype(BF16)
        p.append(_dot(cu_ref[...], taps))
        q.append(_dot(su_ref[...], taps))
        first.append(taps[0:1, :].astype(F32))
    fwd = lambda a: a[:, :ch]
    bwd = lambda a: a[:, ch:]
    hr_ref[nblk - 1] = (fwd(p[0]) + bwd(p[0])) * scale
    hi_ref[nblk - 1] = (bwd(q[0]) - fwd(q[0])) * scale
    for c in range(1, nblk):
        hr_ref[nblk - 1 + c] = (fwd(p[c]) + sgn * fwd(q[c - 1])) * scale
        hi_ref[nblk - 1 + c] = (sgn * (fwd(p[c - 1]) - fwd(first[c - 1])) - fwd(q[c])) * scale
        hr_ref[nblk - 1 - c] = (bwd(p[c]) + sgn * bwd(q[c - 1])) * scale
        hi_ref[nblk - 1 - c] = (bwd(q[c]) - sgn * (bwd(p[c - 1]) - bwd(first[c - 1]))) * scale


def _filters(z, w1, b1, f1, w2, b2, f2, w3, b3, deltas, cu, su, sgn, d_h, order, ch):
    n, fe = z.shape
    hid = w2.shape[0]
    blk = cu.shape[0]
    nh = 2 * (n // blk) - 1
    per_order = d_h // ch
    ncol = order * per_order
    const = lambda shape: pl.BlockSpec(shape, lambda g: (0, 0))
    hspec = pl.BlockSpec((nh, blk, ch), lambda g: (0, 0, g))
    return pl.pallas_call(
        _filt_body,
        grid=(ncol,),
        in_specs=[const((n, fe)), const((fe, hid)), const((1, hid)), const((1, hid)),
                  const((hid, hid)), const((1, hid)), const((1, hid)),
                  pl.BlockSpec((hid, ch), lambda g: (0, g)),
                  pl.BlockSpec((hid, ch), lambda g: (0, ncol + g)),
                  pl.BlockSpec((1, ch), lambda g: (0, g)),
                  pl.BlockSpec((1, ch), lambda g: (0, ncol + g)),
                  pl.BlockSpec((1, ch), lambda g: (0, g % per_order)),
                  const((blk, blk)), const((blk, blk)), const((blk, 1))],
        out_specs=[hspec, hspec],
        out_shape=[jax.ShapeDtypeStruct((nh, blk, order * d_h), F32)] * 2,
        compiler_params=_cparams("arbitrary"),
        name="hyena_filters",
    )(z, w1, b1, f1, w2, b2, f2, w3, w3, b3, b3, deltas, cu, su, sgn)


def _hyena_body(v_ref, x1_ref, x2_ref, wv_ref, w1_ref, w2_ref, fb_ref, c3_ref, s3_ref,
                h0r_ref, h0i_ref, h1r_ref, h1i_ref, o_ref, z_ref, zb_ref, p_ref, q_ref, yr_ref, yi_ref):
    n = o_ref.shape[0]
    blk = c3_ref.shape[0]
    nblk = n // blk
    halo = SUBLANES
    sub = 32

    def conv3(src_ref, w_ref, r0):
        lo, hi = max(r0 - halo, 0), min(r0 + blk + halo, n)
        win = src_ref[lo:hi, :]
        size = hi - lo
        row = lax.broadcasted_iota(jnp.int32, win.shape, 0)
        prev = pltpu.roll(win, 1, 0)
        nxt = pltpu.roll(win, size - 1, 0)
        if lo == 0:
            prev = jnp.where(row == 0, 0.0, prev)
        if hi == n:
            nxt = jnp.where(row == size - 1, 0.0, nxt)
        out = w_ref[0:1, :] * prev + w_ref[1:2, :] * win + w_ref[2:3, :] * nxt
        return out[r0 - lo:r0 - lo + blk, :]

    def long_conv(hr_ref, hi_ref, fb, gate_ref, gate_w_ref, dst_ref):
        for i in range(nblk):
            rows = slice(i * blk, (i + 1) * blk)
            p_ref[rows, :] = _dot(c3_ref[...], zb_ref[rows, :])
            q_ref[rows, :] = _dot(s3_ref[...], zb_ref[rows, :])

        def products(rc, carry):
            r0 = pl.multiple_of(rc * sub, sub)
            for j in range(nblk):
                yr = yi = None
                for i in range(nblk):
                    rows = pl.ds(i * blk + r0, sub)
                    p, q = p_ref[rows, :], q_ref[rows, :]
                    gr = hr_ref[nblk - 1 + j - i, pl.ds(r0, sub), :]
                    gi = hi_ref[nblk - 1 + j - i, pl.ds(r0, sub), :]
                    tr, ti = p * gr + q * gi, q * gr - p * gi
                    yr, yi = (tr, ti) if yr is None else (yr + tr, yi + ti)
                yr_ref[pl.ds(j * blk + r0, sub), :] = yr.astype(BF16)
                yi_ref[pl.ds(j * blk + r0, sub), :] = yi.astype(BF16)
            return carry

        lax.fori_loop(0, blk // sub, products, 0)
        for j in range(nblk):
            rows = slice(j * blk, (j + 1) * blk)
            y = _dot(c3_ref[...], yr_ref[rows, :]) + _dot(s3_ref[...], yi_ref[rows, :])
            dst_ref[rows, :] = conv3(gate_ref, gate_w_ref, j * blk) * (y + z_ref[rows, :] * fb)

    for r0 in range(0, n, blk):
        z_ref[r0:r0 + blk, :] = conv3(v_ref, wv_ref, r0)
    zb_ref[...] = z_ref[...].astype(BF16)
    long_conv(h0r_ref, h0i_ref, fb_ref[0:1, :], x1_ref, w1_ref, z_ref)
    zb_ref[...] = z_ref[...].astype(BF16)
    long_conv(h1r_ref, h1i_ref, fb_ref[1:2, :], x2_ref, w2_ref, o_ref)


def _hyena(p3, col_blk0, conv_w, fbias, c3, s3, hr, hi, d_h, ch):
    b, n, _ = p3.shape
    nh, blk, _ = hr.shape
    per = d_h // ch
    zspec = lambda k: pl.BlockSpec((None, n, ch), lambda h, bi: (bi, 0, col_blk0 + k * per + h))
    wspec = lambda k: pl.BlockSpec((3, ch), lambda h, bi: (0, k * per + h))
    hspec = lambda o: pl.BlockSpec((nh, blk, ch), lambda h, bi: (0, 0, o * per + h), pipeline_mode=pl.Buffered(1))
    const = pl.BlockSpec((blk, blk), lambda h, bi: (0, 0))
    return pl.pallas_call(
        _hyena_body,
        grid=(per, b),
        in_specs=[zspec(0), zspec(1), zspec(2), wspec(0), wspec(1), wspec(2),
                  pl.BlockSpec((2, ch), lambda h, bi: (0, h)), const, const,
                  hspec(0), hspec(0), hspec(1), hspec(1)],
        out_specs=pl.BlockSpec((None, n, ch), lambda h, bi: (bi, 0, h)),
        out_shape=jax.ShapeDtypeStruct((b, n, d_h), F32),
        scratch_shapes=[pltpu.VMEM((n, ch), F32), pltpu.VMEM((n, ch), BF16),
                        pltpu.VMEM((n, ch), F32), pltpu.VMEM((n, ch), F32),
                        pltpu.VMEM((n, ch), BF16), pltpu.VMEM((n, ch), BF16)],
        compiler_params=_cparams("arbitrary", "arbitrary"),
        name="hyena_mix",
    )(p3, p3, p3, conv_w, conv_w, conv_w, fbias, c3, s3, hr, hi, hr, hi)


def _merge_body(ya_ref, yb_ref, x_ref, pe_ref, mod_ref, ga_ref, gb_ref, wo_ref, g2_ref, wr_ref, br_ref,
                x1_ref, h2_ref, rt_ref, cnt_ref, prow_ref, wob_ref):
    first = (pl.program_id(0) == 0) & (pl.program_id(1) == 0)

    @pl.when(first)
    def _():
        wob_ref[...] = wo_ref[...].astype(BF16)

    da = ya_ref.shape[1]
    na = _rms(ya_ref[...], ga_ref[...]).astype(BF16)
    nb = _rms(yb_ref[...], gb_ref[...]).astype(BF16)
    y = _dot(na, wob_ref[0:da, :]) + _dot(nb, wob_ref[da:, :])
    x1 = x_ref[...] + pe_ref[...] + mod_ref[2:3, :] * y
    x1_ref[...] = x1
    h2 = _rms(x1, g2_ref[...]) * (1.0 + mod_ref[4:5, :]) + mod_ref[3:4, :]
    h2_ref[...] = h2.astype(BF16)

    logits = _dot3(h2, wr_ref[...]) + br_ref[...]
    lane = lax.broadcasted_iota(jnp.int32, logits.shape, 1).astype(F32)
    neg = -jnp.inf
    big = jnp.float32(1 << 20)
    gl = jnp.where((lane >= N_EXPERTS) & (lane < N_EXPERTS + N_GROUPS), logits, neg)
    gmax = jnp.max(gl, axis=1, keepdims=True)
    g_p = 1.0 / jnp.sum(jnp.exp(gl - gmax), axis=1, keepdims=True)
    g_i = jnp.min(jnp.where(gl == gmax, lane, big), axis=1, keepdims=True) - N_EXPERTS
    lo = g_i * EXPERTS_PER_GROUP
    el = jnp.where((lane >= lo) & (lane < lo + EXPERTS_PER_GROUP), logits, neg)
    m1 = jnp.max(el, axis=1, keepdims=True)
    i1 = jnp.min(jnp.where(el == m1, lane, big), axis=1, keepdims=True)
    el2 = jnp.where(lane == i1, neg, el)
    m2 = jnp.max(el2, axis=1, keepdims=True)
    i2 = jnp.min(jnp.where(el2 == m2, lane, big), axis=1, keepdims=True)
    e2 = jnp.exp(m2 - m1)
    w1 = g_p / (1.0 + e2)
    w2 = g_p * e2 / (1.0 + e2)
    o1 = jnp.where(lane == i1, 1.0, 0.0)
    o2 = jnp.where(lane == i2, 1.0, 0.0)
    nsub = cnt_ref.shape[0]
    sub = o1.shape[0] // nsub
    sq = lambda shape, d: lax.broadcasted_iota(jnp.int32, shape, d)
    earlier = jnp.where(sq((sub, sub), 1) < sq((sub, sub), 0), 1.0, 0.0).astype(BF16)
    below = jnp.where(sq((LANES, LANES), 0) < sq((LANES, LANES), 1), 1.0, 0.0).astype(BF16)
    pick_row = jnp.where(sq((SUBLANES, LANES), 0) == sq((SUBLANES, LANES), 1), 1.0, 0.0).astype(BF16)
    row_id = sq((SUBLANES, sub), 0)
    pos1, pos2 = [], []
    for k in range(nsub):
        a1, a2 = o1[k * sub:(k + 1) * sub], o2[k * sub:(k + 1) * sub]
        both = a1 + a2
        count = jnp.sum(both, axis=0, keepdims=True)
        cnt_ref[k] = jnp.broadcast_to(count, cnt_ref.shape[1:])
        runs = jnp.broadcast_to(jnp.floor((count + (SEG - 1)) / SEG), (SUBLANES, LANES))
        start = _dot(runs.astype(BF16), below)[0:1, :] * SEG
        before = _dot(earlier, both.astype(BF16)) + start
        p1 = jnp.sum(a1 * before, axis=1, keepdims=True)
        p2 = jnp.sum(a2 * before, axis=1, keepdims=True)
        pos1.append(p1)
        pos2.append(p2)
        lane_k = sq((sub, LANES), 1).astype(F32)
        hi1, hi2 = jnp.floor(p1 / 64.0), jnp.floor(p2 / 64.0)
        parts = (jnp.where(lane_k == 0.0, hi1, 0.0) + jnp.where(lane_k == 1.0, p1 - 64.0 * hi1, 0.0)
                 + jnp.where(lane_k == 2.0, hi2, 0.0) + jnp.where(lane_k == 3.0, p2 - 64.0 * hi2, 0.0))
        rows = lax.dot_general(pick_row, parts.astype(BF16), (((1,), (1,)), ((), ())), preferred_element_type=F32)
        q1 = rows[0:1, :] * 64.0 + rows[1:2, :]
        q2 = rows[2:3, :] * 64.0 + rows[3:4, :]
        prow_ref[k] = jnp.where(row_id == 0, q1, jnp.where(row_id == 1, q2, 0.0))
    p1 = jnp.concatenate(pos1, axis=0)
    p2 = jnp.concatenate(pos2, axis=0)
    rt_ref[...] = (jnp.where(lane == 0.0, i1, 0.0) + jnp.where(lane == 1.0, i2, 0.0)
                   + jnp.where(lane == 2.0, w1, 0.0) + jnp.where(lane == 3.0, w2, 0.0)
                   + jnp.where(lane == 4.0, p1, 0.0) + jnp.where(lane == 5.0, p2, 0.0))


def _merge(ya, yb, x3, pe, mod3, ga, gb, w_out, g2n, w_r, b_r, tm, nsub):
    b, n, d = x3.shape
    da, db = ya.shape[2], yb.shape[2]
    tok = lambda w: pl.BlockSpec((None, tm, w), lambda bi, i: (bi, i, 0))
    const = lambda shape: pl.BlockSpec(shape, lambda bi, i: (0, 0))
    return pl.pallas_call(
        _merge_body,
        grid=(b, n // tm),
        in_specs=[tok(da), tok(db), tok(d), pl.BlockSpec((tm, d), lambda bi, i: (i, 0)),
                  pl.BlockSpec((None, N_MOD, d), lambda bi, i: (bi, 0, 0)),
                  const((1, da)), const((1, db)), const((da + db, d)), const((1, d)),
                  const((d, LANES)), const((1, LANES))],
        out_specs=[tok(d), tok(d), tok(LANES),
                   pl.BlockSpec((nsub, SUBLANES, LANES), lambda bi, i: (bi * (n // tm) + i, 0, 0)),
                   pl.BlockSpec((nsub, SUBLANES, tm // nsub), lambda bi, i: (bi * (n // tm) + i, 0, 0))],
        out_shape=[jax.ShapeDtypeStruct((b, n, d), F32), jax.ShapeDtypeStruct((b, n, d), BF16),
                   jax.ShapeDtypeStruct((b, n, LANES), F32),
                   jax.ShapeDtypeStruct((b * (n // tm) * nsub, SUBLANES, LANES), F32),
                   jax.ShapeDtypeStruct((b * (n // tm) * nsub, SUBLANES, tm // nsub), F32)],
        scratch_shapes=[pltpu.VMEM((da + db, d), BF16)],
        compiler_params=_cparams("arbitrary", "arbitrary"),
        name="merge_route",
    )(ya, yb, x3, pe, mod3, ga.reshape(1, da), gb.reshape(1, db), w_out, g2n.reshape(1, d), w_r, b_r)


def _segment_copy(local_ref, lrow, global_ref, grow, rows, sem, to_global):
    lrows = pl.ds(pl.multiple_of(lrow, SEG), rows)
    grows = pl.ds(pl.multiple_of(grow, SEG), rows)
    if to_global:
        return pltpu.make_async_copy(local_ref.at[lrows, :], global_ref.at[grows, :], sem)
    return pltpu.make_async_copy(global_ref.at[grows, :], local_ref.at[lrows, :], sem)


def _start_segments(tab_ref, blk, local_ref, global_ref, sem, to_global):
    def per_expert(e, carry):
        k = (blk * N_EXPERTS + e) * 3
        ls, gs, nchunk = tab_ref[k], tab_ref[k + 1], tab_ref[k + 2]

        def per_chunk(ci, carry2):
            _segment_copy(local_ref, ls + ci * SEG, global_ref, gs + ci * SEG, SEG, sem, to_global).start()
            return carry2

        return lax.fori_loop(0, nchunk, per_chunk, carry)

    lax.fori_loop(0, N_EXPERTS, per_expert, 0)


def _wait_segments(rows, local_ref, global_ref, sem, to_global):
    chunks = rows // SEG
    nbits = (local_ref.shape[0] // SEG).bit_length()
    for k in range(nbits):
        @pl.when(((chunks >> k) & 1) == 1)
        def _():
            _segment_copy(local_ref, 0, global_ref, 0, SEG << k, sem, to_global).wait()


def _dispatch_body(tab_ref, ends_ref, ltot_ref, prow_ref, h2_ref, xs_ref, loc_ref, zero_ref, sem, zsem):
    blk = pl.program_id(0)
    nblk = pl.num_programs(0)
    tb = h2_ref.shape[0]
    lp = loc_ref.shape[1]
    tile = zero_ref.shape[0]
    slot = blk % 2

    @pl.when(blk == 0)
    def _():
        zero_ref[...] = jnp.zeros_like(zero_ref)

        def zero_fill(first_row):
            return pltpu.make_async_copy(zero_ref, xs_ref.at[pl.ds(pl.multiple_of(first_row, SEG), tile), :], zsem)

        def fill(e, start, wait):
            end = ends_ref[e]

            @pl.when(end > start)
            def _():
                cp = zero_fill(end - tile)
                cp.wait() if wait else cp.start()
            return end

        used = lax.fori_loop(0, N_EXPERTS, functools.partial(fill, wait=False), 0)
        spare = (xs_ref.shape[0] - used) // tile

        def fill_spare(j, carry, wait):
            cp = zero_fill(used + j * tile)
            cp.wait() if wait else cp.start()
            return carry

        lax.fori_loop(0, spare, functools.partial(fill_spare, wait=False), 0)
        lax.fori_loop(0, N_EXPERTS, functools.partial(fill, wait=True), 0)
        lax.fori_loop(0, spare, functools.partial(fill_spare, wait=True), 0)

    pos = lax.broadcasted_iota(jnp.int32, (lp, tb), 0).astype(F32)
    onehot = jnp.where((pos == prow_ref[0:1, :]) | (pos == prow_ref[1:2, :]), 1.0, 0.0).astype(BF16)
    loc_ref[slot] = _dot(onehot, h2_ref[...])

    _start_segments(tab_ref, blk, loc_ref.at[slot], xs_ref, sem.at[slot], True)

    @pl.when(blk > 0)
    def _():
        _wait_segments(ltot_ref[blk - 1], loc_ref.at[1 - slot], xs_ref, sem.at[1 - slot], True)

    @pl.when(blk == nblk - 1)
    def _():
        _wait_segments(ltot_ref[blk], loc_ref.at[slot], xs_ref, sem.at[slot], True)


def _dispatch(tab, ends, ltot, prow, h2, slots, tb, lp, tile):
    t, d = h2.shape
    return pl.pallas_call(
        _dispatch_body,
        grid_spec=pltpu.PrefetchScalarGridSpec(
            num_scalar_prefetch=3,
            grid=(t // tb,),
            in_specs=[pl.BlockSpec((None, SUBLANES, tb), lambda i, *_: (i, 0, 0)),
                      pl.BlockSpec((tb, d), lambda i, *_: (i, 0))],
            out_specs=pl.BlockSpec(memory_space=pl.ANY),
            scratch_shapes=[pltpu.VMEM((2, lp, d), F32), pltpu.VMEM((tile, d), F32),
                            pltpu.SemaphoreType.DMA((2,)), pltpu.SemaphoreType.DMA]),
        out_shape=jax.ShapeDtypeStruct((slots, d), F32),
        compiler_params=_cparams("arbitrary"),
        name="moe_dispatch",
    )(tab, ends, ltot, prow, h2)


def _ffn_body(te_ref, nu_ref, nxt_ref, par_ref, xs_ref, wg_hbm, wu_hbm, wd_hbm, ys_ref,
              wg_buf, wu_buf, wd_buf, wgb_ref, wub_ref, wdb_ref, sem):
    i = pl.program_id(0)
    e = te_ref[i]
    first = (i == 0) | (e != te_ref[jnp.maximum(i - 1, 0)])
    active = i < nu_ref[0]

    def fetch(expert, slot):
        return [pltpu.make_async_copy(w.at[expert], buf.at[slot], sem.at[slot])
                for w, buf in ((wg_hbm, wg_buf), (wu_hbm, wu_buf), (wd_hbm, wd_buf))]

    @pl.when(i == 0)
    def _():
        for cp in fetch(e, par_ref[e]):
            cp.start()

    @pl.when(first & active)
    def _():
        slot = par_ref[e]
        for cp in fetch(e, slot):
            cp.wait()
        nxt = nxt_ref[e]

        @pl.when(nxt >= 0)
        def _():
            for cp in fetch(nxt, 1 - slot):
                cp.start()

        wgb_ref[...] = wg_buf[slot].astype(BF16)
        wub_ref[...] = wu_buf[slot].astype(BF16)
        wdb_ref[...] = wd_buf[slot].astype(BF16)

    @pl.when(active)
    def _():
        xb = xs_ref[...].astype(BF16)
        act = _dot(xb, wgb_ref[...])
        act = act * _sigmoid(act) * _dot(xb, wub_ref[...])
        ys_ref[...] = _dot(act.astype(BF16), wdb_ref[...])


def _ffn(tile_expert, n_used, nxt, par, xs, w_gate, w_up, w_down, tile):
    slots, d = xs.shape
    _, _, de = w_gate.shape
    last = lambda i, te, nu, *_: (jnp.minimum(i, nu[0] - 1), 0)
    hbm = pl.BlockSpec(memory_space=pl.ANY)
    return pl.pallas_call(
        _ffn_body,
        grid_spec=pltpu.PrefetchScalarGridSpec(
            num_scalar_prefetch=4,
            grid=(slots // tile,),
            in_specs=[pl.BlockSpec((tile, d), last), hbm, hbm, hbm],
            out_specs=pl.BlockSpec((tile, d), last),
            scratch_shapes=[pltpu.VMEM((2, d, de), F32), pltpu.VMEM((2, d, de), F32), pltpu.VMEM((2, de, d), F32),
                            pltpu.VMEM((d, de), BF16), pltpu.VMEM((d, de), BF16), pltpu.VMEM((de, d), BF16),
                            pltpu.SemaphoreType.DMA((2,))]),
        out_shape=jax.ShapeDtypeStruct((slots, d), F32),
        input_output_aliases={4: 0},
        compiler_params=_cparams("arbitrary"),
        name="moe_ffn",
    )(tile_expert, n_used, nxt, par, xs, w_gate, w_up, w_down)


def _combine_body(tab_ref, ltot_ref, ys_ref, rt_ref, x1_ref, mod_ref, fg_ref, o_ref, loc_ref, sem):
    blk = pl.program_id(0)
    nblk = pl.num_programs(0)
    _, lp, d = loc_ref.shape
    slot = blk % 2

    @pl.when(blk == 0)
    def _():
        _start_segments(tab_ref, blk, loc_ref.at[slot], ys_ref, sem.at[slot], False)

    @pl.when(blk + 1 < nblk)
    def _():
        _start_segments(tab_ref, blk + 1, loc_ref.at[1 - slot], ys_ref, sem.at[1 - slot], False)

    rt = rt_ref[...]
    lane = lax.broadcasted_iota(jnp.int32, rt.shape, 1)
    col = lambda k: jnp.sum(jnp.where(lane == k, rt, 0.0), axis=1, keepdims=True)
    w1, w2, p1, p2 = col(2), col(3), col(4), col(5)
    pos = lax.broadcasted_iota(jnp.int32, (rt.shape[0], lp), 1).astype(F32)
    weights = (jnp.where(pos == p1, w1, 0.0) + jnp.where(pos == p2, w2, 0.0)).astype(BF16)
    _wait_segments(ltot_ref[blk], loc_ref.at[slot], ys_ref, sem.at[slot], False)

    def clear(ci, carry):
        loc_ref[slot, pl.ds(pl.multiple_of(ci * SEG, SEG), SEG), :] = jnp.zeros((SEG, d), F32)
        return carry

    lax.fori_loop(ltot_ref[blk] // SEG, lp // SEG, clear, 0)
    moe = _dot(weights, loc_ref[slot].astype(BF16))
    o_ref[...] = _rms(x1_ref[...] + mod_ref[5:6, :] * moe, fg_ref[...])


def _combine(tab, ltot, ys, rt2, x1, mod3, final_g, tb, lp, per_batch):
    t, d = x1.shape
    return pl.pallas_call(
        _combine_body,
        grid_spec=pltpu.PrefetchScalarGridSpec(
            num_scalar_prefetch=2,
            grid=(t // tb,),
            in_specs=[pl.BlockSpec(memory_space=pl.ANY),
                      pl.BlockSpec((tb, LANES), lambda i, *_: (i, 0)),
                      pl.BlockSpec((tb, d), lambda i, *_: (i, 0)),
                      pl.BlockSpec((None, N_MOD, d), lambda i, *_: (i // per_batch, 0, 0)),
                      pl.BlockSpec((1, d), lambda i, *_: (0, 0))],
            out_specs=pl.BlockSpec((tb, d), lambda i, *_: (i, 0)),
            scratch_shapes=[pltpu.VMEM((2, lp, d), F32), pltpu.SemaphoreType.DMA((2,))]),
        out_shape=jax.ShapeDtypeStruct((t, d), F32),
        compiler_params=_cparams("arbitrary"),
        name="moe_combine",
    )(tab, ltot, ys, rt2, x1, mod3, final_g.reshape(1, d))


def _moe_layout(cnt_blocks, tile, n_tiles):
    nb = cnt_blocks.shape[0]
    lcnt = (cnt_blocks + SEG - 1) // SEG * SEG
    lstart = jnp.cumsum(lcnt, axis=1) - lcnt
    ltot = jnp.sum(lcnt, axis=1)
    per_expert = jnp.sum(lcnt, axis=0)
    tiles_per = (per_expert + tile - 1) // tile
    tile_ends = jnp.cumsum(tiles_per)
    row_ends = tile_ends * tile
    goff = row_ends - tiles_per * tile
    gstart = goff[None, :] + jnp.cumsum(lcnt, axis=0) - lcnt
    tab = jnp.stack([lstart, gstart, lcnt // SEG], axis=-1).reshape(-1).astype(jnp.int32)
    n_used = tile_ends[-1:]
    tile_ids = jnp.arange(n_tiles, dtype=jnp.int32)
    tile_expert = jnp.sum((tile_ends[None, :] <= jnp.minimum(tile_ids, n_used - 1)[:, None]).astype(jnp.int32), axis=1)
    ids = jnp.arange(N_EXPERTS, dtype=jnp.int32)
    used = tiles_per > 0
    par = (jnp.cumsum(used) - used) % 2
    later = jnp.where(used[None, :] & (ids[None, :] > ids[:, None]), ids[None, :], N_EXPERTS)
    nxt = jnp.min(later, axis=1)
    nxt = jnp.where(nxt == N_EXPERTS, -1, nxt)
    return (tab, row_ends.astype(jnp.int32), ltot.astype(jnp.int32), tile_expert.astype(jnp.int32),
            n_used.astype(jnp.int32), nxt.astype(jnp.int32), par.astype(jnp.int32))


def _sincos_table(rows, cols, dim):
    quarter = dim // 4
    omega = 1.0 / (10000.0 ** (np.arange(quarter, dtype=np.float64) / quarter))
    ang_r = np.arange(rows, dtype=np.float64)[:, None] * omega
    ang_c = np.arange(cols, dtype=np.float64)[:, None] * omega
    emb_r = np.concatenate([np.sin(ang_r), np.cos(ang_r)], axis=-1)
    emb_c = np.concatenate([np.sin(ang_c), np.cos(ang_c)], axis=-1)
    pe = np.concatenate([np.broadcast_to(emb_r[:, None, :], (rows, cols, 2 * quarter)),
                         np.broadcast_to(emb_c[None, :, :], (rows, cols, 2 * quarter))], axis=-1)
    return pe.reshape(rows * cols, 4 * quarter).astype(np.float32)


def _filter_features(n, width):
    pos = np.arange(n, dtype=np.float64)
    t = pos / max(n - 1, 1)
    ang = (2.0 * math.pi * pos / n)[:, None] * np.linspace(1e-4, FILTER_BANDS - 1, FILTER_BANDS)[None, :]
    z = np.concatenate([t[:, None], np.cos(ang), -np.sin(ang)], axis=-1)
    return np.pad(z, ((0, 0), (0, width - z.shape[1]))).astype(np.float32)


def _decay_rates(d_h):
    min_decay = math.log(DECAY_TARGET) / DECAY_SLOW_PCT
    max_decay = math.log(DECAY_TARGET) / DECAY_FAST_PCT
    return np.abs(np.linspace(min_decay, max_decay, d_h)).astype(np.float32).reshape(1, d_h)


def _dft_tables(blk):
    odd = 2 * np.arange(blk, dtype=np.int64) + 1
    shifted = ((odd[:, None] * odd[None, :]) % (8 * blk)).astype(np.float64) * (2.0 * math.pi / (8 * blk))
    plain = ((odd[:, None] * np.arange(blk, dtype=np.int64)[None, :]) % (4 * blk)).astype(np.float64) * (2.0 * math.pi / (4 * blk))
    sgn = (1.0 - 2.0 * (np.arange(blk) % 2)).astype(np.float32).reshape(blk, 1)
    f32 = lambda a: a.astype(np.float32)
    return f32(np.cos(shifted)), f32(np.sin(shifted)), f32(np.cos(plain)), f32(np.sin(plain)), sgn


def kernel(x, c, ctx, c_ctx, w_ada, b_ada, norm1_g, w_in, conv_a_w, conv_a_b, lru_wa, lru_ba, lru_wx, lru_bx, lru_lambda, conv_b_w, filt_w1, filt_b1, filt_freq1, filt_w2, filt_b2, filt_freq2, filt_w3, filt_b3, filt_bias, out_norm_a, out_norm_b, w_out, norm2_g, w_rg, b_rg, w_re, b_re, w_gate, w_up, w_down, final_g):
    bsz, n, d = x.shape
    n_ctx = ctx.shape[1]
    d_rnn = conv_a_w.shape[2]
    d_h = filt_bias.shape[2]
    order = filt_bias.shape[1]
    assert w_ada.shape[0] == 1, "single-layer block"
    l = 0
    ch = 256

    mod_rows = 16
    cc = jnp.concatenate([c, c_ctx[None, :], jnp.zeros((mod_rows - bsz - 1, d), F32)], axis=0)
    mod3 = _ada(cc, w_ada[l], b_ada[l]).reshape(mod_rows, N_MOD, d)

    pe = jnp.asarray(_sincos_table(n // GRID_W, GRID_W, d))
    d_in = w_in.shape[2]
    p3 = _inproj(x, pe, mod3, lambda bi: bi, norm1_g[l], w_in[l], 0, d_in, 512)
    pc3 = _inproj(ctx, None, mod3, lambda bi: bsz, norm1_g[l], w_in[l], 1, d_rnn, n_ctx)

    heads_per_blk = ch // (d_rnn // RNN_HEADS)
    nblk = d_rnn // ch
    eye = jnp.eye(heads_per_blk, dtype=F32)

    def blockdiag(w):
        w5 = w.reshape(2, nblk, heads_per_blk, w.shape[2], w.shape[3])
        return jnp.einsum("dnkij,kl->dnkilj", w5, eye).reshape(2, nblk, ch, ch)

    wa_bd, wx_bd = blockdiag(lru_wa[l]), blockdiag(lru_wx[l])
    w_gates = jnp.concatenate([wa_bd[0], wx_bd[0], wa_bd[1], wx_bd[1]], axis=-1).astype(BF16)
    ba, bx = lru_ba[l].reshape(2, nblk, ch), lru_bx[l].reshape(2, nblk, ch)
    bias = jnp.concatenate([ba[0], bx[0], ba[1], bx[1]], axis=-1).reshape(nblk, 1, 4 * ch)
    sp = jax.nn.softplus(-lru_lambda[l])
    cb = conv_a_b[l].reshape(1, d_rnn)
    zeros_state = jnp.zeros((bsz, 1, d_rnn), F32)
    hf_ctx, hb_ctx = _rglru(pc3, 0, 0, conv_a_w[l], cb, w_gates, bias, sp, zeros_state, zeros_state, False, ch)
    ya, _, _ = _rglru(p3, d_rnn // ch, 0, conv_a_w[l], cb, w_gates, bias, sp, hf_ctx, hb_ctx, True, ch)

    as_bf16 = lambda a: jnp.asarray(a).astype(BF16)
    c3, s3, cu, su, sgn = _dft_tables(CONV_BLOCK)
    fe = 64
    z = jnp.asarray(_filter_features(n, fe))
    w1 = jnp.pad(filt_w1[l], ((0, fe - filt_w1.shape[1]), (0, 0)))
    hid = filt_w2.shape[1]
    hr, hi = _filters(z, w1, filt_b1[l].reshape(1, hid), filt_freq1[l].reshape(1, hid), filt_w2[l],
                      filt_b2[l].reshape(1, hid), filt_freq2[l].reshape(1, hid), filt_w3[l],
                      filt_b3[l].reshape(1, -1), jnp.asarray(_decay_rates(d_h)), as_bf16(cu), as_bf16(su),
                      jnp.asarray(sgn), d_h, order, ch)
    yb = _hyena(p3, 2 * d_rnn // ch, conv_b_w[l], filt_bias[l], as_bf16(c3), as_bf16(s3), hr, hi, d_h, ch)

    w_r = jnp.concatenate([w_re[l], w_rg[l], jnp.zeros((d, LANES - N_EXPERTS - N_GROUPS), F32)], axis=1)
    b_r = jnp.concatenate([b_re[l], b_rg[l], jnp.zeros((LANES - N_EXPERTS - N_GROUPS,), F32)]).reshape(1, LANES)
    tb = 256
    tm = 512
    x1, h2, rt, cnt, prow = _merge(ya, yb, x, pe, mod3, out_norm_a[l], out_norm_b[l], w_out[l], norm2_g[l],
                                   w_r, b_r, tm, tm // tb)

    tile = 512
    t_all = bsz * n
    nb = t_all // tb
    lp = 2 * tb + N_EXPERTS * SEG
    n_tiles = (2 * t_all + nb * N_EXPERTS * (SEG - 1) + N_EXPERTS * (tile - 1)) // tile
    cnt_blocks = cnt[:, 0, :N_EXPERTS].astype(jnp.int32)
    tab, row_ends, ltot, tile_expert, n_used, nxt, par = _moe_layout(cnt_blocks, tile, n_tiles)
    xs = _dispatch(tab, row_ends, ltot, prow, h2.reshape(t_all, d), n_tiles * tile, tb, lp, tile)
    ys = _ffn(tile_expert, n_used, nxt, par, xs, w_gate[l], w_up[l], w_down[l], tile)
    out = _combine(tab, ltot, ys, rt.reshape(t_all, LANES), x1.reshape(t_all, d), mod3, final_g, tb, lp, n // tb)
    return out.reshape(bsz, n, d)
```

```python
import functools
import math

import numpy as np
import jax
import jax.numpy as jnp
from jax import lax
from jax.experimental import pallas as pl
from jax.experimental.pallas import tpu as pltpu

F32 = jnp.float32
BF16 = jnp.bfloat16
EPS = 1e-6
LRU_C = 8.0
N_MOD = 6
GRID_W = 64
RNN_HEADS = 8
N_GROUPS = 4
EXPERTS_PER_GROUP = 8
N_EXPERTS = N_GROUPS * EXPERTS_PER_GROUP
FILTER_BANDS = 16
DECAY_FAST_PCT = 0.3
DECAY_SLOW_PCT = 1.5
DECAY_TARGET = 1e-2
SUBLANES = 8
LANES = 128
SEG = SUBLANES
VMEM_LIMIT = 60 * 1024 * 1024


def _cparams(*sem):
    return pltpu.CompilerParams(dimension_semantics=sem, vmem_limit_bytes=VMEM_LIMIT)


def _dot(a, b):
    return jnp.dot(a, b, preferred_element_type=F32)


def _split(a):
    hi = a.astype(BF16)
    return hi, (a - hi.astype(F32)).astype(BF16)


def _dot3(a, b):
    a_hi, a_lo = _split(a)
    b_hi, b_lo = _split(b)
    return _dot(a_hi, b_hi) + _dot(a_lo, b_hi) + _dot(a_hi, b_lo)


def _rms(v, g):
    return v * lax.rsqrt(jnp.mean(v * v, axis=-1, keepdims=True) + EPS) * g


def _sigmoid(z):
    return 1.0 / (1.0 + jnp.exp(-z))


def _shift_rows(v, d):
    n = v.shape[0]
    rolled = pltpu.roll(v, (-d) % n, 0)
    row = lax.broadcasted_iota(jnp.int32, v.shape, 0)
    ok = (row + d >= 0) & (row + d < n)
    return jnp.where(ok, rolled, 0.0)


def _tile_rot(v, s):
    n, c = v.shape
    return pltpu.roll(v.reshape(n // SUBLANES, SUBLANES, c), s, 1).reshape(n, c)


def _ada_body(c_ref, w_ref, b_ref, o_ref):
    c = c_ref[...]
    o_ref[...] = _dot3(c * _sigmoid(c), w_ref[...]) + b_ref[...]


def _ada(cc, w_ada, b_ada):
    rows, d = cc.shape
    n = w_ada.shape[1]
    tn = 1024
    return pl.pallas_call(
        _ada_body,
        grid=(n // tn,),
        in_specs=[pl.BlockSpec((rows, d), lambda j: (0, 0)),
                  pl.BlockSpec((d, tn), lambda j: (0, j)),
                  pl.BlockSpec((1, tn), lambda j: (0, j))],
        out_specs=pl.BlockSpec((rows, tn), lambda j: (0, j)),
        out_shape=jax.ShapeDtypeStruct((rows, n), F32),
        compiler_params=_cparams("arbitrary"),
        name="ada",
    )(cc, w_ada, b_ada.reshape(1, n))


def _lagged(step, n_tiles, first_half, second_half, bufs):
    @pl.when(step == 0)
    def _():
        first_half(bufs[0])

    for parity in (0, 1):
        @pl.when((step > 0) & (step < n_tiles) & (step % 2 == parity))
        def _():
            first_half(bufs[parity])
            second_half(bufs[1 - parity])

    @pl.when(step == n_tiles)
    def _():
        second_half(bufs[(n_tiles - 1) % 2])


def _inproj_lat_body(x_ref, pe_ref, mod_ref, g_ref, w_ref, o_ref, wb_ref, h0_ref, h1_ref):
    step = pl.program_id(0)

    @pl.when(step == 0)
    def _():
        wb_ref[...] = w_ref[...].astype(BF16)

    def normalize(h_ref):
        x = x_ref[...] + pe_ref[...]
        h = _rms(x, g_ref[...]) * (1.0 + mod_ref[1:2, :]) + mod_ref[0:1, :]
        h_ref[...] = h.astype(BF16)

    def project(h_ref):
        o_ref[...] = _dot(h_ref[...], wb_ref[...])

    _lagged(step, pl.num_programs(0) - 1, normalize, project, (h0_ref, h1_ref))


def _inproj_lat(x3, pe, mod3, g, w, tm):
    b, l, d = x3.shape
    n_out = w.shape[1]
    per = l // tm
    n_tiles = b * per
    cur = lambda s: jnp.minimum(s, n_tiles - 1)
    prev = lambda s: jnp.maximum(s - 1, 0)
    return pl.pallas_call(
        _inproj_lat_body,
        grid=(n_tiles + 1,),
        in_specs=[pl.BlockSpec((None, tm, d), lambda s: (cur(s) // per, cur(s) % per, 0)),
                  pl.BlockSpec((tm, d), lambda s: (cur(s) % per, 0)),
                  pl.BlockSpec((None, N_MOD, d), lambda s: (cur(s) // per, 0, 0)),
                  pl.BlockSpec((1, d), lambda s: (0, 0)),
                  pl.BlockSpec((d, n_out), lambda s: (0, 0), pipeline_mode=pl.Buffered(1))],
        out_specs=pl.BlockSpec((None, tm, n_out), lambda s: (prev(s) // per, prev(s) % per, 0)),
        out_shape=jax.ShapeDtypeStruct((b, l, n_out), F32),
        scratch_shapes=[pltpu.VMEM((d, n_out), BF16), pltpu.VMEM((tm, d), BF16), pltpu.VMEM((tm, d), BF16)],
        compiler_params=_cparams("arbitrary"),
        name="inproj_pe",
    )(x3, pe, mod3, g.reshape(1, d), w)


def _inproj_ctx_body(x_ref, mod_ref, g_ref, w_ref, o_ref):
    h = _rms(x_ref[...], g_ref[...]) * (1.0 + mod_ref[1:2, :]) + mod_ref[0:1, :]
    o_ref[...] = _dot(h.astype(BF16), w_ref[...].astype(BF16))


def _inproj_ctx(x2, mod3, mod_row, g, w, col_block, n_out, tm):
    r, d = x2.shape
    return pl.pallas_call(
        _inproj_ctx_body,
        grid=(r // tm,),
        in_specs=[pl.BlockSpec((tm, d), lambda i: (i, 0)),
                  pl.BlockSpec((None, N_MOD, d), lambda i: (mod_row, 0, 0)),
                  pl.BlockSpec((1, d), lambda i: (0, 0)),
                  pl.BlockSpec((d, n_out), lambda i: (0, col_block))],
        out_specs=pl.BlockSpec((tm, n_out), lambda i: (i, 0)),
        out_shape=jax.ShapeDtypeStruct((r, n_out), F32),
        compiler_params=_cparams("arbitrary"),
        name="inproj_ctx",
    )(x2, mod3, g.reshape(1, d), w)


def _rglru_body(*refs, with_gate):
    if with_gate:
        (u_ref, gate_ref, cw_ref, cb_ref, w_ref, bias_ref, sp_ref, h0f_ref, h0b_ref,
         y_ref, hf_end_ref, hb_end_ref, af_ref, bf_ref, ab_ref, bb_ref) = refs
    else:
        (u_ref, cw_ref, cb_ref, w_ref, bias_ref, sp_ref, h0f_ref, h0b_ref,
         hf_end_ref, hb_end_ref, af_ref, bf_ref, ab_ref, bb_ref) = refs
    n, c = u_ref.shape
    u = u_ref[...]
    v = cb_ref[...] + cw_ref[1:2, :] * _shift_rows(u, -1) + cw_ref[0:1, :] * _shift_rows(u, -2)
    v = v + cw_ref[2:3, :] * u + cw_ref[3:4, :] * _shift_rows(u, 1)
    z = _dot(v.astype(BF16), w_ref[...]) + bias_ref[...]
    sub = lax.broadcasted_iota(jnp.int32, (n, c), 0) % SUBLANES

    def local_scan(k, reverse, a_ref, b_ref):
        r = _sigmoid(z[:, (2 * k) * c:(2 * k + 1) * c])
        i = _sigmoid(z[:, (2 * k + 1) * c:(2 * k + 2) * c])
        log_a = (-LRU_C) * r * sp_ref[k:k + 1, :]
        a = jnp.exp(log_a)
        b = jnp.sqrt(1.0 - a * a) * (i * v)
        for s in (1, 2, 4):
            ok = (sub < SUBLANES - s) if reverse else (sub >= s)
            shift = SUBLANES - s if reverse else s
            b = a * jnp.where(ok, _tile_rot(b, shift), 0.0) + b
            a = a * jnp.where(ok, _tile_rot(a, shift), 1.0)
        a_ref[...] = a
        b_ref[...] = b

    local_scan(0, False, af_ref, bf_ref)
    local_scan(1, True, ab_ref, bb_ref)

    n_tiles = n // SUBLANES

    def carry_step(q, carry):
        cf, cb = carry
        rf = pl.ds(pl.multiple_of(q * SUBLANES, SUBLANES), SUBLANES)
        rb = pl.ds(pl.multiple_of((n_tiles - 1 - q) * SUBLANES, SUBLANES), SUBLANES)
        hf = af_ref[rf, :] * cf + bf_ref[rf, :]
        hb = ab_ref[rb, :] * cb + bb_ref[rb, :]
        bf_ref[rf, :] = hf
        bb_ref[rb, :] = hb
        return (jnp.broadcast_to(hf[SUBLANES - 1:SUBLANES, :], (SUBLANES, c)),
                jnp.broadcast_to(hb[0:1, :], (SUBLANES, c)))

    cf0 = jnp.broadcast_to(h0f_ref[...], (SUBLANES, c))
    cb0 = jnp.broadcast_to(h0b_ref[...], (SUBLANES, c))
    cf, cb = lax.fori_loop(0, n_tiles, carry_step, (cf0, cb0), unroll=4)
    hf_end_ref[...] = cf[0:1, :]
    hb_end_ref[...] = cb[0:1, :]
    if with_gate:
        y_ref[...] = jax.nn.gelu(gate_ref[...], approximate=True) * (bf_ref[...] + bb_ref[...])


def _rglru(p3, u_blk0, gate_blk0, cw, cb, w_gates, bias, sp, h0f, h0b, with_gate, ch):
    b, l, _ = p3.shape
    d_rnn = cw.shape[1]
    nh = d_rnn // ch
    in_specs = [pl.BlockSpec((None, l, ch), lambda bi, h: (bi, 0, u_blk0 + h))]
    args = [p3]
    if with_gate:
        in_specs.append(pl.BlockSpec((None, l, ch), lambda bi, h: (bi, 0, gate_blk0 + h)))
        args.append(p3)
    in_specs += [pl.BlockSpec((4, ch), lambda bi, h: (0, h)),
                 pl.BlockSpec((1, ch), lambda bi, h: (0, h)),
                 pl.BlockSpec((None, ch, 4 * ch), lambda bi, h: (h, 0, 0)),
                 pl.BlockSpec((None, 1, 4 * ch), lambda bi, h: (h, 0, 0)),
                 pl.BlockSpec((2, ch), lambda bi, h: (0, h)),
                 pl.BlockSpec((None, 1, ch), lambda bi, h: (bi, 0, h)),
                 pl.BlockSpec((None, 1, ch), lambda bi, h: (bi, 0, h))]
    args += [cw, cb, w_gates, bias, sp, h0f, h0b]
    end_spec = pl.BlockSpec((None, 1, ch), lambda bi, h: (bi, 0, h))
    end_shape = jax.ShapeDtypeStruct((b, 1, d_rnn), F32)
    out_specs = [end_spec, end_spec]
    out_shape = [end_shape, end_shape]
    if with_gate:
        out_specs = [pl.BlockSpec((None, l, ch), lambda bi, h: (bi, 0, h))] + out_specs
        out_shape = [jax.ShapeDtypeStruct((b, l, d_rnn), F32)] + out_shape
    return pl.pallas_call(
        functools.partial(_rglru_body, with_gate=with_gate),
        grid=(b, nh),
        in_specs=in_specs,
        out_specs=out_specs,
        out_shape=out_shape,
        scratch_shapes=[pltpu.VMEM((l, ch), F32)] * 4,
        compiler_params=_cparams("arbitrary", "arbitrary"),
        name="rglru_lat" if with_gate else "rglru_ctx",
    )(*args)


CONV_BLOCK = 512


def _filt_body(z_ref, w1_ref, b1_ref, f1_ref, w2_ref, b2_ref, f2_ref, w3f_ref, w3b_ref, b3f_ref, b3b_ref,
               dec_ref, cu_ref, su_ref, sgn_ref, hr_ref, hi_ref):
    n = z_ref.shape[0]
    blk = cu_ref.shape[0]
    nblk = n // blk
    ch = hr_ref.shape[2]
    hdn = jnp.sin(f1_ref[...] * (_dot3(z_ref[...], w1_ref[...]) + b1_ref[...]))
    hdn = jnp.sin(f2_ref[...] * (_dot3(hdn, w2_ref[...]) + b2_ref[...]))
    decay = jnp.exp(-z_ref[:, 0:1] * dec_ref[...])
    kf = (_dot3(hdn, w3f_ref[...]) + b3f_ref[...]) * decay
    kb = (_dot3(hdn, w3b_ref[...]) + b3b_ref[...]) * decay
    row = lax.broadcasted_iota(jnp.int32, (n, ch), 0)
    kb = jnp.where(row == 0, 0.0, kb)
    norm = jnp.sum(jnp.abs(kf) + jnp.abs(kb), axis=0, keepdims=True)
    scale = (2.0 / (2 * blk)) / norm
    sgn = sgn_ref[...]

    p, q, first = [], [], []
    for c in range(nblk):
        taps = jnp.concatenate([kf[c * blk:(c + 1) * blk], kb[c * blk:(c + 1) * blk]], axis=1).astype(BF16)
        p.append(_dot(cu_ref[...], taps))
        q.append(_dot(su_ref[...], taps))
        first.append(taps[0:1, :].astype(F32))
    fwd = lambda a: a[:, :ch]
    bwd = lambda a: a[:, ch:]
    hr_ref[nblk - 1] = (fwd(p[0]) + bwd(p[0])) * scale
    hi_ref[nblk - 1] = (bwd(q[0]) - fwd(q[0])) * scale
    for c in range(1, nblk):
        hr_ref[nblk - 1 + c] = (fwd(p[c]) + sgn * fwd(q[c - 1])) * scale
        hi_ref[nblk - 1 + c] = (sgn * (fwd(p[c - 1]) - fwd(first[c - 1])) - fwd(q[c])) * scale
        hr_ref[nblk - 1 - c] = (bwd(p[c]) + sgn * bwd(q[c - 1])) * scale
        hi_ref[nblk - 1 - c] = (bwd(q[c]) - sgn * (bwd(p[c - 1]) - bwd(first[c - 1]))) * scale


def _filters(z, w1, b1, f1, w2, b2, f2, w3, b3, deltas, cu, su, sgn, d_h, order, ch):
    n, fe = z.shape
    hid = w2.shape[0]
    blk = cu.shape[0]
    nh = 2 * (n // blk) - 1
    per_order = d_h // ch
    ncol = order * per_order
    const = lambda shape: pl.BlockSpec(shape, lambda g: (0, 0))
    hspec = pl.BlockSpec((nh, blk, ch), lambda g: (0, 0, g))
    return pl.pallas_call(
        _filt_body,
        grid=(ncol,),
        in_specs=[const((n, fe)), const((fe, hid)), const((1, hid)), const((1, hid)),
                  const((hid, hid)), const((1, hid)), const((1, hid)),
                  pl.BlockSpec((hid, ch), lambda g: (0, g)),
                  pl.BlockSpec((hid, ch), lambda g: (0, ncol + g)),
                  pl.BlockSpec((1, ch), lambda g: (0, g)),
                  pl.BlockSpec((1, ch), lambda g: (0, ncol + g)),
                  pl.BlockSpec((1, ch), lambda g: (0, g % per_order)),
                  const((blk, blk)), const((blk, blk)), const((blk, 1))],
        out_specs=[hspec, hspec],
        out_shape=[jax.ShapeDtypeStruct((nh, blk, order * d_h), F32)] * 2,
        compiler_params=_cparams("arbitrary"),
        name="hyena_filters",
    )(z, w1, b1, f1, w2, b2, f2, w3, w3, b3, b3, deltas, cu, su, sgn)


def _hyena_body(v_ref, x1_ref, x2_ref, wv_ref, w1_ref, w2_ref, fb_ref, c3_ref, s3_ref,
                h0r_ref, h0i_ref, h1r_ref, h1i_ref, o_ref, z_ref, zb_ref, p_ref, q_ref, yr_ref, yi_ref):
    n = o_ref.shape[0]
    blk = c3_ref.shape[0]
    nblk = n // blk
    halo = SUBLANES
    sub = 32

    def conv3(src_ref, w_ref, r0):
        lo, hi = max(r0 - halo, 0), min(r0 + blk + halo, n)
        win = src_ref[lo:hi, :]
        size = hi - lo
        row = lax.broadcasted_iota(jnp.int32, win.shape, 0)
        prev = pltpu.roll(win, 1, 0)
        nxt = pltpu.roll(win, size - 1, 0)
        if lo == 0:
            prev = jnp.where(row == 0, 0.0, prev)
        if hi == n:
            nxt = jnp.where(row == size - 1, 0.0, nxt)
        out = w_ref[0:1, :] * prev + w_ref[1:2, :] * win + w_ref[2:3, :] * nxt
        return out[r0 - lo:r0 - lo + blk, :]

    def long_conv(hr_ref, hi_ref, fb, gate_ref, gate_w_ref, dst_ref):
        for i in range(nblk):
            rows = slice(i * blk, (i + 1) * blk)
            p_ref[rows, :] = _dot(c3_ref[...], zb_ref[rows, :])
            q_ref[rows, :] = _dot(s3_ref[...], zb_ref[rows, :])

        def products(rc, carry):
            r0 = pl.multiple_of(rc * sub, sub)
            for j in range(nblk):
                yr = yi = None
                for i in range(nblk):
                    rows = pl.ds(i * blk + r0, sub)
                    p, q = p_ref[rows, :], q_ref[rows, :]
                    gr = hr_ref[nblk - 1 + j - i, pl.ds(r0, sub), :]
                    gi = hi_ref[nblk - 1 + j - i, pl.ds(r0, sub), :]
                    tr, ti = p * gr + q * gi, q * gr - p * gi
                    yr, yi = (tr, ti) if yr is None else (yr + tr, yi + ti)
                yr_ref[pl.ds(j * blk + r0, sub), :] = yr.astype(BF16)
                yi_ref[pl.ds(j * blk + r0, sub), :] = yi.astype(BF16)
            return carry

        lax.fori_loop(0, blk // sub, products, 0)
        for j in range(nblk):
            rows = slice(j * blk, (j + 1) * blk)
            y = _dot(c3_ref[...], yr_ref[rows, :]) + _dot(s3_ref[...], yi_ref[rows, :])
            dst_ref[rows, :] = conv3(gate_ref, gate_w_ref, j * blk) * (y + z_ref[rows, :] * fb)

    for r0 in range(0, n, blk):
        z_ref[r0:r0 + blk, :] = conv3(v_ref, wv_ref, r0)
    zb_ref[...] = z_ref[...].astype(BF16)
    long_conv(h0r_ref, h0i_ref, fb_ref[0:1, :], x1_ref, w1_ref, z_ref)
    zb_ref[...] = z_ref[...].astype(BF16)
    long_conv(h1r_ref, h1i_ref, fb_ref[1:2, :], x2_ref, w2_ref, o_ref)


def _hyena(p3, col_blk0, conv_w, fbias, c3, s3, hr, hi, d_h, ch):
    b, n, _ = p3.shape
    nh, blk, _ = hr.shape
    per = d_h // ch
    zspec = lambda k: pl.BlockSpec((None, n, ch), lambda h, bi: (bi, 0, col_blk0 + k * per + h))
    wspec = lambda k: pl.BlockSpec((3, ch), lambda h, bi: (0, k * per + h))
    hspec = lambda o: pl.BlockSpec((nh, blk, ch), lambda h, bi: (0, 0, o * per + h), pipeline_mode=pl.Buffered(1))
    const = pl.BlockSpec((blk, blk), lambda h, bi: (0, 0))
    return pl.pallas_call(
        _hyena_body,
        grid=(per, b),
        in_specs=[zspec(0), zspec(1), zspec(2), wspec(0), wspec(1), wspec(2),
                  pl.BlockSpec((2, ch), lambda h, bi: (0, h)), const, const,
                  hspec(0), hspec(0), hspec(1), hspec(1)],
        out_specs=pl.BlockSpec((None, n, ch), lambda h, bi: (bi, 0, h)),
        out_shape=jax.ShapeDtypeStruct((b, n, d_h), F32),
        scratch_shapes=[pltpu.VMEM((n, ch), F32), pltpu.VMEM((n, ch), BF16),
                        pltpu.VMEM((n, ch), F32), pltpu.VMEM((n, ch), F32),
                        pltpu.VMEM((n, ch), BF16), pltpu.VMEM((n, ch), BF16)],
        compiler_params=_cparams("arbitrary", "arbitrary"),
        name="hyena_mix",
    )(p3, p3, p3, conv_w, conv_w, conv_w, fbias, c3, s3, hr, hi, hr, hi)


def _merge_body(ya_ref, yb_ref, x_ref, pe_ref, mod_ref, ga_ref, gb_ref, wo_ref, g2_ref, wr_ref, br_ref,
                x1_ref, h2_ref, rt_ref, cnt_ref, prow_ref, wob_ref, y0_ref, y1_ref):
    step = pl.program_id(0)

    @pl.when(step == 0)
    def _():
        wob_ref[...] = wo_ref[...].astype(BF16)

    def project(y_ref):
        da = ya_ref.shape[1]
        na = _rms(ya_ref[...], ga_ref[...]).astype(BF16)
        nb = _rms(yb_ref[...], gb_ref[...]).astype(BF16)
        y_ref[...] = _dot(na, wob_ref[0:da, :]) + _dot(nb, wob_ref[da:, :])

    def route(y_ref):
        _route(y_ref[...], x_ref, pe_ref, mod_ref, g2_ref, wr_ref, br_ref, x1_ref, h2_ref, rt_ref, cnt_ref, prow_ref)

    _lagged(step, pl.num_programs(0) - 1, project, route, (y0_ref, y1_ref))


def _route(y, x_ref, pe_ref, mod_ref, g2_ref, wr_ref, br_ref, x1_ref, h2_ref, rt_ref, cnt_ref, prow_ref):
    x1 = x_ref[...] + pe_ref[...] + mod_ref[2:3, :] * y
    x1_ref[...] = x1
    h2 = _rms(x1, g2_ref[...]) * (1.0 + mod_ref[4:5, :]) + mod_ref[3:4, :]
    h2_ref[...] = h2.astype(BF16)

    logits = _dot3(h2, wr_ref[...]) + br_ref[...]
    lane = lax.broadcasted_iota(jnp.int32, logits.shape, 1).astype(F32)
    neg = -jnp.inf
    big = jnp.float32(1 << 20)
    gl = jnp.where((lane >= N_EXPERTS) & (lane < N_EXPERTS + N_GROUPS), logits, neg)
    gmax = jnp.max(gl, axis=1, keepdims=True)
    g_p = 1.0 / jnp.sum(jnp.exp(gl - gmax), axis=1, keepdims=True)
    g_i = jnp.min(jnp.where(gl == gmax, lane, big), axis=1, keepdims=True) - N_EXPERTS
    lo = g_i * EXPERTS_PER_GROUP
    el = jnp.where((lane >= lo) & (lane < lo + EXPERTS_PER_GROUP), logits, neg)
    m1 = jnp.max(el, axis=1, keepdims=True)
    i1 = jnp.min(jnp.where(el == m1, lane, big), axis=1, keepdims=True)
    el2 = jnp.where(lane == i1, neg, el)
    m2 = jnp.max(el2, axis=1, keepdims=True)
    i2 = jnp.min(jnp.where(el2 == m2, lane, big), axis=1, keepdims=True)
    e2 = jnp.exp(m2 - m1)
    w1 = g_p / (1.0 + e2)
    w2 = g_p * e2 / (1.0 + e2)
    o1 = jnp.where(lane == i1, 1.0, 0.0)
    o2 = jnp.where(lane == i2, 1.0, 0.0)
    nsub = cnt_ref.shape[0]
    sub = o1.shape[0] // nsub
    sq = lambda shape, d: lax.broadcasted_iota(jnp.int32, shape, d)
    earlier = jnp.where(sq((sub, sub), 1) < sq((sub, sub), 0), 1.0, 0.0).astype(BF16)
    below = jnp.where(sq((LANES, LANES), 0) < sq((LANES, LANES), 1), 1.0, 0.0).astype(BF16)
    pick_row = jnp.where(sq((SUBLANES, LANES), 0) == sq((SUBLANES, LANES), 1), 1.0, 0.0).astype(BF16)
    row_id = sq((SUBLANES, sub), 0)
    pos1, pos2 = [], []
    for k in range(nsub):
        a1, a2 = o1[k * sub:(k + 1) * sub], o2[k * sub:(k + 1) * sub]
        both = a1 + a2
        count = jnp.sum(both, axis=0, keepdims=True)
        cnt_ref[k] = jnp.broadcast_to(count, cnt_ref.shape[1:])
        runs = jnp.broadcast_to(jnp.floor((count + (SEG - 1)) / SEG), (SUBLANES, LANES))
        start = _dot(runs.astype(BF16), below)[0:1, :] * SEG
        before = _dot(earlier, both.astype(BF16)) + start
        p1 = jnp.sum(a1 * before, axis=1, keepdims=True)
        p2 = jnp.sum(a2 * before, axis=1, keepdims=True)
        pos1.append(p1)
        pos2.append(p2)
        lane_k = sq((sub, LANES), 1).astype(F32)
        hi1, hi2 = jnp.floor(p1 / 64.0), jnp.floor(p2 / 64.0)
        parts = (jnp.where(lane_k == 0.0, hi1, 0.0) + jnp.where(lane_k == 1.0, p1 - 64.0 * hi1, 0.0)
                 + jnp.where(lane_k == 2.0, hi2, 0.0) + jnp.where(lane_k == 3.0, p2 - 64.0 * hi2, 0.0))
        rows = lax.dot_general(pick_row, parts.astype(BF16), (((1,), (1,)), ((), ())), preferred_element_type=F32)
        q1 = rows[0:1, :] * 64.0 + rows[1:2, :]
        q2 = rows[2:3, :] * 64.0 + rows[3:4, :]
        prow_ref[k] = jnp.where(row_id == 0, q1, jnp.where(row_id == 1, q2, 0.0))
    p1 = jnp.concatenate(pos1, axis=0)
    p2 = jnp.concatenate(pos2, axis=0)
    rt_ref[...] = (jnp.where(lane == 0.0, i1, 0.0) + jnp.where(lane == 1.0, i2, 0.0)
                   + jnp.where(lane == 2.0, w1, 0.0) + jnp.where(lane == 3.0, w2, 0.0)
                   + jnp.where(lane == 4.0, p1, 0.0) + jnp.where(lane == 5.0, p2, 0.0))


def _merge(ya, yb, x3, pe, mod3, ga, gb, w_out, g2n, w_r, b_r, tm, nsub):
    b, n, d = x3.shape
    da, db = ya.shape[2], yb.shape[2]
    per = n // tm
    n_tiles = b * per
    cur = lambda s: jnp.minimum(s, n_tiles - 1)
    prev = lambda s: jnp.maximum(s - 1, 0)
    tok = lambda w, which: pl.BlockSpec((None, tm, w), lambda s: (which(s) // per, which(s) % per, 0))
    const = lambda shape: pl.BlockSpec(shape, lambda s: (0, 0))
    return pl.pallas_call(
        _merge_body,
        grid=(n_tiles + 1,),
        in_specs=[tok(da, cur), tok(db, cur), tok(d, prev), pl.BlockSpec((tm, d), lambda s: (prev(s) % per, 0)),
                  pl.BlockSpec((None, N_MOD, d), lambda s: (prev(s) // per, 0, 0)),
                  const((1, da)), const((1, db)), const((da + db, d)), const((1, d)),
                  const((d, LANES)), const((1, LANES))],
        out_specs=[tok(d, prev), tok(d, prev), tok(LANES, prev),
                   pl.BlockSpec((nsub, SUBLANES, LANES), lambda s: (prev(s), 0, 0)),
                   pl.BlockSpec((nsub, SUBLANES, tm // nsub), lambda s: (prev(s), 0, 0))],
        out_shape=[jax.ShapeDtypeStruct((b, n, d), F32), jax.ShapeDtypeStruct((b, n, d), BF16),
                   jax.ShapeDtypeStruct((b, n, LANES), F32),
                   jax.ShapeDtypeStruct((n_tiles * nsub, SUBLANES, LANES), F32),
                   jax.ShapeDtypeStruct((n_tiles * nsub, SUBLANES, tm // nsub), F32)],
        scratch_shapes=[pltpu.VMEM((da + db, d), BF16), pltpu.VMEM((tm, d), F32), pltpu.VMEM((tm, d), F32)],
        compiler_params=_cparams("arbitrary"),
        name="merge_route",
    )(ya, yb, x3, pe, mod3, ga.reshape(1, da), gb.reshape(1, db), w_out, g2n.reshape(1, d), w_r, b_r)


def _segment_copy(local_ref, lrow, global_ref, grow, rows, sem, to_global):
    lrows = pl.ds(pl.multiple_of(lrow, SEG), rows)
    grows = pl.ds(pl.multiple_of(grow, SEG), rows)
    if to_global:
        return pltpu.make_async_copy(local_ref.at[lrows, :], global_ref.at[grows, :], sem)
    return pltpu.make_async_copy(global_ref.at[grows, :], local_ref.at[lrows, :], sem)


def _start_segments(tab_ref, blk, local_ref, global_ref, sem, to_global):
    def per_expert(e, carry):
        k = (blk * N_EXPERTS + e) * 3
        ls, gs, nchunk = tab_ref[k], tab_ref[k + 1], tab_ref[k + 2]

        def per_chunk(ci, carry2):
            _segment_copy(local_ref, ls + ci * SEG, global_ref, gs + ci * SEG, SEG, sem, to_global).start()
            return carry2

        return lax.fori_loop(0, nchunk, per_chunk, carry)

    lax.fori_loop(0, N_EXPERTS, per_expert, 0)


def _wait_segments(rows, local_ref, global_ref, sem, to_global):
    chunks = rows // SEG
    nbits = (local_ref.shape[0] // SEG).bit_length()
    for k in range(nbits):
        @pl.when(((chunks >> k) & 1) == 1)
        def _():
            _segment_copy(local_ref, 0, global_ref, 0, SEG << k, sem, to_global).wait()


def _dispatch_body(tab_ref, ends_ref, ltot_ref, prow_ref, h2_ref, xs_ref, loc_ref, zero_ref, sem, zsem):
    blk = pl.program_id(0)
    nblk = pl.num_programs(0)
    tb = h2_ref.shape[0]
    lp = loc_ref.shape[1]
    tile = zero_ref.shape[0]
    slot = blk % 2

    @pl.when(blk == 0)
    def _():
        zero_ref[...] = jnp.zeros_like(zero_ref)

        def zero_fill(first_row):
            return pltpu.make_async_copy(zero_ref, xs_ref.at[pl.ds(pl.multiple_of(first_row, SEG), tile), :], zsem)

        def fill(e, start, wait):
            end = ends_ref[e]

            @pl.when(end > start)
            def _():
                cp = zero_fill(end - tile)
                cp.wait() if wait else cp.start()
            return end

        used = lax.fori_loop(0, N_EXPERTS, functools.partial(fill, wait=False), 0)
        spare = (xs_ref.shape[0] - used) // tile

        def fill_spare(j, carry, wait):
            cp = zero_fill(used + j * tile)
            cp.wait() if wait else cp.start()
            return carry

        lax.fori_loop(0, spare, functools.partial(fill_spare, wait=False), 0)
        lax.fori_loop(0, N_EXPERTS, functools.partial(fill, wait=True), 0)
        lax.fori_loop(0, spare, functools.partial(fill_spare, wait=True), 0)

    pos = lax.broadcasted_iota(jnp.int32, (lp, tb), 0).astype(F32)
    onehot = jnp.where((pos == prow_ref[0:1, :]) | (pos == prow_ref[1:2, :]), 1.0, 0.0).astype(BF16)
    loc_ref[slot] = _dot(onehot, h2_ref[...])

    _start_segments(tab_ref, blk, loc_ref.at[slot], xs_ref, sem.at[slot], True)

    @pl.when(blk > 0)
    def _():
        _wait_segments(ltot_ref[blk - 1], loc_ref.at[1 - slot], xs_ref, sem.at[1 - slot], True)

    @pl.when(blk == nblk - 1)
    def _():
        _wait_segments(ltot_ref[blk], loc_ref.at[slot], xs_ref, sem.at[slot], True)


def _dispatch(tab, ends, ltot, prow, h2, slots, tb, lp, tile):
    t, d = h2.shape
    return pl.pallas_call(
        _dispatch_body,
        grid_spec=pltpu.PrefetchScalarGridSpec(
            num_scalar_prefetch=3,
            grid=(t // tb,),
            in_specs=[pl.BlockSpec((None, SUBLANES, tb), lambda i, *_: (i, 0, 0)),
                      pl.BlockSpec((tb, d), lambda i, *_: (i, 0))],
            out_specs=pl.BlockSpec(memory_space=pl.ANY),
            scratch_shapes=[pltpu.VMEM((2, lp, d), F32), pltpu.VMEM((tile, d), F32),
                            pltpu.SemaphoreType.DMA((2,)), pltpu.SemaphoreType.DMA]),
        out_shape=jax.ShapeDtypeStruct((slots, d), F32),
        compiler_params=_cparams("arbitrary"),
        name="moe_dispatch",
    )(tab, ends, ltot, prow, h2)


def _ffn_body(te_ref, nu_ref, nxt_ref, par_ref, xs_ref, wg_hbm, wu_hbm, wd_hbm, ys_ref,
              wg_buf, wu_buf, wd_buf, wgb_ref, wub_ref, wdb_ref, sem):
    i = pl.program_id(0)
    e = te_ref[i]
    first = (i == 0) | (e != te_ref[jnp.maximum(i - 1, 0)])
    active = i < nu_ref[0]

    def fetch(expert, slot):
        return [pltpu.make_async_copy(w.at[expert], buf.at[slot], sem.at[slot])
                for w, buf in ((wg_hbm, wg_buf), (wu_hbm, wu_buf), (wd_hbm, wd_buf))]

    @pl.when(i == 0)
    def _():
        for cp in fetch(e, par_ref[e]):
            cp.start()

    @pl.when(first & active)
    def _():
        slot = par_ref[e]
        for cp in fetch(e, slot):
            cp.wait()
        nxt = nxt_ref[e]

        @pl.when(nxt >= 0)
        def _():
            for cp in fetch(nxt, 1 - slot):
                cp.start()

        wgb_ref[...] = wg_buf[slot].astype(BF16)
        wub_ref[...] = wu_buf[slot].astype(BF16)
        wdb_ref[...] = wd_buf[slot].astype(BF16)

    @pl.when(active)
    def _():
        xb = xs_ref[...].astype(BF16)
        act = _dot(xb, wgb_ref[...])
        act = act * _sigmoid(act) * _dot(xb, wub_ref[...])
        ys_ref[...] = _dot(act.astype(BF16), wdb_ref[...])


def _ffn(tile_expert, n_used, nxt, par, xs, w_gate, w_up, w_down, tile):
    slots, d = xs.shape
    _, _, de = w_gate.shape
    last = lambda i, te, nu, *_: (jnp.minimum(i, nu[0] - 1), 0)
    hbm = pl.BlockSpec(memory_space=pl.ANY)
    return pl.pallas_call(
        _ffn_body,
        grid_spec=pltpu.PrefetchScalarGridSpec(
            num_scalar_prefetch=4,
            grid=(slots // tile,),
            in_specs=[pl.BlockSpec((tile, d), last), hbm, hbm, hbm],
            out_specs=pl.BlockSpec((tile, d), last),
            scratch_shapes=[pltpu.VMEM((2, d, de), F32), pltpu.VMEM((2, d, de), F32), pltpu.VMEM((2, de, d), F32),
                            pltpu.VMEM((d, de), BF16), pltpu.VMEM((d, de), BF16), pltpu.VMEM((de, d), BF16),
                            pltpu.SemaphoreType.DMA((2,))]),
        out_shape=jax.ShapeDtypeStruct((slots, d), F32),
        input_output_aliases={4: 0},
        compiler_params=_cparams("arbitrary"),
        name="moe_ffn",
    )(tile_expert, n_used, nxt, par, xs, w_gate, w_up, w_down)


def _combine_body(tab_ref, ltot_ref, ys_ref, rt_ref, x1_ref, mod_ref, fg_ref, o_ref, loc_ref, sem):
    blk = pl.program_id(0)
    nblk = pl.num_programs(0)
    _, lp, d = loc_ref.shape
    slot = blk % 2

    @pl.when(blk == 0)
    def _():
        _start_segments(tab_ref, blk, loc_ref.at[slot], ys_ref, sem.at[slot], False)

    @pl.when(blk + 1 < nblk)
    def _():
        _start_segments(tab_ref, blk + 1, loc_ref.at[1 - slot], ys_ref, sem.at[1 - slot], False)

    rt = rt_ref[...]
    lane = lax.broadcasted_iota(jnp.int32, rt.shape, 1)
    col = lambda k: jnp.sum(jnp.where(lane == k, rt, 0.0), axis=1, keepdims=True)
    w1, w2, p1, p2 = col(2), col(3), col(4), col(5)
    pos = lax.broadcasted_iota(jnp.int32, (rt.shape[0], lp), 1).astype(F32)
    weights = (jnp.where(pos == p1, w1, 0.0) + jnp.where(pos == p2, w2, 0.0)).astype(BF16)
    _wait_segments(ltot_ref[blk], loc_ref.at[slot], ys_ref, sem.at[slot], False)

    def clear(ci, carry):
        loc_ref[slot, pl.ds(pl.multiple_of(ci * SEG, SEG), SEG), :] = jnp.zeros((SEG, d), F32)
        return carry

    lax.fori_loop(ltot_ref[blk] // SEG, lp // SEG, clear, 0)
    moe = _dot(weights, loc_ref[slot].astype(BF16))
    o_ref[...] = _rms(x1_ref[...] + mod_ref[5:6, :] * moe, fg_ref[...])


def _combine(tab, ltot, ys, rt2, x1, mod3, final_g, tb, lp, per_batch):
    t, d = x1.shape
    return pl.pallas_call(
        _combine_body,
        grid_spec=pltpu.PrefetchScalarGridSpec(
            num_scalar_prefetch=2,
            grid=(t // tb,),
            in_specs=[pl.BlockSpec(memory_space=pl.ANY),
                      pl.BlockSpec((tb, LANES), lambda i, *_: (i, 0)),
                      pl.BlockSpec((tb, d), lambda i, *_: (i, 0)),
                      pl.BlockSpec((None, N_MOD, d), lambda i, *_: (i // per_batch, 0, 0)),
                      pl.BlockSpec((1, d), lambda i, *_: (0, 0))],
            out_specs=pl.BlockSpec((tb, d), lambda i, *_: (i, 0)),
            scratch_shapes=[pltpu.VMEM((2, lp, d), F32), pltpu.SemaphoreType.DMA((2,))]),
        out_shape=jax.ShapeDtypeStruct((t, d), F32),
        compiler_params=_cparams("arbitrary"),
        name="moe_combine",
    )(tab, ltot, ys, rt2, x1, mod3, final_g.reshape(1, d))


def _moe_layout(cnt_blocks, tile, n_tiles):
    nb = cnt_blocks.shape[0]
    lcnt = (cnt_blocks + SEG - 1) // SEG * SEG
    lstart = jnp.cumsum(lcnt, axis=1) - lcnt
    ltot = jnp.sum(lcnt, axis=1)
    per_expert = jnp.sum(lcnt, axis=0)
    tiles_per = (per_expert + tile - 1) // tile
    tile_ends = jnp.cumsum(tiles_per)
    row_ends = tile_ends * tile
    goff = row_ends - tiles_per * tile
    gstart = goff[None, :] + jnp.cumsum(lcnt, axis=0) - lcnt
    tab = jnp.stack([lstart, gstart, lcnt // SEG], axis=-1).reshape(-1).astype(jnp.int32)
    n_used = tile_ends[-1:]
    tile_ids = jnp.arange(n_tiles, dtype=jnp.int32)
    tile_expert = jnp.sum((tile_ends[None, :] <= jnp.minimum(tile_ids, n_used - 1)[:, None]).astype(jnp.int32), axis=1)
    ids = jnp.arange(N_EXPERTS, dtype=jnp.int32)
    used = tiles_per > 0
    par = (jnp.cumsum(used) - used) % 2
    later = jnp.where(used[None, :] & (ids[None, :] > ids[:, None]), ids[None, :], N_EXPERTS)
    nxt = jnp.min(later, axis=1)
    nxt = jnp.where(nxt == N_EXPERTS, -1, nxt)
    return (tab, row_ends.astype(jnp.int32), ltot.astype(jnp.int32), tile_expert.astype(jnp.int32),
            n_used.astype(jnp.int32), nxt.astype(jnp.int32), par.astype(jnp.int32))


def _sincos_table(rows, cols, dim):
    quarter = dim // 4
    omega = 1.0 / (10000.0 ** (np.arange(quarter, dtype=np.float64) / quarter))
    ang_r = np.arange(rows, dtype=np.float64)[:, None] * omega
    ang_c = np.arange(cols, dtype=np.float64)[:, None] * omega
    emb_r = np.concatenate([np.sin(ang_r), np.cos(ang_r)], axis=-1)
    emb_c = np.concatenate([np.sin(ang_c), np.cos(ang_c)], axis=-1)
    pe = np.concatenate([np.broadcast_to(emb_r[:, None, :], (rows, cols, 2 * quarter)),
                         np.broadcast_to(emb_c[None, :, :], (rows, cols, 2 * quarter))], axis=-1)
    return pe.reshape(rows * cols, 4 * quarter).astype(np.float32)


def _filter_features(n, width):
    pos = np.arange(n, dtype=np.float64)
    t = pos / max(n - 1, 1)
    ang = (2.0 * math.pi * pos / n)[:, None] * np.linspace(1e-4, FILTER_BANDS - 1, FILTER_BANDS)[None, :]
    z = np.concatenate([t[:, None], np.cos(ang), -np.sin(ang)], axis=-1)
    return np.pad(z, ((0, 0), (0, width - z.shape[1]))).astype(np.float32)


def _decay_rates(d_h):
    min_decay = math.log(DECAY_TARGET) / DECAY_SLOW_PCT
    max_decay = math.log(DECAY_TARGET) / DECAY_FAST_PCT
    return np.abs(np.linspace(min_decay, max_decay, d_h)).astype(np.float32).reshape(1, d_h)


def _dft_tables(blk):
    odd = 2 * np.arange(blk, dtype=np.int64) + 1
    shifted = ((odd[:, None] * odd[None, :]) % (8 * blk)).astype(np.float64) * (2.0 * math.pi / (8 * blk))
    plain = ((odd[:, None] * np.arange(blk, dtype=np.int64)[None, :]) % (4 * blk)).astype(np.float64) * (2.0 * math.pi / (4 * blk))
    sgn = (1.0 - 2.0 * (np.arange(blk) % 2)).astype(np.float32).reshape(blk, 1)
    f32 = lambda a: a.astype(np.float32)
    return f32(np.cos(shifted)), f32(np.sin(shifted)), f32(np.cos(plain)), f32(np.sin(plain)), sgn


def kernel(x, c, ctx, c_ctx, w_ada, b_ada, norm1_g, w_in, conv_a_w, conv_a_b, lru_wa, lru_ba, lru_wx, lru_bx, lru_lambda, conv_b_w, filt_w1, filt_b1, filt_freq1, filt_w2, filt_b2, filt_freq2, filt_w3, filt_b3, filt_bias, out_norm_a, out_norm_b, w_out, norm2_g, w_rg, b_rg, w_re, b_re, w_gate, w_up, w_down, final_g):
    bsz, n, d = x.shape
    n_ctx = ctx.shape[1]
    d_rnn = conv_a_w.shape[2]
    d_h = filt_bias.shape[2]
    order = filt_bias.shape[1]
    assert w_ada.shape[0] == 1, "single-layer block"
    l = 0
    ch = 256

    mod_rows = 16
    cc = jnp.concatenate([c, c_ctx[None, :], jnp.zeros((mod_rows - bsz - 1, d), F32)], axis=0)
    mod3 = _ada(cc, w_ada[l], b_ada[l]).reshape(mod_rows, N_MOD, d)

    pe = jnp.asarray(_sincos_table(n // GRID_W, GRID_W, d))
    d_in = w_in.shape[2]
    p3 = _inproj_lat(x, pe, mod3, norm1_g[l], w_in[l], 512)
    pc3 = _inproj_ctx(ctx.reshape(bsz * n_ctx, d), mod3, bsz, norm1_g[l], w_in[l], 1, d_rnn,
                      1024).reshape(bsz, n_ctx, d_rnn)

    heads_per_blk = ch // (d_rnn // RNN_HEADS)
    nblk = d_rnn // ch
    eye = jnp.eye(heads_per_blk, dtype=F32)

    def blockdiag(w):
        w5 = w.reshape(2, nblk, heads_per_blk, w.shape[2], w.shape[3])
        return jnp.einsum("dnkij,kl->dnkilj", w5, eye).reshape(2, nblk, ch, ch)

    wa_bd, wx_bd = blockdiag(lru_wa[l]), blockdiag(lru_wx[l])
    w_gates = jnp.concatenate([wa_bd[0], wx_bd[0], wa_bd[1], wx_bd[1]], axis=-1).astype(BF16)
    ba, bx = lru_ba[l].reshape(2, nblk, ch), lru_bx[l].reshape(2, nblk, ch)
    bias = jnp.concatenate([ba[0], bx[0], ba[1], bx[1]], axis=-1).reshape(nblk, 1, 4 * ch)
    sp = jax.nn.softplus(-lru_lambda[l])
    cb = conv_a_b[l].reshape(1, d_rnn)
    zeros_state = jnp.zeros((bsz, 1, d_rnn), F32)
    hf_ctx, hb_ctx = _rglru(pc3, 0, 0, conv_a_w[l], cb, w_gates, bias, sp, zeros_state, zeros_state, False, ch)
    ya, _, _ = _rglru(p3, d_rnn // ch, 0, conv_a_w[l], cb, w_gates, bias, sp, hf_ctx, hb_ctx, True, ch)

    as_bf16 = lambda a: jnp.asarray(a).astype(BF16)
    c3, s3, cu, su, sgn = _dft_tables(CONV_BLOCK)
    fe = 64
    z = jnp.asarray(_filter_features(n, fe))
    w1 = jnp.pad(filt_w1[l], ((0, fe - filt_w1.shape[1]), (0, 0)))
    hid = filt_w2.shape[1]
    hr, hi = _filters(z, w1, filt_b1[l].reshape(1, hid), filt_freq1[l].reshape(1, hid), filt_w2[l],
                      filt_b2[l].reshape(1, hid), filt_freq2[l].reshape(1, hid), filt_w3[l],
                      filt_b3[l].reshape(1, -1), jnp.asarray(_decay_rates(d_h)), as_bf16(cu), as_bf16(su),
                      jnp.asarray(sgn), d_h, order, ch)
    yb = _hyena(p3, 2 * d_rnn // ch, conv_b_w[l], filt_bias[l], as_bf16(c3), as_bf16(s3), hr, hi, d_h, ch)

    w_r = jnp.concatenate([w_re[l], w_rg[l], jnp.zeros((d, LANES - N_EXPERTS - N_GROUPS), F32)], axis=1)
    b_r = jnp.concatenate([b_re[l], b_rg[l], jnp.zeros((LANES - N_EXPERTS - N_GROUPS,), F32)]).reshape(1, LANES)
    tb = 256
    tm = 512
    x1, h2, rt, cnt, prow = _merge(ya, yb, x, pe, mod3, out_norm_a[l], out_norm_b[l], w_out[l], norm2_g[l],
                                   w_r, b_r, tm, tm // tb)

    tile = 512
    t_all = bsz * n
    nb = t_all // tb
    lp = 2 * tb + N_EXPERTS * SEG
    n_tiles = (2 * t_all + nb * N_EXPERTS * (SEG - 1) + N_EXPERTS * (tile - 1)) // tile
    cnt_blocks = cnt[:, 0, :N_EXPERTS].astype(jnp.int32)
    tab, row_ends, ltot, tile_expert, n_used, nxt, par = _moe_layout(cnt_blocks, tile, n_tiles)
    xs = _dispatch(tab, row_ends, ltot, prow, h2.reshape(t_all, d), n_tiles * tile, tb, lp, tile)
    ys = _ffn(tile_expert, n_used, nxt, par, xs, w_gate[l], w_up[l], w_down[l], tile)
    out = _combine(tab, ltot, ys, rt.reshape(t_all, LANES), x1.reshape(t_all, d), mod3, final_g, tb, lp, n // tb)
    return out.reshape(bsz, n, d)
```

```python
import functools
import math

import numpy as np
import jax
import jax.numpy as jnp
from jax import lax
from jax.experimental import pallas as pl
from jax.experimental.pallas import tpu as pltpu

F32 = jnp.float32
BF16 = jnp.bfloat16
EPS = 1e-6
LRU_C = 8.0
N_MOD = 6
GRID_W = 64
RNN_HEADS = 8
N_GROUPS = 4
EXPERTS_PER_GROUP = 8
N_EXPERTS = N_GROUPS * EXPERTS_PER_GROUP
FILTER_BANDS = 16
DECAY_FAST_PCT = 0.3
DECAY_SLOW_PCT = 1.5
DECAY_TARGET = 1e-2
SUBLANES = 8
LANES = 128
SEG = SUBLANES
VMEM_LIMIT = 60 * 1024 * 1024


def _cparams(*sem):
    return pltpu.CompilerParams(dimension_semantics=sem, vmem_limit_bytes=VMEM_LIMIT)


def _dot(a, b):
    return jnp.dot(a, b, preferred_element_type=F32)


def _split(a):
    hi = a.astype(BF16)
    return hi, (a - hi.astype(F32)).astype(BF16)


def _dot3(a, b):
    a_hi, a_lo = _split(a)
    b_hi, b_lo = _split(b)
    return _dot(a_hi, b_hi) + _dot(a_lo, b_hi) + _dot(a_hi, b_lo)


def _rms(v, g):
    return v * lax.rsqrt(jnp.mean(v * v, axis=-1, keepdims=True) + EPS) * g


def _sigmoid(z):
    return 1.0 / (1.0 + jnp.exp(-z))


def _shift_rows(v, d):
    n = v.shape[0]
    rolled = pltpu.roll(v, (-d) % n, 0)
    row = lax.broadcasted_iota(jnp.int32, v.shape, 0)
    ok = (row + d >= 0) & (row + d < n)
    return jnp.where(ok, rolled, 0.0)


def _tile_rot(v, s):
    n, c = v.shape
    return pltpu.roll(v.reshape(n // SUBLANES, SUBLANES, c), s, 1).reshape(n, c)


def _ada_body(c_ref, w_ref, b_ref, o_ref):
    c = c_ref[...]
    o_ref[...] = _dot3(c * _sigmoid(c), w_ref[...]) + b_ref[...]


def _ada(cc, w_ada, b_ada):
    rows, d = cc.shape
    n = w_ada.shape[1]
    tn = 1024
    return pl.pallas_call(
        _ada_body,
        grid=(n // tn,),
        in_specs=[pl.BlockSpec((rows, d), lambda j: (0, 0)),
                  pl.BlockSpec((d, tn), lambda j: (0, j)),
                  pl.BlockSpec((1, tn), lambda j: (0, j))],
        out_specs=pl.BlockSpec((rows, tn), lambda j: (0, j)),
        out_shape=jax.ShapeDtypeStruct((rows, n), F32),
        compiler_params=_cparams("arbitrary"),
        name="ada",
    )(cc, w_ada, b_ada.reshape(1, n))


def _inproj_lat_body(x_ref, pe_ref, mod_ref, g_ref, w_ref, o_ref, wb_ref):
    @pl.when((pl.program_id(0) == 0) & (pl.program_id(1) == 0))
    def _():
        wb_ref[...] = w_ref[...].astype(BF16)

    x = x_ref[...] + pe_ref[...]
    h = _rms(x, g_ref[...]) * (1.0 + mod_ref[1:2, :]) + mod_ref[0:1, :]
    o_ref[...] = _dot(h.astype(BF16), wb_ref[...])


def _inproj_lat(x3, pe, mod3, g, w, tm):
    b, l, d = x3.shape
    n_out = w.shape[1]
    return pl.pallas_call(
        _inproj_lat_body,
        grid=(l // tm, b),
        in_specs=[pl.BlockSpec((None, tm, d), lambda i, bi: (bi, i, 0)),
                  pl.BlockSpec((tm, d), lambda i, bi: (i, 0)),
                  pl.BlockSpec((None, N_MOD, d), lambda i, bi: (bi, 0, 0)),
                  pl.BlockSpec((1, d), lambda i, bi: (0, 0)),
                  pl.BlockSpec((d, n_out), lambda i, bi: (0, 0), pipeline_mode=pl.Buffered(1))],
        out_specs=pl.BlockSpec((None, tm, n_out), lambda i, bi: (bi, i, 0)),
        out_shape=jax.ShapeDtypeStruct((b, l, n_out), F32),
        scratch_shapes=[pltpu.VMEM((d, n_out), BF16)],
        compiler_params=_cparams("arbitrary", "arbitrary"),
        name="inproj_pe",
    )(x3, pe, mod3, g.reshape(1, d), w)


def _inproj_ctx_body(x_ref, mod_ref, g_ref, w_ref, o_ref):
    h = _rms(x_ref[...], g_ref[...]) * (1.0 + mod_ref[1:2, :]) + mod_ref[0:1, :]
    o_ref[...] = _dot(h.astype(BF16), w_ref[...].astype(BF16))


def _inproj_ctx(x2, mod3, mod_row, g, w, col_block, n_out, tm):
    r, d = x2.shape
    return pl.pallas_call(
        _inproj_ctx_body,
        grid=(r // tm,),
        in_specs=[pl.BlockSpec((tm, d), lambda i: (i, 0)),
                  pl.BlockSpec((None, N_MOD, d), lambda i: (mod_row, 0, 0)),
                  pl.BlockSpec((1, d), lambda i: (0, 0)),
                  pl.BlockSpec((d, n_out), lambda i: (0, col_block))],
        out_specs=pl.BlockSpec((tm, n_out), lambda i: (i, 0)),
        out_shape=jax.ShapeDtypeStruct((r, n_out), F32),
        compiler_params=_cparams("arbitrary"),
        name="inproj_ctx",
    )(x2, mod3, g.reshape(1, d), w)


def _rglru_body(*refs, with_gate):
    if with_gate:
        (u_ref, gate_ref, cw_ref, cb_ref, w_ref, bias_ref, sp_ref, h0f_ref, h0b_ref,
         y_ref, hf_end_ref, hb_end_ref, af_ref, bf_ref, ab_ref, bb_ref) = refs
    else:
        (u_ref, cw_ref, cb_ref, w_ref, bias_ref, sp_ref, h0f_ref, h0b_ref,
         hf_end_ref, hb_end_ref, af_ref, bf_ref, ab_ref, bb_ref) = refs
    n, c = u_ref.shape
    u = u_ref[...]
    v = cb_ref[...] + cw_ref[1:2, :] * _shift_rows(u, -1) + cw_ref[0:1, :] * _shift_rows(u, -2)
    v = v + cw_ref[2:3, :] * u + cw_ref[3:4, :] * _shift_rows(u, 1)
    z = _dot(v.astype(BF16), w_ref[...]) + bias_ref[...]
    sub = lax.broadcasted_iota(jnp.int32, (n, c), 0) % SUBLANES

    def local_scan(k, reverse, a_ref, b_ref):
        r = _sigmoid(z[:, (2 * k) * c:(2 * k + 1) * c])
        i = _sigmoid(z[:, (2 * k + 1) * c:(2 * k + 2) * c])
        log_a = (-LRU_C) * r * sp_ref[k:k + 1, :]
        a = jnp.exp(log_a)
        b = jnp.sqrt(1.0 - a * a) * (i * v)
        for s in (1, 2, 4):
            ok = (sub < SUBLANES - s) if reverse else (sub >= s)
            shift = SUBLANES - s if reverse else s
            b = a * jnp.where(ok, _tile_rot(b, shift), 0.0) + b
            a = a * jnp.where(ok, _tile_rot(a, shift), 1.0)
        a_ref[...] = a
        b_ref[...] = b

    local_scan(0, False, af_ref, bf_ref)
    local_scan(1, True, ab_ref, bb_ref)

    n_tiles = n // SUBLANES

    def carry_step(q, carry):
        cf, cb = carry
        rf = pl.ds(pl.multiple_of(q * SUBLANES, SUBLANES), SUBLANES)
        rb = pl.ds(pl.multiple_of((n_tiles - 1 - q) * SUBLANES, SUBLANES), SUBLANES)
        hf = af_ref[rf, :] * cf + bf_ref[rf, :]
        hb = ab_ref[rb, :] * cb + bb_ref[rb, :]
        bf_ref[rf, :] = hf
        bb_ref[rb, :] = hb
        return (jnp.broadcast_to(hf[SUBLANES - 1:SUBLANES, :], (SUBLANES, c)),
                jnp.broadcast_to(hb[0:1, :], (SUBLANES, c)))

    cf0 = jnp.broadcast_to(h0f_ref[...], (SUBLANES, c))
    cb0 = jnp.broadcast_to(h0b_ref[...], (SUBLANES, c))
    cf, cb = lax.fori_loop(0, n_tiles, carry_step, (cf0, cb0), unroll=4)
    hf_end_ref[...] = cf[0:1, :]
    hb_end_ref[...] = cb[0:1, :]
    if with_gate:
        y_ref[...] = jax.nn.gelu(gate_ref[...], approximate=True) * (bf_ref[...] + bb_ref[...])


def _rglru(p3, u_blk0, gate_blk0, cw, cb, w_gates, bias, sp, h0f, h0b, with_gate, ch):
    b, l, _ = p3.shape
    d_rnn = cw.shape[1]
    nh = d_rnn // ch
    in_specs = [pl.BlockSpec((None, l, ch), lambda bi, h: (bi, 0, u_blk0 + h))]
    args = [p3]
    if with_gate:
        in_specs.append(pl.BlockSpec((None, l, ch), lambda bi, h: (bi, 0, gate_blk0 + h)))
        args.append(p3)
    in_specs += [pl.BlockSpec((4, ch), lambda bi, h: (0, h)),
                 pl.BlockSpec((1, ch), lambda bi, h: (0, h)),
                 pl.BlockSpec((None, ch, 4 * ch), lambda bi, h: (h, 0, 0)),
                 pl.BlockSpec((None, 1, 4 * ch), lambda bi, h: (h, 0, 0)),
                 pl.BlockSpec((2, ch), lambda bi, h: (0, h)),
                 pl.BlockSpec((None, 1, ch), lambda bi, h: (bi, 0, h)),
                 pl.BlockSpec((None, 1, ch), lambda bi, h: (bi, 0, h))]
    args += [cw, cb, w_gates, bias, sp, h0f, h0b]
    end_spec = pl.BlockSpec((None, 1, ch), lambda bi, h: (bi, 0, h))
    end_shape = jax.ShapeDtypeStruct((b, 1, d_rnn), F32)
    out_specs = [end_spec, end_spec]
    out_shape = [end_shape, end_shape]
    if with_gate:
        out_specs = [pl.BlockSpec((None, l, ch), lambda bi, h: (bi, 0, h))] + out_specs
        out_shape = [jax.ShapeDtypeStruct((b, l, d_rnn), F32)] + out_shape
    return pl.pallas_call(
        functools.partial(_rglru_body, with_gate=with_gate),
        grid=(b, nh),
        in_specs=in_specs,
        out_specs=out_specs,
        out_shape=out_shape,
        scratch_shapes=[pltpu.VMEM((l, ch), F32)] * 4,
        compiler_params=_cparams("arbitrary", "arbitrary"),
        name="rglru_lat" if with_gate else "rglru_ctx",
    )(*args)


CONV_BLOCK = 512


def _filt_body(z_ref, w1_ref, b1_ref, f1_ref, w2_ref, b2_ref, f2_ref, w3f_ref, w3b_ref, b3f_ref, b3b_ref,
               dec_ref, cu_ref, su_ref, sgn_ref, hr_ref, hi_ref):
    n = z_ref.shape[0]
    blk = cu_ref.shape[0]
    nblk = n // blk
    ch = hr_ref.shape[2]
    hdn = jnp.sin(f1_ref[...] * (_dot3(z_ref[...], w1_ref[...]) + b1_ref[...]))
    hdn = jnp.sin(f2_ref[...] * (_dot3(hdn, w2_ref[...]) + b2_ref[...]))
    decay = jnp.exp(-z_ref[:, 0:1] * dec_ref[...])
    kf = (_dot3(hdn, w3f_ref[...]) + b3f_ref[...]) * decay
    kb = (_dot3(hdn, w3b_ref[...]) + b3b_ref[...]) * decay
    row = lax.broadcasted_iota(jnp.int32, (n, ch), 0)
    kb = jnp.where(row == 0, 0.0, kb)
    norm = jnp.sum(jnp.abs(kf) + jnp.abs(kb), axis=0, keepdims=True)
    scale = (2.0 / (2 * blk)) / norm
    sgn = sgn_ref[...]

    p, q, first = [], [], []
    for c in range(nblk):
        taps = jnp.concatenate([kf[c * blk:(c + 1) * blk], kb[c * blk:(c + 1) * blk]], axis=1).astype(BF16)
        p.append(_dot(cu_ref[...], taps))
        q.append(_dot(su_ref[...], taps))
        first.append(taps[0:1, :].astype(F32))
    fwd = lambda a: a[:, :ch]
    bwd = lambda a: a[:, ch:]
    hr_ref[nblk - 1] = (fwd(p[0]) + bwd(p[0])) * scale
    hi_ref[nblk - 1] = (bwd(q[0]) - fwd(q[0])) * scale
    for c in range(1, nblk):
        hr_ref[nblk - 1 + c] = (fwd(p[c]) + sgn * fwd(q[c - 1])) * scale
        hi_ref[nblk - 1 + c] = (sgn * (fwd(p[c - 1]) - fwd(first[c - 1])) - fwd(q[c])) * scale
        hr_ref[nblk - 1 - c] = (bwd(p[c]) + sgn * bwd(q[c - 1])) * scale
        hi_ref[nblk - 1 - c] = (bwd(q[c]) - sgn * (bwd(p[c - 1]) - bwd(first[c - 1]))) * scale


def _filters(z, w1, b1, f1, w2, b2, f2, w3, b3, deltas, cu, su, sgn, d_h, order, ch):
    n, fe = z.shape
    hid = w2.shape[0]
    blk = cu.shape[0]
    nh = 2 * (n // blk) - 1
    per_order = d_h // ch
    ncol = order * per_order
    const = lambda shape: pl.BlockSpec(shape, lambda g: (0, 0))
    hspec = pl.BlockSpec((nh, blk, ch), lambda g: (0, 0, g))
    return pl.pallas_call(
        _filt_body,
        grid=(ncol,),
        in_specs=[const((n, fe)), const((fe, hid)), const((1, hid)), const((1, hid)),
                  const((hid, hid)), const((1, hid)), const((1, hid)),
                  pl.BlockSpec((hid, ch), lambda g: (0, g)),
                  pl.BlockSpec((hid, ch), lambda g: (0, ncol + g)),
                  pl.BlockSpec((1, ch), lambda g: (0, g)),
                  pl.BlockSpec((1, ch), lambda g: (0, ncol + g)),
                  pl.BlockSpec((1, ch), lambda g: (0, g % per_order)),
                  const((blk, blk)), const((blk, blk)), const((blk, 1))],
        out_specs=[hspec, hspec],
        out_shape=[jax.ShapeDtypeStruct((nh, blk, order * d_h), F32)] * 2,
        compiler_params=_cparams("arbitrary"),
        name="hyena_filters",
    )(z, w1, b1, f1, w2, b2, f2, w3, w3, b3, b3, deltas, cu, su, sgn)


def _hyena_body(v_ref, x1_ref, x2_ref, wv_ref, w1_ref, w2_ref, fb_ref, c3_ref, s3_ref,
                h0r_ref, h0i_ref, h1r_ref, h1i_ref, o_ref, z_ref, zb_ref, p_ref, q_ref, yr_ref, yi_ref):
    n = o_ref.shape[0]
    blk = c3_ref.shape[0]
    nblk = n // blk
    halo = SUBLANES
    sub = 32

    def conv3(src_ref, w_ref, r0):
        lo, hi = max(r0 - halo, 0), min(r0 + blk + halo, n)
        win = src_ref[lo:hi, :]
        size = hi - lo
        row = lax.broadcasted_iota(jnp.int32, win.shape, 0)
        prev = pltpu.roll(win, 1, 0)
        nxt = pltpu.roll(win, size - 1, 0)
        if lo == 0:
            prev = jnp.where(row == 0, 0.0, prev)
        if hi == n:
            nxt = jnp.where(row == size - 1, 0.0, nxt)
        out = w_ref[0:1, :] * prev + w_ref[1:2, :] * win + w_ref[2:3, :] * nxt
        return out[r0 - lo:r0 - lo + blk, :]

    def long_conv(hr_ref, hi_ref, fb, gate_ref, gate_w_ref, dst_ref):
        for i in range(nblk):
            rows = slice(i * blk, (i + 1) * blk)
            p_ref[rows, :] = _dot(c3_ref[...], zb_ref[rows, :])
            q_ref[rows, :] = _dot(s3_ref[...], zb_ref[rows, :])

        def products(rc, carry):
            r0 = pl.multiple_of(rc * sub, sub)
            for j in range(nblk):
                yr = yi = None
                for i in range(nblk):
                    rows = pl.ds(i * blk + r0, sub)
                    p, q = p_ref[rows, :], q_ref[rows, :]
                    gr = hr_ref[nblk - 1 + j - i, pl.ds(r0, sub), :]
                    gi = hi_ref[nblk - 1 + j - i, pl.ds(r0, sub), :]
                    tr, ti = p * gr + q * gi, q * gr - p * gi
                    yr, yi = (tr, ti) if yr is None else (yr + tr, yi + ti)
                yr_ref[pl.ds(j * blk + r0, sub), :] = yr.astype(BF16)
                yi_ref[pl.ds(j * blk + r0, sub), :] = yi.astype(BF16)
            return carry

        lax.fori_loop(0, blk // sub, products, 0)
        for j in range(nblk):
            rows = slice(j * blk, (j + 1) * blk)
            y = _dot(c3_ref[...], yr_ref[rows, :]) + _dot(s3_ref[...], yi_ref[rows, :])
            dst_ref[rows, :] = conv3(gate_ref, gate_w_ref, j * blk) * (y + z_ref[rows, :] * fb)

    for r0 in range(0, n, blk):
        z_ref[r0:r0 + blk, :] = conv3(v_ref, wv_ref, r0)
    zb_ref[...] = z_ref[...].astype(BF16)
    long_conv(h0r_ref, h0i_ref, fb_ref[0:1, :], x1_ref, w1_ref, z_ref)
    zb_ref[...] = z_ref[...].astype(BF16)
    long_conv(h1r_ref, h1i_ref, fb_ref[1:2, :], x2_ref, w2_ref, o_ref)


def _hyena(p3, col_blk0, conv_w, fbias, c3, s3, hr, hi, d_h, ch):
    b, n, _ = p3.shape
    nh, blk, _ = hr.shape
    per = d_h // ch
    zspec = lambda k: pl.BlockSpec((None, n, ch), lambda h, bi: (bi, 0, col_blk0 + k * per + h))
    wspec = lambda k: pl.BlockSpec((3, ch), lambda h, bi: (0, k * per + h))
    hspec = lambda o: pl.BlockSpec((nh, blk, ch), lambda h, bi: (0, 0, o * per + h), pipeline_mode=pl.Buffered(1))
    const = pl.BlockSpec((blk, blk), lambda h, bi: (0, 0))
    return pl.pallas_call(
        _hyena_body,
        grid=(per, b),
        in_specs=[zspec(0), zspec(1), zspec(2), wspec(0), wspec(1), wspec(2),
                  pl.BlockSpec((2, ch), lambda h, bi: (0, h)), const, const,
                  hspec(0), hspec(0), hspec(1), hspec(1)],
        out_specs=pl.BlockSpec((None, n, ch), lambda h, bi: (bi, 0, h)),
        out_shape=jax.ShapeDtypeStruct((b, n, d_h), F32),
        scratch_shapes=[pltpu.VMEM((n, ch), F32), pltpu.VMEM((n, ch), BF16),
                        pltpu.VMEM((n, ch), F32), pltpu.VMEM((n, ch), F32),
                        pltpu.VMEM((n, ch), BF16), pltpu.VMEM((n, ch), BF16)],
        compiler_params=_cparams("arbitrary", "arbitrary"),
        name="hyena_mix",
    )(p3, p3, p3, conv_w, conv_w, conv_w, fbias, c3, s3, hr, hi, hr, hi)


def _merge_body(ya_ref, yb_ref, x_ref, pe_ref, mod_ref, ga_ref, gb_ref, wo_ref, g2_ref, wr_ref, br_ref,
                x1_ref, h2_ref, rt_ref, cnt_ref, prow_ref, wob_ref):
    @pl.when((pl.program_id(0) == 0) & (pl.program_id(1) == 0))
    def _():
        wob_ref[...] = wo_ref[...].astype(BF16)

    da = ya_ref.shape[1]
    na = _rms(ya_ref[...], ga_ref[...]).astype(BF16)
    nb = _rms(yb_ref[...], gb_ref[...]).astype(BF16)
    y = _dot(na, wob_ref[0:da, :]) + _dot(nb, wob_ref[da:, :])
    _route(y, x_ref, pe_ref, mod_ref, g2_ref, wr_ref, br_ref, x1_ref, h2_ref, rt_ref, cnt_ref, prow_ref)


def _route(y, x_ref, pe_ref, mod_ref, g2_ref, wr_ref, br_ref, x1_ref, h2_ref, rt_ref, cnt_ref, prow_ref):
    x1 = x_ref[...] + pe_ref[...] + mod_ref[2:3, :] * y
    x1_ref[...] = x1
    h2 = _rms(x1, g2_ref[...]) * (1.0 + mod_ref[4:5, :]) + mod_ref[3:4, :]
    h2_ref[...] = h2.astype(BF16)

    logits = _dot3(h2, wr_ref[...]) + br_ref[...]
    lane = lax.broadcasted_iota(jnp.int32, logits.shape, 1).astype(F32)
    neg = -jnp.inf
    big = jnp.float32(1 << 20)
    gl = jnp.where((lane >= N_EXPERTS) & (lane < N_EXPERTS + N_GROUPS), logits, neg)
    gmax = jnp.max(gl, axis=1, keepdims=True)
    g_p = 1.0 / jnp.sum(jnp.exp(gl - gmax), axis=1, keepdims=True)
    g_i = jnp.min(jnp.where(gl == gmax, lane, big), axis=1, keepdims=True) - N_EXPERTS
    lo = g_i * EXPERTS_PER_GROUP
    el = jnp.where((lane >= lo) & (lane < lo + EXPERTS_PER_GROUP), logits, neg)
    m1 = jnp.max(el, axis=1, keepdims=True)
    i1 = jnp.min(jnp.where(el == m1, lane, big), axis=1, keepdims=True)
    el2 = jnp.where(lane == i1, neg, el)
    m2 = jnp.max(el2, axis=1, keepdims=True)
    i2 = jnp.min(jnp.where(el2 == m2, lane, big), axis=1, keepdims=True)
    e2 = jnp.exp(m2 - m1)
    w1 = g_p / (1.0 + e2)
    w2 = g_p * e2 / (1.0 + e2)
    o1 = jnp.where(lane == i1, 1.0, 0.0)
    o2 = jnp.where(lane == i2, 1.0, 0.0)
    nsub = cnt_ref.shape[0]
    sub = o1.shape[0] // nsub
    sq = lambda shape, d: lax.broadcasted_iota(jnp.int32, shape, d)
    earlier = jnp.where(sq((sub, sub), 1) < sq((sub, sub), 0), 1.0, 0.0).astype(BF16)
    below = jnp.where(sq((LANES, LANES), 0) < sq((LANES, LANES), 1), 1.0, 0.0).astype(BF16)
    pick_row = jnp.where(sq((SUBLANES, LANES), 0) == sq((SUBLANES, LANES), 1), 1.0, 0.0).astype(BF16)
    row_id = sq((SUBLANES, sub), 0)
    pos1, pos2 = [], []
    for k in range(nsub):
        a1, a2 = o1[k * sub:(k + 1) * sub], o2[k * sub:(k + 1) * sub]
        both = a1 + a2
        count = jnp.sum(both, axis=0, keepdims=True)
        cnt_ref[k] = jnp.broadcast_to(count, cnt_ref.shape[1:])
        runs = jnp.broadcast_to(jnp.floor((count + (SEG - 1)) / SEG), (SUBLANES, LANES))
        start = _dot(runs.astype(BF16), below)[0:1, :] * SEG
        before = _dot(earlier, both.astype(BF16)) + start
        p1 = jnp.sum(a1 * before, axis=1, keepdims=True)
        p2 = jnp.sum(a2 * before, axis=1, keepdims=True)
        pos1.append(p1)
        pos2.append(p2)
        lane_k = sq((sub, LANES), 1).astype(F32)
        hi1, hi2 = jnp.floor(p1 / 64.0), jnp.floor(p2 / 64.0)
        parts = (jnp.where(lane_k == 0.0, hi1, 0.0) + jnp.where(lane_k == 1.0, p1 - 64.0 * hi1, 0.0)
                 + jnp.where(lane_k == 2.0, hi2, 0.0) + jnp.where(lane_k == 3.0, p2 - 64.0 * hi2, 0.0))
        rows = lax.dot_general(pick_row, parts.astype(BF16), (((1,), (1,)), ((), ())), preferred_element_type=F32)
        q1 = rows[0:1, :] * 64.0 + rows[1:2, :]
        q2 = rows[2:3, :] * 64.0 + rows[3:4, :]
        prow_ref[k] = jnp.where(row_id == 0, q1, jnp.where(row_id == 1, q2, 0.0))
    p1 = jnp.concatenate(pos1, axis=0)
    p2 = jnp.concatenate(pos2, axis=0)
    rt_ref[...] = (jnp.where(lane == 0.0, i1, 0.0) + jnp.where(lane == 1.0, i2, 0.0)
                   + jnp.where(lane == 2.0, w1, 0.0) + jnp.where(lane == 3.0, w2, 0.0)
                   + jnp.where(lane == 4.0, p1, 0.0) + jnp.where(lane == 5.0, p2, 0.0))


def _merge(ya, yb, x3, pe, mod3, ga, gb, w_out, g2n, w_r, b_r, tm, nsub):
    b, n, d = x3.shape
    da, db = ya.shape[2], yb.shape[2]
    per = n // tm
    tok = lambda w: pl.BlockSpec((None, tm, w), lambda i, bi: (bi, i, 0))
    const = lambda shape: pl.BlockSpec(shape, lambda i, bi: (0, 0))
    blk = lambda w: pl.BlockSpec((nsub, SUBLANES, w), lambda i, bi: (bi * per + i, 0, 0))
    return pl.pallas_call(
        _merge_body,
        grid=(per, b),
        in_specs=[tok(da), tok(db), tok(d), pl.BlockSpec((tm, d), lambda i, bi: (i, 0)),
                  pl.BlockSpec((None, N_MOD, d), lambda i, bi: (bi, 0, 0)),
                  const((1, da)), const((1, db)), const((da + db, d)), const((1, d)),
                  const((d, LANES)), const((1, LANES))],
        out_specs=[tok(d), tok(d), tok(LANES), blk(LANES), blk(tm // nsub)],
        out_shape=[jax.ShapeDtypeStruct((b, n, d), F32), jax.ShapeDtypeStruct((b, n, d), BF16),
                   jax.ShapeDtypeStruct((b, n, LANES), F32),
                   jax.ShapeDtypeStruct((b * per * nsub, SUBLANES, LANES), F32),
                   jax.ShapeDtypeStruct((b * per * nsub, SUBLANES, tm // nsub), F32)],
        scratch_shapes=[pltpu.VMEM((da + db, d), BF16)],
        compiler_params=_cparams("arbitrary", "arbitrary"),
        name="merge_route",
    )(ya, yb, x3, pe, mod3, ga.reshape(1, da), gb.reshape(1, db), w_out, g2n.reshape(1, d), w_r, b_r)


def _segment_copy(local_ref, lrow, global_ref, grow, rows, sem, to_global):
    lrows = pl.ds(pl.multiple_of(lrow, SEG), rows)
    grows = pl.ds(pl.multiple_of(grow, SEG), rows)
    if to_global:
        return pltpu.make_async_copy(local_ref.at[lrows, :], global_ref.at[grows, :], sem)
    return pltpu.make_async_copy(global_ref.at[grows, :], local_ref.at[lrows, :], sem)


def _start_segments(tab_ref, blk, local_ref, global_ref, sem, to_global):
    def per_expert(e, carry):
        k = (blk * N_EXPERTS + e) * 3
        ls, gs, nchunk = tab_ref[k], tab_ref[k + 1], tab_ref[k + 2]

        def per_chunk(ci, carry2):
            _segment_copy(local_ref, ls + ci * SEG, global_ref, gs + ci * SEG, SEG, sem, to_global).start()
            return carry2

        return lax.fori_loop(0, nchunk, per_chunk, carry)

    lax.fori_loop(0, N_EXPERTS, per_expert, 0)


def _wait_segments(rows, local_ref, global_ref, sem, to_global):
    chunks = rows // SEG
    nbits = (local_ref.shape[0] // SEG).bit_length()
    for k in range(nbits):
        @pl.when(((chunks >> k) & 1) == 1)
        def _():
            _segment_copy(local_ref, 0, global_ref, 0, SEG << k, sem, to_global).wait()


def _dispatch_body(tab_ref, ends_ref, ltot_ref, prow_ref, h2_ref, xs_ref, loc_ref, zero_ref, sem, zsem):
    blk = pl.program_id(0)
    nblk = pl.num_programs(0)
    tb = h2_ref.shape[0]
    lp = loc_ref.shape[1]
    tile = zero_ref.shape[0]
    slot = blk % 2

    @pl.when(blk == 0)
    def _():
        zero_ref[...] = jnp.zeros_like(zero_ref)

        def zero_fill(first_row, rows):
            dst = xs_ref.at[pl.ds(pl.multiple_of(first_row, SEG), rows), :]
            return pltpu.make_async_copy(zero_ref.at[pl.ds(0, rows), :], dst, zsem)

        pad = 64

        def fill(e, carry, wait):
            end, real = ends_ref[e], ends_ref[N_EXPERTS + e]
            first = real // pad * pad

            def one(j, carry2):
                cp = zero_fill(first + j * pad, pad)
                cp.wait() if wait else cp.start()
                return carry2

            return lax.fori_loop(0, (end - first) // pad, one, carry)

        lax.fori_loop(0, N_EXPERTS, functools.partial(fill, wait=False), 0)
        used = ends_ref[N_EXPERTS - 1]
        spare = (xs_ref.shape[0] - used) // tile

        def fill_spare(j, carry, wait):
            cp = zero_fill(used + j * tile, tile)
            cp.wait() if wait else cp.start()
            return carry

        lax.fori_loop(0, spare, functools.partial(fill_spare, wait=False), 0)
        lax.fori_loop(0, N_EXPERTS, functools.partial(fill, wait=True), 0)
        lax.fori_loop(0, spare, functools.partial(fill_spare, wait=True), 0)

    pos = lax.broadcasted_iota(jnp.int32, (lp, tb), 0).astype(F32)
    onehot = jnp.where((pos == prow_ref[0:1, :]) | (pos == prow_ref[1:2, :]), 1.0, 0.0).astype(BF16)
    loc_ref[slot] = _dot(onehot, h2_ref[...])

    _start_segments(tab_ref, blk, loc_ref.at[slot], xs_ref, sem.at[slot], True)

    @pl.when(blk > 0)
    def _():
        _wait_segments(ltot_ref[blk - 1], loc_ref.at[1 - slot], xs_ref, sem.at[1 - slot], True)

    @pl.when(blk == nblk - 1)
    def _():
        _wait_segments(ltot_ref[blk], loc_ref.at[slot], xs_ref, sem.at[slot], True)


def _dispatch(tab, ends, ltot, prow, h2, slots, tb, lp, tile):
    t, d = h2.shape
    return pl.pallas_call(
        _dispatch_body,
        grid_spec=pltpu.PrefetchScalarGridSpec(
            num_scalar_prefetch=3,
            grid=(t // tb,),
            in_specs=[pl.BlockSpec((None, SUBLANES, tb), lambda i, *_: (i, 0, 0)),
                      pl.BlockSpec((tb, d), lambda i, *_: (i, 0))],
            out_specs=pl.BlockSpec(memory_space=pl.ANY),
            scratch_shapes=[pltpu.VMEM((2, lp, d), F32), pltpu.VMEM((tile, d), F32),
                            pltpu.SemaphoreType.DMA((2,)), pltpu.SemaphoreType.DMA]),
        out_shape=jax.ShapeDtypeStruct((slots, d), F32),
        compiler_params=_cparams("arbitrary"),
        name="moe_dispatch",
    )(tab, ends, ltot, prow, h2)


def _ffn_body(te_ref, nu_ref, nxt_ref, par_ref, xs_ref, wg_hbm, wu_hbm, wd_hbm, ys_ref,
              wg_buf, wu_buf, wd_buf, wgb_ref, wub_ref, wdb_ref, sem):
    i = pl.program_id(0)
    e = te_ref[i]
    first = (i == 0) | (e != te_ref[jnp.maximum(i - 1, 0)])
    active = i < nu_ref[0]

    def fetch(expert, slot):
        return [pltpu.make_async_copy(w.at[expert], buf.at[slot], sem.at[slot])
                for w, buf in ((wg_hbm, wg_buf), (wu_hbm, wu_buf), (wd_hbm, wd_buf))]

    @pl.when(i == 0)
    def _():
        for cp in fetch(e, par_ref[e]):
            cp.start()

    @pl.when(first & active)
    def _():
        slot = par_ref[e]
        for cp in fetch(e, slot):
            cp.wait()
        nxt = nxt_ref[e]

        @pl.when(nxt >= 0)
        def _():
            for cp in fetch(nxt, 1 - slot):
                cp.start()

        wgb_ref[...] = wg_buf[slot].astype(BF16)
        wub_ref[...] = wu_buf[slot].astype(BF16)
        wdb_ref[...] = wd_buf[slot].astype(BF16)

    @pl.when(active)
    def _():
        xb = xs_ref[...].astype(BF16)
        act = _dot(xb, wgb_ref[...])
        act = act * _sigmoid(act) * _dot(xb, wub_ref[...])
        ys_ref[...] = _dot(act.astype(BF16), wdb_ref[...])


def _ffn(tile_expert, n_used, nxt, par, xs, w_gate, w_up, w_down, tile):
    slots, d = xs.shape
    _, _, de = w_gate.shape
    last = lambda i, te, nu, *_: (jnp.minimum(i, nu[0] - 1), 0)
    hbm = pl.BlockSpec(memory_space=pl.ANY)
    return pl.pallas_call(
        _ffn_body,
        grid_spec=pltpu.PrefetchScalarGridSpec(
            num_scalar_prefetch=4,
            grid=(slots // tile,),
            in_specs=[pl.BlockSpec((tile, d), last), hbm, hbm, hbm],
            out_specs=pl.BlockSpec((tile, d), last),
            scratch_shapes=[pltpu.VMEM((2, d, de), F32), pltpu.VMEM((2, d, de), F32), pltpu.VMEM((2, de, d), F32),
                            pltpu.VMEM((d, de), BF16), pltpu.VMEM((d, de), BF16), pltpu.VMEM((de, d), BF16),
                            pltpu.SemaphoreType.DMA((2,))]),
        out_shape=jax.ShapeDtypeStruct((slots, d), F32),
        input_output_aliases={4: 0},
        compiler_params=_cparams("arbitrary"),
        name="moe_ffn",
    )(tile_expert, n_used, nxt, par, xs, w_gate, w_up, w_down)


def _combine_body(tab_ref, ltot_ref, ys_ref, rt_ref, x1_ref, mod_ref, fg_ref, o_ref, loc_ref, sem):
    blk = pl.program_id(0)
    nblk = pl.num_programs(0)
    _, lp, d = loc_ref.shape
    slot = blk % 2

    @pl.when(blk == 0)
    def _():
        _start_segments(tab_ref, blk, loc_ref.at[slot], ys_ref, sem.at[slot], False)

    @pl.when(blk + 1 < nblk)
    def _():
        _start_segments(tab_ref, blk + 1, loc_ref.at[1 - slot], ys_ref, sem.at[1 - slot], False)

    rt = rt_ref[...]
    lane = lax.broadcasted_iota(jnp.int32, rt.shape, 1)
    col = lambda k: jnp.sum(jnp.where(lane == k, rt, 0.0), axis=1, keepdims=True)
    w1, w2, p1, p2 = col(2), col(3), col(4), col(5)
    pos = lax.broadcasted_iota(jnp.int32, (rt.shape[0], lp), 1).astype(F32)
    weights = (jnp.where(pos == p1, w1, 0.0) + jnp.where(pos == p2, w2, 0.0)).astype(BF16)
    _wait_segments(ltot_ref[blk], loc_ref.at[slot], ys_ref, sem.at[slot], False)

    def clear(ci, carry):
        loc_ref[slot, pl.ds(pl.multiple_of(ci * SEG, SEG), SEG), :] = jnp.zeros((SEG, d), F32)
        return carry

    lax.fori_loop(ltot_ref[blk] // SEG, lp // SEG, clear, 0)
    moe = _dot(weights, loc_ref[slot].astype(BF16))
    o_ref[...] = _rms(x1_ref[...] + mod_ref[5:6, :] * moe, fg_ref[...])


def _combine(tab, ltot, ys, rt2, x1, mod3, final_g, tb, lp, per_batch):
    t, d = x1.shape
    return pl.pallas_call(
        _combine_body,
        grid_spec=pltpu.PrefetchScalarGridSpec(
            num_scalar_prefetch=2,
            grid=(t // tb,),
            in_specs=[pl.BlockSpec(memory_space=pl.ANY),
                      pl.BlockSpec((tb, LANES), lambda i, *_: (i, 0)),
                      pl.BlockSpec((tb, d), lambda i, *_: (i, 0)),
                      pl.BlockSpec((None, N_MOD, d), lambda i, *_: (i // per_batch, 0, 0)),
                      pl.BlockSpec((1, d), lambda i, *_: (0, 0))],
            out_specs=pl.BlockSpec((tb, d), lambda i, *_: (i, 0)),
            scratch_shapes=[pltpu.VMEM((2, lp, d), F32), pltpu.SemaphoreType.DMA((2,))]),
        out_shape=jax.ShapeDtypeStruct((t, d), F32),
        compiler_params=_cparams("arbitrary"),
        name="moe_combine",
    )(tab, ltot, ys, rt2, x1, mod3, final_g.reshape(1, d))


def _moe_layout(cnt_blocks, tile, n_tiles):
    lcnt = (cnt_blocks + SEG - 1) // SEG * SEG
    lstart = jnp.cumsum(lcnt, axis=1) - lcnt
    ltot = jnp.sum(lcnt, axis=1)
    per_expert = jnp.sum(lcnt, axis=0)
    tiles_per = (per_expert + tile - 1) // tile
    tile_ends = jnp.cumsum(tiles_per)
    row_ends = tile_ends * tile
    goff = row_ends - tiles_per * tile
    gstart = goff[None, :] + jnp.cumsum(lcnt, axis=0) - lcnt
    tab = jnp.stack([lstart, gstart, lcnt // SEG], axis=-1).reshape(-1).astype(jnp.int32)
    n_used = tile_ends[-1:]
    tile_ids = jnp.arange(n_tiles, dtype=jnp.int32)
    tile_expert = jnp.sum((tile_ends[None, :] <= jnp.minimum(tile_ids, n_used - 1)[:, None]).astype(jnp.int32), axis=1)
    ids = jnp.arange(N_EXPERTS, dtype=jnp.int32)
    used = tiles_per > 0
    par = (jnp.cumsum(used) - used) % 2
    later = jnp.where(used[None, :] & (ids[None, :] > ids[:, None]), ids[None, :], N_EXPERTS)
    nxt = jnp.min(later, axis=1)
    nxt = jnp.where(nxt == N_EXPERTS, -1, nxt)
    ends = jnp.concatenate([row_ends, goff + per_expert])
    return (tab, ends.astype(jnp.int32), ltot.astype(jnp.int32), tile_expert.astype(jnp.int32),
            n_used.astype(jnp.int32), nxt.astype(jnp.int32), par.astype(jnp.int32))


def _sincos_table(rows, cols, dim):
    quarter = dim // 4
    omega = 1.0 / (10000.0 ** (np.arange(quarter, dtype=np.float64) / quarter))
    ang_r = np.arange(rows, dtype=np.float64)[:, None] * omega
    ang_c = np.arange(cols, dtype=np.float64)[:, None] * omega
    emb_r = np.concatenate([np.sin(ang_r), np.cos(ang_r)], axis=-1)
    emb_c = np.concatenate([np.sin(ang_c), np.cos(ang_c)], axis=-1)
    pe = np.concatenate([np.broadcast_to(emb_r[:, None, :], (rows, cols, 2 * quarter)),
                         np.broadcast_to(emb_c[None, :, :], (rows, cols, 2 * quarter))], axis=-1)
    return pe.reshape(rows * cols, 4 * quarter).astype(np.float32)


def _filter_features(n, width):
    pos = np.arange(n, dtype=np.float64)
    t = pos / max(n - 1, 1)
    ang = (2.0 * math.pi * pos / n)[:, None] * np.linspace(1e-4, FILTER_BANDS - 1, FILTER_BANDS)[None, :]
    z = np.concatenate([t[:, None], np.cos(ang), -np.sin(ang)], axis=-1)
    return np.pad(z, ((0, 0), (0, width - z.shape[1]))).astype(np.float32)


def _decay_rates(d_h):
    min_decay = math.log(DECAY_TARGET) / DECAY_SLOW_PCT
    max_decay = math.log(DECAY_TARGET) / DECAY_FAST_PCT
    return np.abs(np.linspace(min_decay, max_decay, d_h)).astype(np.float32).reshape(1, d_h)


def _dft_tables(blk):
    odd = 2 * np.arange(blk, dtype=np.int64) + 1
    shifted = ((odd[:, None] * odd[None, :]) % (8 * blk)).astype(np.float64) * (2.0 * math.pi / (8 * blk))
    plain = ((odd[:, None] * np.arange(blk, dtype=np.int64)[None, :]) % (4 * blk)).astype(np.float64) * (2.0 * math.pi / (4 * blk))
    sgn = (1.0 - 2.0 * (np.arange(blk) % 2)).astype(np.float32).reshape(blk, 1)
    f32 = lambda a: a.astype(np.float32)
    return f32(np.cos(shifted)), f32(np.sin(shifted)), f32(np.cos(plain)), f32(np.sin(plain)), sgn


def kernel(x, c, ctx, c_ctx, w_ada, b_ada, norm1_g, w_in, conv_a_w, conv_a_b, lru_wa, lru_ba, lru_wx, lru_bx, lru_lambda, conv_b_w, filt_w1, filt_b1, filt_freq1, filt_w2, filt_b2, filt_freq2, filt_w3, filt_b3, filt_bias, out_norm_a, out_norm_b, w_out, norm2_g, w_rg, b_rg, w_re, b_re, w_gate, w_up, w_down, final_g):
    bsz, n, d = x.shape
    n_ctx = ctx.shape[1]
    d_rnn = conv_a_w.shape[2]
    d_h = filt_bias.shape[2]
    order = filt_bias.shape[1]
    assert w_ada.shape[0] == 1, "single-layer block"
    l = 0
    ch = 256

    mod_rows = 16
    cc = jnp.concatenate([c, c_ctx[None, :], jnp.zeros((mod_rows - bsz - 1, d), F32)], axis=0)
    mod3 = _ada(cc, w_ada[l], b_ada[l]).reshape(mod_rows, N_MOD, d)

    pe = jnp.asarray(_sincos_table(n // GRID_W, GRID_W, d))
    p3 = _inproj_lat(x, pe, mod3, norm1_g[l], w_in[l], 512)
    pc3 = _inproj_ctx(ctx.reshape(bsz * n_ctx, d), mod3, bsz, norm1_g[l], w_in[l], 1, d_rnn,
                      1024).reshape(bsz, n_ctx, d_rnn)

    heads_per_blk = ch // (d_rnn // RNN_HEADS)
    nblk = d_rnn // ch
    eye = jnp.eye(heads_per_blk, dtype=F32)

    def blockdiag(w):
        w5 = w.reshape(2, nblk, heads_per_blk, w.shape[2], w.shape[3])
        return jnp.einsum("dnkij,kl->dnkilj", w5, eye).reshape(2, nblk, ch, ch)

    wa_bd, wx_bd = blockdiag(lru_wa[l]), blockdiag(lru_wx[l])
    w_gates = jnp.concatenate([wa_bd[0], wx_bd[0], wa_bd[1], wx_bd[1]], axis=-1).astype(BF16)
    ba, bx = lru_ba[l].reshape(2, nblk, ch), lru_bx[l].reshape(2, nblk, ch)
    bias = jnp.concatenate([ba[0], bx[0], ba[1], bx[1]], axis=-1).reshape(nblk, 1, 4 * ch)
    sp = jax.nn.softplus(-lru_lambda[l])
    cb = conv_a_b[l].reshape(1, d_rnn)
    zeros_state = jnp.zeros((bsz, 1, d_rnn), F32)
    hf_ctx, hb_ctx = _rglru(pc3, 0, 0, conv_a_w[l], cb, w_gates, bias, sp, zeros_state, zeros_state, False, ch)
    ya, _, _ = _rglru(p3, d_rnn // ch, 0, conv_a_w[l], cb, w_gates, bias, sp, hf_ctx, hb_ctx, True, ch)

    as_bf16 = lambda a: jnp.asarray(a).astype(BF16)
    c3, s3, cu, su, sgn = _dft_tables(CONV_BLOCK)
    fe = 64
    z = jnp.asarray(_filter_features(n, fe))
    w1 = jnp.pad(filt_w1[l], ((0, fe - filt_w1.shape[1]), (0, 0)))
    hid = filt_w2.shape[1]
    hr, hi = _filters(z, w1, filt_b1[l].reshape(1, hid), filt_freq1[l].reshape(1, hid), filt_w2[l],
                      filt_b2[l].reshape(1, hid), filt_freq2[l].reshape(1, hid), filt_w3[l],
                      filt_b3[l].reshape(1, -1), jnp.asarray(_decay_rates(d_h)), as_bf16(cu), as_bf16(su),
                      jnp.asarray(sgn), d_h, order, ch)
    yb = _hyena(p3, 2 * d_rnn // ch, conv_b_w[l], filt_bias[l], as_bf16(c3), as_bf16(s3), hr, hi, d_h, ch)

    w_r = jnp.concatenate([w_re[l], w_rg[l], jnp.zeros((d, LANES - N_EXPERTS - N_GROUPS), F32)], axis=1)
    b_r = jnp.concatenate([b_re[l], b_rg[l], jnp.zeros((LANES - N_EXPERTS - N_GROUPS,), F32)]).reshape(1, LANES)
    tb = 256
    tm = 512
    x1, h2, rt, cnt, prow = _merge(ya, yb, x, pe, mod3, out_norm_a[l], out_norm_b[l], w_out[l], norm2_g[l],
                                   w_r, b_r, tm, tm // tb)

    tile = 512
    t_all = bsz * n
    nb = t_all // tb
    lp = 2 * tb + N_EXPERTS * SEG
    n_tiles = (2 * t_all + nb * N_EXPERTS * (SEG - 1) + N_EXPERTS * (tile - 1)) // tile
    cnt_blocks = cnt[:, 0, :N_EXPERTS].astype(jnp.int32)
    tab, row_ends, ltot, tile_expert, n_used, nxt, par = _moe_layout(cnt_blocks, tile, n_tiles)
    xs = _dispatch(tab, row_ends, ltot, prow, h2.reshape(t_all, d), n_tiles * tile, tb, lp, tile)
    ys = _ffn(tile_expert, n_used, nxt, par, xs, w_gate[l], w_up[l], w_down[l], tile)
    out = _combine(tab, ltot, ys, rt.reshape(t_all, LANES), x1.reshape(t_all, d), mod3, final_g, tb, lp, n // tb)
    return out.reshape(bsz, n, d)
```

```python
import functools
import math

import numpy as np
import jax
import jax.numpy as jnp
from jax import lax
from jax.experimental import pallas as pl
from jax.experimental.pallas import tpu as pltpu

F32 = jnp.float32
BF16 = jnp.bfloat16
EPS = 1e-6
LRU_C = 8.0
N_MOD = 6
GRID_W = 64
RNN_HEADS = 8
N_GROUPS = 4
EXPERTS_PER_GROUP = 8
N_EXPERTS = N_GROUPS * EXPERTS_PER_GROUP
FILTER_BANDS = 16
DECAY_FAST_PCT = 0.3
DECAY_SLOW_PCT = 1.5
DECAY_TARGET = 1e-2
SUBLANES = 8
LANES = 128
SEG = SUBLANES
VMEM_LIMIT = 60 * 1024 * 1024


def _cparams(*sem):
    return pltpu.CompilerParams(dimension_semantics=sem, vmem_limit_bytes=VMEM_LIMIT)


def _dot(a, b):
    return jnp.dot(a, b, preferred_element_type=F32)


def _split(a):
    hi = a.astype(BF16)
    return hi, (a - hi.astype(F32)).astype(BF16)


def _dot3(a, b):
    a_hi, a_lo = _split(a)
    b_hi, b_lo = _split(b)
    return _dot(a_hi, b_hi) + _dot(a_lo, b_hi) + _dot(a_hi, b_lo)


def _rms(v, g):
    return v * lax.rsqrt(jnp.mean(v * v, axis=-1, keepdims=True) + EPS) * g


def _sigmoid(z):
    return 1.0 / (1.0 + jnp.exp(-z))


def _shift_rows(v, d):
    n = v.shape[0]
    rolled = pltpu.roll(v, (-d) % n, 0)
    row = lax.broadcasted_iota(jnp.int32, v.shape, 0)
    ok = (row + d >= 0) & (row + d < n)
    return jnp.where(ok, rolled, 0.0)


def _tile_rot(v, s):
    n, c = v.shape
    return pltpu.roll(v.reshape(n // SUBLANES, SUBLANES, c), s, 1).reshape(n, c)


def _ada_body(c_ref, w_ref, b_ref, o_ref):
    c = c_ref[...]
    o_ref[...] = _dot3(c * _sigmoid(c), w_ref[...]) + b_ref[...]


def _ada(cc, w_ada, b_ada):
    rows, d = cc.shape
    n = w_ada.shape[1]
    tn = 1024
    return pl.pallas_call(
        _ada_body,
        grid=(n // tn,),
        in_specs=[pl.BlockSpec((rows, d), lambda j: (0, 0)),
                  pl.BlockSpec((d, tn), lambda j: (0, j)),
                  pl.BlockSpec((1, tn), lambda j: (0, j))],
        out_specs=pl.BlockSpec((rows, tn), lambda j: (0, j)),
        out_shape=jax.ShapeDtypeStruct((rows, n), F32),
        compiler_params=_cparams("arbitrary"),
        name="ada",
    )(cc, w_ada, b_ada.reshape(1, n))


def _inproj_lat_body(x_ref, pe_ref, mod_ref, g_ref, w_ref, o_ref, wb_ref):
    @pl.when((pl.program_id(0) == 0) & (pl.program_id(1) == 0))
    def _():
        wb_ref[...] = w_ref[...].astype(BF16)

    x = x_ref[...] + pe_ref[...]
    h = _rms(x, g_ref[...]) * (1.0 + mod_ref[1:2, :]) + mod_ref[0:1, :]
    o_ref[...] = _dot(h.astype(BF16), wb_ref[...])


def _inproj_lat(x3, pe, mod3, g, w, tm):
    b, l, d = x3.shape
    n_out = w.shape[1]
    return pl.pallas_call(
        _inproj_lat_body,
        grid=(l // tm, b),
        in_specs=[pl.BlockSpec((None, tm, d), lambda i, bi: (bi, i, 0)),
                  pl.BlockSpec((tm, d), lambda i, bi: (i, 0)),
                  pl.BlockSpec((None, N_MOD, d), lambda i, bi: (bi, 0, 0)),
                  pl.BlockSpec((1, d), lambda i, bi: (0, 0)),
                  pl.BlockSpec((d, n_out), lambda i, bi: (0, 0), pipeline_mode=pl.Buffered(1))],
        out_specs=pl.BlockSpec((None, tm, n_out), lambda i, bi: (bi, i, 0)),
        out_shape=jax.ShapeDtypeStruct((b, l, n_out), F32),
        scratch_shapes=[pltpu.VMEM((d, n_out), BF16)],
        compiler_params=_cparams("arbitrary", "arbitrary"),
        name="inproj_pe",
    )(x3, pe, mod3, g.reshape(1, d), w)


def _inproj_ctx_body(x_ref, mod_ref, g_ref, w_ref, o_ref):
    h = _rms(x_ref[...], g_ref[...]) * (1.0 + mod_ref[1:2, :]) + mod_ref[0:1, :]
    o_ref[...] = _dot(h.astype(BF16), w_ref[...].astype(BF16))


def _inproj_ctx(x2, mod3, mod_row, g, w, col_block, n_out, tm):
    r, d = x2.shape
    return pl.pallas_call(
        _inproj_ctx_body,
        grid=(r // tm,),
        in_specs=[pl.BlockSpec((tm, d), lambda i: (i, 0)),
                  pl.BlockSpec((None, N_MOD, d), lambda i: (mod_row, 0, 0)),
                  pl.BlockSpec((1, d), lambda i: (0, 0)),
                  pl.BlockSpec((d, n_out), lambda i: (0, col_block))],
        out_specs=pl.BlockSpec((tm, n_out), lambda i: (i, 0)),
        out_shape=jax.ShapeDtypeStruct((r, n_out), F32),
        compiler_params=_cparams("arbitrary"),
        name="inproj_ctx",
    )(x2, mod3, g.reshape(1, d), w)


def _rglru_body(*refs, with_gate):
    if with_gate:
        (u_ref, gate_ref, cw_ref, cb_ref, w_ref, bias_ref, sp_ref, h0f_ref, h0b_ref,
         y_ref, hf_end_ref, hb_end_ref, af_ref, bf_ref, ab_ref, bb_ref) = refs
    else:
        (u_ref, cw_ref, cb_ref, w_ref, bias_ref, sp_ref, h0f_ref, h0b_ref,
         hf_end_ref, hb_end_ref, af_ref, bf_ref, ab_ref, bb_ref) = refs
    n, c = u_ref.shape
    u = u_ref[...]
    v = cb_ref[...] + cw_ref[1:2, :] * _shift_rows(u, -1) + cw_ref[0:1, :] * _shift_rows(u, -2)
    v = v + cw_ref[2:3, :] * u + cw_ref[3:4, :] * _shift_rows(u, 1)
    z = _dot(v.astype(BF16), w_ref[...]) + bias_ref[...]
    sub = lax.broadcasted_iota(jnp.int32, (n, c), 0) % SUBLANES

    def local_scan(k, reverse, a_ref, b_ref):
        r = _sigmoid(z[:, (2 * k) * c:(2 * k + 1) * c])
        i = _sigmoid(z[:, (2 * k + 1) * c:(2 * k + 2) * c])
        log_a = (-LRU_C) * r * sp_ref[k:k + 1, :]
        a = jnp.exp(log_a)
        b = jnp.sqrt(1.0 - a * a) * (i * v)
        for s in (1, 2, 4):
            ok = (sub < SUBLANES - s) if reverse else (sub >= s)
            shift = SUBLANES - s if reverse else s
            b = a * jnp.where(ok, _tile_rot(b, shift), 0.0) + b
            a = a * jnp.where(ok, _tile_rot(a, shift), 1.0)
        a_ref[...] = a
        b_ref[...] = b

    local_scan(0, False, af_ref, bf_ref)
    local_scan(1, True, ab_ref, bb_ref)

    n_tiles = n // SUBLANES

    def carry_step(q, carry):
        cf, cb = carry
        rf = pl.ds(pl.multiple_of(q * SUBLANES, SUBLANES), SUBLANES)
        rb = pl.ds(pl.multiple_of((n_tiles - 1 - q) * SUBLANES, SUBLANES), SUBLANES)
        hf = af_ref[rf, :] * cf + bf_ref[rf, :]
        hb = ab_ref[rb, :] * cb + bb_ref[rb, :]
        bf_ref[rf, :] = hf
        bb_ref[rb, :] = hb
        return (jnp.broadcast_to(hf[SUBLANES - 1:SUBLANES, :], (SUBLANES, c)),
                jnp.broadcast_to(hb[0:1, :], (SUBLANES, c)))

    cf0 = jnp.broadcast_to(h0f_ref[...], (SUBLANES, c))
    cb0 = jnp.broadcast_to(h0b_ref[...], (SUBLANES, c))
    cf, cb = lax.fori_loop(0, n_tiles, carry_step, (cf0, cb0), unroll=4)
    hf_end_ref[...] = cf[0:1, :]
    hb_end_ref[...] = cb[0:1, :]
    if with_gate:
        y_ref[...] = jax.nn.gelu(gate_ref[...], approximate=True) * (bf_ref[...] + bb_ref[...])


def _rglru(p3, u_blk0, gate_blk0, cw, cb, w_gates, bias, sp, h0f, h0b, with_gate, ch):
    b, l, _ = p3.shape
    d_rnn = cw.shape[1]
    nh = d_rnn // ch
    in_specs = [pl.BlockSpec((None, l, ch), lambda bi, h: (bi, 0, u_blk0 + h))]
    args = [p3]
    if with_gate:
        in_specs.append(pl.BlockSpec((None, l, ch), lambda bi, h: (bi, 0, gate_blk0 + h)))
        args.append(p3)
    in_specs += [pl.BlockSpec((4, ch), lambda bi, h: (0, h)),
                 pl.BlockSpec((1, ch), lambda bi, h: (0, h)),
                 pl.BlockSpec((None, ch, 4 * ch), lambda bi, h: (h, 0, 0)),
                 pl.BlockSpec((None, 1, 4 * ch), lambda bi, h: (h, 0, 0)),
                 pl.BlockSpec((2, ch), lambda bi, h: (0, h)),
                 pl.BlockSpec((None, 1, ch), lambda bi, h: (bi, 0, h)),
                 pl.BlockSpec((None, 1, ch), lambda bi, h: (bi, 0, h))]
    args += [cw, cb, w_gates, bias, sp, h0f, h0b]
    end_spec = pl.BlockSpec((None, 1, ch), lambda bi, h: (bi, 0, h))
    end_shape = jax.ShapeDtypeStruct((b, 1, d_rnn), F32)
    out_specs = [end_spec, end_spec]
    out_shape = [end_shape, end_shape]
    if with_gate:
        out_specs = [pl.BlockSpec((None, l, ch), lambda bi, h: (bi, 0, h))] + out_specs
        out_shape = [jax.ShapeDtypeStruct((b, l, d_rnn), F32)] + out_shape
    return pl.pallas_call(
        functools.partial(_rglru_body, with_gate=with_gate),
        grid=(b, nh),
        in_specs=in_specs,
        out_specs=out_specs,
        out_shape=out_shape,
        scratch_shapes=[pltpu.VMEM((l, ch), F32)] * 4,
        compiler_params=_cparams("arbitrary", "arbitrary"),
        name="rglru_lat" if with_gate else "rglru_ctx",
    )(*args)


CONV_BLOCK = 512


def _filt_body(z_ref, w1_ref, b1_ref, f1_ref, w2_ref, b2_ref, f2_ref, w3f_ref, w3b_ref, b3f_ref, b3b_ref,
               dec_ref, cu_ref, su_ref, sgn_ref, hr_ref, hi_ref):
    n = z_ref.shape[0]
    blk = cu_ref.shape[0]
    nblk = n // blk
    ch = hr_ref.shape[2]
    hdn = jnp.sin(f1_ref[...] * (_dot3(z_ref[...], w1_ref[...]) + b1_ref[...]))
    hdn = jnp.sin(f2_ref[...] * (_dot3(hdn, w2_ref[...]) + b2_ref[...]))
    decay = jnp.exp(-z_ref[:, 0:1] * dec_ref[...])
    kf = (_dot3(hdn, w3f_ref[...]) + b3f_ref[...]) * decay
    kb = (_dot3(hdn, w3b_ref[...]) + b3b_ref[...]) * decay
    row = lax.broadcasted_iota(jnp.int32, (n, ch), 0)
    kb = jnp.where(row == 0, 0.0, kb)
    norm = jnp.sum(jnp.abs(kf) + jnp.abs(kb), axis=0, keepdims=True)
    scale = (2.0 / (2 * blk)) / norm
    sgn = sgn_ref[...]

    p, q, first = [], [], []
    for c in range(nblk):
        taps = jnp.concatenate([kf[c * blk:(c + 1) * blk], kb[c * blk:(c + 1) * blk]], axis=1).astype(BF16)
        p.append(_dot(cu_ref[...], taps))
        q.append(_dot(su_ref[...], taps))
        first.append(taps[0:1, :].astype(F32))
    fwd = lambda a: a[:, :ch]
    bwd = lambda a: a[:, ch:]
    hr_ref[nblk - 1] = (fwd(p[0]) + bwd(p[0])) * scale
    hi_ref[nblk - 1] = (bwd(q[0]) - fwd(q[0])) * scale
    for c in range(1, nblk):
        hr_ref[nblk - 1 + c] = (fwd(p[c]) + sgn * fwd(q[c - 1])) * scale
        hi_ref[nblk - 1 + c] = (sgn * (fwd(p[c - 1]) - fwd(first[c - 1])) - fwd(q[c])) * scale
        hr_ref[nblk - 1 - c] = (bwd(p[c]) + sgn * bwd(q[c - 1])) * scale
        hi_ref[nblk - 1 - c] = (bwd(q[c]) - sgn * (bwd(p[c - 1]) - bwd(first[c - 1]))) * scale


def _filters(z, w1, b1, f1, w2, b2, f2, w3, b3, deltas, cu, su, sgn, d_h, order, ch):
    n, fe = z.shape
    hid = w2.shape[0]
    blk = cu.shape[0]
    nh = 2 * (n // blk) - 1
    per_order = d_h // ch
    ncol = order * per_order
    const = lambda shape: pl.BlockSpec(shape, lambda g: (0, 0))
    hspec = pl.BlockSpec((nh, blk, ch), lambda g: (0, 0, g))
    return pl.pallas_call(
        _filt_body,
        grid=(ncol,),
        in_specs=[const((n, fe)), const((fe, hid)), const((1, hid)), const((1, hid)),
                  const((hid, hid)), const((1, hid)), const((1, hid)),
                  pl.BlockSpec((hid, ch), lambda g: (0, g)),
                  pl.BlockSpec((hid, ch), lambda g: (0, ncol + g)),
                  pl.BlockSpec((1, ch), lambda g: (0, g)),
                  pl.BlockSpec((1, ch), lambda g: (0, ncol + g)),
                  pl.BlockSpec((1, ch), lambda g: (0, g % per_order)),
                  const((blk, blk)), const((blk, blk)), const((blk, 1))],
        out_specs=[hspec, hspec],
        out_shape=[jax.ShapeDtypeStruct((nh, blk, order * d_h), F32)] * 2,
        compiler_params=_cparams("arbitrary"),
        name="hyena_filters",
    )(z, w1, b1, f1, w2, b2, f2, w3, w3, b3, b3, deltas, cu, su, sgn)


def _hyena_body(v_ref, x1_ref, x2_ref, wv_ref, w1_ref, w2_ref, fb_ref, c3_ref, s3_ref,
                h0r_ref, h0i_ref, h1r_ref, h1i_ref, o_ref, z_ref, zb_ref, p_ref, q_ref, yr_ref, yi_ref):
    n = o_ref.shape[0]
    blk = c3_ref.shape[0]
    nblk = n // blk
    halo = SUBLANES
    sub = 32

    def conv3(src_ref, w_ref, r0):
        lo, hi = max(r0 - halo, 0), min(r0 + blk + halo, n)
        win = src_ref[lo:hi, :]
        size = hi - lo
        row = lax.broadcasted_iota(jnp.int32, win.shape, 0)
        prev = pltpu.roll(win, 1, 0)
        nxt = pltpu.roll(win, size - 1, 0)
        if lo == 0:
            prev = jnp.where(row == 0, 0.0, prev)
        if hi == n:
            nxt = jnp.where(row == size - 1, 0.0, nxt)
        out = w_ref[0:1, :] * prev + w_ref[1:2, :] * win + w_ref[2:3, :] * nxt
        return out[r0 - lo:r0 - lo + blk, :]

    def long_conv(hr_ref, hi_ref, fb, gate_ref, gate_w_ref, dst_ref):
        for i in range(nblk):
            rows = slice(i * blk, (i + 1) * blk)
            p_ref[rows, :] = _dot(c3_ref[...], zb_ref[rows, :])
            q_ref[rows, :] = _dot(s3_ref[...], zb_ref[rows, :])

        for j in range(nblk):
            for r0 in range(0, blk, sub):
                yr = yi = None
                for i in range(nblk):
                    rows = slice(i * blk + r0, i * blk + r0 + sub)
                    p, q = p_ref[rows, :], q_ref[rows, :]
                    gr = hr_ref[nblk - 1 + j - i, r0:r0 + sub, :]
                    gi = hi_ref[nblk - 1 + j - i, r0:r0 + sub, :]
                    tr, ti = p * gr + q * gi, q * gr - p * gi
                    yr, yi = (tr, ti) if yr is None else (yr + tr, yi + ti)
                yr_ref[j * blk + r0:j * blk + r0 + sub, :] = yr.astype(BF16)
                yi_ref[j * blk + r0:j * blk + r0 + sub, :] = yi.astype(BF16)
            rows = slice(j * blk, (j + 1) * blk)
            y = _dot(c3_ref[...], yr_ref[rows, :]) + _dot(s3_ref[...], yi_ref[rows, :])
            dst_ref[rows, :] = conv3(gate_ref, gate_w_ref, j * blk) * (y + z_ref[rows, :] * fb)

    for r0 in range(0, n, blk):
        z_ref[r0:r0 + blk, :] = conv3(v_ref, wv_ref, r0)
    zb_ref[...] = z_ref[...].astype(BF16)
    long_conv(h0r_ref, h0i_ref, fb_ref[0:1, :], x1_ref, w1_ref, z_ref)
    zb_ref[...] = z_ref[...].astype(BF16)
    long_conv(h1r_ref, h1i_ref, fb_ref[1:2, :], x2_ref, w2_ref, o_ref)


def _hyena(p3, col_blk0, conv_w, fbias, c3, s3, hr, hi, d_h, ch):
    b, n, _ = p3.shape
    nh, blk, _ = hr.shape
    per = d_h // ch
    zspec = lambda k: pl.BlockSpec((None, n, ch), lambda h, bi: (bi, 0, col_blk0 + k * per + h))
    wspec = lambda k: pl.BlockSpec((3, ch), lambda h, bi: (0, k * per + h))
    hspec = lambda o: pl.BlockSpec((nh, blk, ch), lambda h, bi: (0, 0, o * per + h), pipeline_mode=pl.Buffered(1))
    const = pl.BlockSpec((blk, blk), lambda h, bi: (0, 0))
    return pl.pallas_call(
        _hyena_body,
        grid=(per, b),
        in_specs=[zspec(0), zspec(1), zspec(2), wspec(0), wspec(1), wspec(2),
                  pl.BlockSpec((2, ch), lambda h, bi: (0, h)), const, const,
                  hspec(0), hspec(0), hspec(1), hspec(1)],
        out_specs=pl.BlockSpec((None, n, ch), lambda h, bi: (bi, 0, h)),
        out_shape=jax.ShapeDtypeStruct((b, n, d_h), F32),
        scratch_shapes=[pltpu.VMEM((n, ch), F32), pltpu.VMEM((n, ch), BF16),
                        pltpu.VMEM((n, ch), F32), pltpu.VMEM((n, ch), F32),
                        pltpu.VMEM((n, ch), BF16), pltpu.VMEM((n, ch), BF16)],
        compiler_params=_cparams("arbitrary", "arbitrary"),
        name="hyena_mix",
    )(p3, p3, p3, conv_w, conv_w, conv_w, fbias, c3, s3, hr, hi, hr, hi)


def _merge_body(ya_ref, yb_ref, x_ref, pe_ref, mod_ref, ga_ref, gb_ref, wo_ref, g2_ref, wr_ref, br_ref,
                x1_ref, h2_ref, rt_ref, cnt_ref, prow_ref, wob_ref):
    @pl.when((pl.program_id(0) == 0) & (pl.program_id(1) == 0))
    def _():
        wob_ref[...] = wo_ref[...].astype(BF16)

    da = ya_ref.shape[1]
    na = _rms(ya_ref[...], ga_ref[...]).astype(BF16)
    nb = _rms(yb_ref[...], gb_ref[...]).astype(BF16)
    y = _dot(na, wob_ref[0:da, :]) + _dot(nb, wob_ref[da:, :])
    _route(y, x_ref, pe_ref, mod_ref, g2_ref, wr_ref, br_ref, x1_ref, h2_ref, rt_ref, cnt_ref, prow_ref)


def _route(y, x_ref, pe_ref, mod_ref, g2_ref, wr_ref, br_ref, x1_ref, h2_ref, rt_ref, cnt_ref, prow_ref):
    x1 = x_ref[...] + pe_ref[...] + mod_ref[2:3, :] * y
    x1_ref[...] = x1
    h2 = _rms(x1, g2_ref[...]) * (1.0 + mod_ref[4:5, :]) + mod_ref[3:4, :]
    h2_ref[...] = h2.astype(BF16)

    logits = _dot3(h2, wr_ref[...]) + br_ref[...]
    lane = lax.broadcasted_iota(jnp.int32, logits.shape, 1).astype(F32)
    neg = -jnp.inf
    big = jnp.float32(1 << 20)
    gl = jnp.where((lane >= N_EXPERTS) & (lane < N_EXPERTS + N_GROUPS), logits, neg)
    gmax = jnp.max(gl, axis=1, keepdims=True)
    g_p = 1.0 / jnp.sum(jnp.exp(gl - gmax), axis=1, keepdims=True)
    g_i = jnp.min(jnp.where(gl == gmax, lane, big), axis=1, keepdims=True) - N_EXPERTS
    lo = g_i * EXPERTS_PER_GROUP
    el = jnp.where((lane >= lo) & (lane < lo + EXPERTS_PER_GROUP), logits, neg)
    m1 = jnp.max(el, axis=1, keepdims=True)
    i1 = jnp.min(jnp.where(el == m1, lane, big), axis=1, keepdims=True)
    el2 = jnp.where(lane == i1, neg, el)
    m2 = jnp.max(el2, axis=1, keepdims=True)
    i2 = jnp.min(jnp.where(el2 == m2, lane, big), axis=1, keepdims=True)
    e2 = jnp.exp(m2 - m1)
    w1 = g_p / (1.0 + e2)
    w2 = g_p * e2 / (1.0 + e2)
    o1 = jnp.where(lane == i1, 1.0, 0.0)
    o2 = jnp.where(lane == i2, 1.0, 0.0)
    nsub = cnt_ref.shape[0]
    sub = o1.shape[0] // nsub
    sq = lambda shape, d: lax.broadcasted_iota(jnp.int32, shape, d)
    earlier = jnp.where(sq((sub, sub), 1) < sq((sub, sub), 0), 1.0, 0.0).astype(BF16)
    below = jnp.where(sq((LANES, LANES), 0) < sq((LANES, LANES), 1), 1.0, 0.0).astype(BF16)
    pick_row = jnp.where(sq((SUBLANES, LANES), 0) == sq((SUBLANES, LANES), 1), 1.0, 0.0).astype(BF16)
    row_id = sq((SUBLANES, sub), 0)
    pos1, pos2 = [], []
    for k in range(nsub):
        a1, a2 = o1[k * sub:(k + 1) * sub], o2[k * sub:(k + 1) * sub]
        both = a1 + a2
        count = jnp.sum(both, axis=0, keepdims=True)
        cnt_ref[k] = jnp.broadcast_to(count, cnt_ref.shape[1:])
        runs = jnp.broadcast_to(jnp.floor((count + (SEG - 1)) / SEG), (SUBLANES, LANES))
        start = _dot(runs.astype(BF16), below)[0:1, :] * SEG
        before = _dot(earlier, both.astype(BF16)) + start
        p1 = jnp.sum(a1 * before, axis=1, keepdims=True)
        p2 = jnp.sum(a2 * before, axis=1, keepdims=True)
        pos1.append(p1)
        pos2.append(p2)
        lane_k = sq((sub, LANES), 1).astype(F32)
        hi1, hi2 = jnp.floor(p1 / 64.0), jnp.floor(p2 / 64.0)
        parts = (jnp.where(lane_k == 0.0, hi1, 0.0) + jnp.where(lane_k == 1.0, p1 - 64.0 * hi1, 0.0)
                 + jnp.where(lane_k == 2.0, hi2, 0.0) + jnp.where(lane_k == 3.0, p2 - 64.0 * hi2, 0.0))
        rows = lax.dot_general(pick_row, parts.astype(BF16), (((1,), (1,)), ((), ())), preferred_element_type=F32)
        q1 = rows[0:1, :] * 64.0 + rows[1:2, :]
        q2 = rows[2:3, :] * 64.0 + rows[3:4, :]
        prow_ref[k] = jnp.where(row_id == 0, q1, jnp.where(row_id == 1, q2, 0.0))
    p1 = jnp.concatenate(pos1, axis=0)
    p2 = jnp.concatenate(pos2, axis=0)
    rt_ref[...] = (jnp.where(lane == 0.0, i1, 0.0) + jnp.where(lane == 1.0, i2, 0.0)
                   + jnp.where(lane == 2.0, w1, 0.0) + jnp.where(lane == 3.0, w2, 0.0)
                   + jnp.where(lane == 4.0, p1, 0.0) + jnp.where(lane == 5.0, p2, 0.0))


def _merge(ya, yb, x3, pe, mod3, ga, gb, w_out, g2n, w_r, b_r, tm, nsub):
    b, n, d = x3.shape
    da, db = ya.shape[2], yb.shape[2]
    per = n // tm
    tok = lambda w: pl.BlockSpec((None, tm, w), lambda i, bi: (bi, i, 0))
    const = lambda shape: pl.BlockSpec(shape, lambda i, bi: (0, 0))
    blk = lambda w: pl.BlockSpec((nsub, SUBLANES, w), lambda i, bi: (bi * per + i, 0, 0))
    return pl.pallas_call(
        _merge_body,
        grid=(per, b),
        in_specs=[tok(da), tok(db), tok(d), pl.BlockSpec((tm, d), lambda i, bi: (i, 0)),
                  pl.BlockSpec((None, N_MOD, d), lambda i, bi: (bi, 0, 0)),
                  const((1, da)), const((1, db)), const((da + db, d)), const((1, d)),
                  const((d, LANES)), const((1, LANES))],
        out_specs=[tok(d), tok(d), tok(LANES), blk(LANES), blk(tm // nsub)],
        out_shape=[jax.ShapeDtypeStruct((b, n, d), F32), jax.ShapeDtypeStruct((b, n, d), BF16),
                   jax.ShapeDtypeStruct((b, n, LANES), F32),
                   jax.ShapeDtypeStruct((b * per * nsub, SUBLANES, LANES), F32),
                   jax.ShapeDtypeStruct((b * per * nsub, SUBLANES, tm // nsub), F32)],
        scratch_shapes=[pltpu.VMEM((da + db, d), BF16)],
        compiler_params=_cparams("arbitrary", "arbitrary"),
        name="merge_route",
    )(ya, yb, x3, pe, mod3, ga.reshape(1, da), gb.reshape(1, db), w_out, g2n.reshape(1, d), w_r, b_r)


def _segment_copy(local_ref, lrow, global_ref, grow, rows, sem, to_global):
    lrows = pl.ds(pl.multiple_of(lrow, SEG), rows)
    grows = pl.ds(pl.multiple_of(grow, SEG), rows)
    if to_global:
        return pltpu.make_async_copy(local_ref.at[lrows, :], global_ref.at[grows, :], sem)
    return pltpu.make_async_copy(global_ref.at[grows, :], local_ref.at[lrows, :], sem)


def _start_segments(tab_ref, blk, local_ref, global_ref, sem, to_global):
    def per_expert(e, carry):
        k = (blk * N_EXPERTS + e) * 3
        ls, gs, nchunk = tab_ref[k], tab_ref[k + 1], tab_ref[k + 2]

        def per_chunk(ci, carry2):
            _segment_copy(local_ref, ls + ci * SEG, global_ref, gs + ci * SEG, SEG, sem, to_global).start()
            return carry2

        return lax.fori_loop(0, nchunk, per_chunk, carry)

    lax.fori_loop(0, N_EXPERTS, per_expert, 0)


def _wait_segments(rows, local_ref, global_ref, sem, to_global):
    chunks = rows // SEG
    nbits = (local_ref.shape[0] // SEG).bit_length()
    for k in range(nbits):
        @pl.when(((chunks >> k) & 1) == 1)
        def _():
            _segment_copy(local_ref, 0, global_ref, 0, SEG << k, sem, to_global).wait()


def _dispatch_body(tab_ref, ends_ref, ltot_ref, prow_ref, h2_ref, xs_ref, loc_ref, zero_ref, sem, zsem):
    blk = pl.program_id(0)
    nblk = pl.num_programs(0)
    tb = h2_ref.shape[0]
    lp = loc_ref.shape[1]
    tile = zero_ref.shape[0]
    slot = blk % 2

    @pl.when(blk == 0)
    def _():
        zero_ref[...] = jnp.zeros_like(zero_ref)

        def zero_fill(first_row, rows):
            dst = xs_ref.at[pl.ds(pl.multiple_of(first_row, SEG), rows), :]
            return pltpu.make_async_copy(zero_ref.at[pl.ds(0, rows), :], dst, zsem)

        pad = 64

        def fill(e, carry, wait):
            end, real = ends_ref[e], ends_ref[N_EXPERTS + e]
            first = real // pad * pad

            def one(j, carry2):
                cp = zero_fill(first + j * pad, pad)
                cp.wait() if wait else cp.start()
                return carry2

            return lax.fori_loop(0, (end - first) // pad, one, carry)

        lax.fori_loop(0, N_EXPERTS, functools.partial(fill, wait=False), 0)
        used = ends_ref[N_EXPERTS - 1]
        spare = (xs_ref.shape[0] - used) // tile

        def fill_spare(j, carry, wait):
            cp = zero_fill(used + j * tile, tile)
            cp.wait() if wait else cp.start()
            return carry

        lax.fori_loop(0, spare, functools.partial(fill_spare, wait=False), 0)
        lax.fori_loop(0, N_EXPERTS, functools.partial(fill, wait=True), 0)
        lax.fori_loop(0, spare, functools.partial(fill_spare, wait=True), 0)

    pos = lax.broadcasted_iota(jnp.int32, (lp, tb), 0).astype(F32)
    onehot = jnp.where((pos == prow_ref[0:1, :]) | (pos == prow_ref[1:2, :]), 1.0, 0.0).astype(BF16)
    loc_ref[slot] = _dot(onehot, h2_ref[...])

    _start_segments(tab_ref, blk, loc_ref.at[slot], xs_ref, sem.at[slot], True)

    @pl.when(blk > 0)
    def _():
        _wait_segments(ltot_ref[blk - 1], loc_ref.at[1 - slot], xs_ref, sem.at[1 - slot], True)

    @pl.when(blk == nblk - 1)
    def _():
        _wait_segments(ltot_ref[blk], loc_ref.at[slot], xs_ref, sem.at[slot], True)


def _dispatch(tab, ends, ltot, prow, h2, slots, tb, lp, tile):
    t, d = h2.shape
    return pl.pallas_call(
        _dispatch_body,
        grid_spec=pltpu.PrefetchScalarGridSpec(
            num_scalar_prefetch=3,
            grid=(t // tb,),
            in_specs=[pl.BlockSpec((None, SUBLANES, tb), lambda i, *_: (i, 0, 0)),
                      pl.BlockSpec((tb, d), lambda i, *_: (i, 0))],
            out_specs=pl.BlockSpec(memory_space=pl.ANY),
            scratch_shapes=[pltpu.VMEM((2, lp, d), F32), pltpu.VMEM((tile, d), F32),
                            pltpu.SemaphoreType.DMA((2,)), pltpu.SemaphoreType.DMA]),
        out_shape=jax.ShapeDtypeStruct((slots, d), F32),
        compiler_params=_cparams("arbitrary"),
        name="moe_dispatch",
    )(tab, ends, ltot, prow, h2)


def _ffn_body(te_ref, nu_ref, nxt_ref, par_ref, xs_ref, wg_hbm, wu_hbm, wd_hbm, ys_ref,
              wg_buf, wu_buf, wd_buf, wgb_ref, wub_ref, wdb_ref, sem):
    i = pl.program_id(0)
    e = te_ref[i]
    first = (i == 0) | (e != te_ref[jnp.maximum(i - 1, 0)])
    active = i < nu_ref[0]

    def fetch(expert, slot):
        return [pltpu.make_async_copy(w.at[expert], buf.at[slot], sem.at[slot])
                for w, buf in ((wg_hbm, wg_buf), (wu_hbm, wu_buf), (wd_hbm, wd_buf))]

    @pl.when(i == 0)
    def _():
        for cp in fetch(e, par_ref[e]):
            cp.start()

    @pl.when(first & active)
    def _():
        slot = par_ref[e]
        for cp in fetch(e, slot):
            cp.wait()
        nxt = nxt_ref[e]

        @pl.when(nxt >= 0)
        def _():
            for cp in fetch(nxt, 1 - slot):
                cp.start()

        wgb_ref[...] = wg_buf[slot].astype(BF16)
        wub_ref[...] = wu_buf[slot].astype(BF16)
        wdb_ref[...] = wd_buf[slot].astype(BF16)

    @pl.when(active)
    def _():
        xb = xs_ref[...].astype(BF16)
        act = _dot(xb, wgb_ref[...])
        act = act * _sigmoid(act) * _dot(xb, wub_ref[...])
        ys_ref[...] = _dot(act.astype(BF16), wdb_ref[...])


def _ffn(tile_expert, n_used, nxt, par, xs, w_gate, w_up, w_down, tile):
    slots, d = xs.shape
    _, _, de = w_gate.shape
    last = lambda i, te, nu, *_: (jnp.minimum(i, nu[0] - 1), 0)
    hbm = pl.BlockSpec(memory_space=pl.ANY)
    return pl.pallas_call(
        _ffn_body,
        grid_spec=pltpu.PrefetchScalarGridSpec(
            num_scalar_prefetch=4,
            grid=(slots // tile,),
            in_specs=[pl.BlockSpec((tile, d), last), hbm, hbm, hbm],
            out_specs=pl.BlockSpec((tile, d), last),
            scratch_shapes=[pltpu.VMEM((2, d, de), F32), pltpu.VMEM((2, d, de), F32), pltpu.VMEM((2, de, d), F32),
                            pltpu.VMEM((d, de), BF16), pltpu.VMEM((d, de), BF16), pltpu.VMEM((de, d), BF16),
                            pltpu.SemaphoreType.DMA((2,))]),
        out_shape=jax.ShapeDtypeStruct((slots, d), F32),
        input_output_aliases={4: 0},
        compiler_params=_cparams("arbitrary"),
        name="moe_ffn",
    )(tile_expert, n_used, nxt, par, xs, w_gate, w_up, w_down)


def _combine_body(tab_ref, ltot_ref, ys_ref, rt_ref, x1_ref, mod_ref, fg_ref, o_ref, loc_ref, sem):
    blk = pl.program_id(0)
    nblk = pl.num_programs(0)
    _, lp, d = loc_ref.shape
    slot = blk % 2

    @pl.when(blk == 0)
    def _():
        _start_segments(tab_ref, blk, loc_ref.at[slot], ys_ref, sem.at[slot], False)

    @pl.when(blk + 1 < nblk)
    def _():
        _start_segments(tab_ref, blk + 1, loc_ref.at[1 - slot], ys_ref, sem.at[1 - slot], False)

    rt = rt_ref[...]
    lane = lax.broadcasted_iota(jnp.int32, rt.shape, 1)
    col = lambda k: jnp.sum(jnp.where(lane == k, rt, 0.0), axis=1, keepdims=True)
    w1, w2, p1, p2 = col(2), col(3), col(4), col(5)
    pos = lax.broadcasted_iota(jnp.int32, (rt.shape[0], lp), 1).astype(F32)
    weights = (jnp.where(pos == p1, w1, 0.0) + jnp.where(pos == p2, w2, 0.0)).astype(BF16)
    _wait_segments(ltot_ref[blk], loc_ref.at[slot], ys_ref, sem.at[slot], False)

    def clear(ci, carry):
        loc_ref[slot, pl.ds(pl.multiple_of(ci * SEG, SEG), SEG), :] = jnp.zeros((SEG, d), F32)
        return carry

    lax.fori_loop(ltot_ref[blk] // SEG, lp // SEG, clear, 0)
    moe = _dot(weights, loc_ref[slot].astype(BF16))
    o_ref[...] = _rms(x1_ref[...] + mod_ref[5:6, :] * moe, fg_ref[...])


def _combine(tab, ltot, ys, rt2, x1, mod3, final_g, tb, lp, per_batch):
    t, d = x1.shape
    return pl.pallas_call(
        _combine_body,
        grid_spec=pltpu.PrefetchScalarGridSpec(
            num_scalar_prefetch=2,
            grid=(t // tb,),
            in_specs=[pl.BlockSpec(memory_space=pl.ANY),
                      pl.BlockSpec((tb, LANES), lambda i, *_: (i, 0)),
                      pl.BlockSpec((tb, d), lambda i, *_: (i, 0)),
                      pl.BlockSpec((None, N_MOD, d), lambda i, *_: (i // per_batch, 0, 0)),
                      pl.BlockSpec((1, d), lambda i, *_: (0, 0))],
            out_specs=pl.BlockSpec((tb, d), lambda i, *_: (i, 0)),
            scratch_shapes=[pltpu.VMEM((2, lp, d), F32), pltpu.SemaphoreType.DMA((2,))]),
        out_shape=jax.ShapeDtypeStruct((t, d), F32),
        compiler_params=_cparams("arbitrary"),
        name="moe_combine",
    )(tab, ltot, ys, rt2, x1, mod3, final_g.reshape(1, d))


def _moe_layout(cnt_blocks, tile, n_tiles):
    lcnt = (cnt_blocks + SEG - 1) // SEG * SEG
    lstart = jnp.cumsum(lcnt, axis=1) - lcnt
    ltot = jnp.sum(lcnt, axis=1)
    per_expert = jnp.sum(lcnt, axis=0)
    tiles_per = (per_expert + tile - 1) // tile
    tile_ends = jnp.cumsum(tiles_per)
    row_ends = tile_ends * tile
    goff = row_ends - tiles_per * tile
    gstart = goff[None, :] + jnp.cumsum(lcnt, axis=0) - lcnt
    tab = jnp.stack([lstart, gstart, lcnt // SEG], axis=-1).reshape(-1).astype(jnp.int32)
    n_used = tile_ends[-1:]
    tile_ids = jnp.arange(n_tiles, dtype=jnp.int32)
    tile_expert = jnp.sum((tile_ends[None, :] <= jnp.minimum(tile_ids, n_used - 1)[:, None]).astype(jnp.int32), axis=1)
    ids = jnp.arange(N_EXPERTS, dtype=jnp.int32)
    used = tiles_per > 0
    par = (jnp.cumsum(used) - used) % 2
    later = jnp.where(used[None, :] & (ids[None, :] > ids[:, None]), ids[None, :], N_EXPERTS)
    nxt = jnp.min(later, axis=1)
    nxt = jnp.where(nxt == N_EXPERTS, -1, nxt)
    ends = jnp.concatenate([row_ends, goff + per_expert])
    return (tab, ends.astype(jnp.int32), ltot.astype(jnp.int32), tile_expert.astype(jnp.int32),
            n_used.astype(jnp.int32), nxt.astype(jnp.int32), par.astype(jnp.int32))


def _sincos_table(rows, cols, dim):
    quarter = dim // 4
    omega = 1.0 / (10000.0 ** (np.arange(quarter, dtype=np.float64) / quarter))
    ang_r = np.arange(rows, dtype=np.float64)[:, None] * omega
    ang_c = np.arange(cols, dtype=np.float64)[:, None] * omega
    emb_r = np.concatenate([np.sin(ang_r), np.cos(ang_r)], axis=-1)
    emb_c = np.concatenate([np.sin(ang_c), np.cos(ang_c)], axis=-1)
    pe = np.concatenate([np.broadcast_to(emb_r[:, None, :], (rows, cols, 2 * quarter)),
                         np.broadcast_to(emb_c[None, :, :], (rows, cols, 2 * quarter))], axis=-1)
    return pe.reshape(rows * cols, 4 * quarter).astype(np.float32)


def _filter_features(n, width):
    pos = np.arange(n, dtype=np.float64)
    t = pos / max(n - 1, 1)
    ang = (2.0 * math.pi * pos / n)[:, None] * np.linspace(1e-4, FILTER_BANDS - 1, FILTER_BANDS)[None, :]
    z = np.concatenate([t[:, None], np.cos(ang), -np.sin(ang)], axis=-1)
    return np.pad(z, ((0, 0), (0, width - z.shape[1]))).astype(np.float32)


def _decay_rates(d_h):
    min_decay = math.log(DECAY_TARGET) / DECAY_SLOW_PCT
    max_decay = math.log(DECAY_TARGET) / DECAY_FAST_PCT
    return np.abs(np.linspace(min_decay, max_decay, d_h)).astype(np.float32).reshape(1, d_h)


def _dft_tables(blk):
    odd = 2 * np.arange(blk, dtype=np.int64) + 1
    shifted = ((odd[:, None] * odd[None, :]) % (8 * blk)).astype(np.float64) * (2.0 * math.pi / (8 * blk))
    plain = ((odd[:, None] * np.arange(blk, dtype=np.int64)[None, :]) % (4 * blk)).astype(np.float64) * (2.0 * math.pi / (4 * blk))
    sgn = (1.0 - 2.0 * (np.arange(blk) % 2)).astype(np.float32).reshape(blk, 1)
    f32 = lambda a: a.astype(np.float32)
    return f32(np.cos(shifted)), f32(np.sin(shifted)), f32(np.cos(plain)), f32(np.sin(plain)), sgn


def kernel(x, c, ctx, c_ctx, w_ada, b_ada, norm1_g, w_in, conv_a_w, conv_a_b, lru_wa, lru_ba, lru_wx, lru_bx, lru_lambda, conv_b_w, filt_w1, filt_b1, filt_freq1, filt_w2, filt_b2, filt_freq2, filt_w3, filt_b3, filt_bias, out_norm_a, out_norm_b, w_out, norm2_g, w_rg, b_rg, w_re, b_re, w_gate, w_up, w_down, final_g):
    bsz, n, d = x.shape
    n_ctx = ctx.shape[1]
    d_rnn = conv_a_w.shape[2]
    d_h = filt_bias.shape[2]
    order = filt_bias.shape[1]
    assert w_ada.shape[0] == 1, "single-layer block"
    l = 0
    ch = 256

    mod_rows = 16
    cc = jnp.concatenate([c, c_ctx[None, :], jnp.zeros((mod_rows - bsz - 1, d), F32)], axis=0)
    mod3 = _ada(cc, w_ada[l], b_ada[l]).reshape(mod_rows, N_MOD, d)

    pe = jnp.asarray(_sincos_table(n // GRID_W, GRID_W, d))
    p3 = _inproj_lat(x, pe, mod3, norm1_g[l], w_in[l], 512)
    pc3 = _inproj_ctx(ctx.reshape(bsz * n_ctx, d), mod3, bsz, norm1_g[l], w_in[l], 1, d_rnn,
                      1024).reshape(bsz, n_ctx, d_rnn)

    heads_per_blk = ch // (d_rnn // RNN_HEADS)
    nblk = d_rnn // ch
    eye = jnp.eye(heads_per_blk, dtype=F32)

    def blockdiag(w):
        w5 = w.reshape(2, nblk, heads_per_blk, w.shape[2], w.shape[3])
        return jnp.einsum("dnkij,kl->dnkilj", w5, eye).reshape(2, nblk, ch, ch)

    wa_bd, wx_bd = blockdiag(lru_wa[l]), blockdiag(lru_wx[l])
    w_gates = jnp.concatenate([wa_bd[0], wx_bd[0], wa_bd[1], wx_bd[1]], axis=-1).astype(BF16)
    ba, bx = lru_ba[l].reshape(2, nblk, ch), lru_bx[l].reshape(2, nblk, ch)
    bias = jnp.concatenate([ba[0], bx[0], ba[1], bx[1]], axis=-1).reshape(nblk, 1, 4 * ch)
    sp = jax.nn.softplus(-lru_lambda[l])
    cb = conv_a_b[l].reshape(1, d_rnn)
    zeros_state = jnp.zeros((bsz, 1, d_rnn), F32)
    hf_ctx, hb_ctx = _rglru(pc3, 0, 0, conv_a_w[l], cb, w_gates, bias, sp, zeros_state, zeros_state, False, ch)
    ya, _, _ = _rglru(p3, d_rnn // ch, 0, conv_a_w[l], cb, w_gates, bias, sp, hf_ctx, hb_ctx, True, ch)

    as_bf16 = lambda a: jnp.asarray(a).astype(BF16)
    c3, s3, cu, su, sgn = _dft_tables(CONV_BLOCK)
    fe = 64
    z = jnp.asarray(_filter_features(n, fe))
    w1 = jnp.pad(filt_w1[l], ((0, fe - filt_w1.shape[1]), (0, 0)))
    hid = filt_w2.shape[1]
    hr, hi = _filters(z, w1, filt_b1[l].reshape(1, hid), filt_freq1[l].reshape(1, hid), filt_w2[l],
                      filt_b2[l].reshape(1, hid), filt_freq2[l].reshape(1, hid), filt_w3[l],
                      filt_b3[l].reshape(1, -1), jnp.asarray(_decay_rates(d_h)), as_bf16(cu), as_bf16(su),
                      jnp.asarray(sgn), d_h, order, ch)
    yb = _hyena(p3, 2 * d_rnn // ch, conv_b_w[l], filt_bias[l], as_bf16(c3), as_bf16(s3), hr, hi, d_h, ch)

    w_r = jnp.concatenate([w_re[l], w_rg[l], jnp.zeros((d, LANES - N_EXPERTS - N_GROUPS), F32)], axis=1)
    b_r = jnp.concatenate([b_re[l], b_rg[l], jnp.zeros((LANES - N_EXPERTS - N_GROUPS,), F32)]).reshape(1, LANES)
    tb = 256
    tm = 512
    x1, h2, rt, cnt, prow = _merge(ya, yb, x, pe, mod3, out_norm_a[l], out_norm_b[l], w_out[l], norm2_g[l],
                                   w_r, b_r, tm, tm // tb)

    tile = 256
    t_all = bsz * n
    nb = t_all // tb
    lp = 2 * tb + N_EXPERTS * SEG
    n_tiles = (2 * t_all + nb * N_EXPERTS * (SEG - 1) + N_EXPERTS * (tile - 1)) // tile
    cnt_blocks = cnt[:, 0, :N_EXPERTS].astype(jnp.int32)
    tab, row_ends, ltot, tile_expert, n_used, nxt, par = _moe_layout(cnt_blocks, tile, n_tiles)
    xs = _dispatch(tab, row_ends, ltot, prow, h2.reshape(t_all, d), n_tiles * tile, tb, lp, tile)
    ys = _ffn(tile_expert, n_used, nxt, par, xs, w_gate[l], w_up[l], w_down[l], tile)
    out = _combine(tab, ltot, ys, rt.reshape(t_all, LANES), x1.reshape(t_all, d), mod3, final_g, tb, lp, n // tb)
    return out.reshape(bsz, n, d)
```

```python
import functools
import math

import numpy as np
import jax
import jax.numpy as jnp
from jax import lax
from jax.experimental import pallas as pl
from jax.experimental.pallas import tpu as pltpu

F32 = jnp.float32
BF16 = jnp.bfloat16
EPS = 1e-6
LRU_C = 8.0
N_MOD = 6
GRID_W = 64
RNN_HEADS = 8
N_GROUPS = 4
EXPERTS_PER_GROUP = 8
N_EXPERTS = N_GROUPS * EXPERTS_PER_GROUP
FILTER_BANDS = 16
DECAY_FAST_PCT = 0.3
DECAY_SLOW_PCT = 1.5
DECAY_TARGET = 1e-2
SUBLANES = 8
LANES = 128
SEG = SUBLANES
VMEM_LIMIT = 60 * 1024 * 1024


def _cparams(*sem):
    return pltpu.CompilerParams(dimension_semantics=sem, vmem_limit_bytes=VMEM_LIMIT)


def _dot(a, b):
    return jnp.dot(a, b, preferred_element_type=F32)


def _split(a):
    hi = a.astype(BF16)
    return hi, (a - hi.astype(F32)).astype(BF16)


def _dot3(a, b):
    a_hi, a_lo = _split(a)
    b_hi, b_lo = _split(b)
    return _dot(a_hi, b_hi) + _dot(a_lo, b_hi) + _dot(a_hi, b_lo)


def _rms(v, g):
    return v * lax.rsqrt(jnp.mean(v * v, axis=-1, keepdims=True) + EPS) * g


def _sigmoid(z):
    return 1.0 / (1.0 + jnp.exp(-z))


def _shift_rows(v, d):
    n = v.shape[0]
    rolled = pltpu.roll(v, (-d) % n, 0)
    row = lax.broadcasted_iota(jnp.int32, v.shape, 0)
    ok = (row + d >= 0) & (row + d < n)
    return jnp.where(ok, rolled, 0.0)


def _tile_rot(v, s):
    n, c = v.shape
    return pltpu.roll(v.reshape(n // SUBLANES, SUBLANES, c), s, 1).reshape(n, c)


def _ada_body(c_ref, w_ref, b_ref, o_ref):
    c = c_ref[...]
    o_ref[...] = _dot3(c * _sigmoid(c), w_ref[...]) + b_ref[...]


def _ada(cc, w_ada, b_ada):
    rows, d = cc.shape
    n = w_ada.shape[1]
    tn = 1024
    return pl.pallas_call(
        _ada_body,
        grid=(n // tn,),
        in_specs=[pl.BlockSpec((rows, d), lambda j: (0, 0)),
                  pl.BlockSpec((d, tn), lambda j: (0, j)),
                  pl.BlockSpec((1, tn), lambda j: (0, j))],
        out_specs=pl.BlockSpec((rows, tn), lambda j: (0, j)),
        out_shape=jax.ShapeDtypeStruct((rows, n), F32),
        compiler_params=_cparams("arbitrary"),
        name="ada",
    )(cc, w_ada, b_ada.reshape(1, n))


def _inproj_lat_body(x_ref, pe_ref, mod_ref, g_ref, w_ref, o_ref, wb_ref):
    @pl.when((pl.program_id(0) == 0) & (pl.program_id(1) == 0))
    def _():
        wb_ref[...] = w_ref[...].astype(BF16)

    x = x_ref[...] + pe_ref[...]
    h = _rms(x, g_ref[...]) * (1.0 + mod_ref[1:2, :]) + mod_ref[0:1, :]
    o_ref[...] = _dot(h.astype(BF16), wb_ref[...]).astype(o_ref.dtype)


def _inproj_lat(x3, pe, mod3, g, w, tm):
    b, l, d = x3.shape
    n_out = w.shape[1]
    return pl.pallas_call(
        _inproj_lat_body,
        grid=(l // tm, b),
        in_specs=[pl.BlockSpec((None, tm, d), lambda i, bi: (bi, i, 0)),
                  pl.BlockSpec((tm, d), lambda i, bi: (i, 0)),
                  pl.BlockSpec((None, N_MOD, d), lambda i, bi: (bi, 0, 0)),
                  pl.BlockSpec((1, d), lambda i, bi: (0, 0)),
                  pl.BlockSpec((d, n_out), lambda i, bi: (0, 0), pipeline_mode=pl.Buffered(1))],
        out_specs=pl.BlockSpec((None, tm, n_out), lambda i, bi: (bi, i, 0)),
        out_shape=jax.ShapeDtypeStruct((b, l, n_out), BF16),
        scratch_shapes=[pltpu.VMEM((d, n_out), BF16)],
        compiler_params=_cparams("arbitrary", "arbitrary"),
        name="inproj_pe",
    )(x3, pe, mod3, g.reshape(1, d), w)


def _inproj_ctx_body(x_ref, mod_ref, g_ref, w_ref, o_ref):
    h = _rms(x_ref[...], g_ref[...]) * (1.0 + mod_ref[1:2, :]) + mod_ref[0:1, :]
    o_ref[...] = _dot(h.astype(BF16), w_ref[...].astype(BF16))


def _inproj_ctx(x2, mod3, mod_row, g, w, col_block, n_out, tm):
    r, d = x2.shape
    return pl.pallas_call(
        _inproj_ctx_body,
        grid=(r // tm,),
        in_specs=[pl.BlockSpec((tm, d), lambda i: (i, 0)),
                  pl.BlockSpec((None, N_MOD, d), lambda i: (mod_row, 0, 0)),
                  pl.BlockSpec((1, d), lambda i: (0, 0)),
                  pl.BlockSpec((d, n_out), lambda i: (0, col_block))],
        out_specs=pl.BlockSpec((tm, n_out), lambda i: (i, 0)),
        out_shape=jax.ShapeDtypeStruct((r, n_out), F32),
        compiler_params=_cparams("arbitrary"),
        name="inproj_ctx",
    )(x2, mod3, g.reshape(1, d), w)


def _rglru_body(*refs, with_gate):
    if with_gate:
        (u_ref, gate_ref, cw_ref, cb_ref, w_ref, bias_ref, sp_ref, h0f_ref, h0b_ref,
         y_ref, hf_end_ref, hb_end_ref, af_ref, bf_ref, ab_ref, bb_ref) = refs
    else:
        (u_ref, cw_ref, cb_ref, w_ref, bias_ref, sp_ref, h0f_ref, h0b_ref,
         hf_end_ref, hb_end_ref, af_ref, bf_ref, ab_ref, bb_ref) = refs
    n, c = u_ref.shape
    u = u_ref[...].astype(F32)
    v = cb_ref[...] + cw_ref[1:2, :] * _shift_rows(u, -1) + cw_ref[0:1, :] * _shift_rows(u, -2)
    v = v + cw_ref[2:3, :] * u + cw_ref[3:4, :] * _shift_rows(u, 1)
    z = _dot(v.astype(BF16), w_ref[...]) + bias_ref[...]
    sub = lax.broadcasted_iota(jnp.int32, (n, c), 0) % SUBLANES

    def local_scan(k, reverse, a_ref, b_ref):
        r = _sigmoid(z[:, (2 * k) * c:(2 * k + 1) * c])
        i = _sigmoid(z[:, (2 * k + 1) * c:(2 * k + 2) * c])
        log_a = (-LRU_C) * r * sp_ref[k:k + 1, :]
        a = jnp.exp(log_a)
        b = jnp.sqrt(1.0 - a * a) * (i * v)
        for s in (1, 2, 4):
            ok = (sub < SUBLANES - s) if reverse else (sub >= s)
            shift = SUBLANES - s if reverse else s
            b = a * jnp.where(ok, _tile_rot(b, shift), 0.0) + b
            a = a * jnp.where(ok, _tile_rot(a, shift), 1.0)
        a_ref[...] = a
        b_ref[...] = b

    local_scan(0, False, af_ref, bf_ref)
    local_scan(1, True, ab_ref, bb_ref)

    n_tiles = n // SUBLANES

    def carry_step(q, carry):
        cf, cb = carry
        rf = pl.ds(pl.multiple_of(q * SUBLANES, SUBLANES), SUBLANES)
        rb = pl.ds(pl.multiple_of((n_tiles - 1 - q) * SUBLANES, SUBLANES), SUBLANES)
        hf = af_ref[rf, :] * cf + bf_ref[rf, :]
        hb = ab_ref[rb, :] * cb + bb_ref[rb, :]
        bf_ref[rf, :] = hf
        bb_ref[rb, :] = hb
        return (jnp.broadcast_to(hf[SUBLANES - 1:SUBLANES, :], (SUBLANES, c)),
                jnp.broadcast_to(hb[0:1, :], (SUBLANES, c)))

    cf0 = jnp.broadcast_to(h0f_ref[...], (SUBLANES, c))
    cb0 = jnp.broadcast_to(h0b_ref[...], (SUBLANES, c))
    cf, cb = lax.fori_loop(0, n_tiles, carry_step, (cf0, cb0), unroll=4)
    hf_end_ref[...] = cf[0:1, :]
    hb_end_ref[...] = cb[0:1, :]
    if with_gate:
        y_ref[...] = jax.nn.gelu(gate_ref[...].astype(F32), approximate=True) * (bf_ref[...] + bb_ref[...])


def _rglru(p3, u_blk0, gate_blk0, cw, cb, w_gates, bias, sp, h0f, h0b, with_gate, ch):
    b, l, _ = p3.shape
    d_rnn = cw.shape[1]
    nh = d_rnn // ch
    in_specs = [pl.BlockSpec((None, l, ch), lambda bi, h: (bi, 0, u_blk0 + h))]
    args = [p3]
    if with_gate:
        in_specs.append(pl.BlockSpec((None, l, ch), lambda bi, h: (bi, 0, gate_blk0 + h)))
        args.append(p3)
    in_specs += [pl.BlockSpec((4, ch), lambda bi, h: (0, h)),
                 pl.BlockSpec((1, ch), lambda bi, h: (0, h)),
                 pl.BlockSpec((None, ch, 4 * ch), lambda bi, h: (h, 0, 0)),
                 pl.BlockSpec((None, 1, 4 * ch), lambda bi, h: (h, 0, 0)),
                 pl.BlockSpec((2, ch), lambda bi, h: (0, h)),
                 pl.BlockSpec((None, 1, ch), lambda bi, h: (bi, 0, h)),
                 pl.BlockSpec((None, 1, ch), lambda bi, h: (bi, 0, h))]
    args += [cw, cb, w_gates, bias, sp, h0f, h0b]
    end_spec = pl.BlockSpec((None, 1, ch), lambda bi, h: (bi, 0, h))
    end_shape = jax.ShapeDtypeStruct((b, 1, d_rnn), F32)
    out_specs = [end_spec, end_spec]
    out_shape = [end_shape, end_shape]
    if with_gate:
        out_specs = [pl.BlockSpec((None, l, ch), lambda bi, h: (bi, 0, h))] + out_specs
        out_shape = [jax.ShapeDtypeStruct((b, l, d_rnn), F32)] + out_shape
    return pl.pallas_call(
        functools.partial(_rglru_body, with_gate=with_gate),
        grid=(b, nh),
        in_specs=in_specs,
        out_specs=out_specs,
        out_shape=out_shape,
        scratch_shapes=[pltpu.VMEM((l, ch), F32)] * 4,
        compiler_params=_cparams("arbitrary", "arbitrary"),
        name="rglru_lat" if with_gate else "rglru_ctx",
    )(*args)


CONV_BLOCK = 512


def _filt_body(z_ref, w1_ref, b1_ref, f1_ref, w2_ref, b2_ref, f2_ref, w3f_ref, w3b_ref, b3f_ref, b3b_ref,
               dec_ref, cu_ref, su_ref, sgn_ref, hr_ref, hi_ref):
    n = z_ref.shape[0]
    blk = cu_ref.shape[0]
    nblk = n // blk
    ch = hr_ref.shape[2]
    hdn = jnp.sin(f1_ref[...] * (_dot3(z_ref[...], w1_ref[...]) + b1_ref[...]))
    hdn = jnp.sin(f2_ref[...] * (_dot3(hdn, w2_ref[...]) + b2_ref[...]))
    decay = jnp.exp(-z_ref[:, 0:1] * dec_ref[...])
    kf = (_dot3(hdn, w3f_ref[...]) + b3f_ref[...]) * decay
    kb = (_dot3(hdn, w3b_ref[...]) + b3b_ref[...]) * decay
    row = lax.broadcasted_iota(jnp.int32, (n, ch), 0)
    kb = jnp.where(row == 0, 0.0, kb)
    norm = jnp.sum(jnp.abs(kf) + jnp.abs(kb), axis=0, keepdims=True)
    scale = (2.0 / (2 * blk)) / norm
    sgn = sgn_ref[...]

    p, q, first = [], [], []
    for c in range(nblk):
        taps = jnp.concatenate([kf[c * blk:(c + 1) * blk], kb[c * blk:(c + 1) * blk]], axis=1).astype(BF16)
        p.append(_dot(cu_ref[...], taps))
        q.append(_dot(su_ref[...], taps))
        first.append(taps[0:1, :].astype(F32))
    fwd = lambda a: a[:, :ch]
    bwd = lambda a: a[:, ch:]
    out = lambda a: (a * scale).astype(hr_ref.dtype)
    hr_ref[nblk - 1] = out(fwd(p[0]) + bwd(p[0]))
    hi_ref[nblk - 1] = out(bwd(q[0]) - fwd(q[0]))
    for c in range(1, nblk):
        hr_ref[nblk - 1 + c] = out(fwd(p[c]) + sgn * fwd(q[c - 1]))
        hi_ref[nblk - 1 + c] = out(sgn * (fwd(p[c - 1]) - fwd(first[c - 1])) - fwd(q[c]))
        hr_ref[nblk - 1 - c] = out(bwd(p[c]) + sgn * bwd(q[c - 1]))
        hi_ref[nblk - 1 - c] = out(bwd(q[c]) - sgn * (bwd(p[c - 1]) - bwd(first[c - 1])))


def _filters(z, w1, b1, f1, w2, b2, f2, w3, b3, deltas, cu, su, sgn, d_h, order, ch):
    n, fe = z.shape
    hid = w2.shape[0]
    blk = cu.shape[0]
    nh = 2 * (n // blk) - 1
    per_order = d_h // ch
    ncol = order * per_order
    const = lambda shape: pl.BlockSpec(shape, lambda g: (0, 0))
    hspec = pl.BlockSpec((nh, blk, ch), lambda g: (0, 0, g))
    return pl.pallas_call(
        _filt_body,
        grid=(ncol,),
        in_specs=[const((n, fe)), const((fe, hid)), const((1, hid)), const((1, hid)),
                  const((hid, hid)), const((1, hid)), const((1, hid)),
                  pl.BlockSpec((hid, ch), lambda g: (0, g)),
                  pl.BlockSpec((hid, ch), lambda g: (0, ncol + g)),
                  pl.BlockSpec((1, ch), lambda g: (0, g)),
                  pl.BlockSpec((1, ch), lambda g: (0, ncol + g)),
                  pl.BlockSpec((1, ch), lambda g: (0, g % per_order)),
                  const((blk, blk)), const((blk, blk)), const((blk, 1))],
        out_specs=[hspec, hspec],
        out_shape=[jax.ShapeDtypeStruct((nh, blk, order * d_h), BF16)] * 2,
        compiler_params=_cparams("arbitrary"),
        name="hyena_filters",
    )(z, w1, b1, f1, w2, b2, f2, w3, w3, b3, b3, deltas, cu, su, sgn)


def _hyena_body(v_ref, x1_ref, x2_ref, wv_ref, w1_ref, w2_ref, fb_ref, c3_ref, s3_ref,
                h0r_ref, h0i_ref, h1r_ref, h1i_ref, o_ref, z_ref, zb_ref, p_ref, q_ref, yr_ref, yi_ref):
    n = o_ref.shape[0]
    blk = c3_ref.shape[0]
    nblk = n // blk
    halo = 2 * SUBLANES
    sub = 32

    def conv3(src_ref, w_ref, r0):
        lo, hi = max(r0 - halo, 0), min(r0 + blk + halo, n)
        win = src_ref[lo:hi, :].astype(F32)
        size = hi - lo
        row = lax.broadcasted_iota(jnp.int32, win.shape, 0)
        prev = pltpu.roll(win, 1, 0)
        nxt = pltpu.roll(win, size - 1, 0)
        if lo == 0:
            prev = jnp.where(row == 0, 0.0, prev)
        if hi == n:
            nxt = jnp.where(row == size - 1, 0.0, nxt)
        out = w_ref[0:1, :] * prev + w_ref[1:2, :] * win + w_ref[2:3, :] * nxt
        return out[r0 - lo:r0 - lo + blk, :]

    def long_conv(hr_ref, hi_ref, fb, gate_ref, gate_w_ref, dst_ref):
        for i in range(nblk):
            rows = slice(i * blk, (i + 1) * blk)
            p_ref[rows, :] = _dot(c3_ref[...], zb_ref[rows, :]).astype(BF16)
            q_ref[rows, :] = _dot(s3_ref[...], zb_ref[rows, :]).astype(BF16)

        for j in range(nblk):
            for r0 in range(0, blk, sub):
                yr = yi = None
                for i in range(nblk):
                    rows = slice(i * blk + r0, i * blk + r0 + sub)
                    p, q = p_ref[rows, :], q_ref[rows, :]
                    gr = hr_ref[nblk - 1 + j - i, r0:r0 + sub, :]
                    gi = hi_ref[nblk - 1 + j - i, r0:r0 + sub, :]
                    tr, ti = p * gr + q * gi, q * gr - p * gi
                    yr, yi = (tr, ti) if yr is None else (yr + tr, yi + ti)
                yr_ref[j * blk + r0:j * blk + r0 + sub, :] = yr
                yi_ref[j * blk + r0:j * blk + r0 + sub, :] = yi
            rows = slice(j * blk, (j + 1) * blk)
            y = _dot(c3_ref[...], yr_ref[rows, :]) + _dot(s3_ref[...], yi_ref[rows, :])
            dst_ref[rows, :] = conv3(gate_ref, gate_w_ref, j * blk) * (y + z_ref[rows, :] * fb)

    for r0 in range(0, n, blk):
        z_ref[r0:r0 + blk, :] = conv3(v_ref, wv_ref, r0)
    zb_ref[...] = z_ref[...].astype(BF16)
    long_conv(h0r_ref, h0i_ref, fb_ref[0:1, :], x1_ref, w1_ref, z_ref)
    zb_ref[...] = z_ref[...].astype(BF16)
    long_conv(h1r_ref, h1i_ref, fb_ref[1:2, :], x2_ref, w2_ref, o_ref)


def _hyena(p3, col_blk0, conv_w, fbias, c3, s3, hr, hi, d_h, ch):
    b, n, _ = p3.shape
    nh, blk, _ = hr.shape
    per = d_h // ch
    zspec = lambda k: pl.BlockSpec((None, n, ch), lambda h, bi: (bi, 0, col_blk0 + k * per + h))
    wspec = lambda k: pl.BlockSpec((3, ch), lambda h, bi: (0, k * per + h))
    hspec = lambda o: pl.BlockSpec((nh, blk, ch), lambda h, bi: (0, 0, o * per + h), pipeline_mode=pl.Buffered(1))
    const = pl.BlockSpec((blk, blk), lambda h, bi: (0, 0))
    return pl.pallas_call(
        _hyena_body,
        grid=(per, b),
        in_specs=[zspec(0), zspec(1), zspec(2), wspec(0), wspec(1), wspec(2),
                  pl.BlockSpec((2, ch), lambda h, bi: (0, h)), const, const,
                  hspec(0), hspec(0), hspec(1), hspec(1)],
        out_specs=pl.BlockSpec((None, n, ch), lambda h, bi: (bi, 0, h)),
        out_shape=jax.ShapeDtypeStruct((b, n, d_h), F32),
        scratch_shapes=[pltpu.VMEM((n, ch), F32), pltpu.VMEM((n, ch), BF16),
                        pltpu.VMEM((n, ch), BF16), pltpu.VMEM((n, ch), BF16),
                        pltpu.VMEM((n, ch), BF16), pltpu.VMEM((n, ch), BF16)],
        compiler_params=_cparams("arbitrary", "arbitrary"),
        name="hyena_mix",
    )(p3, p3, p3, conv_w, conv_w, conv_w, fbias, c3, s3, hr, hi, hr, hi)


def _merge_body(ya_ref, yb_ref, x_ref, pe_ref, mod_ref, ga_ref, gb_ref, wo_ref, g2_ref, wr_ref, br_ref,
                x1_ref, h2_ref, rt_ref, cnt_ref, prow_ref, wob_ref):
    @pl.when((pl.program_id(0) == 0) & (pl.program_id(1) == 0))
    def _():
        wob_ref[...] = wo_ref[...].astype(BF16)

    da = ya_ref.shape[1]
    na = _rms(ya_ref[...], ga_ref[...]).astype(BF16)
    nb = _rms(yb_ref[...], gb_ref[...]).astype(BF16)
    y = _dot(na, wob_ref[0:da, :]) + _dot(nb, wob_ref[da:, :])
    _route(y, x_ref, pe_ref, mod_ref, g2_ref, wr_ref, br_ref, x1_ref, h2_ref, rt_ref, cnt_ref, prow_ref)


def _route(y, x_ref, pe_ref, mod_ref, g2_ref, wr_ref, br_ref, x1_ref, h2_ref, rt_ref, cnt_ref, prow_ref):
    x1 = x_ref[...] + pe_ref[...] + mod_ref[2:3, :] * y
    x1_ref[...] = x1
    h2 = _rms(x1, g2_ref[...]) * (1.0 + mod_ref[4:5, :]) + mod_ref[3:4, :]
    h2_ref[...] = h2.astype(BF16)

    logits = _dot3(h2, wr_ref[...]) + br_ref[...]
    lane = lax.broadcasted_iota(jnp.int32, logits.shape, 1).astype(F32)
    neg = -jnp.inf
    big = jnp.float32(1 << 20)
    gl = jnp.where((lane >= N_EXPERTS) & (lane < N_EXPERTS + N_GROUPS), logits, neg)
    gmax = jnp.max(gl, axis=1, keepdims=True)
    g_p = 1.0 / jnp.sum(jnp.exp(gl - gmax), axis=1, keepdims=True)
    g_i = jnp.min(jnp.where(gl == gmax, lane, big), axis=1, keepdims=True) - N_EXPERTS
    lo = g_i * EXPERTS_PER_GROUP
    el = jnp.where((lane >= lo) & (lane < lo + EXPERTS_PER_GROUP), logits, neg)
    m1 = jnp.max(el, axis=1, keepdims=True)
    i1 = jnp.min(jnp.where(el == m1, lane, big), axis=1, keepdims=True)
    el2 = jnp.where(lane == i1, neg, el)
    m2 = jnp.max(el2, axis=1, keepdims=True)
    i2 = jnp.min(jnp.where(el2 == m2, lane, big), axis=1, keepdims=True)
    e2 = jnp.exp(m2 - m1)
    w1 = g_p / (1.0 + e2)
    w2 = g_p * e2 / (1.0 + e2)
    o1 = jnp.where(lane == i1, 1.0, 0.0)
    o2 = jnp.where(lane == i2, 1.0, 0.0)
    nsub = cnt_ref.shape[0]
    sub = o1.shape[0] // nsub
    sq = lambda shape, d: lax.broadcasted_iota(jnp.int32, shape, d)
    earlier = jnp.where(sq((sub, sub), 1) < sq((sub, sub), 0), 1.0, 0.0).astype(BF16)
    below = jnp.where(sq((LANES, LANES), 0) < sq((LANES, LANES), 1), 1.0, 0.0).astype(BF16)
    pick_row = jnp.where(sq((SUBLANES, LANES), 0) == sq((SUBLANES, LANES), 1), 1.0, 0.0).astype(BF16)
    row_id = sq((SUBLANES, sub), 0)
    pos1, pos2 = [], []
    for k in range(nsub):
        a1, a2 = o1[k * sub:(k + 1) * sub], o2[k * sub:(k + 1) * sub]
        both = a1 + a2
        count = jnp.sum(both, axis=0, keepdims=True)
        cnt_ref[k] = jnp.broadcast_to(count, cnt_ref.shape[1:])
        runs = jnp.broadcast_to(jnp.floor((count + (SEG - 1)) / SEG), (SUBLANES, LANES))
        start = _dot(runs.astype(BF16), below)[0:1, :] * SEG
        before = _dot(earlier, both.astype(BF16)) + start
        p1 = jnp.sum(a1 * before, axis=1, keepdims=True)
        p2 = jnp.sum(a2 * before, axis=1, keepdims=True)
        pos1.append(p1)
        pos2.append(p2)
        lane_k = sq((sub, LANES), 1).astype(F32)
        hi1, hi2 = jnp.floor(p1 / 64.0), jnp.floor(p2 / 64.0)
        parts = (jnp.where(lane_k == 0.0, hi1, 0.0) + jnp.where(lane_k == 1.0, p1 - 64.0 * hi1, 0.0)
                 + jnp.where(lane_k == 2.0, hi2, 0.0) + jnp.where(lane_k == 3.0, p2 - 64.0 * hi2, 0.0))
        rows = lax.dot_general(pick_row, parts.astype(BF16), (((1,), (1,)), ((), ())), preferred_element_type=F32)
        q1 = rows[0:1, :] * 64.0 + rows[1:2, :]
        q2 = rows[2:3, :] * 64.0 + rows[3:4, :]
        prow_ref[k] = jnp.where(row_id == 0, q1, jnp.where(row_id == 1, q2, 0.0))
    p1 = jnp.concatenate(pos1, axis=0)
    p2 = jnp.concatenate(pos2, axis=0)
    rt_ref[...] = (jnp.where(lane == 0.0, i1, 0.0) + jnp.where(lane == 1.0, i2, 0.0)
                   + jnp.where(lane == 2.0, w1, 0.0) + jnp.where(lane == 3.0, w2, 0.0)
                   + jnp.where(lane == 4.0, p1, 0.0) + jnp.where(lane == 5.0, p2, 0.0))


def _merge(ya, yb, x3, pe, mod3, ga, gb, w_out, g2n, w_r, b_r, tm, nsub):
    b, n, d = x3.shape
    da, db = ya.shape[2], yb.shape[2]
    per = n // tm
    tok = lambda w: pl.BlockSpec((None, tm, w), lambda i, bi: (bi, i, 0))
    const = lambda shape: pl.BlockSpec(shape, lambda i, bi: (0, 0))
    blk = lambda w: pl.BlockSpec((nsub, SUBLANES, w), lambda i, bi: (bi * per + i, 0, 0))
    return pl.pallas_call(
        _merge_body,
        grid=(per, b),
        in_specs=[tok(da), tok(db), tok(d), pl.BlockSpec((tm, d), lambda i, bi: (i, 0)),
                  pl.BlockSpec((None, N_MOD, d), lambda i, bi: (bi, 0, 0)),
                  const((1, da)), const((1, db)), const((da + db, d)), const((1, d)),
                  const((d, LANES)), const((1, LANES))],
        out_specs=[tok(d), tok(d), tok(LANES), blk(LANES), blk(tm // nsub)],
        out_shape=[jax.ShapeDtypeStruct((b, n, d), F32), jax.ShapeDtypeStruct((b, n, d), BF16),
                   jax.ShapeDtypeStruct((b, n, LANES), F32),
                   jax.ShapeDtypeStruct((b * per * nsub, SUBLANES, LANES), F32),
                   jax.ShapeDtypeStruct((b * per * nsub, SUBLANES, tm // nsub), F32)],
        scratch_shapes=[pltpu.VMEM((da + db, d), BF16)],
        compiler_params=_cparams("arbitrary", "arbitrary"),
        name="merge_route",
    )(ya, yb, x3, pe, mod3, ga.reshape(1, da), gb.reshape(1, db), w_out, g2n.reshape(1, d), w_r, b_r)


def _segment_copy(local_ref, lrow, global_ref, grow, rows, sem, to_global):
    lrows = pl.ds(pl.multiple_of(lrow, SEG), rows)
    grows = pl.ds(pl.multiple_of(grow, SEG), rows)
    if to_global:
        return pltpu.make_async_copy(local_ref.at[lrows, :], global_ref.at[grows, :], sem)
    return pltpu.make_async_copy(global_ref.at[grows, :], local_ref.at[lrows, :], sem)


def _start_segments(tab_ref, blk, local_ref, global_ref, sem, to_global):
    def per_expert(e, carry):
        k = (blk * N_EXPERTS + e) * 3
        ls, gs, nchunk = tab_ref[k], tab_ref[k + 1], tab_ref[k + 2]

        def per_chunk(ci, carry2):
            _segment_copy(local_ref, ls + ci * SEG, global_ref, gs + ci * SEG, SEG, sem, to_global).start()
            return carry2

        return lax.fori_loop(0, nchunk, per_chunk, carry)

    lax.fori_loop(0, N_EXPERTS, per_expert, 0)


def _wait_segments(rows, local_ref, global_ref, sem, to_global):
    chunks = rows // SEG
    nbits = (local_ref.shape[0] // SEG).bit_length()
    for k in range(nbits):
        @pl.when(((chunks >> k) & 1) == 1)
        def _():
            _segment_copy(local_ref, 0, global_ref, 0, SEG << k, sem, to_global).wait()


def _dispatch_body(tab_ref, ends_ref, ltot_ref, prow_ref, h2_ref, xs_ref, loc_ref, zero_ref, sem, zsem):
    blk = pl.program_id(0)
    nblk = pl.num_programs(0)
    tb = h2_ref.shape[0]
    lp = loc_ref.shape[1]
    tile = zero_ref.shape[0]
    slot = blk % 2

    @pl.when(blk == 0)
    def _():
        zero_ref[...] = jnp.zeros_like(zero_ref)

        def zero_fill(first_row, rows):
            dst = xs_ref.at[pl.ds(pl.multiple_of(first_row, SEG), rows), :]
            return pltpu.make_async_copy(zero_ref.at[pl.ds(0, rows), :], dst, zsem)

        pad = 64

        def fill(e, carry, wait):
            end, real = ends_ref[e], ends_ref[N_EXPERTS + e]
            first = real // pad * pad

            def one(j, carry2):
                cp = zero_fill(first + j * pad, pad)
                cp.wait() if wait else cp.start()
                return carry2

            return lax.fori_loop(0, (end - first) // pad, one, carry)

        lax.fori_loop(0, N_EXPERTS, functools.partial(fill, wait=False), 0)
        used = ends_ref[N_EXPERTS - 1]
        spare = (xs_ref.shape[0] - used) // tile

        def fill_spare(j, carry, wait):
            cp = zero_fill(used + j * tile, tile)
            cp.wait() if wait else cp.start()
            return carry

        lax.fori_loop(0, spare, functools.partial(fill_spare, wait=False), 0)
        lax.fori_loop(0, N_EXPERTS, functools.partial(fill, wait=True), 0)
        lax.fori_loop(0, spare, functools.partial(fill_spare, wait=True), 0)

    pos = lax.broadcasted_iota(jnp.int32, (lp, tb), 0).astype(F32)
    onehot = jnp.where((pos == prow_ref[0:1, :]) | (pos == prow_ref[1:2, :]), 1.0, 0.0).astype(BF16)
    loc_ref[slot] = _dot(onehot, h2_ref[...])

    _start_segments(tab_ref, blk, loc_ref.at[slot], xs_ref, sem.at[slot], True)

    @pl.when(blk > 0)
    def _():
        _wait_segments(ltot_ref[blk - 1], loc_ref.at[1 - slot], xs_ref, sem.at[1 - slot], True)

    @pl.when(blk == nblk - 1)
    def _():
        _wait_segments(ltot_ref[blk], loc_ref.at[slot], xs_ref, sem.at[slot], True)


def _dispatch(tab, ends, ltot, prow, h2, slots, tb, lp, tile):
    t, d = h2.shape
    return pl.pallas_call(
        _dispatch_body,
        grid_spec=pltpu.PrefetchScalarGridSpec(
            num_scalar_prefetch=3,
            grid=(t // tb,),
            in_specs=[pl.BlockSpec((None, SUBLANES, tb), lambda i, *_: (i, 0, 0)),
                      pl.BlockSpec((tb, d), lambda i, *_: (i, 0))],
            out_specs=pl.BlockSpec(memory_space=pl.ANY),
            scratch_shapes=[pltpu.VMEM((2, lp, d), F32), pltpu.VMEM((tile, d), F32),
                            pltpu.SemaphoreType.DMA((2,)), pltpu.SemaphoreType.DMA]),
        out_shape=jax.ShapeDtypeStruct((slots, d), F32),
        compiler_params=_cparams("arbitrary"),
        name="moe_dispatch",
    )(tab, ends, ltot, prow, h2)


def _ffn_body(te_ref, nu_ref, nxt_ref, par_ref, xs_ref, wg_hbm, wu_hbm, wd_hbm, ys_ref,
              wg_buf, wu_buf, wd_buf, wgb_ref, wub_ref, wdb_ref, sem):
    i = pl.program_id(0)
    e = te_ref[i]
    first = (i == 0) | (e != te_ref[jnp.maximum(i - 1, 0)])
    active = i < nu_ref[0]

    def fetch(expert, slot):
        return [pltpu.make_async_copy(w.at[expert], buf.at[slot], sem.at[slot])
                for w, buf in ((wg_hbm, wg_buf), (wu_hbm, wu_buf), (wd_hbm, wd_buf))]

    @pl.when(i == 0)
    def _():
        for cp in fetch(e, par_ref[e]):
            cp.start()

    @pl.when(first & active)
    def _():
        slot = par_ref[e]
        for cp in fetch(e, slot):
            cp.wait()
        nxt = nxt_ref[e]

        @pl.when(nxt >= 0)
        def _():
            for cp in fetch(nxt, 1 - slot):
                cp.start()

        wgb_ref[...] = wg_buf[slot].astype(BF16)
        wub_ref[...] = wu_buf[slot].astype(BF16)
        wdb_ref[...] = wd_buf[slot].astype(BF16)

    @pl.when(active)
    def _():
        xb = xs_ref[...].astype(BF16)
        act = _dot(xb, wgb_ref[...])
        act = act * _sigmoid(act) * _dot(xb, wub_ref[...])
        ys_ref[...] = _dot(act.astype(BF16), wdb_ref[...])


def _ffn(tile_expert, n_used, nxt, par, xs, w_gate, w_up, w_down, tile):
    slots, d = xs.shape
    _, _, de = w_gate.shape
    last = lambda i, te, nu, *_: (jnp.minimum(i, nu[0] - 1), 0)
    hbm = pl.BlockSpec(memory_space=pl.ANY)
    return pl.pallas_call(
        _ffn_body,
        grid_spec=pltpu.PrefetchScalarGridSpec(
            num_scalar_prefetch=4,
            grid=(slots // tile,),
            in_specs=[pl.BlockSpec((tile, d), last), hbm, hbm, hbm],
            out_specs=pl.BlockSpec((tile, d), last),
            scratch_shapes=[pltpu.VMEM((2, d, de), F32), pltpu.VMEM((2, d, de), F32), pltpu.VMEM((2, de, d), F32),
                            pltpu.VMEM((d, de), BF16), pltpu.VMEM((d, de), BF16), pltpu.VMEM((de, d), BF16),
                            pltpu.SemaphoreType.DMA((2,))]),
        out_shape=jax.ShapeDtypeStruct((slots, d), F32),
        input_output_aliases={4: 0},
        compiler_params=_cparams("arbitrary"),
        name="moe_ffn",
    )(tile_expert, n_used, nxt, par, xs, w_gate, w_up, w_down)


def _combine_body(tab_ref, ltot_ref, ys_ref, rt_ref, x1_ref, mod_ref, fg_ref, o_ref, loc_ref, sem):
    blk = pl.program_id(0)
    nblk = pl.num_programs(0)
    _, lp, d = loc_ref.shape
    slot = blk % 2

    @pl.when(blk == 0)
    def _():
        _start_segments(tab_ref, blk, loc_ref.at[slot], ys_ref, sem.at[slot], False)

    @pl.when(blk + 1 < nblk)
    def _():
        _start_segments(tab_ref, blk + 1, loc_ref.at[1 - slot], ys_ref, sem.at[1 - slot], False)

    rt = rt_ref[...]
    lane = lax.broadcasted_iota(jnp.int32, rt.shape, 1)
    col = lambda k: jnp.sum(jnp.where(lane == k, rt, 0.0), axis=1, keepdims=True)
    w1, w2, p1, p2 = col(2), col(3), col(4), col(5)
    pos = lax.broadcasted_iota(jnp.int32, (rt.shape[0], lp), 1).astype(F32)
    weights = (jnp.where(pos == p1, w1, 0.0) + jnp.where(pos == p2, w2, 0.0)).astype(BF16)
    _wait_segments(ltot_ref[blk], loc_ref.at[slot], ys_ref, sem.at[slot], False)

    def clear(ci, carry):
        loc_ref[slot, pl.ds(pl.multiple_of(ci * SEG, SEG), SEG), :] = jnp.zeros((SEG, d), F32)
        return carry

    lax.fori_loop(ltot_ref[blk] // SEG, lp // SEG, clear, 0)
    moe = _dot(weights, loc_ref[slot].astype(BF16))
    o_ref[...] = _rms(x1_ref[...] + mod_ref[5:6, :] * moe, fg_ref[...])


def _combine(tab, ltot, ys, rt2, x1, mod3, final_g, tb, lp, per_batch):
    t, d = x1.shape
    return pl.pallas_call(
        _combine_body,
        grid_spec=pltpu.PrefetchScalarGridSpec(
            num_scalar_prefetch=2,
            grid=(t // tb,),
            in_specs=[pl.BlockSpec(memory_space=pl.ANY),
                      pl.BlockSpec((tb, LANES), lambda i, *_: (i, 0)),
                      pl.BlockSpec((tb, d), lambda i, *_: (i, 0)),
                      pl.BlockSpec((None, N_MOD, d), lambda i, *_: (i // per_batch, 0, 0)),
                      pl.BlockSpec((1, d), lambda i, *_: (0, 0))],
            out_specs=pl.BlockSpec((tb, d), lambda i, *_: (i, 0)),
            scratch_shapes=[pltpu.VMEM((2, lp, d), F32), pltpu.SemaphoreType.DMA((2,))]),
        out_shape=jax.ShapeDtypeStruct((t, d), F32),
        compiler_params=_cparams("arbitrary"),
        name="moe_combine",
    )(tab, ltot, ys, rt2, x1, mod3, final_g.reshape(1, d))


def _moe_layout(cnt_blocks, tile, n_tiles):
    lcnt = (cnt_blocks + SEG - 1) // SEG * SEG
    lstart = jnp.cumsum(lcnt, axis=1) - lcnt
    ltot = jnp.sum(lcnt, axis=1)
    per_expert = jnp.sum(lcnt, axis=0)
    tiles_per = (per_expert + tile - 1) // tile
    tile_ends = jnp.cumsum(tiles_per)
    row_ends = tile_ends * tile
    goff = row_ends - tiles_per * tile
    gstart = goff[None, :] + jnp.cumsum(lcnt, axis=0) - lcnt
    tab = jnp.stack([lstart, gstart, lcnt // SEG], axis=-1).reshape(-1).astype(jnp.int32)
    n_used = tile_ends[-1:]
    tile_ids = jnp.arange(n_tiles, dtype=jnp.int32)
    tile_expert = jnp.sum((tile_ends[None, :] <= jnp.minimum(tile_ids, n_used - 1)[:, None]).astype(jnp.int32), axis=1)
    ids = jnp.arange(N_EXPERTS, dtype=jnp.int32)
    used = tiles_per > 0
    par = (jnp.cumsum(used) - used) % 2
    later = jnp.where(used[None, :] & (ids[None, :] > ids[:, None]), ids[None, :], N_EXPERTS)
    nxt = jnp.min(later, axis=1)
    nxt = jnp.where(nxt == N_EXPERTS, -1, nxt)
    ends = jnp.concatenate([row_ends, goff + per_expert])
    return (tab, ends.astype(jnp.int32), ltot.astype(jnp.int32), tile_expert.astype(jnp.int32),
            n_used.astype(jnp.int32), nxt.astype(jnp.int32), par.astype(jnp.int32))


def _sincos_table(rows, cols, dim):
    quarter = dim // 4
    omega = 1.0 / (10000.0 ** (np.arange(quarter, dtype=np.float64) / quarter))
    ang_r = np.arange(rows, dtype=np.float64)[:, None] * omega
    ang_c = np.arange(cols, dtype=np.float64)[:, None] * omega
    emb_r = np.concatenate([np.sin(ang_r), np.cos(ang_r)], axis=-1)
    emb_c = np.concatenate([np.sin(ang_c), np.cos(ang_c)], axis=-1)
    pe = np.concatenate([np.broadcast_to(emb_r[:, None, :], (rows, cols, 2 * quarter)),
                         np.broadcast_to(emb_c[None, :, :], (rows, cols, 2 * quarter))], axis=-1)
    return pe.reshape(rows * cols, 4 * quarter).astype(np.float32)


def _filter_features(n, width):
    pos = np.arange(n, dtype=np.float64)
    t = pos / max(n - 1, 1)
    ang = (2.0 * math.pi * pos / n)[:, None] * np.linspace(1e-4, FILTER_BANDS - 1, FILTER_BANDS)[None, :]
    z = np.concatenate([t[:, None], np.cos(ang), -np.sin(ang)], axis=-1)
    return np.pad(z, ((0, 0), (0, width - z.shape[1]))).astype(np.float32)


def _decay_rates(d_h):
    min_decay = math.log(DECAY_TARGET) / DECAY_SLOW_PCT
    max_decay = math.log(DECAY_TARGET) / DECAY_FAST_PCT
    return np.abs(np.linspace(min_decay, max_decay, d_h)).astype(np.float32).reshape(1, d_h)


def _dft_tables(blk):
    odd = 2 * np.arange(blk, dtype=np.int64) + 1
    shifted = ((odd[:, None] * odd[None, :]) % (8 * blk)).astype(np.float64) * (2.0 * math.pi / (8 * blk))
    plain = ((odd[:, None] * np.arange(blk, dtype=np.int64)[None, :]) % (4 * blk)).astype(np.float64) * (2.0 * math.pi / (4 * blk))
    sgn = (1.0 - 2.0 * (np.arange(blk) % 2)).astype(np.float32).reshape(blk, 1)
    f32 = lambda a: a.astype(np.float32)
    return f32(np.cos(shifted)), f32(np.sin(shifted)), f32(np.cos(plain)), f32(np.sin(plain)), sgn


def kernel(x, c, ctx, c_ctx, w_ada, b_ada, norm1_g, w_in, conv_a_w, conv_a_b, lru_wa, lru_ba, lru_wx, lru_bx, lru_lambda, conv_b_w, filt_w1, filt_b1, filt_freq1, filt_w2, filt_b2, filt_freq2, filt_w3, filt_b3, filt_bias, out_norm_a, out_norm_b, w_out, norm2_g, w_rg, b_rg, w_re, b_re, w_gate, w_up, w_down, final_g):
    bsz, n, d = x.shape
    n_ctx = ctx.shape[1]
    d_rnn = conv_a_w.shape[2]
    d_h = filt_bias.shape[2]
    order = filt_bias.shape[1]
    assert w_ada.shape[0] == 1, "single-layer block"
    l = 0
    ch = 256

    mod_rows = 16
    cc = jnp.concatenate([c, c_ctx[None, :], jnp.zeros((mod_rows - bsz - 1, d), F32)], axis=0)
    mod3 = _ada(cc, w_ada[l], b_ada[l]).reshape(mod_rows, N_MOD, d)

    pe = jnp.asarray(_sincos_table(n // GRID_W, GRID_W, d))
    p3 = _inproj_lat(x, pe, mod3, norm1_g[l], w_in[l], 512)
    pc3 = _inproj_ctx(ctx.reshape(bsz * n_ctx, d), mod3, bsz, norm1_g[l], w_in[l], 1, d_rnn,
                      1024).reshape(bsz, n_ctx, d_rnn)

    heads_per_blk = ch // (d_rnn // RNN_HEADS)
    nblk = d_rnn // ch
    eye = jnp.eye(heads_per_blk, dtype=F32)

    def blockdiag(w):
        w5 = w.reshape(2, nblk, heads_per_blk, w.shape[2], w.shape[3])
        return jnp.einsum("dnkij,kl->dnkilj", w5, eye).reshape(2, nblk, ch, ch)

    wa_bd, wx_bd = blockdiag(lru_wa[l]), blockdiag(lru_wx[l])
    w_gates = jnp.concatenate([wa_bd[0], wx_bd[0], wa_bd[1], wx_bd[1]], axis=-1).astype(BF16)
    ba, bx = lru_ba[l].reshape(2, nblk, ch), lru_bx[l].reshape(2, nblk, ch)
    bias = jnp.concatenate([ba[0], bx[0], ba[1], bx[1]], axis=-1).reshape(nblk, 1, 4 * ch)
    sp = jax.nn.softplus(-lru_lambda[l])
    cb = conv_a_b[l].reshape(1, d_rnn)
    zeros_state = jnp.zeros((bsz, 1, d_rnn), F32)
    hf_ctx, hb_ctx = _rglru(pc3, 0, 0, conv_a_w[l], cb, w_gates, bias, sp, zeros_state, zeros_state, False, ch)
    ya, _, _ = _rglru(p3, d_rnn // ch, 0, conv_a_w[l], cb, w_gates, bias, sp, hf_ctx, hb_ctx, True, ch)

    as_bf16 = lambda a: jnp.asarray(a).astype(BF16)
    c3, s3, cu, su, sgn = _dft_tables(CONV_BLOCK)
    fe = 64
    z = jnp.asarray(_filter_features(n, fe))
    w1 = jnp.pad(filt_w1[l], ((0, fe - filt_w1.shape[1]), (0, 0)))
    hid = filt_w2.shape[1]
    hr, hi = _filters(z, w1, filt_b1[l].reshape(1, hid), filt_freq1[l].reshape(1, hid), filt_w2[l],
                      filt_b2[l].reshape(1, hid), filt_freq2[l].reshape(1, hid), filt_w3[l],
                      filt_b3[l].reshape(1, -1), jnp.asarray(_decay_rates(d_h)), as_bf16(cu), as_bf16(su),
                      jnp.asarray(sgn), d_h, order, ch)
    yb = _hyena(p3, 2 * d_rnn // ch, conv_b_w[l], filt_bias[l], as_bf16(c3), as_bf16(s3), hr, hi, d_h, ch)

    w_r = jnp.concatenate([w_re[l], w_rg[l], jnp.zeros((d, LANES - N_EXPERTS - N_GROUPS), F32)], axis=1)
    b_r = jnp.concatenate([b_re[l], b_rg[l], jnp.zeros((LANES - N_EXPERTS - N_GROUPS,), F32)]).reshape(1, LANES)
    tb = 256
    tm = 512
    x1, h2, rt, cnt, prow = _merge(ya, yb, x, pe, mod3, out_norm_a[l], out_norm_b[l], w_out[l], norm2_g[l],
                                   w_r, b_r, tm, tm // tb)

    tile = 512
    t_all = bsz * n
    nb = t_all // tb
    lp = 2 * tb + N_EXPERTS * SEG
    n_tiles = (2 * t_all + nb * N_EXPERTS * (SEG - 1) + N_EXPERTS * (tile - 1)) // tile
    cnt_blocks = cnt[:, 0, :N_EXPERTS].astype(jnp.int32)
    tab, row_ends, ltot, tile_expert, n_used, nxt, par = _moe_layout(cnt_blocks, tile, n_tiles)
    xs = _dispatch(tab, row_ends, ltot, prow, h2.reshape(t_all, d), n_tiles * tile, tb, lp, tile)
    ys = _ffn(tile_expert, n_used, nxt, par, xs, w_gate[l], w_up[l], w_down[l], tile)
    out = _combine(tab, ltot, ys, rt.reshape(t_all, LANES), x1.reshape(t_all, d), mod3, final_g, tb, lp, n // tb)
    return out.reshape(bsz, n, d)
```

```python
import functools
import math

import numpy as np
import jax
import jax.numpy as jnp
from jax import lax
from jax.experimental import pallas as pl
from jax.experimental.pallas import tpu as pltpu

F32 = jnp.float32
BF16 = jnp.bfloat16
EPS = 1e-6
LRU_C = 8.0
N_MOD = 6
GRID_W = 64
RNN_HEADS = 8
N_GROUPS = 4
EXPERTS_PER_GROUP = 8
N_EXPERTS = N_GROUPS * EXPERTS_PER_GROUP
FILTER_BANDS = 16
DECAY_FAST_PCT = 0.3
DECAY_SLOW_PCT = 1.5
DECAY_TARGET = 1e-2
SUBLANES = 8
LANES = 128
SEG = 2 * SUBLANES
VMEM_LIMIT = 60 * 1024 * 1024


def _cparams(*sem):
    return pltpu.CompilerParams(dimension_semantics=sem, vmem_limit_bytes=VMEM_LIMIT)


def _dot(a, b):
    return jnp.dot(a, b, preferred_element_type=F32)


def _split(a):
    hi = a.astype(BF16)
    return hi, (a - hi.astype(F32)).astype(BF16)


def _dot3(a, b):
    a_hi, a_lo = _split(a)
    b_hi, b_lo = _split(b)
    return _dot(a_hi, b_hi) + _dot(a_lo, b_hi) + _dot(a_hi, b_lo)


def _rms(v, g):
    return v * lax.rsqrt(jnp.mean(v * v, axis=-1, keepdims=True) + EPS) * g


def _sigmoid(z):
    return 1.0 / (1.0 + jnp.exp(-z))


def _shift_rows(v, d):
    n = v.shape[0]
    rolled = pltpu.roll(v, (-d) % n, 0)
    row = lax.broadcasted_iota(jnp.int32, v.shape, 0)
    ok = (row + d >= 0) & (row + d < n)
    return jnp.where(ok, rolled, 0.0)


def _tile_rot(v, s):
    n, c = v.shape
    return pltpu.roll(v.reshape(n // SUBLANES, SUBLANES, c), s, 1).reshape(n, c)


def _ada_body(c_ref, w_ref, b_ref, o_ref):
    c = c_ref[...]
    o_ref[...] = _dot3(c * _sigmoid(c), w_ref[...]) + b_ref[...]


def _ada(cc, w_ada, b_ada):
    rows, d = cc.shape
    n = w_ada.shape[1]
    tn = 1024
    return pl.pallas_call(
        _ada_body,
        grid=(n // tn,),
        in_specs=[pl.BlockSpec((rows, d), lambda j: (0, 0)),
                  pl.BlockSpec((d, tn), lambda j: (0, j)),
                  pl.BlockSpec((1, tn), lambda j: (0, j))],
        out_specs=pl.BlockSpec((rows, tn), lambda j: (0, j)),
        out_shape=jax.ShapeDtypeStruct((rows, n), F32),
        compiler_params=_cparams("arbitrary"),
        name="ada",
    )(cc, w_ada, b_ada.reshape(1, n))


def _inproj_lat_body(x_ref, pe_ref, mod_ref, g_ref, w_ref, o_ref, wb_ref):
    @pl.when((pl.program_id(0) == 0) & (pl.program_id(1) == 0))
    def _():
        wb_ref[...] = w_ref[...].astype(BF16)

    x = x_ref[...] + pe_ref[...]
    h = _rms(x, g_ref[...]) * (1.0 + mod_ref[1:2, :]) + mod_ref[0:1, :]
    o_ref[...] = _dot(h.astype(BF16), wb_ref[...]).astype(o_ref.dtype)


def _inproj_lat(x3, pe, mod3, g, w, tm):
    b, l, d = x3.shape
    n_out = w.shape[1]
    return pl.pallas_call(
        _inproj_lat_body,
        grid=(l // tm, b),
        in_specs=[pl.BlockSpec((None, tm, d), lambda i, bi: (bi, i, 0)),
                  pl.BlockSpec((tm, d), lambda i, bi: (i, 0)),
                  pl.BlockSpec((None, N_MOD, d), lambda i, bi: (bi, 0, 0)),
                  pl.BlockSpec((1, d), lambda i, bi: (0, 0)),
                  pl.BlockSpec((d, n_out), lambda i, bi: (0, 0), pipeline_mode=pl.Buffered(1))],
        out_specs=pl.BlockSpec((None, tm, n_out), lambda i, bi: (bi, i, 0)),
        out_shape=jax.ShapeDtypeStruct((b, l, n_out), BF16),
        scratch_shapes=[pltpu.VMEM((d, n_out), BF16)],
        compiler_params=_cparams("arbitrary", "arbitrary"),
        name="inproj_pe",
    )(x3, pe, mod3, g.reshape(1, d), w)


def _inproj_ctx_body(x_ref, mod_ref, g_ref, w_ref, o_ref):
    h = _rms(x_ref[...], g_ref[...]) * (1.0 + mod_ref[1:2, :]) + mod_ref[0:1, :]
    o_ref[...] = _dot(h.astype(BF16), w_ref[...].astype(BF16))


def _inproj_ctx(x2, mod3, mod_row, g, w, col_block, n_out, tm):
    r, d = x2.shape
    return pl.pallas_call(
        _inproj_ctx_body,
        grid=(r // tm,),
        in_specs=[pl.BlockSpec((tm, d), lambda i: (i, 0)),
                  pl.BlockSpec((None, N_MOD, d), lambda i: (mod_row, 0, 0)),
                  pl.BlockSpec((1, d), lambda i: (0, 0)),
                  pl.BlockSpec((d, n_out), lambda i: (0, col_block))],
        out_specs=pl.BlockSpec((tm, n_out), lambda i: (i, 0)),
        out_shape=jax.ShapeDtypeStruct((r, n_out), F32),
        compiler_params=_cparams("arbitrary"),
        name="inproj_ctx",
    )(x2, mod3, g.reshape(1, d), w)


def _rglru_body(*refs, with_gate):
    if with_gate:
        (u_ref, gate_ref, cw_ref, cb_ref, w_ref, bias_ref, sp_ref, h0f_ref, h0b_ref,
         y_ref, hf_end_ref, hb_end_ref, af_ref, bf_ref, ab_ref, bb_ref) = refs
    else:
        (u_ref, cw_ref, cb_ref, w_ref, bias_ref, sp_ref, h0f_ref, h0b_ref,
         hf_end_ref, hb_end_ref, af_ref, bf_ref, ab_ref, bb_ref) = refs
    n, c = u_ref.shape
    u = u_ref[...].astype(F32)
    v = cb_ref[...] + cw_ref[1:2, :] * _shift_rows(u, -1) + cw_ref[0:1, :] * _shift_rows(u, -2)
    v = v + cw_ref[2:3, :] * u + cw_ref[3:4, :] * _shift_rows(u, 1)
    z = _dot(v.astype(BF16), w_ref[...]) + bias_ref[...]
    sub = lax.broadcasted_iota(jnp.int32, (n, c), 0) % SUBLANES

    def local_scan(k, reverse, a_ref, b_ref):
        r = _sigmoid(z[:, (2 * k) * c:(2 * k + 1) * c])
        i = _sigmoid(z[:, (2 * k + 1) * c:(2 * k + 2) * c])
        log_a = (-LRU_C) * r * sp_ref[k:k + 1, :]
        a = jnp.exp(log_a)
        b = jnp.sqrt(1.0 - a * a) * (i * v)
        for s in (1, 2, 4):
            ok = (sub < SUBLANES - s) if reverse else (sub >= s)
            shift = SUBLANES - s if reverse else s
            b = a * jnp.where(ok, _tile_rot(b, shift), 0.0) + b
            a = a * jnp.where(ok, _tile_rot(a, shift), 1.0)
        a_ref[...] = a
        b_ref[...] = b

    local_scan(0, False, af_ref, bf_ref)
    local_scan(1, True, ab_ref, bb_ref)

    n_tiles = n // SUBLANES

    def carry_step(q, carry):
        cf, cb = carry
        rf = pl.ds(pl.multiple_of(q * SUBLANES, SUBLANES), SUBLANES)
        rb = pl.ds(pl.multiple_of((n_tiles - 1 - q) * SUBLANES, SUBLANES), SUBLANES)
        hf = af_ref[rf, :] * cf + bf_ref[rf, :]
        hb = ab_ref[rb, :] * cb + bb_ref[rb, :]
        bf_ref[rf, :] = hf
        bb_ref[rb, :] = hb
        return (jnp.broadcast_to(hf[SUBLANES - 1:SUBLANES, :], (SUBLANES, c)),
                jnp.broadcast_to(hb[0:1, :], (SUBLANES, c)))

    cf0 = jnp.broadcast_to(h0f_ref[...], (SUBLANES, c))
    cb0 = jnp.broadcast_to(h0b_ref[...], (SUBLANES, c))
    cf, cb = lax.fori_loop(0, n_tiles, carry_step, (cf0, cb0), unroll=4)
    hf_end_ref[...] = cf[0:1, :]
    hb_end_ref[...] = cb[0:1, :]
    if with_gate:
        y_ref[...] = jax.nn.gelu(gate_ref[...].astype(F32), approximate=True) * (bf_ref[...] + bb_ref[...])


def _rglru(p3, u_blk0, gate_blk0, cw, cb, w_gates, bias, sp, h0f, h0b, with_gate, ch):
    b, l, _ = p3.shape
    d_rnn = cw.shape[1]
    nh = d_rnn // ch
    in_specs = [pl.BlockSpec((None, l, ch), lambda bi, h: (bi, 0, u_blk0 + h))]
    args = [p3]
    if with_gate:
        in_specs.append(pl.BlockSpec((None, l, ch), lambda bi, h: (bi, 0, gate_blk0 + h)))
        args.append(p3)
    in_specs += [pl.BlockSpec((4, ch), lambda bi, h: (0, h)),
                 pl.BlockSpec((1, ch), lambda bi, h: (0, h)),
                 pl.BlockSpec((None, ch, 4 * ch), lambda bi, h: (h, 0, 0)),
                 pl.BlockSpec((None, 1, 4 * ch), lambda bi, h: (h, 0, 0)),
                 pl.BlockSpec((2, ch), lambda bi, h: (0, h)),
                 pl.BlockSpec((None, 1, ch), lambda bi, h: (bi, 0, h)),
                 pl.BlockSpec((None, 1, ch), lambda bi, h: (bi, 0, h))]
    args += [cw, cb, w_gates, bias, sp, h0f, h0b]
    end_spec = pl.BlockSpec((None, 1, ch), lambda bi, h: (bi, 0, h))
    end_shape = jax.ShapeDtypeStruct((b, 1, d_rnn), F32)
    out_specs = [end_spec, end_spec]
    out_shape = [end_shape, end_shape]
    if with_gate:
        out_specs = [pl.BlockSpec((None, l, ch), lambda bi, h: (bi, 0, h))] + out_specs
        out_shape = [jax.ShapeDtypeStruct((b, l, d_rnn), F32)] + out_shape
    return pl.pallas_call(
        functools.partial(_rglru_body, with_gate=with_gate),
        grid=(b, nh),
        in_specs=in_specs,
        out_specs=out_specs,
        out_shape=out_shape,
        scratch_shapes=[pltpu.VMEM((l, ch), F32)] * 4,
        compiler_params=_cparams("arbitrary", "arbitrary"),
        name="rglru_lat" if with_gate else "rglru_ctx",
    )(*args)


CONV_BLOCK = 512


def _filt_body(z_ref, w1_ref, b1_ref, f1_ref, w2_ref, b2_ref, f2_ref, w3f_ref, w3b_ref, b3f_ref, b3b_ref,
               dec_ref, cu_ref, su_ref, sgn_ref, hr_ref, hi_ref):
    n = z_ref.shape[0]
    blk = cu_ref.shape[0]
    nblk = n // blk
    ch = hr_ref.shape[2]
    hdn = jnp.sin(f1_ref[...] * (_dot3(z_ref[...], w1_ref[...]) + b1_ref[...]))
    hdn = jnp.sin(f2_ref[...] * (_dot3(hdn, w2_ref[...]) + b2_ref[...]))
    decay = jnp.exp(-z_ref[:, 0:1] * dec_ref[...])
    kf = (_dot3(hdn, w3f_ref[...]) + b3f_ref[...]) * decay
    kb = (_dot3(hdn, w3b_ref[...]) + b3b_ref[...]) * decay
    row = lax.broadcasted_iota(jnp.int32, (n, ch), 0)
    kb = jnp.where(row == 0, 0.0, kb)
    norm = jnp.sum(jnp.abs(kf) + jnp.abs(kb), axis=0, keepdims=True)
    scale = (2.0 / (2 * blk)) / norm
    sgn = sgn_ref[...]

    p, q, first = [], [], []
    for c in range(nblk):
        taps = jnp.concatenate([kf[c * blk:(c + 1) * blk], kb[c * blk:(c + 1) * blk]], axis=1).astype(BF16)
        p.append(_dot(cu_ref[...], taps))
        q.append(_dot(su_ref[...], taps))
        first.append(taps[0:1, :].astype(F32))
    fwd = lambda a: a[:, :ch]
    bwd = lambda a: a[:, ch:]
    out = lambda a: (a * scale).astype(hr_ref.dtype)
    hr_ref[nblk - 1] = out(fwd(p[0]) + bwd(p[0]))
    hi_ref[nblk - 1] = out(bwd(q[0]) - fwd(q[0]))
    for c in range(1, nblk):
        hr_ref[nblk - 1 + c] = out(fwd(p[c]) + sgn * fwd(q[c - 1]))
        hi_ref[nblk - 1 + c] = out(sgn * (fwd(p[c - 1]) - fwd(first[c - 1])) - fwd(q[c]))
        hr_ref[nblk - 1 - c] = out(bwd(p[c]) + sgn * bwd(q[c - 1]))
        hi_ref[nblk - 1 - c] = out(bwd(q[c]) - sgn * (bwd(p[c - 1]) - bwd(first[c - 1])))


def _filters(z, w1, b1, f1, w2, b2, f2, w3, b3, deltas, cu, su, sgn, d_h, order, ch):
    n, fe = z.shape
    hid = w2.shape[0]
    blk = cu.shape[0]
    nh = 2 * (n // blk) - 1
    per_order = d_h // ch
    ncol = order * per_order
    const = lambda shape: pl.BlockSpec(shape, lambda g: (0, 0))
    hspec = pl.BlockSpec((nh, blk, ch), lambda g: (0, 0, g))
    return pl.pallas_call(
        _filt_body,
        grid=(ncol,),
        in_specs=[const((n, fe)), const((fe, hid)), const((1, hid)), const((1, hid)),
                  const((hid, hid)), const((1, hid)), const((1, hid)),
                  pl.BlockSpec((hid, ch), lambda g: (0, g)),
                  pl.BlockSpec((hid, ch), lambda g: (0, ncol + g)),
                  pl.BlockSpec((1, ch), lambda g: (0, g)),
                  pl.BlockSpec((1, ch), lambda g: (0, ncol + g)),
                  pl.BlockSpec((1, ch), lambda g: (0, g % per_order)),
                  const((blk, blk)), const((blk, blk)), const((blk, 1))],
        out_specs=[hspec, hspec],
        out_shape=[jax.ShapeDtypeStruct((nh, blk, order * d_h), BF16)] * 2,
        compiler_params=_cparams("arbitrary"),
        name="hyena_filters",
    )(z, w1, b1, f1, w2, b2, f2, w3, w3, b3, b3, deltas, cu, su, sgn)


def _hyena_body(v_ref, x1_ref, x2_ref, wv_ref, w1_ref, w2_ref, fb_ref, c3_ref, s3_ref,
                h0r_ref, h0i_ref, h1r_ref, h1i_ref, o_ref, z_ref, zb_ref, p_ref, q_ref, yr_ref, yi_ref):
    n = o_ref.shape[0]
    blk = c3_ref.shape[0]
    nblk = n // blk
    halo = 2 * SUBLANES
    sub = 32

    def conv3(src_ref, w_ref, r0):
        lo, hi = max(r0 - halo, 0), min(r0 + blk + halo, n)
        win = src_ref[lo:hi, :].astype(F32)
        size = hi - lo
        row = lax.broadcasted_iota(jnp.int32, win.shape, 0)
        prev = pltpu.roll(win, 1, 0)
        nxt = pltpu.roll(win, size - 1, 0)
        if lo == 0:
            prev = jnp.where(row == 0, 0.0, prev)
        if hi == n:
            nxt = jnp.where(row == size - 1, 0.0, nxt)
        out = w_ref[0:1, :] * prev + w_ref[1:2, :] * win + w_ref[2:3, :] * nxt
        return out[r0 - lo:r0 - lo + blk, :]

    def long_conv(hr_ref, hi_ref, fb, gate_ref, gate_w_ref, dst_ref):
        for i in range(nblk):
            rows = slice(i * blk, (i + 1) * blk)
            p_ref[rows, :] = _dot(c3_ref[...], zb_ref[rows, :]).astype(BF16)
            q_ref[rows, :] = _dot(s3_ref[...], zb_ref[rows, :]).astype(BF16)

        for j in range(nblk):
            for r0 in range(0, blk, sub):
                yr = yi = None
                for i in range(nblk):
                    rows = slice(i * blk + r0, i * blk + r0 + sub)
                    p, q = p_ref[rows, :], q_ref[rows, :]
                    gr = hr_ref[nblk - 1 + j - i, r0:r0 + sub, :]
                    gi = hi_ref[nblk - 1 + j - i, r0:r0 + sub, :]
                    tr, ti = p * gr + q * gi, q * gr - p * gi
                    yr, yi = (tr, ti) if yr is None else (yr + tr, yi + ti)
                yr_ref[j * blk + r0:j * blk + r0 + sub, :] = yr
                yi_ref[j * blk + r0:j * blk + r0 + sub, :] = yi
            rows = slice(j * blk, (j + 1) * blk)
            y = _dot(c3_ref[...], yr_ref[rows, :]) + _dot(s3_ref[...], yi_ref[rows, :])
            dst_ref[rows, :] = conv3(gate_ref, gate_w_ref, j * blk) * (y + z_ref[rows, :] * fb)

    for r0 in range(0, n, blk):
        z_ref[r0:r0 + blk, :] = conv3(v_ref, wv_ref, r0)
    zb_ref[...] = z_ref[...].astype(BF16)
    long_conv(h0r_ref, h0i_ref, fb_ref[0:1, :], x1_ref, w1_ref, z_ref)
    zb_ref[...] = z_ref[...].astype(BF16)
    long_conv(h1r_ref, h1i_ref, fb_ref[1:2, :], x2_ref, w2_ref, o_ref)


def _hyena(p3, col_blk0, conv_w, fbias, c3, s3, hr, hi, d_h, ch):
    b, n, _ = p3.shape
    nh, blk, _ = hr.shape
    per = d_h // ch
    zspec = lambda k: pl.BlockSpec((None, n, ch), lambda h, bi: (bi, 0, col_blk0 + k * per + h))
    wspec = lambda k: pl.BlockSpec((3, ch), lambda h, bi: (0, k * per + h))
    hspec = lambda o: pl.BlockSpec((nh, blk, ch), lambda h, bi: (0, 0, o * per + h), pipeline_mode=pl.Buffered(1))
    const = pl.BlockSpec((blk, blk), lambda h, bi: (0, 0))
    return pl.pallas_call(
        _hyena_body,
        grid=(per, b),
        in_specs=[zspec(0), zspec(1), zspec(2), wspec(0), wspec(1), wspec(2),
                  pl.BlockSpec((2, ch), lambda h, bi: (0, h)), const, const,
                  hspec(0), hspec(0), hspec(1), hspec(1)],
        out_specs=pl.BlockSpec((None, n, ch), lambda h, bi: (bi, 0, h)),
        out_shape=jax.ShapeDtypeStruct((b, n, d_h), F32),
        scratch_shapes=[pltpu.VMEM((n, ch), F32), pltpu.VMEM((n, ch), BF16),
                        pltpu.VMEM((n, ch), BF16), pltpu.VMEM((n, ch), BF16),
                        pltpu.VMEM((n, ch), BF16), pltpu.VMEM((n, ch), BF16)],
        compiler_params=_cparams("arbitrary", "arbitrary"),
        name="hyena_mix",
    )(p3, p3, p3, conv_w, conv_w, conv_w, fbias, c3, s3, hr, hi, hr, hi)


def _merge_body(ya_ref, yb_ref, x_ref, pe_ref, mod_ref, ga_ref, gb_ref, wo_ref, g2_ref, wr_ref, br_ref,
                x1_ref, h2_ref, rt_ref, cnt_ref, prow_ref, wob_ref):
    @pl.when((pl.program_id(0) == 0) & (pl.program_id(1) == 0))
    def _():
        wob_ref[...] = wo_ref[...].astype(BF16)

    da = ya_ref.shape[1]
    na = _rms(ya_ref[...], ga_ref[...]).astype(BF16)
    nb = _rms(yb_ref[...], gb_ref[...]).astype(BF16)
    y = _dot(na, wob_ref[0:da, :]) + _dot(nb, wob_ref[da:, :])
    _route(y, x_ref, pe_ref, mod_ref, g2_ref, wr_ref, br_ref, x1_ref, h2_ref, rt_ref, cnt_ref, prow_ref)


def _route(y, x_ref, pe_ref, mod_ref, g2_ref, wr_ref, br_ref, x1_ref, h2_ref, rt_ref, cnt_ref, prow_ref):
    x1 = x_ref[...] + pe_ref[...] + mod_ref[2:3, :] * y
    x1_ref[...] = x1
    h2 = _rms(x1, g2_ref[...]) * (1.0 + mod_ref[4:5, :]) + mod_ref[3:4, :]
    h2_ref[...] = h2.astype(BF16)

    logits = _dot3(h2, wr_ref[...]) + br_ref[...]
    lane = lax.broadcasted_iota(jnp.int32, logits.shape, 1).astype(F32)
    neg = -jnp.inf
    big = jnp.float32(1 << 20)
    gl = jnp.where((lane >= N_EXPERTS) & (lane < N_EXPERTS + N_GROUPS), logits, neg)
    gmax = jnp.max(gl, axis=1, keepdims=True)
    g_p = 1.0 / jnp.sum(jnp.exp(gl - gmax), axis=1, keepdims=True)
    g_i = jnp.min(jnp.where(gl == gmax, lane, big), axis=1, keepdims=True) - N_EXPERTS
    lo = g_i * EXPERTS_PER_GROUP
    el = jnp.where((lane >= lo) & (lane < lo + EXPERTS_PER_GROUP), logits, neg)
    m1 = jnp.max(el, axis=1, keepdims=True)
    i1 = jnp.min(jnp.where(el == m1, lane, big), axis=1, keepdims=True)
    el2 = jnp.where(lane == i1, neg, el)
    m2 = jnp.max(el2, axis=1, keepdims=True)
    i2 = jnp.min(jnp.where(el2 == m2, lane, big), axis=1, keepdims=True)
    e2 = jnp.exp(m2 - m1)
    w1 = g_p / (1.0 + e2)
    w2 = g_p * e2 / (1.0 + e2)
    o1 = jnp.where(lane == i1, 1.0, 0.0)
    o2 = jnp.where(lane == i2, 1.0, 0.0)
    nsub = cnt_ref.shape[0]
    sub = o1.shape[0] // nsub
    sq = lambda shape, d: lax.broadcasted_iota(jnp.int32, shape, d)
    earlier = jnp.where(sq((sub, sub), 1) < sq((sub, sub), 0), 1.0, 0.0).astype(BF16)
    below = jnp.where(sq((LANES, LANES), 0) < sq((LANES, LANES), 1), 1.0, 0.0).astype(BF16)
    pick_row = jnp.where(sq((SUBLANES, LANES), 0) == sq((SUBLANES, LANES), 1), 1.0, 0.0).astype(BF16)
    row_id = sq((SUBLANES, sub), 0)
    pos1, pos2 = [], []
    for k in range(nsub):
        a1, a2 = o1[k * sub:(k + 1) * sub], o2[k * sub:(k + 1) * sub]
        both = a1 + a2
        count = jnp.sum(both, axis=0, keepdims=True)
        cnt_ref[k] = jnp.broadcast_to(count, cnt_ref.shape[1:])
        runs = jnp.broadcast_to(jnp.floor((count + (SEG - 1)) / SEG), (SUBLANES, LANES))
        start = _dot(runs.astype(BF16), below)[0:1, :] * SEG
        before = _dot(earlier, both.astype(BF16)) + start
        p1 = jnp.sum(a1 * before, axis=1, keepdims=True)
        p2 = jnp.sum(a2 * before, axis=1, keepdims=True)
        pos1.append(p1)
        pos2.append(p2)
        lane_k = sq((sub, LANES), 1).astype(F32)
        hi1, hi2 = jnp.floor(p1 / 64.0), jnp.floor(p2 / 64.0)
        parts = (jnp.where(lane_k == 0.0, hi1, 0.0) + jnp.where(lane_k == 1.0, p1 - 64.0 * hi1, 0.0)
                 + jnp.where(lane_k == 2.0, hi2, 0.0) + jnp.where(lane_k == 3.0, p2 - 64.0 * hi2, 0.0))
        rows = lax.dot_general(pick_row, parts.astype(BF16), (((1,), (1,)), ((), ())), preferred_element_type=F32)
        q1 = rows[0:1, :] * 64.0 + rows[1:2, :]
        q2 = rows[2:3, :] * 64.0 + rows[3:4, :]
        prow_ref[k] = jnp.where(row_id == 0, q1, jnp.where(row_id == 1, q2, 0.0))
    p1 = jnp.concatenate(pos1, axis=0)
    p2 = jnp.concatenate(pos2, axis=0)
    rt_ref[...] = (jnp.where(lane == 0.0, i1, 0.0) + jnp.where(lane == 1.0, i2, 0.0)
                   + jnp.where(lane == 2.0, w1, 0.0) + jnp.where(lane == 3.0, w2, 0.0)
                   + jnp.where(lane == 4.0, p1, 0.0) + jnp.where(lane == 5.0, p2, 0.0))


def _merge(ya, yb, x3, pe, mod3, ga, gb, w_out, g2n, w_r, b_r, tm, nsub):
    b, n, d = x3.shape
    da, db = ya.shape[2], yb.shape[2]
    per = n // tm
    tok = lambda w: pl.BlockSpec((None, tm, w), lambda i, bi: (bi, i, 0))
    const = lambda shape: pl.BlockSpec(shape, lambda i, bi: (0, 0))
    blk = lambda w: pl.BlockSpec((nsub, SUBLANES, w), lambda i, bi: (bi * per + i, 0, 0))
    return pl.pallas_call(
        _merge_body,
        grid=(per, b),
        in_specs=[tok(da), tok(db), tok(d), pl.BlockSpec((tm, d), lambda i, bi: (i, 0)),
                  pl.BlockSpec((None, N_MOD, d), lambda i, bi: (bi, 0, 0)),
                  const((1, da)), const((1, db)), const((da + db, d)), const((1, d)),
                  const((d, LANES)), const((1, LANES))],
        out_specs=[tok(d), tok(d), tok(LANES), blk(LANES), blk(tm // nsub)],
        out_shape=[jax.ShapeDtypeStruct((b, n, d), F32), jax.ShapeDtypeStruct((b, n, d), BF16),
                   jax.ShapeDtypeStruct((b, n, LANES), F32),
                   jax.ShapeDtypeStruct((b * per * nsub, SUBLANES, LANES), F32),
                   jax.ShapeDtypeStruct((b * per * nsub, SUBLANES, tm // nsub), F32)],
        scratch_shapes=[pltpu.VMEM((da + db, d), BF16)],
        compiler_params=_cparams("arbitrary", "arbitrary"),
        name="merge_route",
    )(ya, yb, x3, pe, mod3, ga.reshape(1, da), gb.reshape(1, db), w_out, g2n.reshape(1, d), w_r, b_r)


def _segment_copy(local_ref, lrow, global_ref, grow, rows, sem, to_global):
    lrows = pl.ds(pl.multiple_of(lrow, SEG), rows)
    grows = pl.ds(pl.multiple_of(grow, SEG), rows)
    if to_global:
        return pltpu.make_async_copy(local_ref.at[lrows, :], global_ref.at[grows, :], sem)
    return pltpu.make_async_copy(global_ref.at[grows, :], local_ref.at[lrows, :], sem)


def _start_segments(tab_ref, blk, local_ref, global_ref, sem, to_global):
    def per_expert(e, carry):
        k = (blk * N_EXPERTS + e) * 3
        ls, gs, nchunk = tab_ref[k], tab_ref[k + 1], tab_ref[k + 2]

        def per_chunk(ci, carry2):
            _segment_copy(local_ref, ls + ci * SEG, global_ref, gs + ci * SEG, SEG, sem, to_global).start()
            return carry2

        return lax.fori_loop(0, nchunk, per_chunk, carry)

    lax.fori_loop(0, N_EXPERTS, per_expert, 0)


def _wait_segments(rows, local_ref, global_ref, sem, to_global):
    chunks = rows // SEG
    nbits = (local_ref.shape[0] // SEG).bit_length()
    for k in range(nbits):
        @pl.when(((chunks >> k) & 1) == 1)
        def _():
            _segment_copy(local_ref, 0, global_ref, 0, SEG << k, sem, to_global).wait()


def _dispatch_body(tab_ref, ends_ref, ltot_ref, prow_ref, h2_ref, xs_ref, loc_ref, zero_ref, sem, zsem):
    blk = pl.program_id(0)
    nblk = pl.num_programs(0)
    tb = h2_ref.shape[0]
    lp = loc_ref.shape[1]
    tile = zero_ref.shape[0]
    slot = blk % 2

    @pl.when(blk == 0)
    def _():
        zero_ref[...] = jnp.zeros_like(zero_ref)

        def zero_fill(first_row, rows):
            dst = xs_ref.at[pl.ds(pl.multiple_of(first_row, SEG), rows), :]
            return pltpu.make_async_copy(zero_ref.at[pl.ds(0, rows), :], dst, zsem)

        pad = 64

        def fill(e, carry, wait):
            end, real = ends_ref[e], ends_ref[N_EXPERTS + e]
            first = real // pad * pad

            def one(j, carry2):
                cp = zero_fill(first + j * pad, pad)
                cp.wait() if wait else cp.start()
                return carry2

            return lax.fori_loop(0, (end - first) // pad, one, carry)

        lax.fori_loop(0, N_EXPERTS, functools.partial(fill, wait=False), 0)
        used = ends_ref[N_EXPERTS - 1]
        spare = (xs_ref.shape[0] - used) // tile

        def fill_spare(j, carry, wait):
            cp = zero_fill(used + j * tile, tile)
            cp.wait() if wait else cp.start()
            return carry

        lax.fori_loop(0, spare, functools.partial(fill_spare, wait=False), 0)
        lax.fori_loop(0, N_EXPERTS, functools.partial(fill, wait=True), 0)
        lax.fori_loop(0, spare, functools.partial(fill_spare, wait=True), 0)

    pos = lax.broadcasted_iota(jnp.int32, (lp, tb), 0).astype(F32)
    onehot = jnp.where((pos == prow_ref[0:1, :]) | (pos == prow_ref[1:2, :]), 1.0, 0.0).astype(BF16)
    loc_ref[slot] = _dot(onehot, h2_ref[...]).astype(BF16)

    _start_segments(tab_ref, blk, loc_ref.at[slot], xs_ref, sem.at[slot], True)

    @pl.when(blk > 0)
    def _():
        _wait_segments(ltot_ref[blk - 1], loc_ref.at[1 - slot], xs_ref, sem.at[1 - slot], True)

    @pl.when(blk == nblk - 1)
    def _():
        _wait_segments(ltot_ref[blk], loc_ref.at[slot], xs_ref, sem.at[slot], True)


def _dispatch(tab, ends, ltot, prow, h2, slots, tb, lp, tile):
    t, d = h2.shape
    return pl.pallas_call(
        _dispatch_body,
        grid_spec=pltpu.PrefetchScalarGridSpec(
            num_scalar_prefetch=3,
            grid=(t // tb,),
            in_specs=[pl.BlockSpec((None, SUBLANES, tb), lambda i, *_: (i, 0, 0)),
                      pl.BlockSpec((tb, d), lambda i, *_: (i, 0))],
            out_specs=pl.BlockSpec(memory_space=pl.ANY),
            scratch_shapes=[pltpu.VMEM((2, lp, d), BF16), pltpu.VMEM((tile, d), BF16),
                            pltpu.SemaphoreType.DMA((2,)), pltpu.SemaphoreType.DMA]),
        out_shape=jax.ShapeDtypeStruct((slots, d), BF16),
        compiler_params=_cparams("arbitrary"),
        name="moe_dispatch",
    )(tab, ends, ltot, prow, h2)


def _ffn_body(te_ref, nu_ref, nxt_ref, par_ref, xs_ref, wg_hbm, wu_hbm, wd_hbm, ys_ref,
              wg_buf, wu_buf, wd_buf, wgb_ref, wub_ref, wdb_ref, sem):
    i = pl.program_id(0)
    e = te_ref[i]
    first = (i == 0) | (e != te_ref[jnp.maximum(i - 1, 0)])
    active = i < nu_ref[0]

    def fetch(expert, slot):
        return [pltpu.make_async_copy(w.at[expert], buf.at[slot], sem.at[slot])
                for w, buf in ((wg_hbm, wg_buf), (wu_hbm, wu_buf), (wd_hbm, wd_buf))]

    @pl.when(i == 0)
    def _():
        for cp in fetch(e, par_ref[e]):
            cp.start()

    @pl.when(first & active)
    def _():
        slot = par_ref[e]
        for cp in fetch(e, slot):
            cp.wait()
        nxt = nxt_ref[e]

        @pl.when(nxt >= 0)
        def _():
            for cp in fetch(nxt, 1 - slot):
                cp.start()

        wgb_ref[...] = wg_buf[slot].astype(BF16)
        wub_ref[...] = wu_buf[slot].astype(BF16)
        wdb_ref[...] = wd_buf[slot].astype(BF16)

    @pl.when(active)
    def _():
        xb = xs_ref[...]
        act = _dot(xb, wgb_ref[...])
        act = act * _sigmoid(act) * _dot(xb, wub_ref[...])
        ys_ref[...] = _dot(act.astype(BF16), wdb_ref[...]).astype(BF16)


def _ffn(tile_expert, n_used, nxt, par, xs, w_gate, w_up, w_down, tile):
    slots, d = xs.shape
    _, _, de = w_gate.shape
    last = lambda i, te, nu, *_: (jnp.minimum(i, nu[0] - 1), 0)
    hbm = pl.BlockSpec(memory_space=pl.ANY)
    return pl.pallas_call(
        _ffn_body,
        grid_spec=pltpu.PrefetchScalarGridSpec(
            num_scalar_prefetch=4,
            grid=(slots // tile,),
            in_specs=[pl.BlockSpec((tile, d), last), hbm, hbm, hbm],
            out_specs=pl.BlockSpec((tile, d), last),
            scratch_shapes=[pltpu.VMEM((2, d, de), F32), pltpu.VMEM((2, d, de), F32), pltpu.VMEM((2, de, d), F32),
                            pltpu.VMEM((d, de), BF16), pltpu.VMEM((d, de), BF16), pltpu.VMEM((de, d), BF16),
                            pltpu.SemaphoreType.DMA((2,))]),
        out_shape=jax.ShapeDtypeStruct((slots, d), BF16),
        input_output_aliases={4: 0},
        compiler_params=_cparams("arbitrary"),
        name="moe_ffn",
    )(tile_expert, n_used, nxt, par, xs, w_gate, w_up, w_down)


def _combine_body(tab_ref, ltot_ref, ys_ref, rt_ref, x1_ref, mod_ref, fg_ref, o_ref, loc_ref, sem):
    blk = pl.program_id(0)
    nblk = pl.num_programs(0)
    _, lp, d = loc_ref.shape
    slot = blk % 2

    @pl.when(blk == 0)
    def _():
        _start_segments(tab_ref, blk, loc_ref.at[slot], ys_ref, sem.at[slot], False)

    @pl.when(blk + 1 < nblk)
    def _():
        _start_segments(tab_ref, blk + 1, loc_ref.at[1 - slot], ys_ref, sem.at[1 - slot], False)

    rt = rt_ref[...]
    lane = lax.broadcasted_iota(jnp.int32, rt.shape, 1)
    col = lambda k: jnp.sum(jnp.where(lane == k, rt, 0.0), axis=1, keepdims=True)
    w1, w2, p1, p2 = col(2), col(3), col(4), col(5)
    pos = lax.broadcasted_iota(jnp.int32, (rt.shape[0], lp), 1).astype(F32)
    weights = (jnp.where(pos == p1, w1, 0.0) + jnp.where(pos == p2, w2, 0.0)).astype(BF16)
    _wait_segments(ltot_ref[blk], loc_ref.at[slot], ys_ref, sem.at[slot], False)

    def clear(ci, carry):
        loc_ref[slot, pl.ds(pl.multiple_of(ci * SEG, SEG), SEG), :] = jnp.zeros((SEG, d), BF16)
        return carry

    lax.fori_loop(ltot_ref[blk] // SEG, lp // SEG, clear, 0)
    moe = _dot(weights, loc_ref[slot])
    o_ref[...] = _rms(x1_ref[...] + mod_ref[5:6, :] * moe, fg_ref[...])


def _combine(tab, ltot, ys, rt2, x1, mod3, final_g, tb, lp, per_batch):
    t, d = x1.shape
    return pl.pallas_call(
        _combine_body,
        grid_spec=pltpu.PrefetchScalarGridSpec(
            num_scalar_prefetch=2,
            grid=(t // tb,),
            in_specs=[pl.BlockSpec(memory_space=pl.ANY),
                      pl.BlockSpec((tb, LANES), lambda i, *_: (i, 0)),
                      pl.BlockSpec((tb, d), lambda i, *_: (i, 0)),
                      pl.BlockSpec((None, N_MOD, d), lambda i, *_: (i // per_batch, 0, 0)),
                      pl.BlockSpec((1, d), lambda i, *_: (0, 0))],
            out_specs=pl.BlockSpec((tb, d), lambda i, *_: (i, 0)),
            scratch_shapes=[pltpu.VMEM((2, lp, d), BF16), pltpu.SemaphoreType.DMA((2,))]),
        out_shape=jax.ShapeDtypeStruct((t, d), F32),
        compiler_params=_cparams("arbitrary"),
        name="moe_combine",
    )(tab, ltot, ys, rt2, x1, mod3, final_g.reshape(1, d))


def _moe_layout(cnt_blocks, tile, n_tiles):
    lcnt = (cnt_blocks + SEG - 1) // SEG * SEG
    lstart = jnp.cumsum(lcnt, axis=1) - lcnt
    ltot = jnp.sum(lcnt, axis=1)
    per_expert = jnp.sum(lcnt, axis=0)
    tiles_per = (per_expert + tile - 1) // tile
    tile_ends = jnp.cumsum(tiles_per)
    row_ends = tile_ends * tile
    goff = row_ends - tiles_per * tile
    gstart = goff[None, :] + jnp.cumsum(lcnt, axis=0) - lcnt
    tab = jnp.stack([lstart, gstart, lcnt // SEG], axis=-1).reshape(-1).astype(jnp.int32)
    n_used = tile_ends[-1:]
    tile_ids = jnp.arange(n_tiles, dtype=jnp.int32)
    tile_expert = jnp.sum((tile_ends[None, :] <= jnp.minimum(tile_ids, n_used - 1)[:, None]).astype(jnp.int32), axis=1)
    ids = jnp.arange(N_EXPERTS, dtype=jnp.int32)
    used = tiles_per > 0
    par = (jnp.cumsum(used) - used) % 2
    later = jnp.where(used[None, :] & (ids[None, :] > ids[:, None]), ids[None, :], N_EXPERTS)
    nxt = jnp.min(later, axis=1)
    nxt = jnp.where(nxt == N_EXPERTS, -1, nxt)
    ends = jnp.concatenate([row_ends, goff + per_expert])
    return (tab, ends.astype(jnp.int32), ltot.astype(jnp.int32), tile_expert.astype(jnp.int32),
            n_used.astype(jnp.int32), nxt.astype(jnp.int32), par.astype(jnp.int32))


def _sincos_table(rows, cols, dim):
    quarter = dim // 4
    omega = 1.0 / (10000.0 ** (np.arange(quarter, dtype=np.float64) / quarter))
    ang_r = np.arange(rows, dtype=np.float64)[:, None] * omega
    ang_c = np.arange(cols, dtype=np.float64)[:, None] * omega
    emb_r = np.concatenate([np.sin(ang_r), np.cos(ang_r)], axis=-1)
    emb_c = np.concatenate([np.sin(ang_c), np.cos(ang_c)], axis=-1)
    pe = np.concatenate([np.broadcast_to(emb_r[:, None, :], (rows, cols, 2 * quarter)),
                         np.broadcast_to(emb_c[None, :, :], (rows, cols, 2 * quarter))], axis=-1)
    return pe.reshape(rows * cols, 4 * quarter).astype(np.float32)


def _filter_features(n, width):
    pos = np.arange(n, dtype=np.float64)
    t = pos / max(n - 1, 1)
    ang = (2.0 * math.pi * pos / n)[:, None] * np.linspace(1e-4, FILTER_BANDS - 1, FILTER_BANDS)[None, :]
    z = np.concatenate([t[:, None], np.cos(ang), -np.sin(ang)], axis=-1)
    return np.pad(z, ((0, 0), (0, width - z.shape[1]))).astype(np.float32)


def _decay_rates(d_h):
    min_decay = math.log(DECAY_TARGET) / DECAY_SLOW_PCT
    max_decay = math.log(DECAY_TARGET) / DECAY_FAST_PCT
    return np.abs(np.linspace(min_decay, max_decay, d_h)).astype(np.float32).reshape(1, d_h)


def _dft_tables(blk):
    odd = 2 * np.arange(blk, dtype=np.int64) + 1
    shifted = ((odd[:, None] * odd[None, :]) % (8 * blk)).astype(np.float64) * (2.0 * math.pi / (8 * blk))
    plain = ((odd[:, None] * np.arange(blk, dtype=np.int64)[None, :]) % (4 * blk)).astype(np.float64) * (2.0 * math.pi / (4 * blk))
    sgn = (1.0 - 2.0 * (np.arange(blk) % 2)).astype(np.float32).reshape(blk, 1)
    f32 = lambda a: a.astype(np.float32)
    return f32(np.cos(shifted)), f32(np.sin(shifted)), f32(np.cos(plain)), f32(np.sin(plain)), sgn


def kernel(x, c, ctx, c_ctx, w_ada, b_ada, norm1_g, w_in, conv_a_w, conv_a_b, lru_wa, lru_ba, lru_wx, lru_bx, lru_lambda, conv_b_w, filt_w1, filt_b1, filt_freq1, filt_w2, filt_b2, filt_freq2, filt_w3, filt_b3, filt_bias, out_norm_a, out_norm_b, w_out, norm2_g, w_rg, b_rg, w_re, b_re, w_gate, w_up, w_down, final_g):
    bsz, n, d = x.shape
    n_ctx = ctx.shape[1]
    d_rnn = conv_a_w.shape[2]
    d_h = filt_bias.shape[2]
    order = filt_bias.shape[1]
    assert w_ada.shape[0] == 1, "single-layer block"
    l = 0
    ch = 256

    mod_rows = 16
    cc = jnp.concatenate([c, c_ctx[None, :], jnp.zeros((mod_rows - bsz - 1, d), F32)], axis=0)
    mod3 = _ada(cc, w_ada[l], b_ada[l]).reshape(mod_rows, N_MOD, d)

    pe = jnp.asarray(_sincos_table(n // GRID_W, GRID_W, d))
    p3 = _inproj_lat(x, pe, mod3, norm1_g[l], w_in[l], 512)
    pc3 = _inproj_ctx(ctx.reshape(bsz * n_ctx, d), mod3, bsz, norm1_g[l], w_in[l], 1, d_rnn,
                      1024).reshape(bsz, n_ctx, d_rnn)

    heads_per_blk = ch // (d_rnn // RNN_HEADS)
    nblk = d_rnn // ch
    eye = jnp.eye(heads_per_blk, dtype=F32)

    def blockdiag(w):
        w5 = w.reshape(2, nblk, heads_per_blk, w.shape[2], w.shape[3])
        return jnp.einsum("dnkij,kl->dnkilj", w5, eye).reshape(2, nblk, ch, ch)

    wa_bd, wx_bd = blockdiag(lru_wa[l]), blockdiag(lru_wx[l])
    w_gates = jnp.concatenate([wa_bd[0], wx_bd[0], wa_bd[1], wx_bd[1]], axis=-1).astype(BF16)
    ba, bx = lru_ba[l].reshape(2, nblk, ch), lru_bx[l].reshape(2, nblk, ch)
    bias = jnp.concatenate([ba[0], bx[0], ba[1], bx[1]], axis=-1).reshape(nblk, 1, 4 * ch)
    sp = jax.nn.softplus(-lru_lambda[l])
    cb = conv_a_b[l].reshape(1, d_rnn)
    zeros_state = jnp.zeros((bsz, 1, d_rnn), F32)
    hf_ctx, hb_ctx = _rglru(pc3, 0, 0, conv_a_w[l], cb, w_gates, bias, sp, zeros_state, zeros_state, False, ch)
    ya, _, _ = _rglru(p3, d_rnn // ch, 0, conv_a_w[l], cb, w_gates, bias, sp, hf_ctx, hb_ctx, True, ch)

    as_bf16 = lambda a: jnp.asarray(a).astype(BF16)
    c3, s3, cu, su, sgn = _dft_tables(CONV_BLOCK)
    fe = 64
    z = jnp.asarray(_filter_features(n, fe))
    w1 = jnp.pad(filt_w1[l], ((0, fe - filt_w1.shape[1]), (0, 0)))
    hid = filt_w2.shape[1]
    hr, hi = _filters(z, w1, filt_b1[l].reshape(1, hid), filt_freq1[l].reshape(1, hid), filt_w2[l],
                      filt_b2[l].reshape(1, hid), filt_freq2[l].reshape(1, hid), filt_w3[l],
                      filt_b3[l].reshape(1, -1), jnp.asarray(_decay_rates(d_h)), as_bf16(cu), as_bf16(su),
                      jnp.asarray(sgn), d_h, order, ch)
    yb = _hyena(p3, 2 * d_rnn // ch, conv_b_w[l], filt_bias[l], as_bf16(c3), as_bf16(s3), hr, hi, d_h, ch)

    w_r = jnp.concatenate([w_re[l], w_rg[l], jnp.zeros((d, LANES - N_EXPERTS - N_GROUPS), F32)], axis=1)
    b_r = jnp.concatenate([b_re[l], b_rg[l], jnp.zeros((LANES - N_EXPERTS - N_GROUPS,), F32)]).reshape(1, LANES)
    tb = 512
    tm = 512
    x1, h2, rt, cnt, prow = _merge(ya, yb, x, pe, mod3, out_norm_a[l], out_norm_b[l], w_out[l], norm2_g[l],
                                   w_r, b_r, tm, tm // tb)

    tile = 512
    t_all = bsz * n
    nb = t_all // tb
    lp = 2 * tb + N_EXPERTS * SEG
    n_tiles = (2 * t_all + nb * N_EXPERTS * (SEG - 1) + N_EXPERTS * (tile - 1)) // tile
    cnt_blocks = cnt[:, 0, :N_EXPERTS].astype(jnp.int32)
    tab, row_ends, ltot, tile_expert, n_used, nxt, par = _moe_layout(cnt_blocks, tile, n_tiles)
    xs = _dispatch(tab, row_ends, ltot, prow, h2.reshape(t_all, d), n_tiles * tile, tb, lp, tile)
    ys = _ffn(tile_expert, n_used, nxt, par, xs, w_gate[l], w_up[l], w_down[l], tile)
    out = _combine(tab, ltot, ys, rt.reshape(t_all, LANES), x1.reshape(t_all, d), mod3, final_g, tb, lp, n // tb)
    return out.reshape(bsz, n, d)
```

```python
import functools
import math

import numpy as np
import jax
import jax.numpy as jnp
from jax import lax
from jax.experimental import pallas as pl
from jax.experimental.pallas import tpu as pltpu

F32 = jnp.float32
BF16 = jnp.bfloat16
EPS = 1e-6
LRU_C = 8.0
N_MOD = 6
GRID_W = 64
RNN_HEADS = 8
N_GROUPS = 4
EXPERTS_PER_GROUP = 8
N_EXPERTS = N_GROUPS * EXPERTS_PER_GROUP
FILTER_BANDS = 16
DECAY_FAST_PCT = 0.3
DECAY_SLOW_PCT = 1.5
DECAY_TARGET = 1e-2
SUBLANES = 8
LANES = 128
SEG = 2 * SUBLANES
VMEM_LIMIT = 60 * 1024 * 1024


def _cparams(*sem):
    return pltpu.CompilerParams(dimension_semantics=sem, vmem_limit_bytes=VMEM_LIMIT)


def _dot(a, b):
    return jnp.dot(a, b, preferred_element_type=F32)


def _split(a):
    hi = a.astype(BF16)
    return hi, (a - hi.astype(F32)).astype(BF16)


def _dot3(a, b):
    a_hi, a_lo = _split(a)
    b_hi, b_lo = _split(b)
    return _dot(a_hi, b_hi) + _dot(a_lo, b_hi) + _dot(a_hi, b_lo)


def _rms(v, g):
    return v * lax.rsqrt(jnp.mean(v * v, axis=-1, keepdims=True) + EPS) * g


def _sigmoid(z):
    return 1.0 / (1.0 + jnp.exp(-z))


def _shift_rows(v, d):
    n = v.shape[0]
    rolled = pltpu.roll(v, (-d) % n, 0)
    row = lax.broadcasted_iota(jnp.int32, v.shape, 0)
    ok = (row + d >= 0) & (row + d < n)
    return jnp.where(ok, rolled, 0.0)


def _tile_rot(v, s):
    n, c = v.shape
    return pltpu.roll(v.reshape(n // SUBLANES, SUBLANES, c), s, 1).reshape(n, c)


def _ada_body(c_ref, w_ref, b_ref, o_ref):
    c = c_ref[...]
    o_ref[...] = _dot3(c * _sigmoid(c), w_ref[...]) + b_ref[...]


def _ada(cc, w_ada, b_ada):
    rows, d = cc.shape
    n = w_ada.shape[1]
    tn = 1024
    return pl.pallas_call(
        _ada_body,
        grid=(n // tn,),
        in_specs=[pl.BlockSpec((rows, d), lambda j: (0, 0)),
                  pl.BlockSpec((d, tn), lambda j: (0, j)),
                  pl.BlockSpec((1, tn), lambda j: (0, j))],
        out_specs=pl.BlockSpec((rows, tn), lambda j: (0, j)),
        out_shape=jax.ShapeDtypeStruct((rows, n), F32),
        compiler_params=_cparams("arbitrary"),
        name="ada",
    )(cc, w_ada, b_ada.reshape(1, n))


def _inproj_lat_body(x_ref, pe_ref, mod_ref, g_ref, w_ref, o_ref, wb_ref):
    @pl.when((pl.program_id(0) == 0) & (pl.program_id(1) == 0))
    def _():
        wb_ref[...] = w_ref[...].astype(BF16)

    x = x_ref[...] + pe_ref[...]
    h = _rms(x, g_ref[...]) * (1.0 + mod_ref[1:2, :]) + mod_ref[0:1, :]
    o_ref[...] = _dot(h.astype(BF16), wb_ref[...]).astype(o_ref.dtype)


def _inproj_lat(x3, pe, mod3, g, w, tm):
    b, l, d = x3.shape
    n_out = w.shape[1]
    return pl.pallas_call(
        _inproj_lat_body,
        grid=(l // tm, b),
        in_specs=[pl.BlockSpec((None, tm, d), lambda i, bi: (bi, i, 0)),
                  pl.BlockSpec((tm, d), lambda i, bi: (i, 0)),
                  pl.BlockSpec((None, N_MOD, d), lambda i, bi: (bi, 0, 0)),
                  pl.BlockSpec((1, d), lambda i, bi: (0, 0)),
                  pl.BlockSpec((d, n_out), lambda i, bi: (0, 0), pipeline_mode=pl.Buffered(1))],
        out_specs=pl.BlockSpec((None, tm, n_out), lambda i, bi: (bi, i, 0)),
        out_shape=jax.ShapeDtypeStruct((b, l, n_out), BF16),
        scratch_shapes=[pltpu.VMEM((d, n_out), BF16)],
        compiler_params=_cparams("arbitrary", "arbitrary"),
        name="inproj_pe",
    )(x3, pe, mod3, g.reshape(1, d), w)


def _inproj_ctx_body(x_ref, mod_ref, g_ref, w_ref, o_ref):
    h = _rms(x_ref[...], g_ref[...]) * (1.0 + mod_ref[1:2, :]) + mod_ref[0:1, :]
    o_ref[...] = _dot(h.astype(BF16), w_ref[...].astype(BF16))


def _inproj_ctx(x2, mod3, mod_row, g, w, col_block, n_out, tm):
    r, d = x2.shape
    return pl.pallas_call(
        _inproj_ctx_body,
        grid=(r // tm,),
        in_specs=[pl.BlockSpec((tm, d), lambda i: (i, 0)),
                  pl.BlockSpec((None, N_MOD, d), lambda i: (mod_row, 0, 0)),
                  pl.BlockSpec((1, d), lambda i: (0, 0)),
                  pl.BlockSpec((d, n_out), lambda i: (0, col_block))],
        out_specs=pl.BlockSpec((tm, n_out), lambda i: (i, 0)),
        out_shape=jax.ShapeDtypeStruct((r, n_out), F32),
        compiler_params=_cparams("arbitrary"),
        name="inproj_ctx",
    )(x2, mod3, g.reshape(1, d), w)


def _rglru_body(*refs, with_gate):
    if with_gate:
        (u_ref, gate_ref, cw_ref, cb_ref, w_ref, bias_ref, sp_ref, h0f_ref, h0b_ref,
         y_ref, hf_end_ref, hb_end_ref, af_ref, bf_ref, ab_ref, bb_ref) = refs
    else:
        (u_ref, cw_ref, cb_ref, w_ref, bias_ref, sp_ref, h0f_ref, h0b_ref,
         hf_end_ref, hb_end_ref, af_ref, bf_ref, ab_ref, bb_ref) = refs
    n, c = u_ref.shape
    u = u_ref[...].astype(F32)
    v = cb_ref[...] + cw_ref[1:2, :] * _shift_rows(u, -1) + cw_ref[0:1, :] * _shift_rows(u, -2)
    v = v + cw_ref[2:3, :] * u + cw_ref[3:4, :] * _shift_rows(u, 1)
    z = _dot(v.astype(BF16), w_ref[...]) + bias_ref[...]
    sub = lax.broadcasted_iota(jnp.int32, (n, c), 0) % SUBLANES

    def local_scan(k, reverse, a_ref, b_ref):
        r = _sigmoid(z[:, (2 * k) * c:(2 * k + 1) * c])
        i = _sigmoid(z[:, (2 * k + 1) * c:(2 * k + 2) * c])
        log_a = (-LRU_C) * r * sp_ref[k:k + 1, :]
        a = jnp.exp(log_a)
        b = jnp.sqrt(1.0 - a * a) * (i * v)
        for s in (1, 2, 4):
            ok = (sub < SUBLANES - s) if reverse else (sub >= s)
            shift = SUBLANES - s if reverse else s
            b = a * jnp.where(ok, _tile_rot(b, shift), 0.0) + b
            a = a * jnp.where(ok, _tile_rot(a, shift), 1.0)
        a_ref[...] = a
        b_ref[...] = b

    local_scan(0, False, af_ref, bf_ref)
    local_scan(1, True, ab_ref, bb_ref)

    n_tiles = n // SUBLANES

    def carry_step(q, carry):
        cf, cb = carry
        rf = pl.ds(pl.multiple_of(q * SUBLANES, SUBLANES), SUBLANES)
        rb = pl.ds(pl.multiple_of((n_tiles - 1 - q) * SUBLANES, SUBLANES), SUBLANES)
        af, bf, ab, bb = af_ref[rf, :], bf_ref[rf, :], ab_ref[rb, :], bb_ref[rb, :]
        bf_ref[rf, :] = af * cf + bf
        bb_ref[rb, :] = ab * cb + bb
        last = lambda v: jnp.broadcast_to(v[SUBLANES - 1:SUBLANES, :], (SUBLANES, c))
        first = lambda v: jnp.broadcast_to(v[0:1, :], (SUBLANES, c))
        return last(af) * cf + last(bf), first(ab) * cb + first(bb)

    cf0 = jnp.broadcast_to(h0f_ref[...], (SUBLANES, c))
    cb0 = jnp.broadcast_to(h0b_ref[...], (SUBLANES, c))
    cf, cb = lax.fori_loop(0, n_tiles, carry_step, (cf0, cb0), unroll=4)
    hf_end_ref[...] = cf[0:1, :]
    hb_end_ref[...] = cb[0:1, :]
    if with_gate:
        y_ref[...] = jax.nn.gelu(gate_ref[...].astype(F32), approximate=True) * (bf_ref[...] + bb_ref[...])


def _rglru(p3, u_blk0, gate_blk0, cw, cb, w_gates, bias, sp, h0f, h0b, with_gate, ch):
    b, l, _ = p3.shape
    d_rnn = cw.shape[1]
    nh = d_rnn // ch
    in_specs = [pl.BlockSpec((None, l, ch), lambda bi, h: (bi, 0, u_blk0 + h))]
    args = [p3]
    if with_gate:
        in_specs.append(pl.BlockSpec((None, l, ch), lambda bi, h: (bi, 0, gate_blk0 + h)))
        args.append(p3)
    in_specs += [pl.BlockSpec((4, ch), lambda bi, h: (0, h)),
                 pl.BlockSpec((1, ch), lambda bi, h: (0, h)),
                 pl.BlockSpec((None, ch, 4 * ch), lambda bi, h: (h, 0, 0)),
                 pl.BlockSpec((None, 1, 4 * ch), lambda bi, h: (h, 0, 0)),
                 pl.BlockSpec((2, ch), lambda bi, h: (0, h)),
                 pl.BlockSpec((None, 1, ch), lambda bi, h: (bi, 0, h)),
                 pl.BlockSpec((None, 1, ch), lambda bi, h: (bi, 0, h))]
    args += [cw, cb, w_gates, bias, sp, h0f, h0b]
    end_spec = pl.BlockSpec((None, 1, ch), lambda bi, h: (bi, 0, h))
    end_shape = jax.ShapeDtypeStruct((b, 1, d_rnn), F32)
    out_specs = [end_spec, end_spec]
    out_shape = [end_shape, end_shape]
    if with_gate:
        out_specs = [pl.BlockSpec((None, l, ch), lambda bi, h: (bi, 0, h))] + out_specs
        out_shape = [jax.ShapeDtypeStruct((b, l, d_rnn), F32)] + out_shape
    return pl.pallas_call(
        functools.partial(_rglru_body, with_gate=with_gate),
        grid=(b, nh),
        in_specs=in_specs,
        out_specs=out_specs,
        out_shape=out_shape,
        scratch_shapes=[pltpu.VMEM((l, ch), F32)] * 4,
        compiler_params=_cparams("arbitrary", "arbitrary"),
        name="rglru_lat" if with_gate else "rglru_ctx",
    )(*args)


CONV_BLOCK = 512


def _filt_body(z_ref, w1_ref, b1_ref, f1_ref, w2_ref, b2_ref, f2_ref, w3f_ref, w3b_ref, b3f_ref, b3b_ref,
               dec_ref, cu_ref, su_ref, sgn_ref, hr_ref, hi_ref, hdn_ref):
    n = z_ref.shape[0]
    blk = cu_ref.shape[0]
    nblk = n // blk
    ch = hr_ref.shape[2]

    @pl.when(pl.program_id(0) == 0)
    def _():
        hid = jnp.sin(f1_ref[...] * (_dot3(z_ref[...], w1_ref[...]) + b1_ref[...]))
        hdn_ref[...] = jnp.sin(f2_ref[...] * (_dot3(hid, w2_ref[...]) + b2_ref[...]))

    hdn = hdn_ref[...]
    decay = jnp.exp(-z_ref[:, 0:1] * dec_ref[...])
    kf = (_dot3(hdn, w3f_ref[...]) + b3f_ref[...]) * decay
    kb = (_dot3(hdn, w3b_ref[...]) + b3b_ref[...]) * decay
    row = lax.broadcasted_iota(jnp.int32, (n, ch), 0)
    kb = jnp.where(row == 0, 0.0, kb)
    norm = jnp.sum(jnp.abs(kf) + jnp.abs(kb), axis=0, keepdims=True)
    scale = (2.0 / (2 * blk)) / norm
    sgn = sgn_ref[...]

    p, q, first = [], [], []
    for c in range(nblk):
        taps = jnp.concatenate([kf[c * blk:(c + 1) * blk], kb[c * blk:(c + 1) * blk]], axis=1).astype(BF16)
        p.append(_dot(cu_ref[...], taps))
        q.append(_dot(su_ref[...], taps))
        first.append(taps[0:1, :].astype(F32))
    fwd = lambda a: a[:, :ch]
    bwd = lambda a: a[:, ch:]
    out = lambda a: (a * scale).astype(hr_ref.dtype)
    hr_ref[nblk - 1] = out(fwd(p[0]) + bwd(p[0]))
    hi_ref[nblk - 1] = out(bwd(q[0]) - fwd(q[0]))
    for c in range(1, nblk):
        hr_ref[nblk - 1 + c] = out(fwd(p[c]) + sgn * fwd(q[c - 1]))
        hi_ref[nblk - 1 + c] = out(sgn * (fwd(p[c - 1]) - fwd(first[c - 1])) - fwd(q[c]))
        hr_ref[nblk - 1 - c] = out(bwd(p[c]) + sgn * bwd(q[c - 1]))
        hi_ref[nblk - 1 - c] = out(bwd(q[c]) - sgn * (bwd(p[c - 1]) - bwd(first[c - 1])))


def _filters(z, w1, b1, f1, w2, b2, f2, w3, b3, deltas, cu, su, sgn, d_h, order, ch):
    n, fe = z.shape
    hid = w2.shape[0]
    blk = cu.shape[0]
    nh = 2 * (n // blk) - 1
    per_order = d_h // ch
    ncol = order * per_order
    const = lambda shape: pl.BlockSpec(shape, lambda g: (0, 0))
    hspec = pl.BlockSpec((nh, blk, ch), lambda g: (0, 0, g))
    return pl.pallas_call(
        _filt_body,
        grid=(ncol,),
        in_specs=[const((n, fe)), const((fe, hid)), const((1, hid)), const((1, hid)),
                  const((hid, hid)), const((1, hid)), const((1, hid)),
                  pl.BlockSpec((hid, ch), lambda g: (0, g)),
                  pl.BlockSpec((hid, ch), lambda g: (0, ncol + g)),
                  pl.BlockSpec((1, ch), lambda g: (0, g)),
                  pl.BlockSpec((1, ch), lambda g: (0, ncol + g)),
                  pl.BlockSpec((1, ch), lambda g: (0, g % per_order)),
                  const((blk, blk)), const((blk, blk)), const((blk, 1))],
        out_specs=[hspec, hspec],
        out_shape=[jax.ShapeDtypeStruct((nh, blk, order * d_h), BF16)] * 2,
        scratch_shapes=[pltpu.VMEM((n, hid), F32)],
        compiler_params=_cparams("arbitrary"),
        name="hyena_filters",
    )(z, w1, b1, f1, w2, b2, f2, w3, w3, b3, b3, deltas, cu, su, sgn)


def _hyena_body(v_ref, x1_ref, x2_ref, wv_ref, w1_ref, w2_ref, fb_ref, c3_ref, s3_ref,
                h0r_ref, h0i_ref, h1r_ref, h1i_ref, o_ref, z_ref, zb_ref, p_ref, q_ref, yr_ref, yi_ref):
    n = o_ref.shape[0]
    blk = c3_ref.shape[0]
    nblk = n // blk
    halo = 2 * SUBLANES
    sub = 32

    def conv3(src_ref, w_ref, r0):
        lo, hi = max(r0 - halo, 0), min(r0 + blk + halo, n)
        win = src_ref[lo:hi, :].astype(F32)
        size = hi - lo
        row = lax.broadcasted_iota(jnp.int32, win.shape, 0)
        prev = pltpu.roll(win, 1, 0)
        nxt = pltpu.roll(win, size - 1, 0)
        if lo == 0:
            prev = jnp.where(row == 0, 0.0, prev)
        if hi == n:
            nxt = jnp.where(row == size - 1, 0.0, nxt)
        out = w_ref[0:1, :] * prev + w_ref[1:2, :] * win + w_ref[2:3, :] * nxt
        return out[r0 - lo:r0 - lo + blk, :]

    def long_conv(hr_ref, hi_ref, fb, gate_ref, gate_w_ref, dst_ref):
        for i in range(nblk):
            rows = slice(i * blk, (i + 1) * blk)
            p_ref[rows, :] = _dot(c3_ref[...], zb_ref[rows, :]).astype(BF16)
            q_ref[rows, :] = _dot(s3_ref[...], zb_ref[rows, :]).astype(BF16)

        for j in range(nblk):
            for r0 in range(0, blk, sub):
                yr = yi = None
                for i in range(nblk):
                    rows = slice(i * blk + r0, i * blk + r0 + sub)
                    p, q = p_ref[rows, :], q_ref[rows, :]
                    gr = hr_ref[nblk - 1 + j - i, r0:r0 + sub, :]
                    gi = hi_ref[nblk - 1 + j - i, r0:r0 + sub, :]
                    tr, ti = p * gr + q * gi, q * gr - p * gi
                    yr, yi = (tr, ti) if yr is None else (yr + tr, yi + ti)
                yr_ref[j * blk + r0:j * blk + r0 + sub, :] = yr
                yi_ref[j * blk + r0:j * blk + r0 + sub, :] = yi
            rows = slice(j * blk, (j + 1) * blk)
            y = _dot(c3_ref[...], yr_ref[rows, :]) + _dot(s3_ref[...], yi_ref[rows, :])
            dst_ref[rows, :] = conv3(gate_ref, gate_w_ref, j * blk) * (y + z_ref[rows, :] * fb)

    for r0 in range(0, n, blk):
        z_ref[r0:r0 + blk, :] = conv3(v_ref, wv_ref, r0)
    zb_ref[...] = z_ref[...].astype(BF16)
    long_conv(h0r_ref, h0i_ref, fb_ref[0:1, :], x1_ref, w1_ref, z_ref)
    zb_ref[...] = z_ref[...].astype(BF16)
    long_conv(h1r_ref, h1i_ref, fb_ref[1:2, :], x2_ref, w2_ref, o_ref)


def _hyena(p3, col_blk0, conv_w, fbias, c3, s3, hr, hi, d_h, ch):
    b, n, _ = p3.shape
    nh, blk, _ = hr.shape
    per = d_h // ch
    zspec = lambda k: pl.BlockSpec((None, n, ch), lambda h, bi: (bi, 0, col_blk0 + k * per + h))
    wspec = lambda k: pl.BlockSpec((3, ch), lambda h, bi: (0, k * per + h))
    hspec = lambda o: pl.BlockSpec((nh, blk, ch), lambda h, bi: (0, 0, o * per + h), pipeline_mode=pl.Buffered(1))
    const = pl.BlockSpec((blk, blk), lambda h, bi: (0, 0))
    return pl.pallas_call(
        _hyena_body,
        grid=(per, b),
        in_specs=[zspec(0), zspec(1), zspec(2), wspec(0), wspec(1), wspec(2),
                  pl.BlockSpec((2, ch), lambda h, bi: (0, h)), const, const,
                  hspec(0), hspec(0), hspec(1), hspec(1)],
        out_specs=pl.BlockSpec((None, n, ch), lambda h, bi: (bi, 0, h)),
        out_shape=jax.ShapeDtypeStruct((b, n, d_h), F32),
        scratch_shapes=[pltpu.VMEM((n, ch), F32), pltpu.VMEM((n, ch), BF16),
                        pltpu.VMEM((n, ch), BF16), pltpu.VMEM((n, ch), BF16),
                        pltpu.VMEM((n, ch), BF16), pltpu.VMEM((n, ch), BF16)],
        compiler_params=_cparams("arbitrary", "arbitrary"),
        name="hyena_mix",
    )(p3, p3, p3, conv_w, conv_w, conv_w, fbias, c3, s3, hr, hi, hr, hi)


def _merge_body(ya_ref, yb_ref, x_ref, pe_ref, mod_ref, ga_ref, gb_ref, wo_ref, g2_ref, wr_ref, br_ref,
                x1_ref, h2_ref, rt_ref, cnt_ref, prow_ref, wob_ref):
    @pl.when((pl.program_id(0) == 0) & (pl.program_id(1) == 0))
    def _():
        wob_ref[...] = wo_ref[...].astype(BF16)

    da = ya_ref.shape[1]
    na = _rms(ya_ref[...], ga_ref[...]).astype(BF16)
    nb = _rms(yb_ref[...], gb_ref[...]).astype(BF16)
    y = _dot(na, wob_ref[0:da, :]) + _dot(nb, wob_ref[da:, :])
    _route(y, x_ref, pe_ref, mod_ref, g2_ref, wr_ref, br_ref, x1_ref, h2_ref, rt_ref, cnt_ref, prow_ref)


def _route(y, x_ref, pe_ref, mod_ref, g2_ref, wr_ref, br_ref, x1_ref, h2_ref, rt_ref, cnt_ref, prow_ref):
    x1 = x_ref[...] + pe_ref[...] + mod_ref[2:3, :] * y
    x1_ref[...] = x1
    h2 = _rms(x1, g2_ref[...]) * (1.0 + mod_ref[4:5, :]) + mod_ref[3:4, :]
    h2_ref[...] = h2.astype(BF16)

    logits = _dot3(h2, wr_ref[...]) + br_ref[...]
    lane = lax.broadcasted_iota(jnp.int32, logits.shape, 1).astype(F32)
    neg = -jnp.inf
    big = jnp.float32(1 << 20)
    gl = jnp.where((lane >= N_EXPERTS) & (lane < N_EXPERTS + N_GROUPS), logits, neg)
    gmax = jnp.max(gl, axis=1, keepdims=True)
    g_p = 1.0 / jnp.sum(jnp.exp(gl - gmax), axis=1, keepdims=True)
    g_i = jnp.min(jnp.where(gl == gmax, lane, big), axis=1, keepdims=True) - N_EXPERTS
    lo = g_i * EXPERTS_PER_GROUP
    el = jnp.where((lane >= lo) & (lane < lo + EXPERTS_PER_GROUP), logits, neg)
    m1 = jnp.max(el, axis=1, keepdims=True)
    i1 = jnp.min(jnp.where(el == m1, lane, big), axis=1, keepdims=True)
    el2 = jnp.where(lane == i1, neg, el)
    m2 = jnp.max(el2, axis=1, keepdims=True)
    i2 = jnp.min(jnp.where(el2 == m2, lane, big), axis=1, keepdims=True)
    e2 = jnp.exp(m2 - m1)
    w1 = g_p / (1.0 + e2)
    w2 = g_p * e2 / (1.0 + e2)
    o1 = jnp.where(lane == i1, 1.0, 0.0)
    o2 = jnp.where(lane == i2, 1.0, 0.0)
    nsub = cnt_ref.shape[0]
    sub = o1.shape[0] // nsub
    sq = lambda shape, d: lax.broadcasted_iota(jnp.int32, shape, d)
    earlier = jnp.where(sq((sub, sub), 1) < sq((sub, sub), 0), 1.0, 0.0).astype(BF16)
    below = jnp.where(sq((LANES, LANES), 0) < sq((LANES, LANES), 1), 1.0, 0.0).astype(BF16)
    pick_row = jnp.where(sq((SUBLANES, LANES), 0) == sq((SUBLANES, LANES), 1), 1.0, 0.0).astype(BF16)
    row_id = sq((SUBLANES, sub), 0)
    pos1, pos2 = [], []
    for k in range(nsub):
        a1, a2 = o1[k * sub:(k + 1) * sub], o2[k * sub:(k + 1) * sub]
        both = a1 + a2
        count = jnp.sum(both, axis=0, keepdims=True)
        cnt_ref[k] = jnp.broadcast_to(count, cnt_ref.shape[1:])
        runs = jnp.broadcast_to(jnp.floor((count + (SEG - 1)) / SEG), (SUBLANES, LANES))
        start = _dot(runs.astype(BF16), below)[0:1, :] * SEG
        before = _dot(earlier, both.astype(BF16)) + start
        p1 = jnp.sum(a1 * before, axis=1, keepdims=True)
        p2 = jnp.sum(a2 * before, axis=1, keepdims=True)
        pos1.append(p1)
        pos2.append(p2)
        lane_k = sq((sub, LANES), 1).astype(F32)
        hi1, hi2 = jnp.floor(p1 / 64.0), jnp.floor(p2 / 64.0)
        parts = (jnp.where(lane_k == 0.0, hi1, 0.0) + jnp.where(lane_k == 1.0, p1 - 64.0 * hi1, 0.0)
                 + jnp.where(lane_k == 2.0, hi2, 0.0) + jnp.where(lane_k == 3.0, p2 - 64.0 * hi2, 0.0))
        rows = lax.dot_general(pick_row, parts.astype(BF16), (((1,), (1,)), ((), ())), preferred_element_type=F32)
        q1 = rows[0:1, :] * 64.0 + rows[1:2, :]
        q2 = rows[2:3, :] * 64.0 + rows[3:4, :]
        prow_ref[k] = jnp.where(row_id == 0, q1, jnp.where(row_id == 1, q2, 0.0))
    p1 = jnp.concatenate(pos1, axis=0)
    p2 = jnp.concatenate(pos2, axis=0)
    rt_ref[...] = (jnp.where(lane == 0.0, i1, 0.0) + jnp.where(lane == 1.0, i2, 0.0)
                   + jnp.where(lane == 2.0, w1, 0.0) + jnp.where(lane == 3.0, w2, 0.0)
                   + jnp.where(lane == 4.0, p1, 0.0) + jnp.where(lane == 5.0, p2, 0.0))


def _merge(ya, yb, x3, pe, mod3, ga, gb, w_out, g2n, w_r, b_r, tm, nsub):
    b, n, d = x3.shape
    da, db = ya.shape[2], yb.shape[2]
    per = n // tm
    tok = lambda w: pl.BlockSpec((None, tm, w), lambda i, bi: (bi, i, 0))
    const = lambda shape: pl.BlockSpec(shape, lambda i, bi: (0, 0))
    blk = lambda w: pl.BlockSpec((nsub, SUBLANES, w), lambda i, bi: (bi * per + i, 0, 0))
    return pl.pallas_call(
        _merge_body,
        grid=(per, b),
        in_specs=[tok(da), tok(db), tok(d), pl.BlockSpec((tm, d), lambda i, bi: (i, 0)),
                  pl.BlockSpec((None, N_MOD, d), lambda i, bi: (bi, 0, 0)),
                  const((1, da)), const((1, db)), const((da + db, d)), const((1, d)),
                  const((d, LANES)), const((1, LANES))],
        out_specs=[tok(d), tok(d), tok(LANES), blk(LANES), blk(tm // nsub)],
        out_shape=[jax.ShapeDtypeStruct((b, n, d), F32), jax.ShapeDtypeStruct((b, n, d), BF16),
                   jax.ShapeDtypeStruct((b, n, LANES), F32),
                   jax.ShapeDtypeStruct((b * per * nsub, SUBLANES, LANES), F32),
                   jax.ShapeDtypeStruct((b * per * nsub, SUBLANES, tm // nsub), F32)],
        scratch_shapes=[pltpu.VMEM((da + db, d), BF16)],
        compiler_params=_cparams("arbitrary", "arbitrary"),
        name="merge_route",
    )(ya, yb, x3, pe, mod3, ga.reshape(1, da), gb.reshape(1, db), w_out, g2n.reshape(1, d), w_r, b_r)


def _segment_copy(local_ref, lrow, global_ref, grow, rows, sem, to_global):
    lrows = pl.ds(pl.multiple_of(lrow, SEG), rows)
    grows = pl.ds(pl.multiple_of(grow, SEG), rows)
    if to_global:
        return pltpu.make_async_copy(local_ref.at[lrows, :], global_ref.at[grows, :], sem)
    return pltpu.make_async_copy(global_ref.at[grows, :], local_ref.at[lrows, :], sem)


def _start_segments(tab_ref, blk, local_ref, global_ref, sem, to_global):
    def per_expert(e, carry):
        k = (blk * N_EXPERTS + e) * 3
        ls, gs, nchunk = tab_ref[k], tab_ref[k + 1], tab_ref[k + 2]

        def per_chunk(ci, carry2):
            _segment_copy(local_ref, ls + ci * SEG, global_ref, gs + ci * SEG, SEG, sem, to_global).start()
            return carry2

        return lax.fori_loop(0, nchunk, per_chunk, carry)

    lax.fori_loop(0, N_EXPERTS, per_expert, 0, unroll=2)


def _wait_segments(rows, local_ref, global_ref, sem, to_global):
    chunks = rows // SEG
    nbits = (local_ref.shape[0] // SEG).bit_length()
    for k in range(nbits):
        @pl.when(((chunks >> k) & 1) == 1)
        def _():
            _segment_copy(local_ref, 0, global_ref, 0, SEG << k, sem, to_global).wait()


def _dispatch_body(tab_ref, ends_ref, ltot_ref, prow_ref, h2_ref, xs_ref, loc_ref, zero_ref, sem, zsem):
    blk = pl.program_id(0)
    nblk = pl.num_programs(0)
    tb = h2_ref.shape[0]
    lp = loc_ref.shape[1]
    tile = zero_ref.shape[0]
    slot = blk % 2

    @pl.when(blk == 0)
    def _():
        zero_ref[...] = jnp.zeros_like(zero_ref)

        def zero_fill(first_row, rows):
            dst = xs_ref.at[pl.ds(pl.multiple_of(first_row, SEG), rows), :]
            return pltpu.make_async_copy(zero_ref.at[pl.ds(0, rows), :], dst, zsem)

        pad = 64

        def fill(e, carry, wait):
            end, real = ends_ref[e], ends_ref[N_EXPERTS + e]
            first = real // pad * pad

            def one(j, carry2):
                cp = zero_fill(first + j * pad, pad)
                cp.wait() if wait else cp.start()
                return carry2

            return lax.fori_loop(0, (end - first) // pad, one, carry)

        lax.fori_loop(0, N_EXPERTS, functools.partial(fill, wait=False), 0)
        used = ends_ref[N_EXPERTS - 1]
        spare = (xs_ref.shape[0] - used) // tile

        def fill_spare(j, carry, wait):
            cp = zero_fill(used + j * tile, tile)
            cp.wait() if wait else cp.start()
            return carry

        lax.fori_loop(0, spare, functools.partial(fill_spare, wait=False), 0)
        lax.fori_loop(0, N_EXPERTS, functools.partial(fill, wait=True), 0)
        lax.fori_loop(0, spare, functools.partial(fill_spare, wait=True), 0)

    pos = lax.broadcasted_iota(jnp.int32, (lp, tb), 0).astype(F32)
    onehot = jnp.where((pos == prow_ref[0:1, :]) | (pos == prow_ref[1:2, :]), 1.0, 0.0).astype(BF16)
    loc_ref[slot] = _dot(onehot, h2_ref[...]).astype(BF16)

    _start_segments(tab_ref, blk, loc_ref.at[slot], xs_ref, sem.at[slot], True)

    @pl.when(blk > 0)
    def _():
        _wait_segments(ltot_ref[blk - 1], loc_ref.at[1 - slot], xs_ref, sem.at[1 - slot], True)

    @pl.when(blk == nblk - 1)
    def _():
        _wait_segments(ltot_ref[blk], loc_ref.at[slot], xs_ref, sem.at[slot], True)


def _dispatch(tab, ends, ltot, prow, h2, slots, tb, lp, tile):
    t, d = h2.shape
    return pl.pallas_call(
        _dispatch_body,
        grid_spec=pltpu.PrefetchScalarGridSpec(
            num_scalar_prefetch=3,
            grid=(t // tb,),
            in_specs=[pl.BlockSpec((None, SUBLANES, tb), lambda i, *_: (i, 0, 0)),
                      pl.BlockSpec((tb, d), lambda i, *_: (i, 0))],
            out_specs=pl.BlockSpec(memory_space=pl.ANY),
            scratch_shapes=[pltpu.VMEM((2, lp, d), BF16), pltpu.VMEM((tile, d), BF16),
                            pltpu.SemaphoreType.DMA((2,)), pltpu.SemaphoreType.DMA]),
        out_shape=jax.ShapeDtypeStruct((slots, d), BF16),
        compiler_params=_cparams("arbitrary"),
        name="moe_dispatch",
    )(tab, ends, ltot, prow, h2)


def _ffn_body(te_ref, nu_ref, nxt_ref, par_ref, rows_ref, xs_ref, wg_hbm, wu_hbm, wd_hbm, ys_ref,
              wg_buf, wu_buf, wd_buf, wgb_ref, wub_ref, wdb_ref, sem):
    i = pl.program_id(0)
    e = te_ref[i]
    first = (i == 0) | (e != te_ref[jnp.maximum(i - 1, 0)])
    active = i < nu_ref[0]

    def fetch(expert, slot):
        return [pltpu.make_async_copy(w.at[expert], buf.at[slot], sem.at[slot])
                for w, buf in ((wg_hbm, wg_buf), (wu_hbm, wu_buf), (wd_hbm, wd_buf))]

    @pl.when(i == 0)
    def _():
        for cp in fetch(e, par_ref[e]):
            cp.start()

    @pl.when(first & active)
    def _():
        slot = par_ref[e]
        for cp in fetch(e, slot):
            cp.wait()
        nxt = nxt_ref[e]

        @pl.when(nxt >= 0)
        def _():
            for cp in fetch(nxt, 1 - slot):
                cp.start()

        wgb_ref[...] = wg_buf[slot].astype(BF16)
        wub_ref[...] = wu_buf[slot].astype(BF16)
        wdb_ref[...] = wd_buf[slot].astype(BF16)

    def expert(rows):
        xb = xs_ref[rows, :]
        act = _dot(xb, wgb_ref[...])
        act = act * _sigmoid(act) * _dot(xb, wub_ref[...])
        ys_ref[rows, :] = _dot(act.astype(BF16), wdb_ref[...]).astype(BF16)

    half = xs_ref.shape[0] // 2

    @pl.when(active & (rows_ref[i] > half))
    def _():
        expert(slice(None))

    @pl.when(active & (rows_ref[i] <= half))
    def _():
        expert(slice(0, half))
        ys_ref[half:, :] = jnp.zeros((half, ys_ref.shape[1]), BF16)


def _ffn(tile_expert, n_used, nxt, par, tile_rows, xs, w_gate, w_up, w_down, tile):
    slots, d = xs.shape
    _, _, de = w_gate.shape
    last = lambda i, te, nu, *_: (jnp.minimum(i, nu[0] - 1), 0)
    hbm = pl.BlockSpec(memory_space=pl.ANY)
    return pl.pallas_call(
        _ffn_body,
        grid_spec=pltpu.PrefetchScalarGridSpec(
            num_scalar_prefetch=5,
            grid=(slots // tile,),
            in_specs=[pl.BlockSpec((tile, d), last), hbm, hbm, hbm],
            out_specs=pl.BlockSpec((tile, d), last),
            scratch_shapes=[pltpu.VMEM((2, d, de), F32), pltpu.VMEM((2, d, de), F32), pltpu.VMEM((2, de, d), F32),
                            pltpu.VMEM((d, de), BF16), pltpu.VMEM((d, de), BF16), pltpu.VMEM((de, d), BF16),
                            pltpu.SemaphoreType.DMA((2,))]),
        out_shape=jax.ShapeDtypeStruct((slots, d), BF16),
        input_output_aliases={5: 0},
        compiler_params=_cparams("arbitrary"),
        name="moe_ffn",
    )(tile_expert, n_used, nxt, par, tile_rows, xs, w_gate, w_up, w_down)


def _combine_body(tab_ref, ltot_ref, ys_ref, rt_ref, x1_ref, mod_ref, fg_ref, o_ref, loc_ref, sem):
    blk = pl.program_id(0)
    nblk = pl.num_programs(0)
    _, lp, d = loc_ref.shape
    slot = blk % 2

    @pl.when(blk == 0)
    def _():
        _start_segments(tab_ref, blk, loc_ref.at[slot], ys_ref, sem.at[slot], False)

    @pl.when(blk + 1 < nblk)
    def _():
        _start_segments(tab_ref, blk + 1, loc_ref.at[1 - slot], ys_ref, sem.at[1 - slot], False)

    rt = rt_ref[...]
    lane = lax.broadcasted_iota(jnp.int32, rt.shape, 1)
    col = lambda k: jnp.sum(jnp.where(lane == k, rt, 0.0), axis=1, keepdims=True)
    w1, w2, p1, p2 = col(2), col(3), col(4), col(5)
    pos = lax.broadcasted_iota(jnp.int32, (rt.shape[0], lp), 1).astype(F32)
    weights = (jnp.where(pos == p1, w1, 0.0) + jnp.where(pos == p2, w2, 0.0)).astype(BF16)
    _wait_segments(ltot_ref[blk], loc_ref.at[slot], ys_ref, sem.at[slot], False)

    def clear(ci, carry):
        loc_ref[slot, pl.ds(pl.multiple_of(ci * SEG, SEG), SEG), :] = jnp.zeros((SEG, d), BF16)
        return carry

    lax.fori_loop(ltot_ref[blk] // SEG, lp // SEG, clear, 0)
    moe = _dot(weights, loc_ref[slot])
    o_ref[...] = _rms(x1_ref[...] + mod_ref[5:6, :] * moe, fg_ref[...])


def _combine(tab, ltot, ys, rt2, x1, mod3, final_g, tb, lp, per_batch):
    t, d = x1.shape
    return pl.pallas_call(
        _combine_body,
        grid_spec=pltpu.PrefetchScalarGridSpec(
            num_scalar_prefetch=2,
            grid=(t // tb,),
            in_specs=[pl.BlockSpec(memory_space=pl.ANY),
                      pl.BlockSpec((tb, LANES), lambda i, *_: (i, 0)),
                      pl.BlockSpec((tb, d), lambda i, *_: (i, 0)),
                      pl.BlockSpec((None, N_MOD, d), lambda i, *_: (i // per_batch, 0, 0)),
                      pl.BlockSpec((1, d), lambda i, *_: (0, 0))],
            out_specs=pl.BlockSpec((tb, d), lambda i, *_: (i, 0)),
            scratch_shapes=[pltpu.VMEM((2, lp, d), BF16), pltpu.SemaphoreType.DMA((2,))]),
        out_shape=jax.ShapeDtypeStruct((t, d), F32),
        compiler_params=_cparams("arbitrary"),
        name="moe_combine",
    )(tab, ltot, ys, rt2, x1, mod3, final_g.reshape(1, d))


def _moe_layout(cnt_blocks, tile, n_tiles):
    lcnt = (cnt_blocks + SEG - 1) // SEG * SEG
    lstart = jnp.cumsum(lcnt, axis=1) - lcnt
    ltot = jnp.sum(lcnt, axis=1)
    per_expert = jnp.sum(lcnt, axis=0)
    tiles_per = (per_expert + tile - 1) // tile
    tile_ends = jnp.cumsum(tiles_per)
    row_ends = tile_ends * tile
    goff = row_ends - tiles_per * tile
    gstart = goff[None, :] + jnp.cumsum(lcnt, axis=0) - lcnt
    tab = jnp.stack([lstart, gstart, lcnt // SEG], axis=-1).reshape(-1).astype(jnp.int32)
    n_used = tile_ends[-1:]
    tile_ids = jnp.arange(n_tiles, dtype=jnp.int32)
    tile_expert = jnp.sum((tile_ends[None, :] <= jnp.minimum(tile_ids, n_used - 1)[:, None]).astype(jnp.int32), axis=1)
    ids = jnp.arange(N_EXPERTS, dtype=jnp.int32)
    used = tiles_per > 0
    par = (jnp.cumsum(used) - used) % 2
    later = jnp.where(used[None, :] & (ids[None, :] > ids[:, None]), ids[None, :], N_EXPERTS)
    nxt = jnp.min(later, axis=1)
    nxt = jnp.where(nxt == N_EXPERTS, -1, nxt)
    seg_ends = goff + per_expert
    ends = jnp.concatenate([row_ends, seg_ends])
    tile_rows = jnp.clip(seg_ends[tile_expert] - tile_ids * tile, 0, tile)
    return (tab, ends.astype(jnp.int32), ltot.astype(jnp.int32), tile_expert.astype(jnp.int32),
            n_used.astype(jnp.int32), nxt.astype(jnp.int32), par.astype(jnp.int32), tile_rows.astype(jnp.int32))


def _sincos_table(rows, cols, dim):
    quarter = dim // 4
    omega = 1.0 / (10000.0 ** (np.arange(quarter, dtype=np.float64) / quarter))
    ang_r = np.arange(rows, dtype=np.float64)[:, None] * omega
    ang_c = np.arange(cols, dtype=np.float64)[:, None] * omega
    emb_r = np.concatenate([np.sin(ang_r), np.cos(ang_r)], axis=-1)
    emb_c = np.concatenate([np.sin(ang_c), np.cos(ang_c)], axis=-1)
    pe = np.concatenate([np.broadcast_to(emb_r[:, None, :], (rows, cols, 2 * quarter)),
                         np.broadcast_to(emb_c[None, :, :], (rows, cols, 2 * quarter))], axis=-1)
    return pe.reshape(rows * cols, 4 * quarter).astype(np.float32)


def _filter_features(n, width):
    pos = np.arange(n, dtype=np.float64)
    t = pos / max(n - 1, 1)
    ang = (2.0 * math.pi * pos / n)[:, None] * np.linspace(1e-4, FILTER_BANDS - 1, FILTER_BANDS)[None, :]
    z = np.concatenate([t[:, None], np.cos(ang), -np.sin(ang)], axis=-1)
    return np.pad(z, ((0, 0), (0, width - z.shape[1]))).astype(np.float32)


def _decay_rates(d_h):
    min_decay = math.log(DECAY_TARGET) / DECAY_SLOW_PCT
    max_decay = math.log(DECAY_TARGET) / DECAY_FAST_PCT
    return np.abs(np.linspace(min_decay, max_decay, d_h)).astype(np.float32).reshape(1, d_h)


def _dft_tables(blk):
    odd = 2 * np.arange(blk, dtype=np.int64) + 1
    shifted = ((odd[:, None] * odd[None, :]) % (8 * blk)).astype(np.float64) * (2.0 * math.pi / (8 * blk))
    plain = ((odd[:, None] * np.arange(blk, dtype=np.int64)[None, :]) % (4 * blk)).astype(np.float64) * (2.0 * math.pi / (4 * blk))
    sgn = (1.0 - 2.0 * (np.arange(blk) % 2)).astype(np.float32).reshape(blk, 1)
    f32 = lambda a: a.astype(np.float32)
    return f32(np.cos(shifted)), f32(np.sin(shifted)), f32(np.cos(plain)), f32(np.sin(plain)), sgn


def kernel(x, c, ctx, c_ctx, w_ada, b_ada, norm1_g, w_in, conv_a_w, conv_a_b, lru_wa, lru_ba, lru_wx, lru_bx, lru_lambda, conv_b_w, filt_w1, filt_b1, filt_freq1, filt_w2, filt_b2, filt_freq2, filt_w3, filt_b3, filt_bias, out_norm_a, out_norm_b, w_out, norm2_g, w_rg, b_rg, w_re, b_re, w_gate, w_up, w_down, final_g):
    bsz, n, d = x.shape
    n_ctx = ctx.shape[1]
    d_rnn = conv_a_w.shape[2]
    d_h = filt_bias.shape[2]
    order = filt_bias.shape[1]
    assert w_ada.shape[0] == 1, "single-layer block"
    l = 0
    ch = 256

    mod_rows = 16
    cc = jnp.concatenate([c, c_ctx[None, :], jnp.zeros((mod_rows - bsz - 1, d), F32)], axis=0)
    mod3 = _ada(cc, w_ada[l], b_ada[l]).reshape(mod_rows, N_MOD, d)

    pe = jnp.asarray(_sincos_table(n // GRID_W, GRID_W, d))
    p3 = _inproj_lat(x, pe, mod3, norm1_g[l], w_in[l], 512)
    pc3 = _inproj_ctx(ctx.reshape(bsz * n_ctx, d), mod3, bsz, norm1_g[l], w_in[l], 1, d_rnn,
                      1024).reshape(bsz, n_ctx, d_rnn)

    heads_per_blk = ch // (d_rnn // RNN_HEADS)
    nblk = d_rnn // ch
    eye = jnp.eye(heads_per_blk, dtype=F32)

    def blockdiag(w):
        w5 = w.reshape(2, nblk, heads_per_blk, w.shape[2], w.shape[3])
        return jnp.einsum("dnkij,kl->dnkilj", w5, eye).reshape(2, nblk, ch, ch)

    wa_bd, wx_bd = blockdiag(lru_wa[l]), blockdiag(lru_wx[l])
    w_gates = jnp.concatenate([wa_bd[0], wx_bd[0], wa_bd[1], wx_bd[1]], axis=-1).astype(BF16)
    ba, bx = lru_ba[l].reshape(2, nblk, ch), lru_bx[l].reshape(2, nblk, ch)
    bias = jnp.concatenate([ba[0], bx[0], ba[1], bx[1]], axis=-1).reshape(nblk, 1, 4 * ch)
    sp = jax.nn.softplus(-lru_lambda[l])
    cb = conv_a_b[l].reshape(1, d_rnn)
    zeros_state = jnp.zeros((bsz, 1, d_rnn), F32)
    hf_ctx, hb_ctx = _rglru(pc3, 0, 0, conv_a_w[l], cb, w_gates, bias, sp, zeros_state, zeros_state, False, ch)
    ya, _, _ = _rglru(p3, d_rnn // ch, 0, conv_a_w[l], cb, w_gates, bias, sp, hf_ctx, hb_ctx, True, ch)

    as_bf16 = lambda a: jnp.asarray(a).astype(BF16)
    c3, s3, cu, su, sgn = _dft_tables(CONV_BLOCK)
    fe = 64
    z = jnp.asarray(_filter_features(n, fe))
    w1 = jnp.pad(filt_w1[l], ((0, fe - filt_w1.shape[1]), (0, 0)))
    hid = filt_w2.shape[1]
    hr, hi = _filters(z, w1, filt_b1[l].reshape(1, hid), filt_freq1[l].reshape(1, hid), filt_w2[l],
                      filt_b2[l].reshape(1, hid), filt_freq2[l].reshape(1, hid), filt_w3[l],
                      filt_b3[l].reshape(1, -1), jnp.asarray(_decay_rates(d_h)), as_bf16(cu), as_bf16(su),
                      jnp.asarray(sgn), d_h, order, ch)
    yb = _hyena(p3, 2 * d_rnn // ch, conv_b_w[l], filt_bias[l], as_bf16(c3), as_bf16(s3), hr, hi, d_h, ch)

    w_r = jnp.concatenate([w_re[l], w_rg[l], jnp.zeros((d, LANES - N_EXPERTS - N_GROUPS), F32)], axis=1)
    b_r = jnp.concatenate([b_re[l], b_rg[l], jnp.zeros((LANES - N_EXPERTS - N_GROUPS,), F32)]).reshape(1, LANES)
    tb = 512
    tm = 512
    x1, h2, rt, cnt, prow = _merge(ya, yb, x, pe, mod3, out_norm_a[l], out_norm_b[l], w_out[l], norm2_g[l],
                                   w_r, b_r, tm, tm // tb)

    tile = 512
    t_all = bsz * n
    nb = t_all // tb
    lp = 2 * tb + N_EXPERTS * SEG
    n_tiles = (2 * t_all + nb * N_EXPERTS * (SEG - 1) + N_EXPERTS * (tile - 1)) // tile
    cnt_blocks = cnt[:, 0, :N_EXPERTS].astype(jnp.int32)
    tab, ends, ltot, tile_expert, n_used, nxt, par, tile_rows = _moe_layout(cnt_blocks, tile, n_tiles)
    xs = _dispatch(tab, ends, ltot, prow, h2.reshape(t_all, d), n_tiles * tile, tb, lp, tile)
    ys = _ffn(tile_expert, n_used, nxt, par, tile_rows, xs, w_gate[l], w_up[l], w_down[l], tile)
    out = _combine(tab, ltot, ys, rt.reshape(t_all, LANES), x1.reshape(t_all, d), mod3, final_g, tb, lp, n // tb)
    return out.reshape(bsz, n, d)
```

```python
import functools
import math

import numpy as np
import jax
import jax.numpy as jnp
from jax import lax
from jax.experimental import pallas as pl
from jax.experimental.pallas import tpu as pltpu

F32 = jnp.float32
BF16 = jnp.bfloat16
EPS = 1e-6
LRU_C = 8.0
N_MOD = 6
GRID_W = 64
RNN_HEADS = 8
N_GROUPS = 4
EXPERTS_PER_GROUP = 8
N_EXPERTS = N_GROUPS * EXPERTS_PER_GROUP
FILTER_BANDS = 16
DECAY_FAST_PCT = 0.3
DECAY_SLOW_PCT = 1.5
DECAY_TARGET = 1e-2
SUBLANES = 8
LANES = 128
SEG = 2 * SUBLANES
VMEM_LIMIT = 60 * 1024 * 1024


def _cparams(*sem):
    return pltpu.CompilerParams(dimension_semantics=sem, vmem_limit_bytes=VMEM_LIMIT)


def _dot(a, b):
    return jnp.dot(a, b, preferred_element_type=F32)


def _split(a):
    hi = a.astype(BF16)
    return hi, (a - hi.astype(F32)).astype(BF16)


def _dot3(a, b):
    a_hi, a_lo = _split(a)
    b_hi, b_lo = _split(b)
    return _dot(a_hi, b_hi) + _dot(a_lo, b_hi) + _dot(a_hi, b_lo)


def _rms(v, g):
    return v * lax.rsqrt(jnp.mean(v * v, axis=-1, keepdims=True) + EPS) * g


def _sigmoid(z):
    return 1.0 / (1.0 + jnp.exp(-z))


def _shift_rows(v, d):
    n = v.shape[0]
    rolled = pltpu.roll(v, (-d) % n, 0)
    row = lax.broadcasted_iota(jnp.int32, v.shape, 0)
    ok = (row + d >= 0) & (row + d < n)
    return jnp.where(ok, rolled, 0.0)


def _tile_rot(v, s):
    n, c = v.shape
    return pltpu.roll(v.reshape(n // SUBLANES, SUBLANES, c), s, 1).reshape(n, c)


def _ada_body(c_ref, w_ref, b_ref, o_ref):
    c = c_ref[...]
    o_ref[...] = _dot3(c * _sigmoid(c), w_ref[...]) + b_ref[...]


def _ada(cc, w_ada, b_ada):
    rows, d = cc.shape
    n = w_ada.shape[1]
    tn = 1024
    return pl.pallas_call(
        _ada_body,
        grid=(n // tn,),
        in_specs=[pl.BlockSpec((rows, d), lambda j: (0, 0)),
                  pl.BlockSpec((d, tn), lambda j: (0, j)),
                  pl.BlockSpec((1, tn), lambda j: (0, j))],
        out_specs=pl.BlockSpec((rows, tn), lambda j: (0, j)),
        out_shape=jax.ShapeDtypeStruct((rows, n), F32),
        compiler_params=_cparams("arbitrary"),
        name="ada",
    )(cc, w_ada, b_ada.reshape(1, n))


def _inproj_lat_body(x_ref, pe_ref, mod_ref, g_ref, w_ref, o_ref, wb_ref):
    @pl.when((pl.program_id(0) == 0) & (pl.program_id(1) == 0))
    def _():
        wb_ref[...] = w_ref[...].astype(BF16)

    x = x_ref[...] + pe_ref[...]
    h = _rms(x, g_ref[...]) * (1.0 + mod_ref[1:2, :]) + mod_ref[0:1, :]
    o_ref[...] = _dot(h.astype(BF16), wb_ref[...]).astype(o_ref.dtype)


def _inproj_lat(x3, pe, mod3, g, w, tm):
    b, l, d = x3.shape
    n_out = w.shape[1]
    return pl.pallas_call(
        _inproj_lat_body,
        grid=(l // tm, b),
        in_specs=[pl.BlockSpec((None, tm, d), lambda i, bi: (bi, i, 0)),
                  pl.BlockSpec((tm, d), lambda i, bi: (i, 0)),
                  pl.BlockSpec((None, N_MOD, d), lambda i, bi: (bi, 0, 0)),
                  pl.BlockSpec((1, d), lambda i, bi: (0, 0)),
                  pl.BlockSpec((d, n_out), lambda i, bi: (0, 0), pipeline_mode=pl.Buffered(1))],
        out_specs=pl.BlockSpec((None, tm, n_out), lambda i, bi: (bi, i, 0)),
        out_shape=jax.ShapeDtypeStruct((b, l, n_out), BF16),
        scratch_shapes=[pltpu.VMEM((d, n_out), BF16)],
        compiler_params=_cparams("arbitrary", "arbitrary"),
        name="inproj_pe",
    )(x3, pe, mod3, g.reshape(1, d), w)


def _inproj_ctx_body(x_ref, mod_ref, g_ref, w_ref, o_ref):
    h = _rms(x_ref[...], g_ref[...]) * (1.0 + mod_ref[1:2, :]) + mod_ref[0:1, :]
    o_ref[...] = _dot(h.astype(BF16), w_ref[...].astype(BF16))


def _inproj_ctx(x2, mod3, mod_row, g, w, col_block, n_out, tm):
    r, d = x2.shape
    return pl.pallas_call(
        _inproj_ctx_body,
        grid=(r // tm,),
        in_specs=[pl.BlockSpec((tm, d), lambda i: (i, 0)),
                  pl.BlockSpec((None, N_MOD, d), lambda i: (mod_row, 0, 0)),
                  pl.BlockSpec((1, d), lambda i: (0, 0)),
                  pl.BlockSpec((d, n_out), lambda i: (0, col_block))],
        out_specs=pl.BlockSpec((tm, n_out), lambda i: (i, 0)),
        out_shape=jax.ShapeDtypeStruct((r, n_out), F32),
        compiler_params=_cparams("arbitrary"),
        name="inproj_ctx",
    )(x2, mod3, g.reshape(1, d), w)


def _rglru_body(*refs, with_gate):
    if with_gate:
        (u_ref, gate_ref, cw_ref, cb_ref, w_ref, bias_ref, sp_ref, h0f_ref, h0b_ref,
         y_ref, hf_end_ref, hb_end_ref, af_ref, bf_ref, ab_ref, bb_ref) = refs
    else:
        (u_ref, cw_ref, cb_ref, w_ref, bias_ref, sp_ref, h0f_ref, h0b_ref,
         hf_end_ref, hb_end_ref, af_ref, bf_ref, ab_ref, bb_ref) = refs
    n, c = u_ref.shape
    u = u_ref[...].astype(F32)
    v = cb_ref[...] + cw_ref[1:2, :] * _shift_rows(u, -1) + cw_ref[0:1, :] * _shift_rows(u, -2)
    v = v + cw_ref[2:3, :] * u + cw_ref[3:4, :] * _shift_rows(u, 1)
    z = _dot(v.astype(BF16), w_ref[...]) + bias_ref[...]
    sub = lax.broadcasted_iota(jnp.int32, (n, c), 0) % SUBLANES

    def local_scan(k, reverse, a_ref, b_ref):
        r = _sigmoid(z[:, (2 * k) * c:(2 * k + 1) * c])
        i = _sigmoid(z[:, (2 * k + 1) * c:(2 * k + 2) * c])
        log_a = (-LRU_C) * r * sp_ref[k:k + 1, :]
        a = jnp.exp(log_a)
        b = jnp.sqrt(1.0 - a * a) * (i * v)
        for s in (1, 2, 4):
            ok = (sub < SUBLANES - s) if reverse else (sub >= s)
            shift = SUBLANES - s if reverse else s
            b = a * jnp.where(ok, _tile_rot(b, shift), 0.0) + b
            a = a * jnp.where(ok, _tile_rot(a, shift), 1.0)
        a_ref[...] = a
        b_ref[...] = b

    local_scan(0, False, af_ref, bf_ref)
    local_scan(1, True, ab_ref, bb_ref)

    n_tiles = n // SUBLANES

    def carry_step(q, carry):
        cf, cb = carry
        rf = pl.ds(pl.multiple_of(q * SUBLANES, SUBLANES), SUBLANES)
        rb = pl.ds(pl.multiple_of((n_tiles - 1 - q) * SUBLANES, SUBLANES), SUBLANES)
        af, bf, ab, bb = af_ref[rf, :], bf_ref[rf, :], ab_ref[rb, :], bb_ref[rb, :]
        bf_ref[rf, :] = af * cf + bf
        bb_ref[rb, :] = ab * cb + bb
        last = lambda v: jnp.broadcast_to(v[SUBLANES - 1:SUBLANES, :], (SUBLANES, c))
        first = lambda v: jnp.broadcast_to(v[0:1, :], (SUBLANES, c))
        return last(af) * cf + last(bf), first(ab) * cb + first(bb)

    cf0 = jnp.broadcast_to(h0f_ref[...], (SUBLANES, c))
    cb0 = jnp.broadcast_to(h0b_ref[...], (SUBLANES, c))
    cf, cb = lax.fori_loop(0, n_tiles, carry_step, (cf0, cb0), unroll=4)
    hf_end_ref[...] = cf[0:1, :]
    hb_end_ref[...] = cb[0:1, :]
    if with_gate:
        y_ref[...] = jax.nn.gelu(gate_ref[...].astype(F32), approximate=True) * (bf_ref[...] + bb_ref[...])


def _rglru(p3, u_blk0, gate_blk0, cw, cb, w_gates, bias, sp, h0f, h0b, with_gate, ch):
    b, l, _ = p3.shape
    d_rnn = cw.shape[1]
    nh = d_rnn // ch
    in_specs = [pl.BlockSpec((None, l, ch), lambda bi, h: (bi, 0, u_blk0 + h))]
    args = [p3]
    if with_gate:
        in_specs.append(pl.BlockSpec((None, l, ch), lambda bi, h: (bi, 0, gate_blk0 + h)))
        args.append(p3)
    in_specs += [pl.BlockSpec((4, ch), lambda bi, h: (0, h)),
                 pl.BlockSpec((1, ch), lambda bi, h: (0, h)),
                 pl.BlockSpec((None, ch, 4 * ch), lambda bi, h: (h, 0, 0)),
                 pl.BlockSpec((None, 1, 4 * ch), lambda bi, h: (h, 0, 0)),
                 pl.BlockSpec((2, ch), lambda bi, h: (0, h)),
                 pl.BlockSpec((None, 1, ch), lambda bi, h: (bi, 0, h)),
                 pl.BlockSpec((None, 1, ch), lambda bi, h: (bi, 0, h))]
    args += [cw, cb, w_gates, bias, sp, h0f, h0b]
    end_spec = pl.BlockSpec((None, 1, ch), lambda bi, h: (bi, 0, h))
    end_shape = jax.ShapeDtypeStruct((b, 1, d_rnn), F32)
    out_specs = [end_spec, end_spec]
    out_shape = [end_shape, end_shape]
    if with_gate:
        out_specs = [pl.BlockSpec((None, l, ch), lambda bi, h: (bi, 0, h))] + out_specs
        out_shape = [jax.ShapeDtypeStruct((b, l, d_rnn), F32)] + out_shape
    return pl.pallas_call(
        functools.partial(_rglru_body, with_gate=with_gate),
        grid=(b, nh),
        in_specs=in_specs,
        out_specs=out_specs,
        out_shape=out_shape,
        scratch_shapes=[pltpu.VMEM((l, ch), F32)] * 4,
        compiler_params=_cparams("arbitrary", "arbitrary"),
        name="rglru_lat" if with_gate else "rglru_ctx",
    )(*args)


CONV_BLOCK = 512


def _filt_body(z_ref, w1_ref, b1_ref, f1_ref, w2_ref, b2_ref, f2_ref, w3f_ref, w3b_ref, b3f_ref, b3b_ref,
               dec_ref, cu_ref, su_ref, sgn_ref, hr_ref, hi_ref, hdn_ref):
    n = z_ref.shape[0]
    blk = cu_ref.shape[0]
    nblk = n // blk
    ch = hr_ref.shape[2]

    @pl.when(pl.program_id(0) == 0)
    def _():
        hid = jnp.sin(f1_ref[...] * (_dot3(z_ref[...], w1_ref[...]) + b1_ref[...]))
        hdn_ref[...] = jnp.sin(f2_ref[...] * (_dot3(hid, w2_ref[...]) + b2_ref[...]))

    hdn = hdn_ref[...]
    decay = jnp.exp(-z_ref[:, 0:1] * dec_ref[...])
    kf = (_dot3(hdn, w3f_ref[...]) + b3f_ref[...]) * decay
    kb = (_dot3(hdn, w3b_ref[...]) + b3b_ref[...]) * decay
    row = lax.broadcasted_iota(jnp.int32, (n, ch), 0)
    kb = jnp.where(row == 0, 0.0, kb)
    norm = jnp.sum(jnp.abs(kf) + jnp.abs(kb), axis=0, keepdims=True)
    scale = (2.0 / (2 * blk)) / norm
    sgn = sgn_ref[...]

    p, q, first = [], [], []
    for c in range(nblk):
        taps = jnp.concatenate([kf[c * blk:(c + 1) * blk], kb[c * blk:(c + 1) * blk]], axis=1).astype(BF16)
        p.append(_dot(cu_ref[...], taps))
        q.append(_dot(su_ref[...], taps))
        first.append(taps[0:1, :].astype(F32))
    fwd = lambda a: a[:, :ch]
    bwd = lambda a: a[:, ch:]
    out = lambda a: (a * scale).astype(hr_ref.dtype)
    hr_ref[nblk - 1] = out(fwd(p[0]) + bwd(p[0]))
    hi_ref[nblk - 1] = out(bwd(q[0]) - fwd(q[0]))
    for c in range(1, nblk):
        hr_ref[nblk - 1 + c] = out(fwd(p[c]) + sgn * fwd(q[c - 1]))
        hi_ref[nblk - 1 + c] = out(sgn * (fwd(p[c - 1]) - fwd(first[c - 1])) - fwd(q[c]))
        hr_ref[nblk - 1 - c] = out(bwd(p[c]) + sgn * bwd(q[c - 1]))
        hi_ref[nblk - 1 - c] = out(bwd(q[c]) - sgn * (bwd(p[c - 1]) - bwd(first[c - 1])))


def _filters(z, w1, b1, f1, w2, b2, f2, w3, b3, deltas, cu, su, sgn, d_h, order, ch):
    n, fe = z.shape
    hid = w2.shape[0]
    blk = cu.shape[0]
    nh = 2 * (n // blk) - 1
    per_order = d_h // ch
    ncol = order * per_order
    const = lambda shape: pl.BlockSpec(shape, lambda g: (0, 0))
    hspec = pl.BlockSpec((nh, blk, ch), lambda g: (0, 0, g))
    return pl.pallas_call(
        _filt_body,
        grid=(ncol,),
        in_specs=[const((n, fe)), const((fe, hid)), const((1, hid)), const((1, hid)),
                  const((hid, hid)), const((1, hid)), const((1, hid)),
                  pl.BlockSpec((hid, ch), lambda g: (0, g)),
                  pl.BlockSpec((hid, ch), lambda g: (0, ncol + g)),
                  pl.BlockSpec((1, ch), lambda g: (0, g)),
                  pl.BlockSpec((1, ch), lambda g: (0, ncol + g)),
                  pl.BlockSpec((1, ch), lambda g: (0, g % per_order)),
                  const((blk, blk)), const((blk, blk)), const((blk, 1))],
        out_specs=[hspec, hspec],
        out_shape=[jax.ShapeDtypeStruct((nh, blk, order * d_h), BF16)] * 2,
        scratch_shapes=[pltpu.VMEM((n, hid), F32)],
        compiler_params=_cparams("arbitrary"),
        name="hyena_filters",
    )(z, w1, b1, f1, w2, b2, f2, w3, w3, b3, b3, deltas, cu, su, sgn)


def _hyena_body(v_ref, x1_ref, x2_ref, wv_ref, w1_ref, w2_ref, fb_ref, c3_ref, s3_ref,
                h0r_ref, h0i_ref, h1r_ref, h1i_ref, o_ref, z_ref, zb_ref, p_ref, q_ref, yr_ref, yi_ref):
    n = o_ref.shape[0]
    blk = c3_ref.shape[0]
    nblk = n // blk
    halo = 2 * SUBLANES
    sub = 32

    def conv3(src_ref, w_ref, r0):
        lo, hi = max(r0 - halo, 0), min(r0 + blk + halo, n)
        win = src_ref[lo:hi, :].astype(F32)
        size = hi - lo
        row = lax.broadcasted_iota(jnp.int32, win.shape, 0)
        prev = pltpu.roll(win, 1, 0)
        nxt = pltpu.roll(win, size - 1, 0)
        if lo == 0:
            prev = jnp.where(row == 0, 0.0, prev)
        if hi == n:
            nxt = jnp.where(row == size - 1, 0.0, nxt)
        out = w_ref[0:1, :] * prev + w_ref[1:2, :] * win + w_ref[2:3, :] * nxt
        return out[r0 - lo:r0 - lo + blk, :]

    def long_conv(hr_ref, hi_ref, fb, gate_ref, gate_w_ref, dst_ref):
        for i in range(nblk):
            rows = slice(i * blk, (i + 1) * blk)
            p_ref[rows, :] = _dot(c3_ref[...], zb_ref[rows, :]).astype(BF16)
            q_ref[rows, :] = _dot(s3_ref[...], zb_ref[rows, :]).astype(BF16)

        for j in range(nblk):
            for r0 in range(0, blk, sub):
                yr = yi = None
                for i in range(nblk):
                    rows = slice(i * blk + r0, i * blk + r0 + sub)
                    p, q = p_ref[rows, :], q_ref[rows, :]
                    gr = hr_ref[nblk - 1 + j - i, r0:r0 + sub, :]
                    gi = hi_ref[nblk - 1 + j - i, r0:r0 + sub, :]
                    tr, ti = p * gr + q * gi, q * gr - p * gi
                    yr, yi = (tr, ti) if yr is None else (yr + tr, yi + ti)
                yr_ref[j * blk + r0:j * blk + r0 + sub, :] = yr
                yi_ref[j * blk + r0:j * blk + r0 + sub, :] = yi
            rows = slice(j * blk, (j + 1) * blk)
            y = _dot(c3_ref[...], yr_ref[rows, :]) + _dot(s3_ref[...], yi_ref[rows, :])
            dst_ref[rows, :] = conv3(gate_ref, gate_w_ref, j * blk) * (y + z_ref[rows, :] * fb)

    for r0 in range(0, n, blk):
        z_ref[r0:r0 + blk, :] = conv3(v_ref, wv_ref, r0)
    zb_ref[...] = z_ref[...].astype(BF16)
    long_conv(h0r_ref, h0i_ref, fb_ref[0:1, :], x1_ref, w1_ref, z_ref)
    zb_ref[...] = z_ref[...].astype(BF16)
    long_conv(h1r_ref, h1i_ref, fb_ref[1:2, :], x2_ref, w2_ref, o_ref)


def _hyena(p3, col_blk0, conv_w, fbias, c3, s3, hr, hi, d_h, ch):
    b, n, _ = p3.shape
    nh, blk, _ = hr.shape
    per = d_h // ch
    zspec = lambda k: pl.BlockSpec((None, n, ch), lambda h, bi: (bi, 0, col_blk0 + k * per + h))
    wspec = lambda k: pl.BlockSpec((3, ch), lambda h, bi: (0, k * per + h))
    hspec = lambda o: pl.BlockSpec((nh, blk, ch), lambda h, bi: (0, 0, o * per + h), pipeline_mode=pl.Buffered(1))
    const = pl.BlockSpec((blk, blk), lambda h, bi: (0, 0))
    return pl.pallas_call(
        _hyena_body,
        grid=(per, b),
        in_specs=[zspec(0), zspec(1), zspec(2), wspec(0), wspec(1), wspec(2),
                  pl.BlockSpec((2, ch), lambda h, bi: (0, h)), const, const,
                  hspec(0), hspec(0), hspec(1), hspec(1)],
        out_specs=pl.BlockSpec((None, n, ch), lambda h, bi: (bi, 0, h)),
        out_shape=jax.ShapeDtypeStruct((b, n, d_h), F32),
        scratch_shapes=[pltpu.VMEM((n, ch), F32), pltpu.VMEM((n, ch), BF16),
                        pltpu.VMEM((n, ch), BF16), pltpu.VMEM((n, ch), BF16),
                        pltpu.VMEM((n, ch), BF16), pltpu.VMEM((n, ch), BF16)],
        compiler_params=_cparams("arbitrary", "arbitrary"),
        name="hyena_mix",
    )(p3, p3, p3, conv_w, conv_w, conv_w, fbias, c3, s3, hr, hi, hr, hi)


def _merge_body(ya_ref, yb_ref, x_ref, pe_ref, mod_ref, ga_ref, gb_ref, wo_ref, g2_ref, wr_ref, br_ref,
                x1_ref, h2_ref, rt_ref, cnt_ref, prow_ref, wob_ref, tri_ref):
    @pl.when((pl.program_id(0) == 0) & (pl.program_id(1) == 0))
    def _():
        wob_ref[...] = wo_ref[...].astype(BF16)
        earlier = lax.broadcasted_iota(jnp.int32, tri_ref.shape, 1) < lax.broadcasted_iota(jnp.int32, tri_ref.shape, 0)
        tri_ref[...] = jnp.where(earlier, 1.0, 0.0).astype(BF16)

    da = ya_ref.shape[1]
    na = _rms(ya_ref[...], ga_ref[...]).astype(BF16)
    nb = _rms(yb_ref[...], gb_ref[...]).astype(BF16)
    y = _dot(na, wob_ref[0:da, :]) + _dot(nb, wob_ref[da:, :])
    _route(y, x_ref, pe_ref, mod_ref, g2_ref, wr_ref, br_ref, x1_ref, h2_ref, rt_ref, cnt_ref, prow_ref, tri_ref)


def _route(y, x_ref, pe_ref, mod_ref, g2_ref, wr_ref, br_ref, x1_ref, h2_ref, rt_ref, cnt_ref, prow_ref, tri_ref):
    x1 = x_ref[...] + pe_ref[...] + mod_ref[2:3, :] * y
    x1_ref[...] = x1
    h2 = _rms(x1, g2_ref[...]) * (1.0 + mod_ref[4:5, :]) + mod_ref[3:4, :]
    h2_ref[...] = h2.astype(BF16)

    logits = _dot3(h2, wr_ref[...]) + br_ref[...]
    lane = lax.broadcasted_iota(jnp.int32, logits.shape, 1).astype(F32)
    neg = -jnp.inf
    big = jnp.float32(1 << 20)
    gl = jnp.where((lane >= N_EXPERTS) & (lane < N_EXPERTS + N_GROUPS), logits, neg)
    gmax = jnp.max(gl, axis=1, keepdims=True)
    g_p = 1.0 / jnp.sum(jnp.exp(gl - gmax), axis=1, keepdims=True)
    g_i = jnp.min(jnp.where(gl == gmax, lane, big), axis=1, keepdims=True) - N_EXPERTS
    lo = g_i * EXPERTS_PER_GROUP
    el = jnp.where((lane >= lo) & (lane < lo + EXPERTS_PER_GROUP), logits, neg)
    m1 = jnp.max(el, axis=1, keepdims=True)
    i1 = jnp.min(jnp.where(el == m1, lane, big), axis=1, keepdims=True)
    el2 = jnp.where(lane == i1, neg, el)
    m2 = jnp.max(el2, axis=1, keepdims=True)
    i2 = jnp.min(jnp.where(el2 == m2, lane, big), axis=1, keepdims=True)
    e2 = jnp.exp(m2 - m1)
    w1 = g_p / (1.0 + e2)
    w2 = g_p * e2 / (1.0 + e2)
    o1 = jnp.where(lane == i1, 1.0, 0.0)
    o2 = jnp.where(lane == i2, 1.0, 0.0)
    nsub = cnt_ref.shape[0]
    sub = o1.shape[0] // nsub
    sq = lambda shape, d: lax.broadcasted_iota(jnp.int32, shape, d)
    earlier = tri_ref[...]
    below = jnp.where(sq((LANES, LANES), 0) < sq((LANES, LANES), 1), 1.0, 0.0).astype(BF16)
    pick_row = jnp.where(sq((SUBLANES, LANES), 0) == sq((SUBLANES, LANES), 1), 1.0, 0.0).astype(BF16)
    row_id = sq((SUBLANES, sub), 0)
    pos1, pos2 = [], []
    for k in range(nsub):
        a1, a2 = o1[k * sub:(k + 1) * sub], o2[k * sub:(k + 1) * sub]
        both = a1 + a2
        count = jnp.sum(both, axis=0, keepdims=True)
        cnt_ref[k] = jnp.broadcast_to(count, cnt_ref.shape[1:])
        runs = jnp.broadcast_to(jnp.floor((count + (SEG - 1)) / SEG), (SUBLANES, LANES))
        start = _dot(runs.astype(BF16), below)[0:1, :] * SEG
        before = _dot(earlier, both.astype(BF16)) + start
        p1 = jnp.sum(a1 * before, axis=1, keepdims=True)
        p2 = jnp.sum(a2 * before, axis=1, keepdims=True)
        pos1.append(p1)
        pos2.append(p2)
        lane_k = sq((sub, LANES), 1).astype(F32)
        hi1, hi2 = jnp.floor(p1 / 64.0), jnp.floor(p2 / 64.0)
        parts = (jnp.where(lane_k == 0.0, hi1, 0.0) + jnp.where(lane_k == 1.0, p1 - 64.0 * hi1, 0.0)
                 + jnp.where(lane_k == 2.0, hi2, 0.0) + jnp.where(lane_k == 3.0, p2 - 64.0 * hi2, 0.0))
        rows = lax.dot_general(pick_row, parts.astype(BF16), (((1,), (1,)), ((), ())), preferred_element_type=F32)
        q1 = rows[0:1, :] * 64.0 + rows[1:2, :]
        q2 = rows[2:3, :] * 64.0 + rows[3:4, :]
        prow_ref[k] = jnp.where(row_id == 0, q1, jnp.where(row_id == 1, q2, 0.0))
    p1 = jnp.concatenate(pos1, axis=0)
    p2 = jnp.concatenate(pos2, axis=0)
    rt_ref[...] = (jnp.where(lane == 0.0, i1, 0.0) + jnp.where(lane == 1.0, i2, 0.0)
                   + jnp.where(lane == 2.0, w1, 0.0) + jnp.where(lane == 3.0, w2, 0.0)
                   + jnp.where(lane == 4.0, p1, 0.0) + jnp.where(lane == 5.0, p2, 0.0))


def _merge(ya, yb, x3, pe, mod3, ga, gb, w_out, g2n, w_r, b_r, tm, nsub):
    b, n, d = x3.shape
    da, db = ya.shape[2], yb.shape[2]
    per = n // tm
    tok = lambda w: pl.BlockSpec((None, tm, w), lambda i, bi: (bi, i, 0))
    const = lambda shape: pl.BlockSpec(shape, lambda i, bi: (0, 0))
    blk = lambda w: pl.BlockSpec((nsub, SUBLANES, w), lambda i, bi: (bi * per + i, 0, 0))
    return pl.pallas_call(
        _merge_body,
        grid=(per, b),
        in_specs=[tok(da), tok(db), tok(d), pl.BlockSpec((tm, d), lambda i, bi: (i, 0)),
                  pl.BlockSpec((None, N_MOD, d), lambda i, bi: (bi, 0, 0)),
                  const((1, da)), const((1, db)), const((da + db, d)), const((1, d)),
                  const((d, LANES)), const((1, LANES))],
        out_specs=[tok(d), tok(d), tok(LANES), blk(LANES), blk(tm // nsub)],
        out_shape=[jax.ShapeDtypeStruct((b, n, d), F32), jax.ShapeDtypeStruct((b, n, d), BF16),
                   jax.ShapeDtypeStruct((b, n, LANES), F32),
                   jax.ShapeDtypeStruct((b * per * nsub, SUBLANES, LANES), F32),
                   jax.ShapeDtypeStruct((b * per * nsub, SUBLANES, tm // nsub), F32)],
        scratch_shapes=[pltpu.VMEM((da + db, d), BF16), pltpu.VMEM((tm // nsub, tm // nsub), BF16)],
        compiler_params=_cparams("arbitrary", "arbitrary"),
        name="merge_route",
    )(ya, yb, x3, pe, mod3, ga.reshape(1, da), gb.reshape(1, db), w_out, g2n.reshape(1, d), w_r, b_r)


def _segment_copy(local_ref, lrow, global_ref, grow, rows, sem, to_global):
    lrows = pl.ds(pl.multiple_of(lrow, SEG), rows)
    grows = pl.ds(pl.multiple_of(grow, SEG), rows)
    if to_global:
        return pltpu.make_async_copy(local_ref.at[lrows, :], global_ref.at[grows, :], sem)
    return pltpu.make_async_copy(global_ref.at[grows, :], local_ref.at[lrows, :], sem)


def _start_segments(tab_ref, blk, local_ref, global_ref, sem, to_global):
    def per_expert(e, carry):
        k = (blk * N_EXPERTS + e) * 3
        ls, gs, nchunk = tab_ref[k], tab_ref[k + 1], tab_ref[k + 2]

        def per_chunk(ci, carry2):
            _segment_copy(local_ref, ls + ci * SEG, global_ref, gs + ci * SEG, SEG, sem, to_global).start()
            return carry2

        return lax.fori_loop(0, nchunk, per_chunk, carry)

    lax.fori_loop(0, N_EXPERTS, per_expert, 0, unroll=2)


def _wait_segments(rows, local_ref, global_ref, sem, to_global):
    chunks = rows // SEG
    nbits = (local_ref.shape[0] // SEG).bit_length()
    for k in range(nbits):
        @pl.when(((chunks >> k) & 1) == 1)
        def _():
            _segment_copy(local_ref, 0, global_ref, 0, SEG << k, sem, to_global).wait()


def _dispatch_body(tab_ref, ends_ref, ltot_ref, prow_ref, h2_ref, xs_ref, loc_ref, zero_ref, sem, zsem):
    blk = pl.program_id(0)
    nblk = pl.num_programs(0)
    tb = h2_ref.shape[0]
    lp = loc_ref.shape[1]
    tile = zero_ref.shape[0]
    slot = blk % 2

    @pl.when(blk == 0)
    def _():
        zero_ref[...] = jnp.zeros_like(zero_ref)

        def zero_fill(first_row, rows):
            dst = xs_ref.at[pl.ds(pl.multiple_of(first_row, SEG), rows), :]
            return pltpu.make_async_copy(zero_ref.at[pl.ds(0, rows), :], dst, zsem)

        pad = 64

        def fill(e, carry, wait):
            end, real = ends_ref[e], ends_ref[N_EXPERTS + e]
            first = real // pad * pad

            def one(j, carry2):
                cp = zero_fill(first + j * pad, pad)
                cp.wait() if wait else cp.start()
                return carry2

            return lax.fori_loop(0, (end - first) // pad, one, carry)

        lax.fori_loop(0, N_EXPERTS, functools.partial(fill, wait=False), 0)
        used = ends_ref[N_EXPERTS - 1]
        spare = (xs_ref.shape[0] - used) // tile

        def fill_spare(j, carry, wait):
            cp = zero_fill(used + j * tile, tile)
            cp.wait() if wait else cp.start()
            return carry

        lax.fori_loop(0, spare, functools.partial(fill_spare, wait=False), 0)
        lax.fori_loop(0, N_EXPERTS, functools.partial(fill, wait=True), 0)
        lax.fori_loop(0, spare, functools.partial(fill_spare, wait=True), 0)

    pos = lax.broadcasted_iota(jnp.int32, (lp, tb), 0).astype(F32)
    onehot = jnp.where((pos == prow_ref[0:1, :]) | (pos == prow_ref[1:2, :]), 1.0, 0.0).astype(BF16)
    loc_ref[slot] = _dot(onehot, h2_ref[...]).astype(BF16)

    _start_segments(tab_ref, blk, loc_ref.at[slot], xs_ref, sem.at[slot], True)

    @pl.when(blk > 0)
    def _():
        _wait_segments(ltot_ref[blk - 1], loc_ref.at[1 - slot], xs_ref, sem.at[1 - slot], True)

    @pl.when(blk == nblk - 1)
    def _():
        _wait_segments(ltot_ref[blk], loc_ref.at[slot], xs_ref, sem.at[slot], True)


def _dispatch(tab, ends, ltot, prow, h2, slots, tb, lp, tile):
    t, d = h2.shape
    return pl.pallas_call(
        _dispatch_body,
        grid_spec=pltpu.PrefetchScalarGridSpec(
            num_scalar_prefetch=3,
            grid=(t // tb,),
            in_specs=[pl.BlockSpec((None, SUBLANES, tb), lambda i, *_: (i, 0, 0)),
                      pl.BlockSpec((tb, d), lambda i, *_: (i, 0))],
            out_specs=pl.BlockSpec(memory_space=pl.ANY),
            scratch_shapes=[pltpu.VMEM((2, lp, d), BF16), pltpu.VMEM((tile, d), BF16),
                            pltpu.SemaphoreType.DMA((2,)), pltpu.SemaphoreType.DMA]),
        out_shape=jax.ShapeDtypeStruct((slots, d), BF16),
        compiler_params=_cparams("arbitrary"),
        name="moe_dispatch",
    )(tab, ends, ltot, prow, h2)


def _ffn_body(te_ref, nu_ref, nxt_ref, par_ref, rows_ref, xs_ref, wg_hbm, wu_hbm, wd_hbm, ys_ref,
              wg_buf, wu_buf, wd_buf, wgb_ref, wub_ref, wdb_ref, sem):
    i = pl.program_id(0)
    e = te_ref[i]
    first = (i == 0) | (e != te_ref[jnp.maximum(i - 1, 0)])
    active = i < nu_ref[0]

    def fetch(expert, slot):
        return [pltpu.make_async_copy(w.at[expert], buf.at[slot], sem.at[slot])
                for w, buf in ((wg_hbm, wg_buf), (wu_hbm, wu_buf), (wd_hbm, wd_buf))]

    @pl.when(i == 0)
    def _():
        for cp in fetch(e, par_ref[e]):
            cp.start()

    @pl.when(first & active)
    def _():
        slot = par_ref[e]
        for cp in fetch(e, slot):
            cp.wait()
        nxt = nxt_ref[e]

        @pl.when(nxt >= 0)
        def _():
            for cp in fetch(nxt, 1 - slot):
                cp.start()

        wgb_ref[...] = wg_buf[slot].astype(BF16)
        wub_ref[...] = wu_buf[slot].astype(BF16)
        wdb_ref[...] = wd_buf[slot].astype(BF16)

    def expert(rows):
        xb = xs_ref[rows, :]
        act = _dot(xb, wgb_ref[...])
        act = act * _sigmoid(act) * _dot(xb, wub_ref[...])
        ys_ref[rows, :] = _dot(act.astype(BF16), wdb_ref[...]).astype(BF16)

    half = xs_ref.shape[0] // 2

    @pl.when(active & (rows_ref[i] > half))
    def _():
        expert(slice(None))

    @pl.when(active & (rows_ref[i] <= half))
    def _():
        expert(slice(0, half))
        ys_ref[half:, :] = jnp.zeros((half, ys_ref.shape[1]), BF16)


def _ffn(tile_expert, n_used, nxt, par, tile_rows, xs, w_gate, w_up, w_down, tile):
    slots, d = xs.shape
    _, _, de = w_gate.shape
    last = lambda i, te, nu, *_: (jnp.minimum(i, nu[0] - 1), 0)
    hbm = pl.BlockSpec(memory_space=pl.ANY)
    return pl.pallas_call(
        _ffn_body,
        grid_spec=pltpu.PrefetchScalarGridSpec(
            num_scalar_prefetch=5,
            grid=(slots // tile,),
            in_specs=[pl.BlockSpec((tile, d), last), hbm, hbm, hbm],
            out_specs=pl.BlockSpec((tile, d), last),
            scratch_shapes=[pltpu.VMEM((2, d, de), F32), pltpu.VMEM((2, d, de), F32), pltpu.VMEM((2, de, d), F32),
                            pltpu.VMEM((d, de), BF16), pltpu.VMEM((d, de), BF16), pltpu.VMEM((de, d), BF16),
                            pltpu.SemaphoreType.DMA((2,))]),
        out_shape=jax.ShapeDtypeStruct((slots, d), BF16),
        input_output_aliases={5: 0},
        compiler_params=_cparams("arbitrary"),
        name="moe_ffn",
    )(tile_expert, n_used, nxt, par, tile_rows, xs, w_gate, w_up, w_down)


def _combine_body(tab_ref, ltot_ref, ys_ref, rt_ref, x1_ref, mod_ref, fg_ref, o_ref, loc_ref, sem):
    blk = pl.program_id(0)
    nblk = pl.num_programs(0)
    _, lp, d = loc_ref.shape
    slot = blk % 2

    @pl.when(blk == 0)
    def _():
        _start_segments(tab_ref, blk, loc_ref.at[slot], ys_ref, sem.at[slot], False)

    @pl.when(blk + 1 < nblk)
    def _():
        _start_segments(tab_ref, blk + 1, loc_ref.at[1 - slot], ys_ref, sem.at[1 - slot], False)

    rt = rt_ref[...]
    lane = lax.broadcasted_iota(jnp.int32, rt.shape, 1)
    col = lambda k: jnp.sum(jnp.where(lane == k, rt, 0.0), axis=1, keepdims=True)
    w1, w2, p1, p2 = col(2), col(3), col(4), col(5)
    pos = lax.broadcasted_iota(jnp.int32, (rt.shape[0], lp), 1).astype(F32)
    weights = (jnp.where(pos == p1, w1, 0.0) + jnp.where(pos == p2, w2, 0.0)).astype(BF16)
    _wait_segments(ltot_ref[blk], loc_ref.at[slot], ys_ref, sem.at[slot], False)

    def clear(ci, carry):
        loc_ref[slot, pl.ds(pl.multiple_of(ci * SEG, SEG), SEG), :] = jnp.zeros((SEG, d), BF16)
        return carry

    lax.fori_loop(ltot_ref[blk] // SEG, lp // SEG, clear, 0)
    moe = _dot(weights, loc_ref[slot])
    o_ref[...] = _rms(x1_ref[...] + mod_ref[5:6, :] * moe, fg_ref[...])


def _combine(tab, ltot, ys, rt2, x1, mod3, final_g, tb, lp, per_batch):
    t, d = x1.shape
    return pl.pallas_call(
        _combine_body,
        grid_spec=pltpu.PrefetchScalarGridSpec(
            num_scalar_prefetch=2,
            grid=(t // tb,),
            in_specs=[pl.BlockSpec(memory_space=pl.ANY),
                      pl.BlockSpec((tb, LANES), lambda i, *_: (i, 0)),
                      pl.BlockSpec((tb, d), lambda i, *_: (i, 0)),
                      pl.BlockSpec((None, N_MOD, d), lambda i, *_: (i // per_batch, 0, 0)),
                      pl.BlockSpec((1, d), lambda i, *_: (0, 0))],
            out_specs=pl.BlockSpec((tb, d), lambda i, *_: (i, 0)),
            scratch_shapes=[pltpu.VMEM((2, lp, d), BF16), pltpu.SemaphoreType.DMA((2,))]),
        out_shape=jax.ShapeDtypeStruct((t, d), F32),
        compiler_params=_cparams("arbitrary"),
        name="moe_combine",
    )(tab, ltot, ys, rt2, x1, mod3, final_g.reshape(1, d))


def _moe_layout(cnt_blocks, tile, n_tiles):
    lcnt = (cnt_blocks + SEG - 1) // SEG * SEG
    lstart = jnp.cumsum(lcnt, axis=1) - lcnt
    ltot = jnp.sum(lcnt, axis=1)
    per_expert = jnp.sum(lcnt, axis=0)
    tiles_per = (per_expert + tile - 1) // tile
    tile_ends = jnp.cumsum(tiles_per)
    row_ends = tile_ends * tile
    goff = row_ends - tiles_per * tile
    gstart = goff[None, :] + jnp.cumsum(lcnt, axis=0) - lcnt
    tab = jnp.stack([lstart, gstart, lcnt // SEG], axis=-1).reshape(-1).astype(jnp.int32)
    n_used = tile_ends[-1:]
    tile_ids = jnp.arange(n_tiles, dtype=jnp.int32)
    tile_expert = jnp.sum((tile_ends[None, :] <= jnp.minimum(tile_ids, n_used - 1)[:, None]).astype(jnp.int32), axis=1)
    ids = jnp.arange(N_EXPERTS, dtype=jnp.int32)
    used = tiles_per > 0
    par = (jnp.cumsum(used) - used) % 2
    later = jnp.where(used[None, :] & (ids[None, :] > ids[:, None]), ids[None, :], N_EXPERTS)
    nxt = jnp.min(later, axis=1)
    nxt = jnp.where(nxt == N_EXPERTS, -1, nxt)
    seg_ends = goff + per_expert
    ends = jnp.concatenate([row_ends, seg_ends])
    tile_rows = jnp.clip(seg_ends[tile_expert] - tile_ids * tile, 0, tile)
    return (tab, ends.astype(jnp.int32), ltot.astype(jnp.int32), tile_expert.astype(jnp.int32),
            n_used.astype(jnp.int32), nxt.astype(jnp.int32), par.astype(jnp.int32), tile_rows.astype(jnp.int32))


def _sincos_table(rows, cols, dim):
    quarter = dim // 4
    omega = 1.0 / (10000.0 ** (np.arange(quarter, dtype=np.float64) / quarter))
    ang_r = np.arange(rows, dtype=np.float64)[:, None] * omega
    ang_c = np.arange(cols, dtype=np.float64)[:, None] * omega
    emb_r = np.concatenate([np.sin(ang_r), np.cos(ang_r)], axis=-1)
    emb_c = np.concatenate([np.sin(ang_c), np.cos(ang_c)], axis=-1)
    pe = np.concatenate([np.broadcast_to(emb_r[:, None, :], (rows, cols, 2 * quarter)),
                         np.broadcast_to(emb_c[None, :, :], (rows, cols, 2 * quarter))], axis=-1)
    return pe.reshape(rows * cols, 4 * quarter).astype(np.float32)


def _filter_features(n, width):
    pos = np.arange(n, dtype=np.float64)
    t = pos / max(n - 1, 1)
    ang = (2.0 * math.pi * pos / n)[:, None] * np.linspace(1e-4, FILTER_BANDS - 1, FILTER_BANDS)[None, :]
    z = np.concatenate([t[:, None], np.cos(ang), -np.sin(ang)], axis=-1)
    return np.pad(z, ((0, 0), (0, width - z.shape[1]))).astype(np.float32)


def _decay_rates(d_h):
    min_decay = math.log(DECAY_TARGET) / DECAY_SLOW_PCT
    max_decay = math.log(DECAY_TARGET) / DECAY_FAST_PCT
    return np.abs(np.linspace(min_decay, max_decay, d_h)).astype(np.float32).reshape(1, d_h)


def _dft_tables(blk):
    odd = 2 * np.arange(blk, dtype=np.int64) + 1
    shifted = ((odd[:, None] * odd[None, :]) % (8 * blk)).astype(np.float64) * (2.0 * math.pi / (8 * blk))
    plain = ((odd[:, None] * np.arange(blk, dtype=np.int64)[None, :]) % (4 * blk)).astype(np.float64) * (2.0 * math.pi / (4 * blk))
    sgn = (1.0 - 2.0 * (np.arange(blk) % 2)).astype(np.float32).reshape(blk, 1)
    f32 = lambda a: a.astype(np.float32)
    return f32(np.cos(shifted)), f32(np.sin(shifted)), f32(np.cos(plain)), f32(np.sin(plain)), sgn


def kernel(x, c, ctx, c_ctx, w_ada, b_ada, norm1_g, w_in, conv_a_w, conv_a_b, lru_wa, lru_ba, lru_wx, lru_bx, lru_lambda, conv_b_w, filt_w1, filt_b1, filt_freq1, filt_w2, filt_b2, filt_freq2, filt_w3, filt_b3, filt_bias, out_norm_a, out_norm_b, w_out, norm2_g, w_rg, b_rg, w_re, b_re, w_gate, w_up, w_down, final_g):
    bsz, n, d = x.shape
    n_ctx = ctx.shape[1]
    d_rnn = conv_a_w.shape[2]
    d_h = filt_bias.shape[2]
    order = filt_bias.shape[1]
    assert w_ada.shape[0] == 1, "single-layer block"
    l = 0
    ch = 256

    mod_rows = 16
    cc = jnp.concatenate([c, c_ctx[None, :], jnp.zeros((mod_rows - bsz - 1, d), F32)], axis=0)
    mod3 = _ada(cc, w_ada[l], b_ada[l]).reshape(mod_rows, N_MOD, d)

    pe = jnp.asarray(_sincos_table(n // GRID_W, GRID_W, d))
    p3 = _inproj_lat(x, pe, mod3, norm1_g[l], w_in[l], 1024)
    pc3 = _inproj_ctx(ctx.reshape(bsz * n_ctx, d), mod3, bsz, norm1_g[l], w_in[l], 1, d_rnn,
                      1024).reshape(bsz, n_ctx, d_rnn)

    heads_per_blk = ch // (d_rnn // RNN_HEADS)
    nblk = d_rnn // ch
    eye = jnp.eye(heads_per_blk, dtype=F32)

    def blockdiag(w):
        w5 = w.reshape(2, nblk, heads_per_blk, w.shape[2], w.shape[3])
        return jnp.einsum("dnkij,kl->dnkilj", w5, eye).reshape(2, nblk, ch, ch)

    wa_bd, wx_bd = blockdiag(lru_wa[l]), blockdiag(lru_wx[l])
    w_gates = jnp.concatenate([wa_bd[0], wx_bd[0], wa_bd[1], wx_bd[1]], axis=-1).astype(BF16)
    ba, bx = lru_ba[l].reshape(2, nblk, ch), lru_bx[l].reshape(2, nblk, ch)
    bias = jnp.concatenate([ba[0], bx[0], ba[1], bx[1]], axis=-1).reshape(nblk, 1, 4 * ch)
    sp = jax.nn.softplus(-lru_lambda[l])
    cb = conv_a_b[l].reshape(1, d_rnn)
    zeros_state = jnp.zeros((bsz, 1, d_rnn), F32)
    hf_ctx, hb_ctx = _rglru(pc3, 0, 0, conv_a_w[l], cb, w_gates, bias, sp, zeros_state, zeros_state, False, ch)
    ya, _, _ = _rglru(p3, d_rnn // ch, 0, conv_a_w[l], cb, w_gates, bias, sp, hf_ctx, hb_ctx, True, ch)

    as_bf16 = lambda a: jnp.asarray(a).astype(BF16)
    c3, s3, cu, su, sgn = _dft_tables(CONV_BLOCK)
    fe = 64
    z = jnp.asarray(_filter_features(n, fe))
    w1 = jnp.pad(filt_w1[l], ((0, fe - filt_w1.shape[1]), (0, 0)))
    hid = filt_w2.shape[1]
    hr, hi = _filters(z, w1, filt_b1[l].reshape(1, hid), filt_freq1[l].reshape(1, hid), filt_w2[l],
                      filt_b2[l].reshape(1, hid), filt_freq2[l].reshape(1, hid), filt_w3[l],
                      filt_b3[l].reshape(1, -1), jnp.asarray(_decay_rates(d_h)), as_bf16(cu), as_bf16(su),
                      jnp.asarray(sgn), d_h, order, ch)
    yb = _hyena(p3, 2 * d_rnn // ch, conv_b_w[l], filt_bias[l], as_bf16(c3), as_bf16(s3), hr, hi, d_h, ch)

    w_r = jnp.concatenate([w_re[l], w_rg[l], jnp.zeros((d, LANES - N_EXPERTS - N_GROUPS), F32)], axis=1)
    b_r = jnp.concatenate([b_re[l], b_rg[l], jnp.zeros((LANES - N_EXPERTS - N_GROUPS,), F32)]).reshape(1, LANES)
    tb = 512
    tm = 512
    x1, h2, rt, cnt, prow = _merge(ya, yb, x, pe, mod3, out_norm_a[l], out_norm_b[l], w_out[l], norm2_g[l],
                                   w_r, b_r, tm, tm // tb)

    tile = 512
    t_all = bsz * n
    nb = t_all // tb
    lp = 2 * tb + N_EXPERTS * SEG
    n_tiles = (2 * t_all + nb * N_EXPERTS * (SEG - 1) + N_EXPERTS * (tile - 1)) // tile
    cnt_blocks = cnt[:, 0, :N_EXPERTS].astype(jnp.int32)
    tab, ends, ltot, tile_expert, n_used, nxt, par, tile_rows = _moe_layout(cnt_blocks, tile, n_tiles)
    xs = _dispatch(tab, ends, ltot, prow, h2.reshape(t_all, d), n_tiles * tile, tb, lp, tile)
    ys = _ffn(tile_expert, n_used, nxt, par, tile_rows, xs, w_gate[l], w_up[l], w_down[l], tile)
    out = _combine(tab, ltot, ys, rt.reshape(t_all, LANES), x1.reshape(t_all, d), mod3, final_g, tb, lp, n // tb)
    return out.reshape(bsz, n, d)
```

```python
import functools
import math

import numpy as np
import jax
import jax.numpy as jnp
from jax import lax
from jax.experimental import pallas as pl
from jax.experimental.pallas import tpu as pltpu

F32 = jnp.float32
BF16 = jnp.bfloat16
EPS = 1e-6
LRU_C = 8.0
N_MOD = 6
GRID_W = 64
RNN_HEADS = 8
N_GROUPS = 4
EXPERTS_PER_GROUP = 8
N_EXPERTS = N_GROUPS * EXPERTS_PER_GROUP
FILTER_BANDS = 16
DECAY_FAST_PCT = 0.3
DECAY_SLOW_PCT = 1.5
DECAY_TARGET = 1e-2
SUBLANES = 8
LANES = 128
SEG = 2 * SUBLANES
SLOT_SPLIT = 64.0
VMEM_LIMIT = 60 * 1024 * 1024


def _cparams(*sem):
    return pltpu.CompilerParams(dimension_semantics=sem, vmem_limit_bytes=VMEM_LIMIT)


def _dot(a, b):
    return jnp.dot(a, b, preferred_element_type=F32)


def _split(a):
    hi = a.astype(BF16)
    return hi, (a - hi.astype(F32)).astype(BF16)


def _dot3(a, b):
    a_hi, a_lo = _split(a)
    b_hi, b_lo = _split(b)
    return _dot(a_hi, b_hi) + _dot(a_lo, b_hi) + _dot(a_hi, b_lo)


def _rms(v, g):
    return v * lax.rsqrt(jnp.mean(v * v, axis=-1, keepdims=True) + EPS) * g


def _sigmoid(z):
    return 1.0 / (1.0 + jnp.exp(-z))


def _shift_rows(v, d):
    n = v.shape[0]
    rolled = pltpu.roll(v, (-d) % n, 0)
    row = lax.broadcasted_iota(jnp.int32, v.shape, 0)
    ok = (row + d >= 0) & (row + d < n)
    return jnp.where(ok, rolled, 0.0)


def _tile_rot(v, s):
    n, c = v.shape
    return pltpu.roll(v.reshape(n // SUBLANES, SUBLANES, c), s, 1).reshape(n, c)


def _ada_body(c_ref, w_ref, b_ref, o_ref):
    c = c_ref[...]
    o_ref[...] = _dot3(c * _sigmoid(c), w_ref[...]) + b_ref[...]


def _ada(cc, w_ada, b_ada):
    rows, d = cc.shape
    n = w_ada.shape[1]
    tn = 1024
    return pl.pallas_call(
        _ada_body,
        grid=(n // tn,),
        in_specs=[pl.BlockSpec((rows, d), lambda j: (0, 0)),
                  pl.BlockSpec((d, tn), lambda j: (0, j)),
                  pl.BlockSpec((1, tn), lambda j: (0, j))],
        out_specs=pl.BlockSpec((rows, tn), lambda j: (0, j)),
        out_shape=jax.ShapeDtypeStruct((rows, n), F32),
        compiler_params=_cparams("arbitrary"),
        name="ada",
    )(cc, w_ada, b_ada.reshape(1, n))


def _inproj_lat_body(x_ref, pe_ref, mod_ref, g_ref, w_ref, o_ref, wb_ref):
    @pl.when((pl.program_id(0) == 0) & (pl.program_id(1) == 0))
    def _():
        wb_ref[...] = w_ref[...].astype(BF16)

    x = x_ref[...] + pe_ref[...]
    h = _rms(x, g_ref[...]) * (1.0 + mod_ref[1:2, :]) + mod_ref[0:1, :]
    o_ref[...] = _dot(h.astype(BF16), wb_ref[...]).astype(o_ref.dtype)


def _inproj_lat(x3, pe, mod3, g, w, tm):
    b, l, d = x3.shape
    n_out = w.shape[1]
    return pl.pallas_call(
        _inproj_lat_body,
        grid=(l // tm, b),
        in_specs=[pl.BlockSpec((None, tm, d), lambda i, bi: (bi, i, 0)),
                  pl.BlockSpec((tm, d), lambda i, bi: (i, 0)),
                  pl.BlockSpec((None, N_MOD, d), lambda i, bi: (bi, 0, 0)),
                  pl.BlockSpec((1, d), lambda i, bi: (0, 0)),
                  pl.BlockSpec((d, n_out), lambda i, bi: (0, 0), pipeline_mode=pl.Buffered(1))],
        out_specs=pl.BlockSpec((None, tm, n_out), lambda i, bi: (bi, i, 0)),
        out_shape=jax.ShapeDtypeStruct((b, l, n_out), BF16),
        scratch_shapes=[pltpu.VMEM((d, n_out), BF16)],
        compiler_params=_cparams("arbitrary", "arbitrary"),
        name="inproj_pe",
    )(x3, pe, mod3, g.reshape(1, d), w)


def _inproj_ctx_body(x_ref, mod_ref, g_ref, w_ref, o_ref):
    h = _rms(x_ref[...], g_ref[...]) * (1.0 + mod_ref[1:2, :]) + mod_ref[0:1, :]
    o_ref[...] = _dot(h.astype(BF16), w_ref[...].astype(BF16))


def _inproj_ctx(x2, mod3, mod_row, g, w, col_block, n_out, tm):
    r, d = x2.shape
    return pl.pallas_call(
        _inproj_ctx_body,
        grid=(r // tm,),
        in_specs=[pl.BlockSpec((tm, d), lambda i: (i, 0)),
                  pl.BlockSpec((None, N_MOD, d), lambda i: (mod_row, 0, 0)),
                  pl.BlockSpec((1, d), lambda i: (0, 0)),
                  pl.BlockSpec((d, n_out), lambda i: (0, col_block))],
        out_specs=pl.BlockSpec((tm, n_out), lambda i: (i, 0)),
        out_shape=jax.ShapeDtypeStruct((r, n_out), F32),
        compiler_params=_cparams("arbitrary"),
        name="inproj_ctx",
    )(x2, mod3, g.reshape(1, d), w)


def _rglru_body(*refs, with_gate):
    if with_gate:
        (u_ref, gate_ref, cw_ref, cb_ref, w_ref, bias_ref, sp_ref, h0f_ref, h0b_ref,
         y_ref, hf_end_ref, hb_end_ref, af_ref, bf_ref, ab_ref, bb_ref) = refs
    else:
        (u_ref, cw_ref, cb_ref, w_ref, bias_ref, sp_ref, h0f_ref, h0b_ref,
         hf_end_ref, hb_end_ref, af_ref, bf_ref, ab_ref, bb_ref) = refs
    n, c = u_ref.shape
    u = u_ref[...].astype(F32)
    v = cb_ref[...] + cw_ref[1:2, :] * _shift_rows(u, -1) + cw_ref[0:1, :] * _shift_rows(u, -2)
    v = v + cw_ref[2:3, :] * u + cw_ref[3:4, :] * _shift_rows(u, 1)
    z = _dot(v.astype(BF16), w_ref[...]) + bias_ref[...]
    sub = lax.broadcasted_iota(jnp.int32, (n, c), 0) % SUBLANES

    def local_scan(k, reverse, a_ref, b_ref):
        r = _sigmoid(z[:, (2 * k) * c:(2 * k + 1) * c])
        i = _sigmoid(z[:, (2 * k + 1) * c:(2 * k + 2) * c])
        log_a = (-LRU_C) * r * sp_ref[k:k + 1, :]
        a = jnp.exp(log_a)
        b = jnp.sqrt(1.0 - a * a) * (i * v)
        for s in (1, 2, 4):
            ok = (sub < SUBLANES - s) if reverse else (sub >= s)
            shift = SUBLANES - s if reverse else s
            b = a * jnp.where(ok, _tile_rot(b, shift), 0.0) + b
            a = a * jnp.where(ok, _tile_rot(a, shift), 1.0)
        a_ref[...] = a
        b_ref[...] = b

    local_scan(0, False, af_ref, bf_ref)
    local_scan(1, True, ab_ref, bb_ref)

    n_tiles = n // SUBLANES

    def carry_step(q, carry):
        cf, cb = carry
        rf = pl.ds(pl.multiple_of(q * SUBLANES, SUBLANES), SUBLANES)
        rb = pl.ds(pl.multiple_of((n_tiles - 1 - q) * SUBLANES, SUBLANES), SUBLANES)
        af, bf, ab, bb = af_ref[rf, :], bf_ref[rf, :], ab_ref[rb, :], bb_ref[rb, :]
        bf_ref[rf, :] = af * cf + bf
        bb_ref[rb, :] = ab * cb + bb
        last = lambda v: jnp.broadcast_to(v[SUBLANES - 1:SUBLANES, :], (SUBLANES, c))
        first = lambda v: jnp.broadcast_to(v[0:1, :], (SUBLANES, c))
        return last(af) * cf + last(bf), first(ab) * cb + first(bb)

    cf0 = jnp.broadcast_to(h0f_ref[...], (SUBLANES, c))
    cb0 = jnp.broadcast_to(h0b_ref[...], (SUBLANES, c))
    cf, cb = lax.fori_loop(0, n_tiles, carry_step, (cf0, cb0), unroll=4)
    hf_end_ref[...] = cf[0:1, :]
    hb_end_ref[...] = cb[0:1, :]
    if with_gate:
        y_ref[...] = jax.nn.gelu(gate_ref[...].astype(F32), approximate=True) * (bf_ref[...] + bb_ref[...])


def _rglru(p3, u_blk0, gate_blk0, cw, cb, w_gates, bias, sp, h0f, h0b, with_gate, ch):
    b, l, _ = p3.shape
    d_rnn = cw.shape[1]
    nh = d_rnn // ch
    in_specs = [pl.BlockSpec((None, l, ch), lambda bi, h: (bi, 0, u_blk0 + h))]
    args = [p3]
    if with_gate:
        in_specs.append(pl.BlockSpec((None, l, ch), lambda bi, h: (bi, 0, gate_blk0 + h)))
        args.append(p3)
    in_specs += [pl.BlockSpec((4, ch), lambda bi, h: (0, h)),
                 pl.BlockSpec((1, ch), lambda bi, h: (0, h)),
                 pl.BlockSpec((None, ch, 4 * ch), lambda bi, h: (h, 0, 0)),
                 pl.BlockSpec((None, 1, 4 * ch), lambda bi, h: (h, 0, 0)),
                 pl.BlockSpec((2, ch), lambda bi, h: (0, h)),
                 pl.BlockSpec((None, 1, ch), lambda bi, h: (bi, 0, h)),
                 pl.BlockSpec((None, 1, ch), lambda bi, h: (bi, 0, h))]
    args += [cw, cb, w_gates, bias, sp, h0f, h0b]
    end_spec = pl.BlockSpec((None, 1, ch), lambda bi, h: (bi, 0, h))
    end_shape = jax.ShapeDtypeStruct((b, 1, d_rnn), F32)
    out_specs = [end_spec, end_spec]
    out_shape = [end_shape, end_shape]
    if with_gate:
        out_specs = [pl.BlockSpec((None, l, ch), lambda bi, h: (bi, 0, h))] + out_specs
        out_shape = [jax.ShapeDtypeStruct((b, l, d_rnn), F32)] + out_shape
    return pl.pallas_call(
        functools.partial(_rglru_body, with_gate=with_gate),
        grid=(b, nh),
        in_specs=in_specs,
        out_specs=out_specs,
        out_shape=out_shape,
        scratch_shapes=[pltpu.VMEM((l, ch), F32)] * 4,
        compiler_params=_cparams("arbitrary", "arbitrary"),
        name="rglru_lat" if with_gate else "rglru_ctx",
    )(*args)


CONV_BLOCK = 512


def _filt_body(z_ref, w1_ref, b1_ref, f1_ref, w2_ref, b2_ref, f2_ref, w3f_ref, w3b_ref, b3f_ref, b3b_ref,
               dec_ref, cu_ref, su_ref, sgn_ref, hr_ref, hi_ref, hdn_ref):
    n = z_ref.shape[0]
    blk = cu_ref.shape[0]
    nblk = n // blk
    ch = hr_ref.shape[2]

    @pl.when(pl.program_id(0) == 0)
    def _():
        hid = jnp.sin(f1_ref[...] * (_dot3(z_ref[...], w1_ref[...]) + b1_ref[...]))
        hdn_ref[...] = jnp.sin(f2_ref[...] * (_dot3(hid, w2_ref[...]) + b2_ref[...]))

    hdn = hdn_ref[...]
    decay = jnp.exp(-z_ref[:, 0:1] * dec_ref[...])
    kf = (_dot3(hdn, w3f_ref[...]) + b3f_ref[...]) * decay
    kb = (_dot3(hdn, w3b_ref[...]) + b3b_ref[...]) * decay
    row = lax.broadcasted_iota(jnp.int32, (n, ch), 0)
    kb = jnp.where(row == 0, 0.0, kb)
    norm = jnp.sum(jnp.abs(kf) + jnp.abs(kb), axis=0, keepdims=True)
    scale = (2.0 / (2 * blk)) / norm
    sgn = sgn_ref[...]

    p, q, first = [], [], []
    for c in range(nblk):
        taps = jnp.concatenate([kf[c * blk:(c + 1) * blk], kb[c * blk:(c + 1) * blk]], axis=1).astype(BF16)
        p.append(_dot(cu_ref[...], taps))
        q.append(_dot(su_ref[...], taps))
        first.append(taps[0:1, :].astype(F32))
    fwd = lambda a: a[:, :ch]
    bwd = lambda a: a[:, ch:]
    out = lambda a: (a * scale).astype(hr_ref.dtype)
    hr_ref[nblk - 1] = out(fwd(p[0]) + bwd(p[0]))
    hi_ref[nblk - 1] = out(bwd(q[0]) - fwd(q[0]))
    for c in range(1, nblk):
        hr_ref[nblk - 1 + c] = out(fwd(p[c]) + sgn * fwd(q[c - 1]))
        hi_ref[nblk - 1 + c] = out(sgn * (fwd(p[c - 1]) - fwd(first[c - 1])) - fwd(q[c]))
        hr_ref[nblk - 1 - c] = out(bwd(p[c]) + sgn * bwd(q[c - 1]))
        hi_ref[nblk - 1 - c] = out(bwd(q[c]) - sgn * (bwd(p[c - 1]) - bwd(first[c - 1])))


def _filters(z, w1, b1, f1, w2, b2, f2, w3, b3, deltas, cu, su, sgn, d_h, order, ch):
    n, fe = z.shape
    hid = w2.shape[0]
    blk = cu.shape[0]
    nh = 2 * (n // blk) - 1
    per_order = d_h // ch
    ncol = order * per_order
    const = lambda shape: pl.BlockSpec(shape, lambda g: (0, 0))
    hspec = pl.BlockSpec((nh, blk, ch), lambda g: (0, 0, g))
    return pl.pallas_call(
        _filt_body,
        grid=(ncol,),
        in_specs=[const((n, fe)), const((fe, hid)), const((1, hid)), const((1, hid)),
                  const((hid, hid)), const((1, hid)), const((1, hid)),
                  pl.BlockSpec((hid, ch), lambda g: (0, g)),
                  pl.BlockSpec((hid, ch), lambda g: (0, ncol + g)),
                  pl.BlockSpec((1, ch), lambda g: (0, g)),
                  pl.BlockSpec((1, ch), lambda g: (0, ncol + g)),
                  pl.BlockSpec((1, ch), lambda g: (0, g % per_order)),
                  const((blk, blk)), const((blk, blk)), const((blk, 1))],
        out_specs=[hspec, hspec],
        out_shape=[jax.ShapeDtypeStruct((nh, blk, order * d_h), BF16)] * 2,
        scratch_shapes=[pltpu.VMEM((n, hid), F32)],
        compiler_params=_cparams("arbitrary"),
        name="hyena_filters",
    )(z, w1, b1, f1, w2, b2, f2, w3, w3, b3, b3, deltas, cu, su, sgn)


def _hyena_body(v_ref, x1_ref, x2_ref, wv_ref, w1_ref, w2_ref, fb_ref, c3_ref, s3_ref,
                h0r_ref, h0i_ref, h1r_ref, h1i_ref, o_ref, z_ref, zb_ref, p_ref, q_ref, yr_ref, yi_ref):
    n = o_ref.shape[0]
    blk = c3_ref.shape[0]
    nblk = n // blk
    halo = 2 * SUBLANES
    sub = 32

    def conv3(src_ref, w_ref, r0):
        lo, hi = max(r0 - halo, 0), min(r0 + blk + halo, n)
        win = src_ref[lo:hi, :].astype(F32)
        size = hi - lo
        row = lax.broadcasted_iota(jnp.int32, win.shape, 0)
        prev = pltpu.roll(win, 1, 0)
        nxt = pltpu.roll(win, size - 1, 0)
        if lo == 0:
            prev = jnp.where(row == 0, 0.0, prev)
        if hi == n:
            nxt = jnp.where(row == size - 1, 0.0, nxt)
        out = w_ref[0:1, :] * prev + w_ref[1:2, :] * win + w_ref[2:3, :] * nxt
        return out[r0 - lo:r0 - lo + blk, :]

    def long_conv(hr_ref, hi_ref, fb, gate_ref, gate_w_ref, dst_ref):
        for i in range(nblk):
            rows = slice(i * blk, (i + 1) * blk)
            p_ref[rows, :] = _dot(c3_ref[...], zb_ref[rows, :]).astype(BF16)
            q_ref[rows, :] = _dot(s3_ref[...], zb_ref[rows, :]).astype(BF16)

        for j in range(nblk):
            for r0 in range(0, blk, sub):
                yr = yi = None
                for i in range(nblk):
                    rows = slice(i * blk + r0, i * blk + r0 + sub)
                    p, q = p_ref[rows, :], q_ref[rows, :]
                    gr = hr_ref[nblk - 1 + j - i, r0:r0 + sub, :]
                    gi = hi_ref[nblk - 1 + j - i, r0:r0 + sub, :]
                    tr, ti = p * gr + q * gi, q * gr - p * gi
                    yr, yi = (tr, ti) if yr is None else (yr + tr, yi + ti)
                yr_ref[j * blk + r0:j * blk + r0 + sub, :] = yr
                yi_ref[j * blk + r0:j * blk + r0 + sub, :] = yi
            rows = slice(j * blk, (j + 1) * blk)
            y = _dot(c3_ref[...], yr_ref[rows, :]) + _dot(s3_ref[...], yi_ref[rows, :])
            dst_ref[rows, :] = conv3(gate_ref, gate_w_ref, j * blk) * (y + z_ref[rows, :] * fb)

    for r0 in range(0, n, blk):
        z_ref[r0:r0 + blk, :] = conv3(v_ref, wv_ref, r0)
    zb_ref[...] = z_ref[...].astype(BF16)
    long_conv(h0r_ref, h0i_ref, fb_ref[0:1, :], x1_ref, w1_ref, z_ref)
    zb_ref[...] = z_ref[...].astype(BF16)
    long_conv(h1r_ref, h1i_ref, fb_ref[1:2, :], x2_ref, w2_ref, o_ref)


def _hyena(p3, col_blk0, conv_w, fbias, c3, s3, hr, hi, d_h, ch):
    b, n, _ = p3.shape
    nh, blk, _ = hr.shape
    per = d_h // ch
    zspec = lambda k: pl.BlockSpec((None, n, ch), lambda h, bi: (bi, 0, col_blk0 + k * per + h))
    wspec = lambda k: pl.BlockSpec((3, ch), lambda h, bi: (0, k * per + h))
    hspec = lambda o: pl.BlockSpec((nh, blk, ch), lambda h, bi: (0, 0, o * per + h), pipeline_mode=pl.Buffered(1))
    const = pl.BlockSpec((blk, blk), lambda h, bi: (0, 0))
    return pl.pallas_call(
        _hyena_body,
        grid=(per, b),
        in_specs=[zspec(0), zspec(1), zspec(2), wspec(0), wspec(1), wspec(2),
                  pl.BlockSpec((2, ch), lambda h, bi: (0, h)), const, const,
                  hspec(0), hspec(0), hspec(1), hspec(1)],
        out_specs=pl.BlockSpec((None, n, ch), lambda h, bi: (bi, 0, h)),
        out_shape=jax.ShapeDtypeStruct((b, n, d_h), F32),
        scratch_shapes=[pltpu.VMEM((n, ch), F32), pltpu.VMEM((n, ch), BF16),
                        pltpu.VMEM((n, ch), BF16), pltpu.VMEM((n, ch), BF16),
                        pltpu.VMEM((n, ch), BF16), pltpu.VMEM((n, ch), BF16)],
        compiler_params=_cparams("arbitrary", "arbitrary"),
        name="hyena_mix",
    )(p3, p3, p3, conv_w, conv_w, conv_w, fbias, c3, s3, hr, hi, hr, hi)


def _merge_body(ya_ref, yb_ref, x_ref, pe_ref, mod_ref, ga_ref, gb_ref, wo_ref, g2_ref, wr_ref, br_ref,
                x1_ref, h2_ref, rt_ref, cnt_ref, prow_ref, wob_ref, tri_ref):
    @pl.when((pl.program_id(0) == 0) & (pl.program_id(1) == 0))
    def _():
        wob_ref[...] = wo_ref[...].astype(BF16)
        earlier = lax.broadcasted_iota(jnp.int32, tri_ref.shape, 1) < lax.broadcasted_iota(jnp.int32, tri_ref.shape, 0)
        tri_ref[...] = jnp.where(earlier, 1.0, 0.0).astype(BF16)

    da = ya_ref.shape[1]
    na = _rms(ya_ref[...], ga_ref[...]).astype(BF16)
    nb = _rms(yb_ref[...], gb_ref[...]).astype(BF16)
    y = _dot(na, wob_ref[0:da, :]) + _dot(nb, wob_ref[da:, :])
    _route(y, x_ref, pe_ref, mod_ref, g2_ref, wr_ref, br_ref, x1_ref, h2_ref, rt_ref, cnt_ref, prow_ref, tri_ref)


def _route(y, x_ref, pe_ref, mod_ref, g2_ref, wr_ref, br_ref, x1_ref, h2_ref, rt_ref, cnt_ref, prow_ref, tri_ref):
    x1 = x_ref[...] + pe_ref[...] + mod_ref[2:3, :] * y
    x1_ref[...] = x1
    h2 = _rms(x1, g2_ref[...]) * (1.0 + mod_ref[4:5, :]) + mod_ref[3:4, :]
    h2_ref[...] = h2.astype(BF16)

    logits = _dot3(h2, wr_ref[...]) + br_ref[...]
    lane = lax.broadcasted_iota(jnp.int32, logits.shape, 1).astype(F32)
    neg = -jnp.inf
    big = jnp.float32(1 << 20)
    gl = jnp.where((lane >= N_EXPERTS) & (lane < N_EXPERTS + N_GROUPS), logits, neg)
    gmax = jnp.max(gl, axis=1, keepdims=True)
    g_p = 1.0 / jnp.sum(jnp.exp(gl - gmax), axis=1, keepdims=True)
    g_i = jnp.min(jnp.where(gl == gmax, lane, big), axis=1, keepdims=True) - N_EXPERTS
    lo = g_i * EXPERTS_PER_GROUP
    el = jnp.where((lane >= lo) & (lane < lo + EXPERTS_PER_GROUP), logits, neg)
    m1 = jnp.max(el, axis=1, keepdims=True)
    i1 = jnp.min(jnp.where(el == m1, lane, big), axis=1, keepdims=True)
    el2 = jnp.where(lane == i1, neg, el)
    m2 = jnp.max(el2, axis=1, keepdims=True)
    i2 = jnp.min(jnp.where(el2 == m2, lane, big), axis=1, keepdims=True)
    e2 = jnp.exp(m2 - m1)
    w1 = g_p / (1.0 + e2)
    w2 = g_p * e2 / (1.0 + e2)
    o1 = jnp.where(lane == i1, 1.0, 0.0)
    o2 = jnp.where(lane == i2, 1.0, 0.0)
    nsub = cnt_ref.shape[0]
    sub = o1.shape[0] // nsub
    sq = lambda shape, d: lax.broadcasted_iota(jnp.int32, shape, d)
    earlier = tri_ref[...]
    below = jnp.where(sq((LANES, LANES), 0) < sq((LANES, LANES), 1), 1.0, 0.0).astype(BF16)
    pick_row = jnp.where(sq((SUBLANES, LANES), 0) == sq((SUBLANES, LANES), 1), 1.0, 0.0).astype(BF16)
    row_id = sq((SUBLANES, sub), 0)
    pos1, pos2 = [], []
    for k in range(nsub):
        a1, a2 = o1[k * sub:(k + 1) * sub], o2[k * sub:(k + 1) * sub]
        both = a1 + a2
        count = jnp.sum(both, axis=0, keepdims=True)
        cnt_ref[k] = jnp.broadcast_to(count, cnt_ref.shape[1:])
        runs = jnp.broadcast_to(jnp.floor((count + (SEG - 1)) / SEG), (SUBLANES, LANES))
        start = _dot(runs.astype(BF16), below)[0:1, :] * SEG
        before = _dot(earlier, both.astype(BF16)) + start
        p1 = jnp.sum(a1 * before, axis=1, keepdims=True)
        p2 = jnp.sum(a2 * before, axis=1, keepdims=True)
        pos1.append(p1)
        pos2.append(p2)
        lane_k = sq((sub, LANES), 1).astype(F32)
        hi1, hi2 = jnp.floor(p1 / SLOT_SPLIT), jnp.floor(p2 / SLOT_SPLIT)
        parts = (jnp.where(lane_k == 0.0, hi1, 0.0) + jnp.where(lane_k == 1.0, p1 - SLOT_SPLIT * hi1, 0.0)
                 + jnp.where(lane_k == 2.0, hi2, 0.0) + jnp.where(lane_k == 3.0, p2 - SLOT_SPLIT * hi2, 0.0))
        rows = lax.dot_general(pick_row, parts.astype(BF16), (((1,), (1,)), ((), ())), preferred_element_type=F32)
        q1 = rows[0:1, :] * SLOT_SPLIT + rows[1:2, :]
        q2 = rows[2:3, :] * SLOT_SPLIT + rows[3:4, :]
        prow_ref[k] = jnp.where(row_id == 0, q1, jnp.where(row_id == 1, q2, 0.0))
    p1 = jnp.concatenate(pos1, axis=0)
    p2 = jnp.concatenate(pos2, axis=0)
    rt_ref[...] = (jnp.where(lane == 0.0, i1, 0.0) + jnp.where(lane == 1.0, i2, 0.0)
                   + jnp.where(lane == 2.0, w1, 0.0) + jnp.where(lane == 3.0, w2, 0.0)
                   + jnp.where(lane == 4.0, p1, 0.0) + jnp.where(lane == 5.0, p2, 0.0))


def _merge(ya, yb, x3, pe, mod3, ga, gb, w_out, g2n, w_r, b_r, tm, nsub):
    b, n, d = x3.shape
    da, db = ya.shape[2], yb.shape[2]
    per = n // tm
    tok = lambda w: pl.BlockSpec((None, tm, w), lambda i, bi: (bi, i, 0))
    const = lambda shape: pl.BlockSpec(shape, lambda i, bi: (0, 0))
    blk = lambda w: pl.BlockSpec((nsub, SUBLANES, w), lambda i, bi: (bi * per + i, 0, 0))
    return pl.pallas_call(
        _merge_body,
        grid=(per, b),
        in_specs=[tok(da), tok(db), tok(d), pl.BlockSpec((tm, d), lambda i, bi: (i, 0)),
                  pl.BlockSpec((None, N_MOD, d), lambda i, bi: (bi, 0, 0)),
                  const((1, da)), const((1, db)), const((da + db, d)), const((1, d)),
                  const((d, LANES)), const((1, LANES))],
        out_specs=[tok(d), tok(d), tok(LANES), blk(LANES), blk(tm // nsub)],
        out_shape=[jax.ShapeDtypeStruct((b, n, d), F32), jax.ShapeDtypeStruct((b, n, d), BF16),
                   jax.ShapeDtypeStruct((b, n, LANES), F32),
                   jax.ShapeDtypeStruct((b * per * nsub, SUBLANES, LANES), F32),
                   jax.ShapeDtypeStruct((b * per * nsub, SUBLANES, tm // nsub), F32)],
        scratch_shapes=[pltpu.VMEM((da + db, d), BF16), pltpu.VMEM((tm // nsub, tm // nsub), BF16)],
        compiler_params=_cparams("arbitrary", "arbitrary"),
        name="merge_route",
    )(ya, yb, x3, pe, mod3, ga.reshape(1, da), gb.reshape(1, db), w_out, g2n.reshape(1, d), w_r, b_r)


def _segment_copy(local_ref, lrow, global_ref, grow, rows, sem, to_global):
    lrows = pl.ds(pl.multiple_of(lrow, SEG), rows)
    grows = pl.ds(pl.multiple_of(grow, SEG), rows)
    if to_global:
        return pltpu.make_async_copy(local_ref.at[lrows, :], global_ref.at[grows, :], sem)
    return pltpu.make_async_copy(global_ref.at[grows, :], local_ref.at[lrows, :], sem)


def _start_segments(tab_ref, blk, local_ref, global_ref, sem, to_global):
    def per_expert(e, carry):
        k = (blk * N_EXPERTS + e) * 3
        ls, gs, nchunk = tab_ref[k], tab_ref[k + 1], tab_ref[k + 2]

        def per_chunk(ci, carry2):
            _segment_copy(local_ref, ls + ci * SEG, global_ref, gs + ci * SEG, SEG, sem, to_global).start()
            return carry2

        return lax.fori_loop(0, nchunk, per_chunk, carry)

    lax.fori_loop(0, N_EXPERTS, per_expert, 0, unroll=2)


def _wait_segments(rows, local_ref, global_ref, sem, to_global):
    chunks = rows // SEG
    nbits = (local_ref.shape[0] // SEG).bit_length()
    for k in range(nbits):
        @pl.when(((chunks >> k) & 1) == 1)
        def _():
            _segment_copy(local_ref, 0, global_ref, 0, SEG << k, sem, to_global).wait()


def _dispatch_body(tab_ref, ends_ref, ltot_ref, prow_ref, h2_ref, xs_ref, loc_ref, zero_ref, sem, zsem):
    blk = pl.program_id(0)
    nblk = pl.num_programs(0)
    tb = h2_ref.shape[0]
    lp = loc_ref.shape[1]
    tile = zero_ref.shape[0]
    slot = blk % 2

    @pl.when(blk == 0)
    def _():
        zero_ref[...] = jnp.zeros_like(zero_ref)

        def zero_fill(first_row, rows):
            dst = xs_ref.at[pl.ds(pl.multiple_of(first_row, SEG), rows), :]
            return pltpu.make_async_copy(zero_ref.at[pl.ds(0, rows), :], dst, zsem)

        pad = 64

        def fill(e, carry, wait):
            end, real = ends_ref[e], ends_ref[N_EXPERTS + e]
            first = real // pad * pad

            def one(j, carry2):
                cp = zero_fill(first + j * pad, pad)
                cp.wait() if wait else cp.start()
                return carry2

            return lax.fori_loop(0, (end - first) // pad, one, carry)

        lax.fori_loop(0, N_EXPERTS, functools.partial(fill, wait=False), 0)
        used = ends_ref[N_EXPERTS - 1]
        spare = (xs_ref.shape[0] - used) // tile

        def fill_spare(j, carry, wait):
            cp = zero_fill(used + j * tile, tile)
            cp.wait() if wait else cp.start()
            return carry

        lax.fori_loop(0, spare, functools.partial(fill_spare, wait=False), 0)
        lax.fori_loop(0, N_EXPERTS, functools.partial(fill, wait=True), 0)
        lax.fori_loop(0, spare, functools.partial(fill_spare, wait=True), 0)

    pos = lax.broadcasted_iota(jnp.int32, (lp, tb), 0).astype(F32)
    onehot = jnp.where((pos == prow_ref[0:1, :]) | (pos == prow_ref[1:2, :]), 1.0, 0.0).astype(BF16)
    loc_ref[slot] = _dot(onehot, h2_ref[...]).astype(BF16)

    _start_segments(tab_ref, blk, loc_ref.at[slot], xs_ref, sem.at[slot], True)

    @pl.when(blk > 0)
    def _():
        _wait_segments(ltot_ref[blk - 1], loc_ref.at[1 - slot], xs_ref, sem.at[1 - slot], True)

    @pl.when(blk == nblk - 1)
    def _():
        _wait_segments(ltot_ref[blk], loc_ref.at[slot], xs_ref, sem.at[slot], True)


def _dispatch(tab, ends, ltot, prow, h2, slots, tb, lp, tile):
    t, d = h2.shape
    return pl.pallas_call(
        _dispatch_body,
        grid_spec=pltpu.PrefetchScalarGridSpec(
            num_scalar_prefetch=3,
            grid=(t // tb,),
            in_specs=[pl.BlockSpec((None, SUBLANES, tb), lambda i, *_: (i, 0, 0)),
                      pl.BlockSpec((tb, d), lambda i, *_: (i, 0))],
            out_specs=pl.BlockSpec(memory_space=pl.ANY),
            scratch_shapes=[pltpu.VMEM((2, lp, d), BF16), pltpu.VMEM((tile, d), BF16),
                            pltpu.SemaphoreType.DMA((2,)), pltpu.SemaphoreType.DMA]),
        out_shape=jax.ShapeDtypeStruct((slots, d), BF16),
        compiler_params=_cparams("arbitrary"),
        name="moe_dispatch",
    )(tab, ends, ltot, prow, h2)


def _ffn_body(te_ref, nu_ref, nxt_ref, par_ref, rows_ref, xs_ref, wg_hbm, wu_hbm, wd_hbm, ys_ref,
              wg_buf, wu_buf, wd_buf, wgb_ref, wub_ref, wdb_ref, sem):
    i = pl.program_id(0)
    e = te_ref[i]
    first = (i == 0) | (e != te_ref[jnp.maximum(i - 1, 0)])
    active = i < nu_ref[0]

    def fetch(expert, slot):
        return [pltpu.make_async_copy(w.at[expert], buf.at[slot], sem.at[slot])
                for w, buf in ((wg_hbm, wg_buf), (wu_hbm, wu_buf), (wd_hbm, wd_buf))]

    @pl.when(i == 0)
    def _():
        for cp in fetch(e, par_ref[e]):
            cp.start()

    @pl.when(first & active)
    def _():
        slot = par_ref[e]
        for cp in fetch(e, slot):
            cp.wait()
        nxt = nxt_ref[e]

        @pl.when(nxt >= 0)
        def _():
            for cp in fetch(nxt, 1 - slot):
                cp.start()

        wgb_ref[...] = wg_buf[slot].astype(BF16)
        wub_ref[...] = wu_buf[slot].astype(BF16)
        wdb_ref[...] = wd_buf[slot].astype(BF16)

    def expert(rows):
        xb = xs_ref[rows, :]
        act = _dot(xb, wgb_ref[...])
        act = act * _sigmoid(act) * _dot(xb, wub_ref[...])
        ys_ref[rows, :] = _dot(act.astype(BF16), wdb_ref[...]).astype(BF16)

    parts = 4
    quarter = xs_ref.shape[0] // parts
    filled = (rows_ref[i] + quarter - 1) // quarter
    for q in range(1, parts + 1):
        @pl.when(active & (filled == q))
        def _():
            expert(slice(0, q * quarter))
            if q < parts:
                ys_ref[q * quarter:, :] = jnp.zeros(((parts - q) * quarter, ys_ref.shape[1]), BF16)


def _ffn(tile_expert, n_used, nxt, par, tile_rows, xs, w_gate, w_up, w_down, tile):
    slots, d = xs.shape
    _, _, de = w_gate.shape
    last = lambda i, te, nu, *_: (jnp.minimum(i, nu[0] - 1), 0)
    hbm = pl.BlockSpec(memory_space=pl.ANY)
    return pl.pallas_call(
        _ffn_body,
        grid_spec=pltpu.PrefetchScalarGridSpec(
            num_scalar_prefetch=5,
            grid=(slots // tile,),
            in_specs=[pl.BlockSpec((tile, d), last), hbm, hbm, hbm],
            out_specs=pl.BlockSpec((tile, d), last),
            scratch_shapes=[pltpu.VMEM((2, d, de), F32), pltpu.VMEM((2, d, de), F32), pltpu.VMEM((2, de, d), F32),
                            pltpu.VMEM((d, de), BF16), pltpu.VMEM((d, de), BF16), pltpu.VMEM((de, d), BF16),
                            pltpu.SemaphoreType.DMA((2,))]),
        out_shape=jax.ShapeDtypeStruct((slots, d), BF16),
        input_output_aliases={5: 0},
        compiler_params=_cparams("arbitrary"),
        name="moe_ffn",
    )(tile_expert, n_used, nxt, par, tile_rows, xs, w_gate, w_up, w_down)


def _combine_body(tab_ref, ltot_ref, ys_ref, rt_ref, x1_ref, mod_ref, fg_ref, o_ref, loc_ref, sem):
    blk = pl.program_id(0)
    nblk = pl.num_programs(0)
    _, lp, d = loc_ref.shape
    slot = blk % 2

    @pl.when(blk == 0)
    def _():
        _start_segments(tab_ref, blk, loc_ref.at[slot], ys_ref, sem.at[slot], False)

    @pl.when(blk + 1 < nblk)
    def _():
        _start_segments(tab_ref, blk + 1, loc_ref.at[1 - slot], ys_ref, sem.at[1 - slot], False)

    rt = rt_ref[...]
    lane = lax.broadcasted_iota(jnp.int32, rt.shape, 1)
    col = lambda k: jnp.sum(jnp.where(lane == k, rt, 0.0), axis=1, keepdims=True)
    w1, w2, p1, p2 = col(2), col(3), col(4), col(5)
    pos = lax.broadcasted_iota(jnp.int32, (rt.shape[0], lp), 1).astype(F32)
    weights = (jnp.where(pos == p1, w1, 0.0) + jnp.where(pos == p2, w2, 0.0)).astype(BF16)
    _wait_segments(ltot_ref[blk], loc_ref.at[slot], ys_ref, sem.at[slot], False)

    def clear(ci, carry):
        loc_ref[slot, pl.ds(pl.multiple_of(ci * SEG, SEG), SEG), :] = jnp.zeros((SEG, d), BF16)
        return carry

    lax.fori_loop(ltot_ref[blk] // SEG, lp // SEG, clear, 0)
    moe = _dot(weights, loc_ref[slot])
    o_ref[...] = _rms(x1_ref[...] + mod_ref[5:6, :] * moe, fg_ref[...])


def _combine(tab, ltot, ys, rt2, x1, mod3, final_g, tb, lp, per_batch):
    t, d = x1.shape
    return pl.pallas_call(
        _combine_body,
        grid_spec=pltpu.PrefetchScalarGridSpec(
            num_scalar_prefetch=2,
            grid=(t // tb,),
            in_specs=[pl.BlockSpec(memory_space=pl.ANY),
                      pl.BlockSpec((tb, LANES), lambda i, *_: (i, 0)),
                      pl.BlockSpec((tb, d), lambda i, *_: (i, 0)),
                      pl.BlockSpec((None, N_MOD, d), lambda i, *_: (i // per_batch, 0, 0)),
                      pl.BlockSpec((1, d), lambda i, *_: (0, 0))],
            out_specs=pl.BlockSpec((tb, d), lambda i, *_: (i, 0)),
            scratch_shapes=[pltpu.VMEM((2, lp, d), BF16), pltpu.SemaphoreType.DMA((2,))]),
        out_shape=jax.ShapeDtypeStruct((t, d), F32),
        compiler_params=_cparams("arbitrary"),
        name="moe_combine",
    )(tab, ltot, ys, rt2, x1, mod3, final_g.reshape(1, d))


def _moe_layout(cnt_blocks, tile, n_tiles):
    lcnt = (cnt_blocks + SEG - 1) // SEG * SEG
    lstart = jnp.cumsum(lcnt, axis=1) - lcnt
    ltot = jnp.sum(lcnt, axis=1)
    per_expert = jnp.sum(lcnt, axis=0)
    tiles_per = (per_expert + tile - 1) // tile
    tile_ends = jnp.cumsum(tiles_per)
    row_ends = tile_ends * tile
    goff = row_ends - tiles_per * tile
    gstart = goff[None, :] + jnp.cumsum(lcnt, axis=0) - lcnt
    tab = jnp.stack([lstart, gstart, lcnt // SEG], axis=-1).reshape(-1).astype(jnp.int32)
    n_used = tile_ends[-1:]
    tile_ids = jnp.arange(n_tiles, dtype=jnp.int32)
    tile_expert = jnp.sum((tile_ends[None, :] <= jnp.minimum(tile_ids, n_used - 1)[:, None]).astype(jnp.int32), axis=1)
    ids = jnp.arange(N_EXPERTS, dtype=jnp.int32)
    used = tiles_per > 0
    par = (jnp.cumsum(used) - used) % 2
    later = jnp.where(used[None, :] & (ids[None, :] > ids[:, None]), ids[None, :], N_EXPERTS)
    nxt = jnp.min(later, axis=1)
    nxt = jnp.where(nxt == N_EXPERTS, -1, nxt)
    seg_ends = goff + per_expert
    ends = jnp.concatenate([row_ends, seg_ends])
    tile_rows = jnp.clip(seg_ends[tile_expert] - tile_ids * tile, 0, tile)
    return (tab, ends.astype(jnp.int32), ltot.astype(jnp.int32), tile_expert.astype(jnp.int32),
            n_used.astype(jnp.int32), nxt.astype(jnp.int32), par.astype(jnp.int32), tile_rows.astype(jnp.int32))


def _sincos_table(rows, cols, dim):
    quarter = dim // 4
    omega = 1.0 / (10000.0 ** (np.arange(quarter, dtype=np.float64) / quarter))
    ang_r = np.arange(rows, dtype=np.float64)[:, None] * omega
    ang_c = np.arange(cols, dtype=np.float64)[:, None] * omega
    emb_r = np.concatenate([np.sin(ang_r), np.cos(ang_r)], axis=-1)
    emb_c = np.concatenate([np.sin(ang_c), np.cos(ang_c)], axis=-1)
    pe = np.concatenate([np.broadcast_to(emb_r[:, None, :], (rows, cols, 2 * quarter)),
                         np.broadcast_to(emb_c[None, :, :], (rows, cols, 2 * quarter))], axis=-1)
    return pe.reshape(rows * cols, 4 * quarter).astype(np.float32)


def _filter_features(n, width):
    pos = np.arange(n, dtype=np.float64)
    t = pos / max(n - 1, 1)
    ang = (2.0 * math.pi * pos / n)[:, None] * np.linspace(1e-4, FILTER_BANDS - 1, FILTER_BANDS)[None, :]
    z = np.concatenate([t[:, None], np.cos(ang), -np.sin(ang)], axis=-1)
    return np.pad(z, ((0, 0), (0, width - z.shape[1]))).astype(np.float32)


def _decay_rates(d_h):
    min_decay = math.log(DECAY_TARGET) / DECAY_SLOW_PCT
    max_decay = math.log(DECAY_TARGET) / DECAY_FAST_PCT
    return np.abs(np.linspace(min_decay, max_decay, d_h)).astype(np.float32).reshape(1, d_h)


def _dft_tables(blk):
    odd = 2 * np.arange(blk, dtype=np.int64) + 1
    shifted = ((odd[:, None] * odd[None, :]) % (8 * blk)).astype(np.float64) * (2.0 * math.pi / (8 * blk))
    plain = ((odd[:, None] * np.arange(blk, dtype=np.int64)[None, :]) % (4 * blk)).astype(np.float64) * (2.0 * math.pi / (4 * blk))
    sgn = (1.0 - 2.0 * (np.arange(blk) % 2)).astype(np.float32).reshape(blk, 1)
    f32 = lambda a: a.astype(np.float32)
    return f32(np.cos(shifted)), f32(np.sin(shifted)), f32(np.cos(plain)), f32(np.sin(plain)), sgn


def kernel(x, c, ctx, c_ctx, w_ada, b_ada, norm1_g, w_in, conv_a_w, conv_a_b, lru_wa, lru_ba, lru_wx, lru_bx, lru_lambda, conv_b_w, filt_w1, filt_b1, filt_freq1, filt_w2, filt_b2, filt_freq2, filt_w3, filt_b3, filt_bias, out_norm_a, out_norm_b, w_out, norm2_g, w_rg, b_rg, w_re, b_re, w_gate, w_up, w_down, final_g):
    bsz, n, d = x.shape
    n_ctx = ctx.shape[1]
    d_rnn = conv_a_w.shape[2]
    d_h = filt_bias.shape[2]
    order = filt_bias.shape[1]
    assert w_ada.shape[0] == 1, "single-layer block"
    l = 0
    ch = 256

    mod_rows = 16
    cc = jnp.concatenate([c, c_ctx[None, :], jnp.zeros((mod_rows - bsz - 1, d), F32)], axis=0)
    mod3 = _ada(cc, w_ada[l], b_ada[l]).reshape(mod_rows, N_MOD, d)

    pe = jnp.asarray(_sincos_table(n // GRID_W, GRID_W, d))
    p3 = _inproj_lat(x, pe, mod3, norm1_g[l], w_in[l], 1024)
    pc3 = _inproj_ctx(ctx.reshape(bsz * n_ctx, d), mod3, bsz, norm1_g[l], w_in[l], 1, d_rnn,
                      1024).reshape(bsz, n_ctx, d_rnn)

    heads_per_blk = ch // (d_rnn // RNN_HEADS)
    nblk = d_rnn // ch
    eye = jnp.eye(heads_per_blk, dtype=F32)

    def blockdiag(w):
        w5 = w.reshape(2, nblk, heads_per_blk, w.shape[2], w.shape[3])
        return jnp.einsum("dnkij,kl->dnkilj", w5, eye).reshape(2, nblk, ch, ch)

    wa_bd, wx_bd = blockdiag(lru_wa[l]), blockdiag(lru_wx[l])
    w_gates = jnp.concatenate([wa_bd[0], wx_bd[0], wa_bd[1], wx_bd[1]], axis=-1).astype(BF16)
    ba, bx = lru_ba[l].reshape(2, nblk, ch), lru_bx[l].reshape(2, nblk, ch)
    bias = jnp.concatenate([ba[0], bx[0], ba[1], bx[1]], axis=-1).reshape(nblk, 1, 4 * ch)
    sp = jax.nn.softplus(-lru_lambda[l])
    cb = conv_a_b[l].reshape(1, d_rnn)
    zeros_state = jnp.zeros((bsz, 1, d_rnn), F32)
    hf_ctx, hb_ctx = _rglru(pc3, 0, 0, conv_a_w[l], cb, w_gates, bias, sp, zeros_state, zeros_state, False, ch)
    ya, _, _ = _rglru(p3, d_rnn // ch, 0, conv_a_w[l], cb, w_gates, bias, sp, hf_ctx, hb_ctx, True, ch)

    as_bf16 = lambda a: jnp.asarray(a).astype(BF16)
    c3, s3, cu, su, sgn = _dft_tables(CONV_BLOCK)
    fe = 64
    z = jnp.asarray(_filter_features(n, fe))
    w1 = jnp.pad(filt_w1[l], ((0, fe - filt_w1.shape[1]), (0, 0)))
    hid = filt_w2.shape[1]
    hr, hi = _filters(z, w1, filt_b1[l].reshape(1, hid), filt_freq1[l].reshape(1, hid), filt_w2[l],
                      filt_b2[l].reshape(1, hid), filt_freq2[l].reshape(1, hid), filt_w3[l],
                      filt_b3[l].reshape(1, -1), jnp.asarray(_decay_rates(d_h)), as_bf16(cu), as_bf16(su),
                      jnp.asarray(sgn), d_h, order, ch)
    yb = _hyena(p3, 2 * d_rnn // ch, conv_b_w[l], filt_bias[l], as_bf16(c3), as_bf16(s3), hr, hi, d_h, ch)

    w_r = jnp.concatenate([w_re[l], w_rg[l], jnp.zeros((d, LANES - N_EXPERTS - N_GROUPS), F32)], axis=1)
    b_r = jnp.concatenate([b_re[l], b_rg[l], jnp.zeros((LANES - N_EXPERTS - N_GROUPS,), F32)]).reshape(1, LANES)
    tb = 512
    tm = 512
    x1, h2, rt, cnt, prow = _merge(ya, yb, x, pe, mod3, out_norm_a[l], out_norm_b[l], w_out[l], norm2_g[l],
                                   w_r, b_r, tm, tm // tb)

    tile = 512
    t_all = bsz * n
    nb = t_all // tb
    lp = 2 * tb + N_EXPERTS * SEG
    n_tiles = (2 * t_all + nb * N_EXPERTS * (SEG - 1) + N_EXPERTS * (tile - 1)) // tile
    cnt_blocks = cnt[:, 0, :N_EXPERTS].astype(jnp.int32)
    tab, ends, ltot, tile_expert, n_used, nxt, par, tile_rows = _moe_layout(cnt_blocks, tile, n_tiles)
    xs = _dispatch(tab, ends, ltot, prow, h2.reshape(t_all, d), n_tiles * tile, tb, lp, tile)
    ys = _ffn(tile_expert, n_used, nxt, par, tile_rows, xs, w_gate[l], w_up[l], w_down[l], tile)
    out = _combine(tab, ltot, ys, rt.reshape(t_all, LANES), x1.reshape(t_all, d), mod3, final_g, tb, lp, n // tb)
    return out.reshape(bsz, n, d)
```

```python
import functools
import math

import numpy as np
import jax
import jax.numpy as jnp
from jax import lax
from jax.experimental import pallas as pl
from jax.experimental.pallas import tpu as pltpu

F32 = jnp.float32
BF16 = jnp.bfloat16
EPS = 1e-6
LRU_C = 8.0
N_MOD = 6
GRID_W = 64
RNN_HEADS = 8
N_GROUPS = 4
EXPERTS_PER_GROUP = 8
N_EXPERTS = N_GROUPS * EXPERTS_PER_GROUP
FILTER_BANDS = 16
DECAY_FAST_PCT = 0.3
DECAY_SLOW_PCT = 1.5
DECAY_TARGET = 1e-2
SUBLANES = 8
LANES = 128
SEG = 2 * SUBLANES
SLOT_SPLIT = 64.0
VMEM_LIMIT = 60 * 1024 * 1024


def _cparams(*sem):
    return pltpu.CompilerParams(dimension_semantics=sem, vmem_limit_bytes=VMEM_LIMIT)


def _dot(a, b):
    return jnp.dot(a, b, preferred_element_type=F32)


def _split(a):
    hi = a.astype(BF16)
    return hi, (a - hi.astype(F32)).astype(BF16)


def _dot3(a, b):
    a_hi, a_lo = _split(a)
    b_hi, b_lo = _split(b)
    return _dot(a_hi, b_hi) + _dot(a_lo, b_hi) + _dot(a_hi, b_lo)


def _rms(v, g):
    return v * lax.rsqrt(jnp.mean(v * v, axis=-1, keepdims=True) + EPS) * g


def _sigmoid(z):
    return 1.0 / (1.0 + jnp.exp(-z))


def _shift_rows(v, d):
    n = v.shape[0]
    rolled = pltpu.roll(v, (-d) % n, 0)
    row = lax.broadcasted_iota(jnp.int32, v.shape, 0)
    ok = (row + d >= 0) & (row + d < n)
    return jnp.where(ok, rolled, 0.0)


def _tile_rot(v, s):
    n, c = v.shape
    return pltpu.roll(v.reshape(n // SUBLANES, SUBLANES, c), s, 1).reshape(n, c)


def _ada_body(c_ref, w_ref, b_ref, o_ref):
    c = c_ref[...]
    o_ref[...] = _dot3(c * _sigmoid(c), w_ref[...]) + b_ref[...]


def _ada(cc, w_ada, b_ada):
    rows, d = cc.shape
    n = w_ada.shape[1]
    tn = 1024
    return pl.pallas_call(
        _ada_body,
        grid=(n // tn,),
        in_specs=[pl.BlockSpec((rows, d), lambda j: (0, 0)),
                  pl.BlockSpec((d, tn), lambda j: (0, j)),
                  pl.BlockSpec((1, tn), lambda j: (0, j))],
        out_specs=pl.BlockSpec((rows, tn), lambda j: (0, j)),
        out_shape=jax.ShapeDtypeStruct((rows, n), F32),
        compiler_params=_cparams("arbitrary"),
        name="ada",
    )(cc, w_ada, b_ada.reshape(1, n))


def _inproj_lat_body(x_ref, pe_ref, mod_ref, g_ref, w_ref, o_ref, wb_ref):
    @pl.when((pl.program_id(0) == 0) & (pl.program_id(1) == 0))
    def _():
        wb_ref[...] = w_ref[...].astype(BF16)

    x = x_ref[...] + pe_ref[...]
    h = _rms(x, g_ref[...]) * (1.0 + mod_ref[1:2, :]) + mod_ref[0:1, :]
    o_ref[...] = _dot(h.astype(BF16), wb_ref[...]).astype(o_ref.dtype)


def _inproj_lat(x3, pe, mod3, g, w, tm):
    b, l, d = x3.shape
    n_out = w.shape[1]
    return pl.pallas_call(
        _inproj_lat_body,
        grid=(l // tm, b),
        in_specs=[pl.BlockSpec((None, tm, d), lambda i, bi: (bi, i, 0)),
                  pl.BlockSpec((tm, d), lambda i, bi: (i, 0)),
                  pl.BlockSpec((None, N_MOD, d), lambda i, bi: (bi, 0, 0)),
                  pl.BlockSpec((1, d), lambda i, bi: (0, 0)),
                  pl.BlockSpec((d, n_out), lambda i, bi: (0, 0), pipeline_mode=pl.Buffered(1))],
        out_specs=pl.BlockSpec((None, tm, n_out), lambda i, bi: (bi, i, 0)),
        out_shape=jax.ShapeDtypeStruct((b, l, n_out), BF16),
        scratch_shapes=[pltpu.VMEM((d, n_out), BF16)],
        compiler_params=_cparams("arbitrary", "arbitrary"),
        name="inproj_pe",
    )(x3, pe, mod3, g.reshape(1, d), w)


def _inproj_ctx_body(x_ref, mod_ref, g_ref, w_ref, o_ref):
    h = _rms(x_ref[...], g_ref[...]) * (1.0 + mod_ref[1:2, :]) + mod_ref[0:1, :]
    o_ref[...] = _dot(h.astype(BF16), w_ref[...].astype(BF16))


def _inproj_ctx(x2, mod3, mod_row, g, w, col_block, n_out, tm):
    r, d = x2.shape
    return pl.pallas_call(
        _inproj_ctx_body,
        grid=(r // tm,),
        in_specs=[pl.BlockSpec((tm, d), lambda i: (i, 0)),
                  pl.BlockSpec((None, N_MOD, d), lambda i: (mod_row, 0, 0)),
                  pl.BlockSpec((1, d), lambda i: (0, 0)),
                  pl.BlockSpec((d, n_out), lambda i: (0, col_block))],
        out_specs=pl.BlockSpec((tm, n_out), lambda i: (i, 0)),
        out_shape=jax.ShapeDtypeStruct((r, n_out), F32),
        compiler_params=_cparams("arbitrary"),
        name="inproj_ctx",
    )(x2, mod3, g.reshape(1, d), w)


def _rglru_body(*refs, with_gate):
    if with_gate:
        (u_ref, gate_ref, cw_ref, cb_ref, w_ref, bias_ref, sp_ref, h0f_ref, h0b_ref,
         y_ref, hf_end_ref, hb_end_ref, af_ref, bf_ref, ab_ref, bb_ref) = refs
    else:
        (u_ref, cw_ref, cb_ref, w_ref, bias_ref, sp_ref, h0f_ref, h0b_ref,
         hf_end_ref, hb_end_ref, af_ref, bf_ref, ab_ref, bb_ref) = refs
    n, c = u_ref.shape
    u = u_ref[...].astype(F32)
    v = cb_ref[...] + cw_ref[1:2, :] * _shift_rows(u, -1) + cw_ref[0:1, :] * _shift_rows(u, -2)
    v = v + cw_ref[2:3, :] * u + cw_ref[3:4, :] * _shift_rows(u, 1)
    z = _dot(v.astype(BF16), w_ref[...]) + bias_ref[...]
    sub = lax.broadcasted_iota(jnp.int32, (n, c), 0) % SUBLANES

    def local_scan(k, reverse, a_ref, b_ref):
        r = _sigmoid(z[:, (2 * k) * c:(2 * k + 1) * c])
        i = _sigmoid(z[:, (2 * k + 1) * c:(2 * k + 2) * c])
        log_a = (-LRU_C) * r * sp_ref[k:k + 1, :]
        a = jnp.exp(log_a)
        b = jnp.sqrt(1.0 - a * a) * (i * v)
        for s in (1, 2, 4):
            ok = (sub < SUBLANES - s) if reverse else (sub >= s)
            shift = SUBLANES - s if reverse else s
            b = a * jnp.where(ok, _tile_rot(b, shift), 0.0) + b
            a = a * jnp.where(ok, _tile_rot(a, shift), 1.0)
        a_ref[...] = a
        b_ref[...] = b

    local_scan(0, False, af_ref, bf_ref)
    local_scan(1, True, ab_ref, bb_ref)

    n_tiles = n // SUBLANES

    def carry_step(q, carry):
        cf, cb = carry
        rf = pl.ds(pl.multiple_of(q * SUBLANES, SUBLANES), SUBLANES)
        rb = pl.ds(pl.multiple_of((n_tiles - 1 - q) * SUBLANES, SUBLANES), SUBLANES)
        af, bf, ab, bb = af_ref[rf, :], bf_ref[rf, :], ab_ref[rb, :], bb_ref[rb, :]
        bf_ref[rf, :] = af * cf + bf
        bb_ref[rb, :] = ab * cb + bb
        last = lambda v: jnp.broadcast_to(v[SUBLANES - 1:SUBLANES, :], (SUBLANES, c))
        first = lambda v: jnp.broadcast_to(v[0:1, :], (SUBLANES, c))
        return last(af) * cf + last(bf), first(ab) * cb + first(bb)

    cf0 = jnp.broadcast_to(h0f_ref[...], (SUBLANES, c))
    cb0 = jnp.broadcast_to(h0b_ref[...], (SUBLANES, c))
    cf, cb = lax.fori_loop(0, n_tiles, carry_step, (cf0, cb0), unroll=4)
    hf_end_ref[...] = cf[0:1, :]
    hb_end_ref[...] = cb[0:1, :]
    if with_gate:
        y_ref[...] = jax.nn.gelu(gate_ref[...].astype(F32), approximate=True) * (bf_ref[...] + bb_ref[...])


def _rglru(p3, u_blk0, gate_blk0, cw, cb, w_gates, bias, sp, h0f, h0b, with_gate, ch):
    b, l, _ = p3.shape
    d_rnn = cw.shape[1]
    nh = d_rnn // ch
    in_specs = [pl.BlockSpec((None, l, ch), lambda bi, h: (bi, 0, u_blk0 + h))]
    args = [p3]
    if with_gate:
        in_specs.append(pl.BlockSpec((None, l, ch), lambda bi, h: (bi, 0, gate_blk0 + h)))
        args.append(p3)
    in_specs += [pl.BlockSpec((4, ch), lambda bi, h: (0, h)),
                 pl.BlockSpec((1, ch), lambda bi, h: (0, h)),
                 pl.BlockSpec((None, ch, 4 * ch), lambda bi, h: (h, 0, 0)),
                 pl.BlockSpec((None, 1, 4 * ch), lambda bi, h: (h, 0, 0)),
                 pl.BlockSpec((2, ch), lambda bi, h: (0, h)),
                 pl.BlockSpec((None, 1, ch), lambda bi, h: (bi, 0, h)),
                 pl.BlockSpec((None, 1, ch), lambda bi, h: (bi, 0, h))]
    args += [cw, cb, w_gates, bias, sp, h0f, h0b]
    end_spec = pl.BlockSpec((None, 1, ch), lambda bi, h: (bi, 0, h))
    end_shape = jax.ShapeDtypeStruct((b, 1, d_rnn), F32)
    out_specs = [end_spec, end_spec]
    out_shape = [end_shape, end_shape]
    if with_gate:
        out_specs = [pl.BlockSpec((None, l, ch), lambda bi, h: (bi, 0, h))] + out_specs
        out_shape = [jax.ShapeDtypeStruct((b, l, d_rnn), F32)] + out_shape
    return pl.pallas_call(
        functools.partial(_rglru_body, with_gate=with_gate),
        grid=(b, nh),
        in_specs=in_specs,
        out_specs=out_specs,
        out_shape=out_shape,
        scratch_shapes=[pltpu.VMEM((l, ch), F32)] * 4,
        compiler_params=_cparams("arbitrary", "arbitrary"),
        name="rglru_lat" if with_gate else "rglru_ctx",
    )(*args)


CONV_BLOCK = 512


def _filt_body(z_ref, w1_ref, b1_ref, f1_ref, w2_ref, b2_ref, f2_ref, w3f_ref, w3b_ref, b3f_ref, b3b_ref,
               dec_ref, cu_ref, su_ref, sgn_ref, hr_ref, hi_ref, hdn_ref):
    n = z_ref.shape[0]
    blk = cu_ref.shape[0]
    nblk = n // blk
    ch = hr_ref.shape[2]

    @pl.when(pl.program_id(0) == 0)
    def _():
        hid = jnp.sin(f1_ref[...] * (_dot3(z_ref[...], w1_ref[...]) + b1_ref[...]))
        hdn_ref[...] = jnp.sin(f2_ref[...] * (_dot3(hid, w2_ref[...]) + b2_ref[...]))

    hdn = hdn_ref[...]
    decay = jnp.exp(-z_ref[:, 0:1] * dec_ref[...])
    kf = (_dot3(hdn, w3f_ref[...]) + b3f_ref[...]) * decay
    kb = (_dot3(hdn, w3b_ref[...]) + b3b_ref[...]) * decay
    row = lax.broadcasted_iota(jnp.int32, (n, ch), 0)
    kb = jnp.where(row == 0, 0.0, kb)
    norm = jnp.sum(jnp.abs(kf) + jnp.abs(kb), axis=0, keepdims=True)
    scale = (2.0 / (2 * blk)) / norm
    sgn = sgn_ref[...]

    p, q, first = [], [], []
    for c in range(nblk):
        taps = jnp.concatenate([kf[c * blk:(c + 1) * blk], kb[c * blk:(c + 1) * blk]], axis=1).astype(BF16)
        p.append(_dot(cu_ref[...], taps))
        q.append(_dot(su_ref[...], taps))
        first.append(taps[0:1, :].astype(F32))
    fwd = lambda a: a[:, :ch]
    bwd = lambda a: a[:, ch:]
    out = lambda a: (a * scale).astype(hr_ref.dtype)
    hr_ref[nblk - 1] = out(fwd(p[0]) + bwd(p[0]))
    hi_ref[nblk - 1] = out(bwd(q[0]) - fwd(q[0]))
    for c in range(1, nblk):
        hr_ref[nblk - 1 + c] = out(fwd(p[c]) + sgn * fwd(q[c - 1]))
        hi_ref[nblk - 1 + c] = out(sgn * (fwd(p[c - 1]) - fwd(first[c - 1])) - fwd(q[c]))
        hr_ref[nblk - 1 - c] = out(bwd(p[c]) + sgn * bwd(q[c - 1]))
        hi_ref[nblk - 1 - c] = out(bwd(q[c]) - sgn * (bwd(p[c - 1]) - bwd(first[c - 1])))


def _filters(z, w1, b1, f1, w2, b2, f2, w3, b3, deltas, cu, su, sgn, d_h, order, ch):
    n, fe = z.shape
    hid = w2.shape[0]
    blk = cu.shape[0]
    nh = 2 * (n // blk) - 1
    per_order = d_h // ch
    ncol = order * per_order
    const = lambda shape: pl.BlockSpec(shape, lambda g: (0, 0))
    hspec = pl.BlockSpec((nh, blk, ch), lambda g: (0, 0, g))
    return pl.pallas_call(
        _filt_body,
        grid=(ncol,),
        in_specs=[const((n, fe)), const((fe, hid)), const((1, hid)), const((1, hid)),
                  const((hid, hid)), const((1, hid)), const((1, hid)),
                  pl.BlockSpec((hid, ch), lambda g: (0, g)),
                  pl.BlockSpec((hid, ch), lambda g: (0, ncol + g)),
                  pl.BlockSpec((1, ch), lambda g: (0, g)),
                  pl.BlockSpec((1, ch), lambda g: (0, ncol + g)),
                  pl.BlockSpec((1, ch), lambda g: (0, g % per_order)),
                  const((blk, blk)), const((blk, blk)), const((blk, 1))],
        out_specs=[hspec, hspec],
        out_shape=[jax.ShapeDtypeStruct((nh, blk, order * d_h), BF16)] * 2,
        scratch_shapes=[pltpu.VMEM((n, hid), F32)],
        compiler_params=_cparams("arbitrary"),
        name="hyena_filters",
    )(z, w1, b1, f1, w2, b2, f2, w3, w3, b3, b3, deltas, cu, su, sgn)


def _hyena_body(v_ref, x1_ref, x2_ref, wv_ref, w1_ref, w2_ref, fb_ref, c3_ref, s3_ref,
                h0r_ref, h0i_ref, h1r_ref, h1i_ref, o_ref, z_ref, zb_ref, p_ref, q_ref, yr_ref, yi_ref):
    n = o_ref.shape[0]
    blk = c3_ref.shape[0]
    nblk = n // blk
    halo = 2 * SUBLANES
    sub = 32

    def conv3(src_ref, w_ref, r0):
        lo, hi = max(r0 - halo, 0), min(r0 + blk + halo, n)
        win = src_ref[lo:hi, :].astype(F32)
        size = hi - lo
        row = lax.broadcasted_iota(jnp.int32, win.shape, 0)
        prev = pltpu.roll(win, 1, 0)
        nxt = pltpu.roll(win, size - 1, 0)
        if lo == 0:
            prev = jnp.where(row == 0, 0.0, prev)
        if hi == n:
            nxt = jnp.where(row == size - 1, 0.0, nxt)
        out = w_ref[0:1, :] * prev + w_ref[1:2, :] * win + w_ref[2:3, :] * nxt
        return out[r0 - lo:r0 - lo + blk, :]

    def long_conv(hr_ref, hi_ref, fb, gate_ref, gate_w_ref, dst_ref):
        for i in range(nblk):
            rows = slice(i * blk, (i + 1) * blk)
            p_ref[rows, :] = _dot(c3_ref[...], zb_ref[rows, :]).astype(BF16)
            q_ref[rows, :] = _dot(s3_ref[...], zb_ref[rows, :]).astype(BF16)

        for j in range(nblk):
            for r0 in range(0, blk, sub):
                yr = yi = None
                for i in range(nblk):
                    rows = slice(i * blk + r0, i * blk + r0 + sub)
                    p, q = p_ref[rows, :], q_ref[rows, :]
                    gr = hr_ref[nblk - 1 + j - i, r0:r0 + sub, :]
                    gi = hi_ref[nblk - 1 + j - i, r0:r0 + sub, :]
                    tr, ti = p * gr + q * gi, q * gr - p * gi
                    yr, yi = (tr, ti) if yr is None else (yr + tr, yi + ti)
                yr_ref[j * blk + r0:j * blk + r0 + sub, :] = yr
                yi_ref[j * blk + r0:j * blk + r0 + sub, :] = yi
            rows = slice(j * blk, (j + 1) * blk)
            y = _dot(c3_ref[...], yr_ref[rows, :]) + _dot(s3_ref[...], yi_ref[rows, :])
            dst_ref[rows, :] = conv3(gate_ref, gate_w_ref, j * blk) * (y + z_ref[rows, :] * fb)

    for r0 in range(0, n, blk):
        z_ref[r0:r0 + blk, :] = conv3(v_ref, wv_ref, r0)
    zb_ref[...] = z_ref[...].astype(BF16)
    long_conv(h0r_ref, h0i_ref, fb_ref[0:1, :], x1_ref, w1_ref, z_ref)
    zb_ref[...] = z_ref[...].astype(BF16)
    long_conv(h1r_ref, h1i_ref, fb_ref[1:2, :], x2_ref, w2_ref, o_ref)


def _hyena(p3, col_blk0, conv_w, fbias, c3, s3, hr, hi, d_h, ch):
    b, n, _ = p3.shape
    nh, blk, _ = hr.shape
    per = d_h // ch
    zspec = lambda k: pl.BlockSpec((None, n, ch), lambda h, bi: (bi, 0, col_blk0 + k * per + h))
    wspec = lambda k: pl.BlockSpec((3, ch), lambda h, bi: (0, k * per + h))
    hspec = lambda o: pl.BlockSpec((nh, blk, ch), lambda h, bi: (0, 0, o * per + h), pipeline_mode=pl.Buffered(1))
    const = pl.BlockSpec((blk, blk), lambda h, bi: (0, 0))
    return pl.pallas_call(
        _hyena_body,
        grid=(per, b),
        in_specs=[zspec(0), zspec(1), zspec(2), wspec(0), wspec(1), wspec(2),
                  pl.BlockSpec((2, ch), lambda h, bi: (0, h)), const, const,
                  hspec(0), hspec(0), hspec(1), hspec(1)],
        out_specs=pl.BlockSpec((None, n, ch), lambda h, bi: (bi, 0, h)),
        out_shape=jax.ShapeDtypeStruct((b, n, d_h), F32),
        scratch_shapes=[pltpu.VMEM((n, ch), F32), pltpu.VMEM((n, ch), BF16),
                        pltpu.VMEM((n, ch), BF16), pltpu.VMEM((n, ch), BF16),
                        pltpu.VMEM((n, ch), BF16), pltpu.VMEM((n, ch), BF16)],
        compiler_params=_cparams("arbitrary", "arbitrary"),
        name="hyena_mix",
    )(p3, p3, p3, conv_w, conv_w, conv_w, fbias, c3, s3, hr, hi, hr, hi)


def _merge_body(ya_ref, yb_ref, x_ref, pe_ref, mod_ref, ga_ref, gb_ref, wo_ref, g2_ref, wr_ref, br_ref,
                x1_ref, h2_ref, rt_ref, cnt_ref, prow_ref, wob_ref, tri_ref):
    @pl.when((pl.program_id(0) == 0) & (pl.program_id(1) == 0))
    def _():
        wob_ref[...] = wo_ref[...].astype(BF16)
        earlier = lax.broadcasted_iota(jnp.int32, tri_ref.shape, 1) < lax.broadcasted_iota(jnp.int32, tri_ref.shape, 0)
        tri_ref[...] = jnp.where(earlier, 1.0, 0.0).astype(BF16)

    da = ya_ref.shape[1]
    na = _rms(ya_ref[...], ga_ref[...]).astype(BF16)
    nb = _rms(yb_ref[...], gb_ref[...]).astype(BF16)
    y = _dot(na, wob_ref[0:da, :]) + _dot(nb, wob_ref[da:, :])
    _route(y, x_ref, pe_ref, mod_ref, g2_ref, wr_ref, br_ref, x1_ref, h2_ref, rt_ref, cnt_ref, prow_ref, tri_ref)


def _route(y, x_ref, pe_ref, mod_ref, g2_ref, wr_ref, br_ref, x1_ref, h2_ref, rt_ref, cnt_ref, prow_ref, tri_ref):
    x1 = x_ref[...] + pe_ref[...] + mod_ref[2:3, :] * y
    x1_ref[...] = x1
    h2 = _rms(x1, g2_ref[...]) * (1.0 + mod_ref[4:5, :]) + mod_ref[3:4, :]
    h2_ref[...] = h2.astype(BF16)

    logits = _dot3(h2, wr_ref[...]) + br_ref[...]
    lane = lax.broadcasted_iota(jnp.int32, logits.shape, 1).astype(F32)
    neg = -jnp.inf
    big = jnp.float32(1 << 20)
    gl = jnp.where((lane >= N_EXPERTS) & (lane < N_EXPERTS + N_GROUPS), logits, neg)
    gmax = jnp.max(gl, axis=1, keepdims=True)
    g_p = 1.0 / jnp.sum(jnp.exp(gl - gmax), axis=1, keepdims=True)
    g_i = jnp.min(jnp.where(gl == gmax, lane, big), axis=1, keepdims=True) - N_EXPERTS
    lo = g_i * EXPERTS_PER_GROUP
    el = jnp.where((lane >= lo) & (lane < lo + EXPERTS_PER_GROUP), logits, neg)
    m1 = jnp.max(el, axis=1, keepdims=True)
    i1 = jnp.min(jnp.where(el == m1, lane, big), axis=1, keepdims=True)
    el2 = jnp.where(lane == i1, neg, el)
    m2 = jnp.max(el2, axis=1, keepdims=True)
    i2 = jnp.min(jnp.where(el2 == m2, lane, big), axis=1, keepdims=True)
    e2 = jnp.exp(m2 - m1)
    w1 = g_p / (1.0 + e2)
    w2 = g_p * e2 / (1.0 + e2)
    o1 = jnp.where(lane == i1, 1.0, 0.0)
    o2 = jnp.where(lane == i2, 1.0, 0.0)
    nsub = cnt_ref.shape[0]
    sub = o1.shape[0] // nsub
    sq = lambda shape, d: lax.broadcasted_iota(jnp.int32, shape, d)
    earlier = tri_ref[...]
    below = jnp.where(sq((LANES, LANES), 0) < sq((LANES, LANES), 1), 1.0, 0.0).astype(BF16)
    pick_row = jnp.where(sq((SUBLANES, LANES), 0) == sq((SUBLANES, LANES), 1), 1.0, 0.0).astype(BF16)
    row_id = sq((SUBLANES, sub), 0)
    pos1, pos2 = [], []
    for k in range(nsub):
        a1, a2 = o1[k * sub:(k + 1) * sub], o2[k * sub:(k + 1) * sub]
        both = a1 + a2
        count = jnp.sum(both, axis=0, keepdims=True)
        cnt_ref[k] = jnp.broadcast_to(count, cnt_ref.shape[1:])
        runs = jnp.broadcast_to(jnp.floor((count + (SEG - 1)) / SEG), (SUBLANES, LANES))
        start = _dot(runs.astype(BF16), below)[0:1, :] * SEG
        before = _dot(earlier, both.astype(BF16)) + start
        p1 = jnp.sum(a1 * before, axis=1, keepdims=True)
        p2 = jnp.sum(a2 * before, axis=1, keepdims=True)
        pos1.append(p1)
        pos2.append(p2)
        lane_k = sq((sub, LANES), 1).astype(F32)
        hi1, hi2 = jnp.floor(p1 / SLOT_SPLIT), jnp.floor(p2 / SLOT_SPLIT)
        parts = (jnp.where(lane_k == 0.0, hi1, 0.0) + jnp.where(lane_k == 1.0, p1 - SLOT_SPLIT * hi1, 0.0)
                 + jnp.where(lane_k == 2.0, hi2, 0.0) + jnp.where(lane_k == 3.0, p2 - SLOT_SPLIT * hi2, 0.0))
        rows = lax.dot_general(pick_row, parts.astype(BF16), (((1,), (1,)), ((), ())), preferred_element_type=F32)
        q1 = rows[0:1, :] * SLOT_SPLIT + rows[1:2, :]
        q2 = rows[2:3, :] * SLOT_SPLIT + rows[3:4, :]
        prow_ref[k] = jnp.where(row_id == 0, q1, jnp.where(row_id == 1, q2, 0.0))
    p1 = jnp.concatenate(pos1, axis=0)
    p2 = jnp.concatenate(pos2, axis=0)
    rt_ref[...] = (jnp.where(lane == 0.0, i1, 0.0) + jnp.where(lane == 1.0, i2, 0.0)
                   + jnp.where(lane == 2.0, w1, 0.0) + jnp.where(lane == 3.0, w2, 0.0)
                   + jnp.where(lane == 4.0, p1, 0.0) + jnp.where(lane == 5.0, p2, 0.0))


def _merge(ya, yb, x3, pe, mod3, ga, gb, w_out, g2n, w_r, b_r, tm, nsub):
    b, n, d = x3.shape
    da, db = ya.shape[2], yb.shape[2]
    per = n // tm
    tok = lambda w: pl.BlockSpec((None, tm, w), lambda i, bi: (bi, i, 0))
    const = lambda shape: pl.BlockSpec(shape, lambda i, bi: (0, 0))
    blk = lambda w: pl.BlockSpec((nsub, SUBLANES, w), lambda i, bi: (bi * per + i, 0, 0))
    return pl.pallas_call(
        _merge_body,
        grid=(per, b),
        in_specs=[tok(da), tok(db), tok(d), pl.BlockSpec((tm, d), lambda i, bi: (i, 0)),
                  pl.BlockSpec((None, N_MOD, d), lambda i, bi: (bi, 0, 0)),
                  const((1, da)), const((1, db)), const((da + db, d)), const((1, d)),
                  const((d, LANES)), const((1, LANES))],
        out_specs=[tok(d), tok(d), tok(LANES), blk(LANES), blk(tm // nsub)],
        out_shape=[jax.ShapeDtypeStruct((b, n, d), F32), jax.ShapeDtypeStruct((b, n, d), BF16),
                   jax.ShapeDtypeStruct((b, n, LANES), F32),
                   jax.ShapeDtypeStruct((b * per * nsub, SUBLANES, LANES), F32),
                   jax.ShapeDtypeStruct((b * per * nsub, SUBLANES, tm // nsub), F32)],
        scratch_shapes=[pltpu.VMEM((da + db, d), BF16), pltpu.VMEM((tm // nsub, tm // nsub), BF16)],
        compiler_params=_cparams("arbitrary", "arbitrary"),
        name="merge_route",
    )(ya, yb, x3, pe, mod3, ga.reshape(1, da), gb.reshape(1, db), w_out, g2n.reshape(1, d), w_r, b_r)


def _segment_copy(local_ref, lrow, global_ref, grow, rows, sem, to_global):
    lrows = pl.ds(pl.multiple_of(lrow, SEG), rows)
    grows = pl.ds(pl.multiple_of(grow, SEG), rows)
    if to_global:
        return pltpu.make_async_copy(local_ref.at[lrows, :], global_ref.at[grows, :], sem)
    return pltpu.make_async_copy(global_ref.at[grows, :], local_ref.at[lrows, :], sem)


def _start_segments(tab_ref, blk, local_ref, global_ref, sem, to_global):
    def per_expert(e, carry):
        k = (blk * N_EXPERTS + e) * 3
        ls, gs, nchunk = tab_ref[k], tab_ref[k + 1], tab_ref[k + 2]

        def per_chunk(ci, carry2):
            _segment_copy(local_ref, ls + ci * SEG, global_ref, gs + ci * SEG, SEG, sem, to_global).start()
            return carry2

        return lax.fori_loop(0, nchunk, per_chunk, carry)

    lax.fori_loop(0, N_EXPERTS, per_expert, 0, unroll=2)


def _wait_segments(rows, local_ref, global_ref, sem, to_global):
    chunks = rows // SEG
    nbits = (local_ref.shape[0] // SEG).bit_length()
    for k in range(nbits):
        @pl.when(((chunks >> k) & 1) == 1)
        def _():
            _segment_copy(local_ref, 0, global_ref, 0, SEG << k, sem, to_global).wait()


def _dispatch_body(tab_ref, ends_ref, ltot_ref, prow_ref, h2_ref, xs_ref, loc_ref, zero_ref, sem, zsem):
    blk = pl.program_id(0)
    nblk = pl.num_programs(0)
    tb = h2_ref.shape[0]
    lp = loc_ref.shape[1]
    tile = zero_ref.shape[0]
    slot = blk % 2

    @pl.when(blk == 0)
    def _():
        zero_ref[...] = jnp.zeros_like(zero_ref)

        def zero_fill(first_row, rows):
            dst = xs_ref.at[pl.ds(pl.multiple_of(first_row, SEG), rows), :]
            return pltpu.make_async_copy(zero_ref.at[pl.ds(0, rows), :], dst, zsem)

        pad = 64

        def fill(e, carry, wait):
            end, real = ends_ref[e], ends_ref[N_EXPERTS + e]
            first = real // pad * pad

            def one(j, carry2):
                cp = zero_fill(first + j * pad, pad)
                cp.wait() if wait else cp.start()
                return carry2

            return lax.fori_loop(0, (end - first) // pad, one, carry)

        lax.fori_loop(0, N_EXPERTS, functools.partial(fill, wait=False), 0)
        used = ends_ref[N_EXPERTS - 1]
        spare = (xs_ref.shape[0] - used) // tile

        def fill_spare(j, carry, wait):
            cp = zero_fill(used + j * tile, tile)
            cp.wait() if wait else cp.start()
            return carry

        lax.fori_loop(0, spare, functools.partial(fill_spare, wait=False), 0)
        lax.fori_loop(0, N_EXPERTS, functools.partial(fill, wait=True), 0)
        lax.fori_loop(0, spare, functools.partial(fill_spare, wait=True), 0)

    pos = lax.broadcasted_iota(jnp.int32, (lp, tb), 0).astype(F32)
    onehot = jnp.where((pos == prow_ref[0:1, :]) | (pos == prow_ref[1:2, :]), 1.0, 0.0).astype(BF16)
    loc_ref[slot] = _dot(onehot, h2_ref[...]).astype(BF16)

    _start_segments(tab_ref, blk, loc_ref.at[slot], xs_ref, sem.at[slot], True)

    @pl.when(blk > 0)
    def _():
        _wait_segments(ltot_ref[blk - 1], loc_ref.at[1 - slot], xs_ref, sem.at[1 - slot], True)

    @pl.when(blk == nblk - 1)
    def _():
        _wait_segments(ltot_ref[blk], loc_ref.at[slot], xs_ref, sem.at[slot], True)


def _dispatch(tab, ends, ltot, prow, h2, slots, tb, lp, tile):
    t, d = h2.shape
    return pl.pallas_call(
        _dispatch_body,
        grid_spec=pltpu.PrefetchScalarGridSpec(
            num_scalar_prefetch=3,
            grid=(t // tb,),
            in_specs=[pl.BlockSpec((None, SUBLANES, tb), lambda i, *_: (i, 0, 0)),
                      pl.BlockSpec((tb, d), lambda i, *_: (i, 0))],
            out_specs=pl.BlockSpec(memory_space=pl.ANY),
            scratch_shapes=[pltpu.VMEM((2, lp, d), BF16), pltpu.VMEM((tile, d), BF16),
                            pltpu.SemaphoreType.DMA((2,)), pltpu.SemaphoreType.DMA]),
        out_shape=jax.ShapeDtypeStruct((slots, d), BF16),
        compiler_params=_cparams("arbitrary"),
        name="moe_dispatch",
    )(tab, ends, ltot, prow, h2)


def _ffn_body(te_ref, nu_ref, nxt_ref, par_ref, rows_ref, xs_ref, wg_hbm, wu_hbm, wd_hbm, ys_ref,
              wg_buf, wu_buf, wd_buf, wgb_ref, wub_ref, wdb_ref, sem):
    i = pl.program_id(0)
    e = te_ref[i]
    first = (i == 0) | (e != te_ref[jnp.maximum(i - 1, 0)])
    active = i < nu_ref[0]

    def fetch(expert, slot):
        return [pltpu.make_async_copy(w.at[expert], buf.at[slot], sem.at[slot])
                for w, buf in ((wg_hbm, wg_buf), (wu_hbm, wu_buf), (wd_hbm, wd_buf))]

    @pl.when(i == 0)
    def _():
        for cp in fetch(e, par_ref[e]):
            cp.start()

    @pl.when(first & active)
    def _():
        slot = par_ref[e]
        for cp in fetch(e, slot):
            cp.wait()
        nxt = nxt_ref[e]

        @pl.when(nxt >= 0)
        def _():
            for cp in fetch(nxt, 1 - slot):
                cp.start()

        wgb_ref[...] = wg_buf[slot].astype(BF16)
        wub_ref[...] = wu_buf[slot].astype(BF16)
        wdb_ref[...] = wd_buf[slot].astype(BF16)

    def expert(rows):
        xb = xs_ref[rows, :]
        act = _dot(xb, wgb_ref[...])
        act = act * _sigmoid(act) * _dot(xb, wub_ref[...])
        ys_ref[rows, :] = _dot(act.astype(BF16), wdb_ref[...]).astype(BF16)

    parts = 4
    quarter = xs_ref.shape[0] // parts
    filled = (rows_ref[i] + quarter - 1) // quarter
    for q in range(1, parts + 1):
        @pl.when(active & (filled == q))
        def _():
            expert(slice(0, q * quarter))
            if q < parts:
                ys_ref[q * quarter:, :] = jnp.zeros(((parts - q) * quarter, ys_ref.shape[1]), BF16)


def _ffn(tile_expert, n_used, nxt, par, tile_rows, xs, w_gate, w_up, w_down, tile):
    slots, d = xs.shape
    _, _, de = w_gate.shape
    last = lambda i, te, nu, *_: (jnp.minimum(i, nu[0] - 1), 0)
    hbm = pl.BlockSpec(memory_space=pl.ANY)
    return pl.pallas_call(
        _ffn_body,
        grid_spec=pltpu.PrefetchScalarGridSpec(
            num_scalar_prefetch=5,
            grid=(slots // tile,),
            in_specs=[pl.BlockSpec((tile, d), last), hbm, hbm, hbm],
            out_specs=pl.BlockSpec((tile, d), last),
            scratch_shapes=[pltpu.VMEM((2, d, de), F32), pltpu.VMEM((2, d, de), F32), pltpu.VMEM((2, de, d), F32),
                            pltpu.VMEM((d, de), BF16), pltpu.VMEM((d, de), BF16), pltpu.VMEM((de, d), BF16),
                            pltpu.SemaphoreType.DMA((2,))]),
        out_shape=jax.ShapeDtypeStruct((slots, d), BF16),
        input_output_aliases={5: 0},
        compiler_params=_cparams("arbitrary"),
        name="moe_ffn",
    )(tile_expert, n_used, nxt, par, tile_rows, xs, w_gate, w_up, w_down)


def _combine_body(tab_ref, ltot_ref, ys_ref, rt_ref, x1_ref, mod_ref, fg_ref, o_ref, loc_ref, sem):
    blk = pl.program_id(0)
    nblk = pl.num_programs(0)
    _, lp, d = loc_ref.shape
    slot = blk % 2

    @pl.when(blk == 0)
    def _():
        _start_segments(tab_ref, blk, loc_ref.at[slot], ys_ref, sem.at[slot], False)

    @pl.when(blk + 1 < nblk)
    def _():
        _start_segments(tab_ref, blk + 1, loc_ref.at[1 - slot], ys_ref, sem.at[1 - slot], False)

    rt = rt_ref[...]
    lane = lax.broadcasted_iota(jnp.int32, rt.shape, 1)
    col = lambda k: jnp.sum(jnp.where(lane == k, rt, 0.0), axis=1, keepdims=True)
    w1, w2, p1, p2 = col(2), col(3), col(4), col(5)
    pos = lax.broadcasted_iota(jnp.int32, (rt.shape[0], lp), 1).astype(F32)
    weights = (jnp.where(pos == p1, w1, 0.0) + jnp.where(pos == p2, w2, 0.0)).astype(BF16)
    _wait_segments(ltot_ref[blk], loc_ref.at[slot], ys_ref, sem.at[slot], False)

    def clear(ci, carry):
        loc_ref[slot, pl.ds(pl.multiple_of(ci * SEG, SEG), SEG), :] = jnp.zeros((SEG, d), BF16)
        return carry

    lax.fori_loop(ltot_ref[blk] // SEG, lp // SEG, clear, 0)
    moe = _dot(weights, loc_ref[slot])
    o_ref[...] = _rms(x1_ref[...] + mod_ref[5:6, :] * moe, fg_ref[...])


def _combine(tab, ltot, ys, rt2, x1, mod3, final_g, tb, lp, per_batch):
    t, d = x1.shape
    return pl.pallas_call(
        _combine_body,
        grid_spec=pltpu.PrefetchScalarGridSpec(
            num_scalar_prefetch=2,
            grid=(t // tb,),
            in_specs=[pl.BlockSpec(memory_space=pl.ANY),
                      pl.BlockSpec((tb, LANES), lambda i, *_: (i, 0)),
                      pl.BlockSpec((tb, d), lambda i, *_: (i, 0)),
                      pl.BlockSpec((None, N_MOD, d), lambda i, *_: (i // per_batch, 0, 0)),
                      pl.BlockSpec((1, d), lambda i, *_: (0, 0))],
            out_specs=pl.BlockSpec((tb, d), lambda i, *_: (i, 0)),
            scratch_shapes=[pltpu.VMEM((2, lp, d), BF16), pltpu.SemaphoreType.DMA((2,))]),
        out_shape=jax.ShapeDtypeStruct((t, d), F32),
        compiler_params=_cparams("arbitrary"),
        name="moe_combine",
    )(tab, ltot, ys, rt2, x1, mod3, final_g.reshape(1, d))


def _moe_layout(cnt_blocks, tile, n_tiles):
    lcnt = (cnt_blocks + SEG - 1) // SEG * SEG
    lstart = jnp.cumsum(lcnt, axis=1) - lcnt
    ltot = jnp.sum(lcnt, axis=1)
    per_expert = jnp.sum(lcnt, axis=0)
    tiles_per = (per_expert + tile - 1) // tile
    tile_ends = jnp.cumsum(tiles_per)
    row_ends = tile_ends * tile
    goff = row_ends - tiles_per * tile
    gstart = goff[None, :] + jnp.cumsum(lcnt, axis=0) - lcnt
    tab = jnp.stack([lstart, gstart, lcnt // SEG], axis=-1).reshape(-1).astype(jnp.int32)
    n_used = tile_ends[-1:]
    tile_ids = jnp.arange(n_tiles, dtype=jnp.int32)
    tile_expert = jnp.sum((tile_ends[None, :] <= jnp.minimum(tile_ids, n_used - 1)[:, None]).astype(jnp.int32), axis=1)
    ids = jnp.arange(N_EXPERTS, dtype=jnp.int32)
    used = tiles_per > 0
    par = (jnp.cumsum(used) - used) % 2
    later = jnp.where(used[None, :] & (ids[None, :] > ids[:, None]), ids[None, :], N_EXPERTS)
    nxt = jnp.min(later, axis=1)
    nxt = jnp.where(nxt == N_EXPERTS, -1, nxt)
    seg_ends = goff + per_expert
    ends = jnp.concatenate([row_ends, seg_ends])
    tile_rows = jnp.clip(seg_ends[tile_expert] - tile_ids * tile, 0, tile)
    return (tab, ends.astype(jnp.int32), ltot.astype(jnp.int32), tile_expert.astype(jnp.int32),
            n_used.astype(jnp.int32), nxt.astype(jnp.int32), par.astype(jnp.int32), tile_rows.astype(jnp.int32))


def _sincos_table(rows, cols, dim):
    quarter = dim // 4
    omega = 1.0 / (10000.0 ** (np.arange(quarter, dtype=np.float64) / quarter))
    ang_r = np.arange(rows, dtype=np.float64)[:, None] * omega
    ang_c = np.arange(cols, dtype=np.float64)[:, None] * omega
    emb_r = np.concatenate([np.sin(ang_r), np.cos(ang_r)], axis=-1)
    emb_c = np.concatenate([np.sin(ang_c), np.cos(ang_c)], axis=-1)
    pe = np.concatenate([np.broadcast_to(emb_r[:, None, :], (rows, cols, 2 * quarter)),
                         np.broadcast_to(emb_c[None, :, :], (rows, cols, 2 * quarter))], axis=-1)
    return pe.reshape(rows * cols, 4 * quarter).astype(np.float32)


def _filter_features(n, width):
    pos = np.arange(n, dtype=np.float64)
    t = pos / max(n - 1, 1)
    ang = (2.0 * math.pi * pos / n)[:, None] * np.linspace(1e-4, FILTER_BANDS - 1, FILTER_BANDS)[None, :]
    z = np.concatenate([t[:, None], np.cos(ang), -np.sin(ang)], axis=-1)
    return np.pad(z, ((0, 0), (0, width - z.shape[1]))).astype(np.float32)


def _decay_rates(d_h):
    min_decay = math.log(DECAY_TARGET) / DECAY_SLOW_PCT
    max_decay = math.log(DECAY_TARGET) / DECAY_FAST_PCT
    return np.abs(np.linspace(min_decay, max_decay, d_h)).astype(np.float32).reshape(1, d_h)


def _dft_tables(blk):
    odd = 2 * np.arange(blk, dtype=np.int64) + 1
    shifted = ((odd[:, None] * odd[None, :]) % (8 * blk)).astype(np.float64) * (2.0 * math.pi / (8 * blk))
    plain = ((odd[:, None] * np.arange(blk, dtype=np.int64)[None, :]) % (4 * blk)).astype(np.float64) * (2.0 * math.pi / (4 * blk))
    sgn = (1.0 - 2.0 * (np.arange(blk) % 2)).astype(np.float32).reshape(blk, 1)
    f32 = lambda a: a.astype(np.float32)
    return f32(np.cos(shifted)), f32(np.sin(shifted)), f32(np.cos(plain)), f32(np.sin(plain)), sgn


def kernel(x, c, ctx, c_ctx, w_ada, b_ada, norm1_g, w_in, conv_a_w, conv_a_b, lru_wa, lru_ba, lru_wx, lru_bx, lru_lambda, conv_b_w, filt_w1, filt_b1, filt_freq1, filt_w2, filt_b2, filt_freq2, filt_w3, filt_b3, filt_bias, out_norm_a, out_norm_b, w_out, norm2_g, w_rg, b_rg, w_re, b_re, w_gate, w_up, w_down, final_g):
    bsz, n, d = x.shape
    n_ctx = ctx.shape[1]
    d_rnn = conv_a_w.shape[2]
    d_h = filt_bias.shape[2]
    order = filt_bias.shape[1]
    assert w_ada.shape[0] == 1, "single-layer block"
    l = 0
    ch = 256

    mod_rows = 16
    cc = jnp.concatenate([c, c_ctx[None, :], jnp.zeros((mod_rows - bsz - 1, d), F32)], axis=0)
    mod3 = _ada(cc, w_ada[l], b_ada[l]).reshape(mod_rows, N_MOD, d)

    pe = jnp.asarray(_sincos_table(n // GRID_W, GRID_W, d))
    p3 = _inproj_lat(x, pe, mod3, norm1_g[l], w_in[l], 1024)
    pc3 = _inproj_ctx(ctx.reshape(bsz * n_ctx, d), mod3, bsz, norm1_g[l], w_in[l], 1, d_rnn,
                      1024).reshape(bsz, n_ctx, d_rnn)

    heads_per_blk = ch // (d_rnn // RNN_HEADS)
    nblk = d_rnn // ch
    eye = jnp.eye(heads_per_blk, dtype=F32)

    def blockdiag(w):
        w5 = w.reshape(2, nblk, heads_per_blk, w.shape[2], w.shape[3])
        return jnp.einsum("dnkij,kl->dnkilj", w5, eye).reshape(2, nblk, ch, ch)

    wa_bd, wx_bd = blockdiag(lru_wa[l]), blockdiag(lru_wx[l])
    w_gates = jnp.concatenate([wa_bd[0], wx_bd[0], wa_bd[1], wx_bd[1]], axis=-1).astype(BF16)
    ba, bx = lru_ba[l].reshape(2, nblk, ch), lru_bx[l].reshape(2, nblk, ch)
    bias = jnp.concatenate([ba[0], bx[0], ba[1], bx[1]], axis=-1).reshape(nblk, 1, 4 * ch)
    sp = jax.nn.softplus(-lru_lambda[l])
    cb = conv_a_b[l].reshape(1, d_rnn)
    zeros_state = jnp.zeros((bsz, 1, d_rnn), F32)
    hf_ctx, hb_ctx = _rglru(pc3, 0, 0, conv_a_w[l], cb, w_gates, bias, sp, zeros_state, zeros_state, False, ch)
    ya, _, _ = _rglru(p3, d_rnn // ch, 0, conv_a_w[l], cb, w_gates, bias, sp, hf_ctx, hb_ctx, True, ch)

    as_bf16 = lambda a: jnp.asarray(a).astype(BF16)
    c3, s3, cu, su, sgn = _dft_tables(CONV_BLOCK)
    fe = 64
    z = jnp.asarray(_filter_features(n, fe))
    w1 = jnp.pad(filt_w1[l], ((0, fe - filt_w1.shape[1]), (0, 0)))
    hid = filt_w2.shape[1]
    hr, hi = _filters(z, w1, filt_b1[l].reshape(1, hid), filt_freq1[l].reshape(1, hid), filt_w2[l],
                      filt_b2[l].reshape(1, hid), filt_freq2[l].reshape(1, hid), filt_w3[l],
                      filt_b3[l].reshape(1, -1), jnp.asarray(_decay_rates(d_h)), as_bf16(cu), as_bf16(su),
                      jnp.asarray(sgn), d_h, order, ch)
    yb = _hyena(p3, 2 * d_rnn // ch, conv_b_w[l], filt_bias[l], as_bf16(c3), as_bf16(s3), hr, hi, d_h, ch)

    w_r = jnp.concatenate([w_re[l], w_rg[l], jnp.zeros((d, LANES - N_EXPERTS - N_GROUPS), F32)], axis=1)
    b_r = jnp.concatenate([b_re[l], b_rg[l], jnp.zeros((LANES - N_EXPERTS - N_GROUPS,), F32)]).reshape(1, LANES)
    tb = 512
    tm = 512
    x1, h2, rt, cnt, prow = _merge(ya, yb, x, pe, mod3, out_norm_a[l], out_norm_b[l], w_out[l], norm2_g[l],
                                   w_r, b_r, tm, tm // tb)

    tile = 1024
    t_all = bsz * n
    nb = t_all // tb
    lp = 2 * tb + N_EXPERTS * SEG
    n_tiles = (2 * t_all + nb * N_EXPERTS * (SEG - 1) + N_EXPERTS * (tile - 1)) // tile
    cnt_blocks = cnt[:, 0, :N_EXPERTS].astype(jnp.int32)
    tab, ends, ltot, tile_expert, n_used, nxt, par, tile_rows = _moe_layout(cnt_blocks, tile, n_tiles)
    xs = _dispatch(tab, ends, ltot, prow, h2.reshape(t_all, d), n_tiles * tile, tb, lp, tile)
    ys = _ffn(tile_expert, n_used, nxt, par, tile_rows, xs, w_gate[l], w_up[l], w_down[l], tile)
    out = _combine(tab, ltot, ys, rt.reshape(t_all, LANES), x1.reshape(t_all, d), mod3, final_g, tb, lp, n // tb)
    return out.reshape(bsz, n, d)
```

```python
import functools
import math

import numpy as np
import jax
import jax.numpy as jnp
from jax import lax
from jax.experimental import pallas as pl
from jax.experimental.pallas import tpu as pltpu

F32 = jnp.float32
BF16 = jnp.bfloat16
EPS = 1e-6
LRU_C = 8.0
N_MOD = 6
GRID_W = 64
RNN_HEADS = 8
N_GROUPS = 4
EXPERTS_PER_GROUP = 8
N_EXPERTS = N_GROUPS * EXPERTS_PER_GROUP
FILTER_BANDS = 16
DECAY_FAST_PCT = 0.3
DECAY_SLOW_PCT = 1.5
DECAY_TARGET = 1e-2
SUBLANES = 8
LANES = 128
SEG = 2 * SUBLANES
SLOT_SPLIT = 64.0
VMEM_LIMIT = 60 * 1024 * 1024


def _cparams(*sem):
    return pltpu.CompilerParams(dimension_semantics=sem, vmem_limit_bytes=VMEM_LIMIT)


def _dot(a, b):
    return jnp.dot(a, b, preferred_element_type=F32)


def _split(a):
    hi = a.astype(BF16)
    return hi, (a - hi.astype(F32)).astype(BF16)


def _dot3(a, b):
    a_hi, a_lo = _split(a)
    b_hi, b_lo = _split(b)
    return _dot(a_hi, b_hi) + _dot(a_lo, b_hi) + _dot(a_hi, b_lo)


def _rms(v, g):
    return v * lax.rsqrt(jnp.mean(v * v, axis=-1, keepdims=True) + EPS) * g


def _sigmoid(z):
    return 1.0 / (1.0 + jnp.exp(-z))


def _shift_rows(v, d):
    n = v.shape[0]
    rolled = pltpu.roll(v, (-d) % n, 0)
    row = lax.broadcasted_iota(jnp.int32, v.shape, 0)
    ok = (row + d >= 0) & (row + d < n)
    return jnp.where(ok, rolled, 0.0)


def _tile_rot(v, s):
    n, c = v.shape
    return pltpu.roll(v.reshape(n // SUBLANES, SUBLANES, c), s, 1).reshape(n, c)


def _ada_body(c_ref, w_ref, b_ref, o_ref):
    c = c_ref[...]
    o_ref[...] = _dot3(c * _sigmoid(c), w_ref[...]) + b_ref[...]


def _ada(cc, w_ada, b_ada):
    rows, d = cc.shape
    n = w_ada.shape[1]
    tn = 1024
    return pl.pallas_call(
        _ada_body,
        grid=(n // tn,),
        in_specs=[pl.BlockSpec((rows, d), lambda j: (0, 0)),
                  pl.BlockSpec((d, tn), lambda j: (0, j)),
                  pl.BlockSpec((1, tn), lambda j: (0, j))],
        out_specs=pl.BlockSpec((rows, tn), lambda j: (0, j)),
        out_shape=jax.ShapeDtypeStruct((rows, n), F32),
        compiler_params=_cparams("arbitrary"),
        name="ada",
    )(cc, w_ada, b_ada.reshape(1, n))


def _inproj_lat_body(x_ref, pe_ref, mod_ref, g_ref, w_ref, o_ref, wb_ref):
    @pl.when((pl.program_id(0) == 0) & (pl.program_id(1) == 0))
    def _():
        wb_ref[...] = w_ref[...].astype(BF16)

    x = x_ref[...] + pe_ref[...]
    h = _rms(x, g_ref[...]) * (1.0 + mod_ref[1:2, :]) + mod_ref[0:1, :]
    o_ref[...] = _dot(h.astype(BF16), wb_ref[...]).astype(o_ref.dtype)


def _inproj_lat(x3, pe, mod3, g, w, tm):
    b, l, d = x3.shape
    n_out = w.shape[1]
    return pl.pallas_call(
        _inproj_lat_body,
        grid=(l // tm, b),
        in_specs=[pl.BlockSpec((None, tm, d), lambda i, bi: (bi, i, 0)),
                  pl.BlockSpec((tm, d), lambda i, bi: (i, 0)),
                  pl.BlockSpec((None, N_MOD, d), lambda i, bi: (bi, 0, 0)),
                  pl.BlockSpec((1, d), lambda i, bi: (0, 0)),
                  pl.BlockSpec((d, n_out), lambda i, bi: (0, 0), pipeline_mode=pl.Buffered(1))],
        out_specs=pl.BlockSpec((None, tm, n_out), lambda i, bi: (bi, i, 0)),
        out_shape=jax.ShapeDtypeStruct((b, l, n_out), BF16),
        scratch_shapes=[pltpu.VMEM((d, n_out), BF16)],
        compiler_params=_cparams("arbitrary", "arbitrary"),
        name="inproj_pe",
    )(x3, pe, mod3, g.reshape(1, d), w)


def _inproj_ctx_body(x_ref, mod_ref, g_ref, w_ref, o_ref):
    h = _rms(x_ref[...], g_ref[...]) * (1.0 + mod_ref[1:2, :]) + mod_ref[0:1, :]
    o_ref[...] = _dot(h.astype(BF16), w_ref[...].astype(BF16))


def _inproj_ctx(x2, mod3, mod_row, g, w, col_block, n_out, tm):
    r, d = x2.shape
    return pl.pallas_call(
        _inproj_ctx_body,
        grid=(r // tm,),
        in_specs=[pl.BlockSpec((tm, d), lambda i: (i, 0)),
                  pl.BlockSpec((None, N_MOD, d), lambda i: (mod_row, 0, 0)),
                  pl.BlockSpec((1, d), lambda i: (0, 0)),
                  pl.BlockSpec((d, n_out), lambda i: (0, col_block))],
        out_specs=pl.BlockSpec((tm, n_out), lambda i: (i, 0)),
        out_shape=jax.ShapeDtypeStruct((r, n_out), F32),
        compiler_params=_cparams("arbitrary"),
        name="inproj_ctx",
    )(x2, mod3, g.reshape(1, d), w)


def _rglru_body(*refs, with_gate):
    if with_gate:
        (u_ref, gate_ref, cw_ref, cb_ref, w_ref, bias_ref, sp_ref, h0f_ref, h0b_ref,
         y_ref, hf_end_ref, hb_end_ref, af_ref, bf_ref, ab_ref, bb_ref) = refs
    else:
        (u_ref, cw_ref, cb_ref, w_ref, bias_ref, sp_ref, h0f_ref, h0b_ref,
         hf_end_ref, hb_end_ref, af_ref, bf_ref, ab_ref, bb_ref) = refs
    n, c = u_ref.shape
    u = u_ref[...].astype(F32)
    v = cb_ref[...] + cw_ref[1:2, :] * _shift_rows(u, -1) + cw_ref[0:1, :] * _shift_rows(u, -2)
    v = v + cw_ref[2:3, :] * u + cw_ref[3:4, :] * _shift_rows(u, 1)
    z = _dot(v.astype(BF16), w_ref[...]) + bias_ref[...]
    sub = lax.broadcasted_iota(jnp.int32, (n, c), 0) % SUBLANES

    def local_scan(k, reverse, a_ref, b_ref):
        r = _sigmoid(z[:, (2 * k) * c:(2 * k + 1) * c])
        i = _sigmoid(z[:, (2 * k + 1) * c:(2 * k + 2) * c])
        log_a = (-LRU_C) * r * sp_ref[k:k + 1, :]
        a = jnp.exp(log_a)
        b = jnp.sqrt(1.0 - a * a) * (i * v)
        for s in (1, 2, 4):
            ok = (sub < SUBLANES - s) if reverse else (sub >= s)
            shift = SUBLANES - s if reverse else s
            b = a * jnp.where(ok, _tile_rot(b, shift), 0.0) + b
            a = a * jnp.where(ok, _tile_rot(a, shift), 1.0)
        a_ref[...] = a
        b_ref[...] = b

    local_scan(0, False, af_ref, bf_ref)
    local_scan(1, True, ab_ref, bb_ref)

    n_tiles = n // SUBLANES

    def carry_step(q, carry):
        cf, cb = carry
        rf = pl.ds(pl.multiple_of(q * SUBLANES, SUBLANES), SUBLANES)
        rb = pl.ds(pl.multiple_of((n_tiles - 1 - q) * SUBLANES, SUBLANES), SUBLANES)
        af, bf, ab, bb = af_ref[rf, :], bf_ref[rf, :], ab_ref[rb, :], bb_ref[rb, :]
        bf_ref[rf, :] = af * cf + bf
        bb_ref[rb, :] = ab * cb + bb
        last = lambda v: jnp.broadcast_to(v[SUBLANES - 1:SUBLANES, :], (SUBLANES, c))
        first = lambda v: jnp.broadcast_to(v[0:1, :], (SUBLANES, c))
        return last(af) * cf + last(bf), first(ab) * cb + first(bb)

    cf0 = jnp.broadcast_to(h0f_ref[...], (SUBLANES, c))
    cb0 = jnp.broadcast_to(h0b_ref[...], (SUBLANES, c))
    cf, cb = lax.fori_loop(0, n_tiles, carry_step, (cf0, cb0), unroll=4)
    hf_end_ref[...] = cf[0:1, :]
    hb_end_ref[...] = cb[0:1, :]
    if with_gate:
        y_ref[...] = jax.nn.gelu(gate_ref[...].astype(F32), approximate=True) * (bf_ref[...] + bb_ref[...])


def _rglru(p3, u_blk0, gate_blk0, cw, cb, w_gates, bias, sp, h0f, h0b, with_gate, ch):
    b, l, _ = p3.shape
    d_rnn = cw.shape[1]
    nh = d_rnn // ch
    in_specs = [pl.BlockSpec((None, l, ch), lambda bi, h: (bi, 0, u_blk0 + h))]
    args = [p3]
    if with_gate:
        in_specs.append(pl.BlockSpec((None, l, ch), lambda bi, h: (bi, 0, gate_blk0 + h)))
        args.append(p3)
    in_specs += [pl.BlockSpec((4, ch), lambda bi, h: (0, h)),
                 pl.BlockSpec((1, ch), lambda bi, h: (0, h)),
                 pl.BlockSpec((None, ch, 4 * ch), lambda bi, h: (h, 0, 0)),
                 pl.BlockSpec((None, 1, 4 * ch), lambda bi, h: (h, 0, 0)),
                 pl.BlockSpec((2, ch), lambda bi, h: (0, h)),
                 pl.BlockSpec((None, 1, ch), lambda bi, h: (bi, 0, h)),
                 pl.BlockSpec((None, 1, ch), lambda bi, h: (bi, 0, h))]
    args += [cw, cb, w_gates, bias, sp, h0f, h0b]
    end_spec = pl.BlockSpec((None, 1, ch), lambda bi, h: (bi, 0, h))
    end_shape = jax.ShapeDtypeStruct((b, 1, d_rnn), F32)
    out_specs = [end_spec, end_spec]
    out_shape = [end_shape, end_shape]
    if with_gate:
        out_specs = [pl.BlockSpec((None, l, ch), lambda bi, h: (bi, 0, h))] + out_specs
        out_shape = [jax.ShapeDtypeStruct((b, l, d_rnn), F32)] + out_shape
    return pl.pallas_call(
        functools.partial(_rglru_body, with_gate=with_gate),
        grid=(b, nh),
        in_specs=in_specs,
        out_specs=out_specs,
        out_shape=out_shape,
        scratch_shapes=[pltpu.VMEM((l, ch), F32)] * 4,
        compiler_params=_cparams("arbitrary", "arbitrary"),
        name="rglru_lat" if with_gate else "rglru_ctx",
    )(*args)


CONV_BLOCK = 512


def _filt_body(z_ref, w1_ref, b1_ref, f1_ref, w2_ref, b2_ref, f2_ref, w3f_ref, w3b_ref, b3f_ref, b3b_ref,
               dec_ref, cu_ref, su_ref, sgn_ref, hr_ref, hi_ref, hdn_ref):
    n = z_ref.shape[0]
    blk = cu_ref.shape[0]
    nblk = n // blk
    ch = hr_ref.shape[2]

    @pl.when(pl.program_id(0) == 0)
    def _():
        hid = jnp.sin(f1_ref[...] * (_dot3(z_ref[...], w1_ref[...]) + b1_ref[...]))
        hdn_ref[...] = jnp.sin(f2_ref[...] * (_dot3(hid, w2_ref[...]) + b2_ref[...]))

    hdn = hdn_ref[...]
    decay = jnp.exp(-z_ref[:, 0:1] * dec_ref[...])
    kf = (_dot3(hdn, w3f_ref[...]) + b3f_ref[...]) * decay
    kb = (_dot3(hdn, w3b_ref[...]) + b3b_ref[...]) * decay
    row = lax.broadcasted_iota(jnp.int32, (n, ch), 0)
    kb = jnp.where(row == 0, 0.0, kb)
    norm = jnp.sum(jnp.abs(kf) + jnp.abs(kb), axis=0, keepdims=True)
    scale = (2.0 / (2 * blk)) / norm
    sgn = sgn_ref[...]

    p, q, first = [], [], []
    for c in range(nblk):
        taps = jnp.concatenate([kf[c * blk:(c + 1) * blk], kb[c * blk:(c + 1) * blk]], axis=1).astype(BF16)
        p.append(_dot(cu_ref[...], taps))
        q.append(_dot(su_ref[...], taps))
        first.append(taps[0:1, :].astype(F32))
    fwd = lambda a: a[:, :ch]
    bwd = lambda a: a[:, ch:]
    out = lambda a: (a * scale).astype(hr_ref.dtype)
    hr_ref[nblk - 1] = out(fwd(p[0]) + bwd(p[0]))
    hi_ref[nblk - 1] = out(bwd(q[0]) - fwd(q[0]))
    for c in range(1, nblk):
        hr_ref[nblk - 1 + c] = out(fwd(p[c]) + sgn * fwd(q[c - 1]))
        hi_ref[nblk - 1 + c] = out(sgn * (fwd(p[c - 1]) - fwd(first[c - 1])) - fwd(q[c]))
        hr_ref[nblk - 1 - c] = out(bwd(p[c]) + sgn * bwd(q[c - 1]))
        hi_ref[nblk - 1 - c] = out(bwd(q[c]) - sgn * (bwd(p[c - 1]) - bwd(first[c - 1])))


def _filters(z, w1, b1, f1, w2, b2, f2, w3, b3, deltas, cu, su, sgn, d_h, order, ch):
    n, fe = z.shape
    hid = w2.shape[0]
    blk = cu.shape[0]
    nh = 2 * (n // blk) - 1
    per_order = d_h // ch
    ncol = order * per_order
    const = lambda shape: pl.BlockSpec(shape, lambda g: (0, 0))
    hspec = pl.BlockSpec((nh, blk, ch), lambda g: (0, 0, g))
    return pl.pallas_call(
        _filt_body,
        grid=(ncol,),
        in_specs=[const((n, fe)), const((fe, hid)), const((1, hid)), const((1, hid)),
                  const((hid, hid)), const((1, hid)), const((1, hid)),
                  pl.BlockSpec((hid, ch), lambda g: (0, g)),
                  pl.BlockSpec((hid, ch), lambda g: (0, ncol + g)),
                  pl.BlockSpec((1, ch), lambda g: (0, g)),
                  pl.BlockSpec((1, ch), lambda g: (0, ncol + g)),
                  pl.BlockSpec((1, ch), lambda g: (0, g % per_order)),
                  const((blk, blk)), const((blk, blk)), const((blk, 1))],
        out_specs=[hspec, hspec],
        out_shape=[jax.ShapeDtypeStruct((nh, blk, order * d_h), BF16)] * 2,
        scratch_shapes=[pltpu.VMEM((n, hid), F32)],
        compiler_params=_cparams("arbitrary"),
        name="hyena_filters",
    )(z, w1, b1, f1, w2, b2, f2, w3, w3, b3, b3, deltas, cu, su, sgn)


def _hyena_body(v_ref, x1_ref, x2_ref, wv_ref, w1_ref, w2_ref, fb_ref, c3_ref, s3_ref,
                h0r_ref, h0i_ref, h1r_ref, h1i_ref, o_ref, z_ref, zb_ref, p_ref, q_ref, yr_ref, yi_ref):
    n = o_ref.shape[0]
    blk = c3_ref.shape[0]
    nblk = n // blk
    halo = 2 * SUBLANES
    sub = 32

    def conv3(src_ref, w_ref, r0):
        lo, hi = max(r0 - halo, 0), min(r0 + blk + halo, n)
        win = src_ref[lo:hi, :].astype(F32)
        size = hi - lo
        row = lax.broadcasted_iota(jnp.int32, win.shape, 0)
        prev = pltpu.roll(win, 1, 0)
        nxt = pltpu.roll(win, size - 1, 0)
        if lo == 0:
            prev = jnp.where(row == 0, 0.0, prev)
        if hi == n:
            nxt = jnp.where(row == size - 1, 0.0, nxt)
        out = w_ref[0:1, :] * prev + w_ref[1:2, :] * win + w_ref[2:3, :] * nxt
        return out[r0 - lo:r0 - lo + blk, :]

    def long_conv(hr_ref, hi_ref, fb, gate_ref, gate_w_ref, dst_ref):
        for i in range(nblk):
            rows = slice(i * blk, (i + 1) * blk)
            p_ref[rows, :] = _dot(c3_ref[...], zb_ref[rows, :]).astype(BF16)
            q_ref[rows, :] = _dot(s3_ref[...], zb_ref[rows, :]).astype(BF16)

        for j in range(nblk):
            for r0 in range(0, blk, sub):
                yr = yi = None
                for i in range(nblk):
                    rows = slice(i * blk + r0, i * blk + r0 + sub)
                    p, q = p_ref[rows, :], q_ref[rows, :]
                    gr = hr_ref[nblk - 1 + j - i, r0:r0 + sub, :]
                    gi = hi_ref[nblk - 1 + j - i, r0:r0 + sub, :]
                    tr, ti = p * gr + q * gi, q * gr - p * gi
                    yr, yi = (tr, ti) if yr is None else (yr + tr, yi + ti)
                yr_ref[j * blk + r0:j * blk + r0 + sub, :] = yr
                yi_ref[j * blk + r0:j * blk + r0 + sub, :] = yi
            rows = slice(j * blk, (j + 1) * blk)
            y = _dot(c3_ref[...], yr_ref[rows, :]) + _dot(s3_ref[...], yi_ref[rows, :])
            dst_ref[rows, :] = conv3(gate_ref, gate_w_ref, j * blk) * (y + z_ref[rows, :] * fb)

    for r0 in range(0, n, blk):
        z_ref[r0:r0 + blk, :] = conv3(v_ref, wv_ref, r0)
    zb_ref[...] = z_ref[...].astype(BF16)
    long_conv(h0r_ref, h0i_ref, fb_ref[0:1, :], x1_ref, w1_ref, z_ref)
    zb_ref[...] = z_ref[...].astype(BF16)
    long_conv(h1r_ref, h1i_ref, fb_ref[1:2, :], x2_ref, w2_ref, o_ref)


def _hyena(p3, col_blk0, conv_w, fbias, c3, s3, hr, hi, d_h, ch):
    b, n, _ = p3.shape
    nh, blk, _ = hr.shape
    per = d_h // ch
    zspec = lambda k: pl.BlockSpec((None, n, ch), lambda h, bi: (bi, 0, col_blk0 + k * per + h))
    wspec = lambda k: pl.BlockSpec((3, ch), lambda h, bi: (0, k * per + h))
    hspec = lambda o: pl.BlockSpec((nh, blk, ch), lambda h, bi: (0, 0, o * per + h), pipeline_mode=pl.Buffered(1))
    const = pl.BlockSpec((blk, blk), lambda h, bi: (0, 0))
    return pl.pallas_call(
        _hyena_body,
        grid=(per, b),
        in_specs=[zspec(0), zspec(1), zspec(2), wspec(0), wspec(1), wspec(2),
                  pl.BlockSpec((2, ch), lambda h, bi: (0, h)), const, const,
                  hspec(0), hspec(0), hspec(1), hspec(1)],
        out_specs=pl.BlockSpec((None, n, ch), lambda h, bi: (bi, 0, h)),
        out_shape=jax.ShapeDtypeStruct((b, n, d_h), F32),
        scratch_shapes=[pltpu.VMEM((n, ch), F32), pltpu.VMEM((n, ch), BF16),
                        pltpu.VMEM((n, ch), BF16), pltpu.VMEM((n, ch), BF16),
                        pltpu.VMEM((n, ch), BF16), pltpu.VMEM((n, ch), BF16)],
        compiler_params=_cparams("arbitrary", "arbitrary"),
        name="hyena_mix",
    )(p3, p3, p3, conv_w, conv_w, conv_w, fbias, c3, s3, hr, hi, hr, hi)


def _merge_body(ya_ref, yb_ref, x_ref, pe_ref, mod_ref, ga_ref, gb_ref, wo_ref, g2_ref, wr_ref, br_ref,
                x1_ref, h2_ref, rt_ref, cnt_ref, prow_ref, wob_ref, tri_ref):
    @pl.when((pl.program_id(0) == 0) & (pl.program_id(1) == 0))
    def _():
        wob_ref[...] = wo_ref[...].astype(BF16)
        earlier = lax.broadcasted_iota(jnp.int32, tri_ref.shape, 1) < lax.broadcasted_iota(jnp.int32, tri_ref.shape, 0)
        tri_ref[...] = jnp.where(earlier, 1.0, 0.0).astype(BF16)

    da = ya_ref.shape[1]
    na = _rms(ya_ref[...], ga_ref[...]).astype(BF16)
    nb = _rms(yb_ref[...], gb_ref[...]).astype(BF16)
    y = _dot(na, wob_ref[0:da, :]) + _dot(nb, wob_ref[da:, :])
    _route(y, x_ref, pe_ref, mod_ref, g2_ref, wr_ref, br_ref, x1_ref, h2_ref, rt_ref, cnt_ref, prow_ref, tri_ref)


def _route(y, x_ref, pe_ref, mod_ref, g2_ref, wr_ref, br_ref, x1_ref, h2_ref, rt_ref, cnt_ref, prow_ref, tri_ref):
    x1 = x_ref[...] + pe_ref[...] + mod_ref[2:3, :] * y
    x1_ref[...] = x1
    h2 = _rms(x1, g2_ref[...]) * (1.0 + mod_ref[4:5, :]) + mod_ref[3:4, :]
    h2_ref[...] = h2.astype(BF16)

    logits = _dot3(h2, wr_ref[...]) + br_ref[...]
    lane = lax.broadcasted_iota(jnp.int32, logits.shape, 1).astype(F32)
    neg = -jnp.inf
    big = jnp.float32(1 << 20)
    gl = jnp.where((lane >= N_EXPERTS) & (lane < N_EXPERTS + N_GROUPS), logits, neg)
    gmax = jnp.max(gl, axis=1, keepdims=True)
    g_p = 1.0 / jnp.sum(jnp.exp(gl - gmax), axis=1, keepdims=True)
    g_i = jnp.min(jnp.where(gl == gmax, lane, big), axis=1, keepdims=True) - N_EXPERTS
    lo = g_i * EXPERTS_PER_GROUP
    el = jnp.where((lane >= lo) & (lane < lo + EXPERTS_PER_GROUP), logits, neg)
    m1 = jnp.max(el, axis=1, keepdims=True)
    i1 = jnp.min(jnp.where(el == m1, lane, big), axis=1, keepdims=True)
    el2 = jnp.where(lane == i1, neg, el)
    m2 = jnp.max(el2, axis=1, keepdims=True)
    i2 = jnp.min(jnp.where(el2 == m2, lane, big), axis=1, keepdims=True)
    e2 = jnp.exp(m2 - m1)
    w1 = g_p / (1.0 + e2)
    w2 = g_p * e2 / (1.0 + e2)
    o1 = jnp.where(lane == i1, 1.0, 0.0)
    o2 = jnp.where(lane == i2, 1.0, 0.0)
    nsub = cnt_ref.shape[0]
    sub = o1.shape[0] // nsub
    sq = lambda shape, d: lax.broadcasted_iota(jnp.int32, shape, d)
    earlier = tri_ref[...]
    below = jnp.where(sq((LANES, LANES), 0) < sq((LANES, LANES), 1), 1.0, 0.0).astype(BF16)
    pick_row = jnp.where(sq((SUBLANES, LANES), 0) == sq((SUBLANES, LANES), 1), 1.0, 0.0).astype(BF16)
    row_id = sq((SUBLANES, sub), 0)
    pos1, pos2 = [], []
    for k in range(nsub):
        a1, a2 = o1[k * sub:(k + 1) * sub], o2[k * sub:(k + 1) * sub]
        both = a1 + a2
        count = jnp.sum(both, axis=0, keepdims=True)
        cnt_ref[k] = jnp.broadcast_to(count, cnt_ref.shape[1:])
        runs = jnp.broadcast_to(jnp.floor((count + (SEG - 1)) / SEG), (SUBLANES, LANES))
        start = _dot(runs.astype(BF16), below)[0:1, :] * SEG
        before = _dot(earlier, both.astype(BF16)) + start
        p1 = jnp.sum(a1 * before, axis=1, keepdims=True)
        p2 = jnp.sum(a2 * before, axis=1, keepdims=True)
        pos1.append(p1)
        pos2.append(p2)
        lane_k = sq((sub, LANES), 1).astype(F32)
        hi1, hi2 = jnp.floor(p1 / SLOT_SPLIT), jnp.floor(p2 / SLOT_SPLIT)
        parts = (jnp.where(lane_k == 0.0, hi1, 0.0) + jnp.where(lane_k == 1.0, p1 - SLOT_SPLIT * hi1, 0.0)
                 + jnp.where(lane_k == 2.0, hi2, 0.0) + jnp.where(lane_k == 3.0, p2 - SLOT_SPLIT * hi2, 0.0))
        rows = lax.dot_general(pick_row, parts.astype(BF16), (((1,), (1,)), ((), ())), preferred_element_type=F32)
        q1 = rows[0:1, :] * SLOT_SPLIT + rows[1:2, :]
        q2 = rows[2:3, :] * SLOT_SPLIT + rows[3:4, :]
        prow_ref[k] = jnp.where(row_id == 0, q1, jnp.where(row_id == 1, q2, 0.0))
    p1 = jnp.concatenate(pos1, axis=0)
    p2 = jnp.concatenate(pos2, axis=0)
    rt_ref[...] = (jnp.where(lane == 0.0, i1, 0.0) + jnp.where(lane == 1.0, i2, 0.0)
                   + jnp.where(lane == 2.0, w1, 0.0) + jnp.where(lane == 3.0, w2, 0.0)
                   + jnp.where(lane == 4.0, p1, 0.0) + jnp.where(lane == 5.0, p2, 0.0))


def _merge(ya, yb, x3, pe, mod3, ga, gb, w_out, g2n, w_r, b_r, tm, nsub):
    b, n, d = x3.shape
    da, db = ya.shape[2], yb.shape[2]
    per = n // tm
    tok = lambda w: pl.BlockSpec((None, tm, w), lambda i, bi: (bi, i, 0))
    const = lambda shape: pl.BlockSpec(shape, lambda i, bi: (0, 0))
    blk = lambda w: pl.BlockSpec((nsub, SUBLANES, w), lambda i, bi: (bi * per + i, 0, 0))
    return pl.pallas_call(
        _merge_body,
        grid=(per, b),
        in_specs=[tok(da), tok(db), tok(d), pl.BlockSpec((tm, d), lambda i, bi: (i, 0)),
                  pl.BlockSpec((None, N_MOD, d), lambda i, bi: (bi, 0, 0)),
                  const((1, da)), const((1, db)), const((da + db, d)), const((1, d)),
                  const((d, LANES)), const((1, LANES))],
        out_specs=[tok(d), tok(d), tok(LANES), blk(LANES), blk(tm // nsub)],
        out_shape=[jax.ShapeDtypeStruct((b, n, d), F32), jax.ShapeDtypeStruct((b, n, d), BF16),
                   jax.ShapeDtypeStruct((b, n, LANES), F32),
                   jax.ShapeDtypeStruct((b * per * nsub, SUBLANES, LANES), F32),
                   jax.ShapeDtypeStruct((b * per * nsub, SUBLANES, tm // nsub), F32)],
        scratch_shapes=[pltpu.VMEM((da + db, d), BF16), pltpu.VMEM((tm // nsub, tm // nsub), BF16)],
        compiler_params=_cparams("arbitrary", "arbitrary"),
        name="merge_route",
    )(ya, yb, x3, pe, mod3, ga.reshape(1, da), gb.reshape(1, db), w_out, g2n.reshape(1, d), w_r, b_r)


def _segment_copy(local_ref, lrow, global_ref, grow, rows, sem, to_global):
    lrows = pl.ds(pl.multiple_of(lrow, SEG), rows)
    grows = pl.ds(pl.multiple_of(grow, SEG), rows)
    if to_global:
        return pltpu.make_async_copy(local_ref.at[lrows, :], global_ref.at[grows, :], sem)
    return pltpu.make_async_copy(global_ref.at[grows, :], local_ref.at[lrows, :], sem)


def _start_segments(tab_ref, blk, local_ref, global_ref, sem, to_global):
    def per_expert(e, carry):
        k = (blk * N_EXPERTS + e) * 3
        ls, gs, nchunk = tab_ref[k], tab_ref[k + 1], tab_ref[k + 2]

        def per_chunk(ci, carry2):
            _segment_copy(local_ref, ls + ci * SEG, global_ref, gs + ci * SEG, SEG, sem, to_global).start()
            return carry2

        return lax.fori_loop(0, nchunk, per_chunk, carry)

    lax.fori_loop(0, N_EXPERTS, per_expert, 0, unroll=2)


def _wait_segments(rows, local_ref, global_ref, sem, to_global):
    chunks = rows // SEG
    nbits = (local_ref.shape[0] // SEG).bit_length()
    for k in range(nbits):
        @pl.when(((chunks >> k) & 1) == 1)
        def _():
            _segment_copy(local_ref, 0, global_ref, 0, SEG << k, sem, to_global).wait()


def _dispatch_body(tab_ref, ends_ref, ltot_ref, prow_ref, h2_ref, xs_ref, loc_ref, zero_ref, sem, zsem):
    blk = pl.program_id(0)
    nblk = pl.num_programs(0)
    tb = h2_ref.shape[0]
    lp = loc_ref.shape[1]
    tile = zero_ref.shape[0]
    slot = blk % 2

    @pl.when(blk == 0)
    def _():
        zero_ref[...] = jnp.zeros_like(zero_ref)

        def zero_fill(first_row, rows):
            dst = xs_ref.at[pl.ds(pl.multiple_of(first_row, SEG), rows), :]
            return pltpu.make_async_copy(zero_ref.at[pl.ds(0, rows), :], dst, zsem)

        pad = 64

        def fill(e, carry, wait):
            end, real = ends_ref[e], ends_ref[N_EXPERTS + e]
            first = real // pad * pad

            def one(j, carry2):
                cp = zero_fill(first + j * pad, pad)
                cp.wait() if wait else cp.start()
                return carry2

            return lax.fori_loop(0, (end - first) // pad, one, carry)

        lax.fori_loop(0, N_EXPERTS, functools.partial(fill, wait=False), 0)
        used = ends_ref[N_EXPERTS - 1]
        spare = (xs_ref.shape[0] - used) // tile

        def fill_spare(j, carry, wait):
            cp = zero_fill(used + j * tile, tile)
            cp.wait() if wait else cp.start()
            return carry

        lax.fori_loop(0, spare, functools.partial(fill_spare, wait=False), 0)
        lax.fori_loop(0, N_EXPERTS, functools.partial(fill, wait=True), 0)
        lax.fori_loop(0, spare, functools.partial(fill_spare, wait=True), 0)

    pos = lax.broadcasted_iota(jnp.int32, (lp, tb), 0).astype(F32)
    onehot = jnp.where((pos == prow_ref[0:1, :]) | (pos == prow_ref[1:2, :]), 1.0, 0.0).astype(BF16)
    loc_ref[slot] = _dot(onehot, h2_ref[...]).astype(BF16)

    _start_segments(tab_ref, blk, loc_ref.at[slot], xs_ref, sem.at[slot], True)

    @pl.when(blk > 0)
    def _():
        _wait_segments(ltot_ref[blk - 1], loc_ref.at[1 - slot], xs_ref, sem.at[1 - slot], True)

    @pl.when(blk == nblk - 1)
    def _():
        _wait_segments(ltot_ref[blk], loc_ref.at[slot], xs_ref, sem.at[slot], True)


def _dispatch(tab, ends, ltot, prow, h2, slots, tb, lp, tile):
    t, d = h2.shape
    return pl.pallas_call(
        _dispatch_body,
        grid_spec=pltpu.PrefetchScalarGridSpec(
            num_scalar_prefetch=3,
            grid=(t // tb,),
            in_specs=[pl.BlockSpec((None, SUBLANES, tb), lambda i, *_: (i, 0, 0)),
                      pl.BlockSpec((tb, d), lambda i, *_: (i, 0))],
            out_specs=pl.BlockSpec(memory_space=pl.ANY),
            scratch_shapes=[pltpu.VMEM((2, lp, d), BF16), pltpu.VMEM((tile, d), BF16),
                            pltpu.SemaphoreType.DMA((2,)), pltpu.SemaphoreType.DMA]),
        out_shape=jax.ShapeDtypeStruct((slots, d), BF16),
        compiler_params=_cparams("arbitrary"),
        name="moe_dispatch",
    )(tab, ends, ltot, prow, h2)


def _ffn_body(te_ref, nu_ref, nxt_ref, par_ref, rows_ref, xs_ref, wg_hbm, wu_hbm, wd_hbm, ys_ref,
              wg_buf, wu_buf, wd_buf, wgb_ref, wub_ref, wdb_ref, sem):
    i = pl.program_id(0)
    e = te_ref[i]
    first = (i == 0) | (e != te_ref[jnp.maximum(i - 1, 0)])
    active = i < nu_ref[0]

    def fetch(expert, slot):
        return [pltpu.make_async_copy(w.at[expert], buf.at[slot], sem.at[slot])
                for w, buf in ((wg_hbm, wg_buf), (wu_hbm, wu_buf), (wd_hbm, wd_buf))]

    @pl.when(i == 0)
    def _():
        for cp in fetch(e, par_ref[e]):
            cp.start()

    @pl.when(first & active)
    def _():
        slot = par_ref[e]
        for cp in fetch(e, slot):
            cp.wait()
        nxt = nxt_ref[e]

        @pl.when(nxt >= 0)
        def _():
            for cp in fetch(nxt, 1 - slot):
                cp.start()

        wgb_ref[...] = wg_buf[slot].astype(BF16)
        wub_ref[...] = wu_buf[slot].astype(BF16)
        wdb_ref[...] = wd_buf[slot].astype(BF16)

    def expert(rows):
        xb = xs_ref[rows, :]
        act = _dot(xb, wgb_ref[...])
        act = act * _sigmoid(act) * _dot(xb, wub_ref[...])
        ys_ref[rows, :] = _dot(act.astype(BF16), wdb_ref[...]).astype(BF16)

    parts = 4
    quarter = xs_ref.shape[0] // parts
    filled = (rows_ref[i] + quarter - 1) // quarter
    for q in range(1, parts + 1):
        @pl.when(active & (filled == q))
        def _():
            expert(slice(0, q * quarter))
            if q < parts:
                ys_ref[q * quarter:, :] = jnp.zeros(((parts - q) * quarter, ys_ref.shape[1]), BF16)


def _ffn(tile_expert, n_used, nxt, par, tile_rows, xs, w_gate, w_up, w_down, tile):
    slots, d = xs.shape
    _, _, de = w_gate.shape
    last = lambda i, te, nu, *_: (jnp.minimum(i, nu[0] - 1), 0)
    hbm = pl.BlockSpec(memory_space=pl.ANY)
    return pl.pallas_call(
        _ffn_body,
        grid_spec=pltpu.PrefetchScalarGridSpec(
            num_scalar_prefetch=5,
            grid=(slots // tile,),
            in_specs=[pl.BlockSpec((tile, d), last), hbm, hbm, hbm],
            out_specs=pl.BlockSpec((tile, d), last),
            scratch_shapes=[pltpu.VMEM((2, d, de), F32), pltpu.VMEM((2, d, de), F32), pltpu.VMEM((2, de, d), F32),
                            pltpu.VMEM((d, de), BF16), pltpu.VMEM((d, de), BF16), pltpu.VMEM((de, d), BF16),
                            pltpu.SemaphoreType.DMA((2,))]),
        out_shape=jax.ShapeDtypeStruct((slots, d), BF16),
        input_output_aliases={5: 0},
        compiler_params=_cparams("arbitrary"),
        name="moe_ffn",
    )(tile_expert, n_used, nxt, par, tile_rows, xs, w_gate, w_up, w_down)


def _combine_body(tab_ref, ltot_ref, ys_ref, rt_ref, x1_ref, mod_ref, fg_ref, o_ref, loc_ref, sem):
    blk = pl.program_id(0)
    nblk = pl.num_programs(0)
    _, lp, d = loc_ref.shape
    slot = blk % 2

    @pl.when(blk == 0)
    def _():
        _start_segments(tab_ref, blk, loc_ref.at[slot], ys_ref, sem.at[slot], False)

    @pl.when(blk + 1 < nblk)
    def _():
        _start_segments(tab_ref, blk + 1, loc_ref.at[1 - slot], ys_ref, sem.at[1 - slot], False)

    rt = rt_ref[...]
    lane = lax.broadcasted_iota(jnp.int32, rt.shape, 1)
    col = lambda k: jnp.sum(jnp.where(lane == k, rt, 0.0), axis=1, keepdims=True)
    w1, w2, p1, p2 = col(2), col(3), col(4), col(5)
    pos = lax.broadcasted_iota(jnp.int32, (rt.shape[0], lp), 1).astype(F32)
    weights = (jnp.where(pos == p1, w1, 0.0) + jnp.where(pos == p2, w2, 0.0)).astype(BF16)
    _wait_segments(ltot_ref[blk], loc_ref.at[slot], ys_ref, sem.at[slot], False)

    def clear(ci, carry):
        loc_ref[slot, pl.ds(pl.multiple_of(ci * SEG, SEG), SEG), :] = jnp.zeros((SEG, d), BF16)
        return carry

    lax.fori_loop(ltot_ref[blk] // SEG, lp // SEG, clear, 0)
    moe = _dot(weights, loc_ref[slot])
    o_ref[...] = _rms(x1_ref[...] + mod_ref[5:6, :] * moe, fg_ref[...])


def _combine(tab, ltot, ys, rt2, x1, mod3, final_g, tb, lp, per_batch):
    t, d = x1.shape
    return pl.pallas_call(
        _combine_body,
        grid_spec=pltpu.PrefetchScalarGridSpec(
            num_scalar_prefetch=2,
            grid=(t // tb,),
            in_specs=[pl.BlockSpec(memory_space=pl.ANY),
                      pl.BlockSpec((tb, LANES), lambda i, *_: (i, 0)),
                      pl.BlockSpec((tb, d), lambda i, *_: (i, 0)),
                      pl.BlockSpec((None, N_MOD, d), lambda i, *_: (i // per_batch, 0, 0)),
                      pl.BlockSpec((1, d), lambda i, *_: (0, 0))],
            out_specs=pl.BlockSpec((tb, d), lambda i, *_: (i, 0)),
            scratch_shapes=[pltpu.VMEM((2, lp, d), BF16), pltpu.SemaphoreType.DMA((2,))]),
        out_shape=jax.ShapeDtypeStruct((t, d), F32),
        compiler_params=_cparams("arbitrary"),
        name="moe_combine",
    )(tab, ltot, ys, rt2, x1, mod3, final_g.reshape(1, d))


def _moe_layout(cnt_blocks, tile, n_tiles):
    lcnt = (cnt_blocks + SEG - 1) // SEG * SEG
    lstart = jnp.cumsum(lcnt, axis=1) - lcnt
    ltot = jnp.sum(lcnt, axis=1)
    per_expert = jnp.sum(lcnt, axis=0)
    tiles_per = (per_expert + tile - 1) // tile
    tile_ends = jnp.cumsum(tiles_per)
    row_ends = tile_ends * tile
    goff = row_ends - tiles_per * tile
    gstart = goff[None, :] + jnp.cumsum(lcnt, axis=0) - lcnt
    tab = jnp.stack([lstart, gstart, lcnt // SEG], axis=-1).reshape(-1).astype(jnp.int32)
    n_used = tile_ends[-1:]
    tile_ids = jnp.arange(n_tiles, dtype=jnp.int32)
    tile_expert = jnp.sum((tile_ends[None, :] <= jnp.minimum(tile_ids, n_used - 1)[:, None]).astype(jnp.int32), axis=1)
    ids = jnp.arange(N_EXPERTS, dtype=jnp.int32)
    used = tiles_per > 0
    par = (jnp.cumsum(used) - used) % 2
    later = jnp.where(used[None, :] & (ids[None, :] > ids[:, None]), ids[None, :], N_EXPERTS)
    nxt = jnp.min(later, axis=1)
    nxt = jnp.where(nxt == N_EXPERTS, -1, nxt)
    seg_ends = goff + per_expert
    ends = jnp.concatenate([row_ends, seg_ends])
    tile_rows = jnp.clip(seg_ends[tile_expert] - tile_ids * tile, 0, tile)
    return (tab, ends.astype(jnp.int32), ltot.astype(jnp.int32), tile_expert.astype(jnp.int32),
            n_used.astype(jnp.int32), nxt.astype(jnp.int32), par.astype(jnp.int32), tile_rows.astype(jnp.int32))


def _sincos_table(rows, cols, dim):
    quarter = dim // 4
    omega = 1.0 / (10000.0 ** (np.arange(quarter, dtype=np.float64) / quarter))
    ang_r = np.arange(rows, dtype=np.float64)[:, None] * omega
    ang_c = np.arange(cols, dtype=np.float64)[:, None] * omega
    emb_r = np.concatenate([np.sin(ang_r), np.cos(ang_r)], axis=-1)
    emb_c = np.concatenate([np.sin(ang_c), np.cos(ang_c)], axis=-1)
    pe = np.concatenate([np.broadcast_to(emb_r[:, None, :], (rows, cols, 2 * quarter)),
                         np.broadcast_to(emb_c[None, :, :], (rows, cols, 2 * quarter))], axis=-1)
    return pe.reshape(rows * cols, 4 * quarter).astype(np.float32)


def _filter_features(n, width):
    pos = np.arange(n, dtype=np.float64)
    t = pos / max(n - 1, 1)
    ang = (2.0 * math.pi * pos / n)[:, None] * np.linspace(1e-4, FILTER_BANDS - 1, FILTER_BANDS)[None, :]
    z = np.concatenate([t[:, None], np.cos(ang), -np.sin(ang)], axis=-1)
    return np.pad(z, ((0, 0), (0, width - z.shape[1]))).astype(np.float32)


def _decay_rates(d_h):
    min_decay = math.log(DECAY_TARGET) / DECAY_SLOW_PCT
    max_decay = math.log(DECAY_TARGET) / DECAY_FAST_PCT
    return np.abs(np.linspace(min_decay, max_decay, d_h)).astype(np.float32).reshape(1, d_h)


def _dft_tables(blk):
    odd = 2 * np.arange(blk, dtype=np.int64) + 1
    shifted = ((odd[:, None] * odd[None, :]) % (8 * blk)).astype(np.float64) * (2.0 * math.pi / (8 * blk))
    plain = ((odd[:, None] * np.arange(blk, dtype=np.int64)[None, :]) % (4 * blk)).astype(np.float64) * (2.0 * math.pi / (4 * blk))
    sgn = (1.0 - 2.0 * (np.arange(blk) % 2)).astype(np.float32).reshape(blk, 1)
    f32 = lambda a: a.astype(np.float32)
    return f32(np.cos(shifted)), f32(np.sin(shifted)), f32(np.cos(plain)), f32(np.sin(plain)), sgn


def kernel(x, c, ctx, c_ctx, w_ada, b_ada, norm1_g, w_in, conv_a_w, conv_a_b, lru_wa, lru_ba, lru_wx, lru_bx, lru_lambda, conv_b_w, filt_w1, filt_b1, filt_freq1, filt_w2, filt_b2, filt_freq2, filt_w3, filt_b3, filt_bias, out_norm_a, out_norm_b, w_out, norm2_g, w_rg, b_rg, w_re, b_re, w_gate, w_up, w_down, final_g):
    bsz, n, d = x.shape
    n_ctx = ctx.shape[1]
    d_rnn = conv_a_w.shape[2]
    d_h = filt_bias.shape[2]
    order = filt_bias.shape[1]
    assert w_ada.shape[0] == 1, "single-layer block"
    l = 0
    ch = 256

    mod_rows = 16
    cc = jnp.concatenate([c, c_ctx[None, :], jnp.zeros((mod_rows - bsz - 1, d), F32)], axis=0)
    mod3 = _ada(cc, w_ada[l], b_ada[l]).reshape(mod_rows, N_MOD, d)

    pe = jnp.asarray(_sincos_table(n // GRID_W, GRID_W, d))
    p3 = _inproj_lat(x, pe, mod3, norm1_g[l], w_in[l], 1024)
    pc3 = _inproj_ctx(ctx.reshape(bsz * n_ctx, d), mod3, bsz, norm1_g[l], w_in[l], 1, d_rnn,
                      1024).reshape(bsz, n_ctx, d_rnn)

    heads_per_blk = ch // (d_rnn // RNN_HEADS)
    nblk = d_rnn // ch
    eye = jnp.eye(heads_per_blk, dtype=F32)

    def blockdiag(w):
        w5 = w.reshape(2, nblk, heads_per_blk, w.shape[2], w.shape[3])
        return jnp.einsum("dnkij,kl->dnkilj", w5, eye).reshape(2, nblk, ch, ch)

    wa_bd, wx_bd = blockdiag(lru_wa[l]), blockdiag(lru_wx[l])
    w_gates = jnp.concatenate([wa_bd[0], wx_bd[0], wa_bd[1], wx_bd[1]], axis=-1).astype(BF16)
    ba, bx = lru_ba[l].reshape(2, nblk, ch), lru_bx[l].reshape(2, nblk, ch)
    bias = jnp.concatenate([ba[0], bx[0], ba[1], bx[1]], axis=-1).reshape(nblk, 1, 4 * ch)
    sp = jax.nn.softplus(-lru_lambda[l])
    cb = conv_a_b[l].reshape(1, d_rnn)
    zeros_state = jnp.zeros((bsz, 1, d_rnn), F32)
    hf_ctx, hb_ctx = _rglru(pc3, 0, 0, conv_a_w[l], cb, w_gates, bias, sp, zeros_state, zeros_state, False, ch)
    ya, _, _ = _rglru(p3, d_rnn // ch, 0, conv_a_w[l], cb, w_gates, bias, sp, hf_ctx, hb_ctx, True, ch)

    as_bf16 = lambda a: jnp.asarray(a).astype(BF16)
    c3, s3, cu, su, sgn = _dft_tables(CONV_BLOCK)
    fe = 64
    z = jnp.asarray(_filter_features(n, fe))
    w1 = jnp.pad(filt_w1[l], ((0, fe - filt_w1.shape[1]), (0, 0)))
    hid = filt_w2.shape[1]
    hr, hi = _filters(z, w1, filt_b1[l].reshape(1, hid), filt_freq1[l].reshape(1, hid), filt_w2[l],
                      filt_b2[l].reshape(1, hid), filt_freq2[l].reshape(1, hid), filt_w3[l],
                      filt_b3[l].reshape(1, -1), jnp.asarray(_decay_rates(d_h)), as_bf16(cu), as_bf16(su),
                      jnp.asarray(sgn), d_h, order, ch)
    yb = _hyena(p3, 2 * d_rnn // ch, conv_b_w[l], filt_bias[l], as_bf16(c3), as_bf16(s3), hr, hi, d_h, ch)

    w_r = jnp.concatenate([w_re[l], w_rg[l], jnp.zeros((d, LANES - N_EXPERTS - N_GROUPS), F32)], axis=1)
    b_r = jnp.concatenate([b_re[l], b_rg[l], jnp.zeros((LANES - N_EXPERTS - N_GROUPS,), F32)]).reshape(1, LANES)
    tb = 512
    tm = 1024
    x1, h2, rt, cnt, prow = _merge(ya, yb, x, pe, mod3, out_norm_a[l], out_norm_b[l], w_out[l], norm2_g[l],
                                   w_r, b_r, tm, tm // tb)

    tile = 512
    t_all = bsz * n
    nb = t_all // tb
    lp = 2 * tb + N_EXPERTS * SEG
    n_tiles = (2 * t_all + nb * N_EXPERTS * (SEG - 1) + N_EXPERTS * (tile - 1)) // tile
    cnt_blocks = cnt[:, 0, :N_EXPERTS].astype(jnp.int32)
    tab, ends, ltot, tile_expert, n_used, nxt, par, tile_rows = _moe_layout(cnt_blocks, tile, n_tiles)
    xs = _dispatch(tab, ends, ltot, prow, h2.reshape(t_all, d), n_tiles * tile, tb, lp, tile)
    ys = _ffn(tile_expert, n_used, nxt, par, tile_rows, xs, w_gate[l], w_up[l], w_down[l], tile)
    out = _combine(tab, ltot, ys, rt.reshape(t_all, LANES), x1.reshape(t_all, d), mod3, final_g, tb, lp, n // tb)
    return out.reshape(bsz, n, d)
```

```python
import functools
import math

import numpy as np
import jax
import jax.numpy as jnp
from jax import lax
from jax.experimental import pallas as pl
from jax.experimental.pallas import tpu as pltpu

F32 = jnp.float32
BF16 = jnp.bfloat16
EPS = 1e-6
LRU_C = 8.0
N_MOD = 6
GRID_W = 64
RNN_HEADS = 8
N_GROUPS = 4
EXPERTS_PER_GROUP = 8
N_EXPERTS = N_GROUPS * EXPERTS_PER_GROUP
FILTER_BANDS = 16
DECAY_FAST_PCT = 0.3
DECAY_SLOW_PCT = 1.5
DECAY_TARGET = 1e-2
SUBLANES = 8
LANES = 128
SEG = 2 * SUBLANES
SLOT_SPLIT = 64.0
VMEM_LIMIT = 60 * 1024 * 1024


def _cparams(*sem):
    return pltpu.CompilerParams(dimension_semantics=sem, vmem_limit_bytes=VMEM_LIMIT)


def _dot(a, b):
    return jnp.dot(a, b, preferred_element_type=F32)


def _split(a):
    hi = a.astype(BF16)
    return hi, (a - hi.astype(F32)).astype(BF16)


def _dot3(a, b):
    a_hi, a_lo = _split(a)
    b_hi, b_lo = _split(b)
    return _dot(a_hi, b_hi) + _dot(a_lo, b_hi) + _dot(a_hi, b_lo)


def _rms(v, g):
    return v * lax.rsqrt(jnp.mean(v * v, axis=-1, keepdims=True) + EPS) * g


def _sigmoid(z):
    return 1.0 / (1.0 + jnp.exp(-z))


def _shift_rows(v, d):
    n = v.shape[0]
    rolled = pltpu.roll(v, (-d) % n, 0)
    row = lax.broadcasted_iota(jnp.int32, v.shape, 0)
    ok = (row + d >= 0) & (row + d < n)
    return jnp.where(ok, rolled, 0.0)


def _tile_rot(v, s):
    n, c = v.shape
    return pltpu.roll(v.reshape(n // SUBLANES, SUBLANES, c), s, 1).reshape(n, c)


def _ada_body(c_ref, w_ref, b_ref, o_ref):
    c = c_ref[...]
    o_ref[...] = _dot3(c * _sigmoid(c), w_ref[...]) + b_ref[...]


def _ada(cc, w_ada, b_ada):
    rows, d = cc.shape
    n = w_ada.shape[1]
    tn = 1024
    return pl.pallas_call(
        _ada_body,
        grid=(n // tn,),
        in_specs=[pl.BlockSpec((rows, d), lambda j: (0, 0)),
                  pl.BlockSpec((d, tn), lambda j: (0, j)),
                  pl.BlockSpec((1, tn), lambda j: (0, j))],
        out_specs=pl.BlockSpec((rows, tn), lambda j: (0, j)),
        out_shape=jax.ShapeDtypeStruct((rows, n), F32),
        compiler_params=_cparams("arbitrary"),
        name="ada",
    )(cc, w_ada, b_ada.reshape(1, n))


def _inproj_lat_body(x_ref, pe_ref, mod_ref, g_ref, w_ref, o_ref, wb_ref):
    @pl.when((pl.program_id(0) == 0) & (pl.program_id(1) == 0))
    def _():
        wb_ref[...] = w_ref[...].astype(BF16)

    x = x_ref[...] + pe_ref[...]
    h = _rms(x, g_ref[...]) * (1.0 + mod_ref[1:2, :]) + mod_ref[0:1, :]
    o_ref[...] = _dot(h.astype(BF16), wb_ref[...]).astype(o_ref.dtype)


def _inproj_lat(x3, pe, mod3, g, w, tm):
    b, l, d = x3.shape
    n_out = w.shape[1]
    return pl.pallas_call(
        _inproj_lat_body,
        grid=(l // tm, b),
        in_specs=[pl.BlockSpec((None, tm, d), lambda i, bi: (bi, i, 0)),
                  pl.BlockSpec((tm, d), lambda i, bi: (i, 0)),
                  pl.BlockSpec((None, N_MOD, d), lambda i, bi: (bi, 0, 0)),
                  pl.BlockSpec((1, d), lambda i, bi: (0, 0)),
                  pl.BlockSpec((d, n_out), lambda i, bi: (0, 0), pipeline_mode=pl.Buffered(1))],
        out_specs=pl.BlockSpec((None, tm, n_out), lambda i, bi: (bi, i, 0)),
        out_shape=jax.ShapeDtypeStruct((b, l, n_out), BF16),
        scratch_shapes=[pltpu.VMEM((d, n_out), BF16)],
        compiler_params=_cparams("arbitrary", "arbitrary"),
        name="inproj_pe",
    )(x3, pe, mod3, g.reshape(1, d), w)


def _inproj_ctx_body(x_ref, mod_ref, g_ref, w_ref, o_ref):
    h = _rms(x_ref[...], g_ref[...]) * (1.0 + mod_ref[1:2, :]) + mod_ref[0:1, :]
    o_ref[...] = _dot(h.astype(BF16), w_ref[...].astype(BF16))


def _inproj_ctx(x2, mod3, mod_row, g, w, col_block, n_out, tm):
    r, d = x2.shape
    return pl.pallas_call(
        _inproj_ctx_body,
        grid=(r // tm,),
        in_specs=[pl.BlockSpec((tm, d), lambda i: (i, 0)),
                  pl.BlockSpec((None, N_MOD, d), lambda i: (mod_row, 0, 0)),
                  pl.BlockSpec((1, d), lambda i: (0, 0)),
                  pl.BlockSpec((d, n_out), lambda i: (0, col_block))],
        out_specs=pl.BlockSpec((tm, n_out), lambda i: (i, 0)),
        out_shape=jax.ShapeDtypeStruct((r, n_out), F32),
        compiler_params=_cparams("arbitrary"),
        name="inproj_ctx",
    )(x2, mod3, g.reshape(1, d), w)


def _rglru_body(*refs, with_gate):
    if with_gate:
        (u_ref, gate_ref, cw_ref, cb_ref, w_ref, bias_ref, sp_ref, h0f_ref, h0b_ref,
         y_ref, hf_end_ref, hb_end_ref, af_ref, bf_ref, ab_ref, bb_ref) = refs
    else:
        (u_ref, cw_ref, cb_ref, w_ref, bias_ref, sp_ref, h0f_ref, h0b_ref,
         hf_end_ref, hb_end_ref, af_ref, bf_ref, ab_ref, bb_ref) = refs
    n, c = u_ref.shape
    u = u_ref[...].astype(F32)
    v = cb_ref[...] + cw_ref[1:2, :] * _shift_rows(u, -1) + cw_ref[0:1, :] * _shift_rows(u, -2)
    v = v + cw_ref[2:3, :] * u + cw_ref[3:4, :] * _shift_rows(u, 1)
    z = _dot(v.astype(BF16), w_ref[...]) + bias_ref[...]
    sub = lax.broadcasted_iota(jnp.int32, (n, c), 0) % SUBLANES

    def local_scan(k, reverse, a_ref, b_ref):
        r = _sigmoid(z[:, (2 * k) * c:(2 * k + 1) * c])
        i = _sigmoid(z[:, (2 * k + 1) * c:(2 * k + 2) * c])
        log_a = (-LRU_C) * r * sp_ref[k:k + 1, :]
        a = jnp.exp(log_a)
        b = jnp.sqrt(1.0 - a * a) * (i * v)
        for s in (1, 2, 4):
            ok = (sub < SUBLANES - s) if reverse else (sub >= s)
            shift = SUBLANES - s if reverse else s
            b = a * jnp.where(ok, _tile_rot(b, shift), 0.0) + b
            a = a * jnp.where(ok, _tile_rot(a, shift), 1.0)
        a_ref[...] = a
        b_ref[...] = b

    local_scan(0, False, af_ref, bf_ref)
    local_scan(1, True, ab_ref, bb_ref)

    n_tiles = n // SUBLANES

    def carry_step(q, carry):
        cf, cb = carry
        rf = pl.ds(pl.multiple_of(q * SUBLANES, SUBLANES), SUBLANES)
        rb = pl.ds(pl.multiple_of((n_tiles - 1 - q) * SUBLANES, SUBLANES), SUBLANES)
        af, bf, ab, bb = af_ref[rf, :], bf_ref[rf, :], ab_ref[rb, :], bb_ref[rb, :]
        bf_ref[rf, :] = af * cf + bf
        bb_ref[rb, :] = ab * cb + bb
        last = lambda v: jnp.broadcast_to(v[SUBLANES - 1:SUBLANES, :], (SUBLANES, c))
        first = lambda v: jnp.broadcast_to(v[0:1, :], (SUBLANES, c))
        return last(af) * cf + last(bf), first(ab) * cb + first(bb)

    cf0 = jnp.broadcast_to(h0f_ref[...], (SUBLANES, c))
    cb0 = jnp.broadcast_to(h0b_ref[...], (SUBLANES, c))
    cf, cb = lax.fori_loop(0, n_tiles, carry_step, (cf0, cb0), unroll=4)
    hf_end_ref[...] = cf[0:1, :]
    hb_end_ref[...] = cb[0:1, :]
    if with_gate:
        y_ref[...] = jax.nn.gelu(gate_ref[...].astype(F32), approximate=True) * (bf_ref[...] + bb_ref[...])


def _rglru(p3, u_blk0, gate_blk0, cw, cb, w_gates, bias, sp, h0f, h0b, with_gate, ch):
    b, l, _ = p3.shape
    d_rnn = cw.shape[1]
    nh = d_rnn // ch
    in_specs = [pl.BlockSpec((None, l, ch), lambda bi, h: (bi, 0, u_blk0 + h))]
    args = [p3]
    if with_gate:
        in_specs.append(pl.BlockSpec((None, l, ch), lambda bi, h: (bi, 0, gate_blk0 + h)))
        args.append(p3)
    in_specs += [pl.BlockSpec((4, ch), lambda bi, h: (0, h)),
                 pl.BlockSpec((1, ch), lambda bi, h: (0, h)),
                 pl.BlockSpec((None, ch, 4 * ch), lambda bi, h: (h, 0, 0)),
                 pl.BlockSpec((None, 1, 4 * ch), lambda bi, h: (h, 0, 0)),
                 pl.BlockSpec((2, ch), lambda bi, h: (0, h)),
                 pl.BlockSpec((None, 1, ch), lambda bi, h: (bi, 0, h)),
                 pl.BlockSpec((None, 1, ch), lambda bi, h: (bi, 0, h))]
    args += [cw, cb, w_gates, bias, sp, h0f, h0b]
    end_spec = pl.BlockSpec((None, 1, ch), lambda bi, h: (bi, 0, h))
    end_shape = jax.ShapeDtypeStruct((b, 1, d_rnn), F32)
    out_specs = [end_spec, end_spec]
    out_shape = [end_shape, end_shape]
    if with_gate:
        out_specs = [pl.BlockSpec((None, l, ch), lambda bi, h: (bi, 0, h))] + out_specs
        out_shape = [jax.ShapeDtypeStruct((b, l, d_rnn), F32)] + out_shape
    return pl.pallas_call(
        functools.partial(_rglru_body, with_gate=with_gate),
        grid=(b, nh),
        in_specs=in_specs,
        out_specs=out_specs,
        out_shape=out_shape,
        scratch_shapes=[pltpu.VMEM((l, ch), F32)] * 4,
        compiler_params=_cparams("arbitrary", "arbitrary"),
        name="rglru_lat" if with_gate else "rglru_ctx",
    )(*args)


CONV_BLOCK = 512


def _filt_body(z_ref, w1_ref, b1_ref, f1_ref, w2_ref, b2_ref, f2_ref, w3f_ref, w3b_ref, b3f_ref, b3b_ref,
               dec_ref, cu_ref, su_ref, sgn_ref, hr_ref, hi_ref, hdn_ref):
    n = z_ref.shape[0]
    blk = cu_ref.shape[0]
    nblk = n // blk
    ch = hr_ref.shape[2]

    @pl.when(pl.program_id(0) == 0)
    def _():
        hid = jnp.sin(f1_ref[...] * (_dot3(z_ref[...], w1_ref[...]) + b1_ref[...]))
        hdn_ref[...] = jnp.sin(f2_ref[...] * (_dot3(hid, w2_ref[...]) + b2_ref[...]))

    hdn = hdn_ref[...]
    decay = jnp.exp(-z_ref[:, 0:1] * dec_ref[...])
    kf = (_dot3(hdn, w3f_ref[...]) + b3f_ref[...]) * decay
    kb = (_dot3(hdn, w3b_ref[...]) + b3b_ref[...]) * decay
    row = lax.broadcasted_iota(jnp.int32, (n, ch), 0)
    kb = jnp.where(row == 0, 0.0, kb)
    norm = jnp.sum(jnp.abs(kf) + jnp.abs(kb), axis=0, keepdims=True)
    scale = (2.0 / (2 * blk)) / norm
    sgn = sgn_ref[...]

    p, q, first = [], [], []
    for c in range(nblk):
        taps = jnp.concatenate([kf[c * blk:(c + 1) * blk], kb[c * blk:(c + 1) * blk]], axis=1).astype(BF16)
        p.append(_dot(cu_ref[...], taps))
        q.append(_dot(su_ref[...], taps))
        first.append(taps[0:1, :].astype(F32))
    fwd = lambda a: a[:, :ch]
    bwd = lambda a: a[:, ch:]
    out = lambda a: (a * scale).astype(hr_ref.dtype)
    hr_ref[nblk - 1] = out(fwd(p[0]) + bwd(p[0]))
    hi_ref[nblk - 1] = out(bwd(q[0]) - fwd(q[0]))
    for c in range(1, nblk):
        hr_ref[nblk - 1 + c] = out(fwd(p[c]) + sgn * fwd(q[c - 1]))
        hi_ref[nblk - 1 + c] = out(sgn * (fwd(p[c - 1]) - fwd(first[c - 1])) - fwd(q[c]))
        hr_ref[nblk - 1 - c] = out(bwd(p[c]) + sgn * bwd(q[c - 1]))
        hi_ref[nblk - 1 - c] = out(bwd(q[c]) - sgn * (bwd(p[c - 1]) - bwd(first[c - 1])))


def _filters(z, w1, b1, f1, w2, b2, f2, w3, b3, deltas, cu, su, sgn, d_h, order, ch):
    n, fe = z.shape
    hid = w2.shape[0]
    blk = cu.shape[0]
    nh = 2 * (n // blk) - 1
    per_order = d_h // ch
    ncol = order * per_order
    const = lambda shape: pl.BlockSpec(shape, lambda g: (0, 0))
    hspec = pl.BlockSpec((nh, blk, ch), lambda g: (0, 0, g))
    return pl.pallas_call(
        _filt_body,
        grid=(ncol,),
        in_specs=[const((n, fe)), const((fe, hid)), const((1, hid)), const((1, hid)),
                  const((hid, hid)), const((1, hid)), const((1, hid)),
                  pl.BlockSpec((hid, ch), lambda g: (0, g)),
                  pl.BlockSpec((hid, ch), lambda g: (0, ncol + g)),
                  pl.BlockSpec((1, ch), lambda g: (0, g)),
                  pl.BlockSpec((1, ch), lambda g: (0, ncol + g)),
                  pl.BlockSpec((1, ch), lambda g: (0, g % per_order)),
                  const((blk, blk)), const((blk, blk)), const((blk, 1))],
        out_specs=[hspec, hspec],
        out_shape=[jax.ShapeDtypeStruct((nh, blk, order * d_h), BF16)] * 2,
        scratch_shapes=[pltpu.VMEM((n, hid), F32)],
        compiler_params=_cparams("arbitrary"),
        name="hyena_filters",
    )(z, w1, b1, f1, w2, b2, f2, w3, w3, b3, b3, deltas, cu, su, sgn)


def _hyena_body(v_ref, x1_ref, x2_ref, wv_ref, w1_ref, w2_ref, fb_ref, c3_ref, s3_ref,
                h0r_ref, h0i_ref, h1r_ref, h1i_ref, o_ref, z_ref, zb_ref, p_ref, q_ref, yr_ref, yi_ref):
    n = o_ref.shape[0]
    blk = c3_ref.shape[0]
    nblk = n // blk
    halo = 2 * SUBLANES
    sub = 32

    def conv3(src_ref, w_ref, r0):
        lo, hi = max(r0 - halo, 0), min(r0 + blk + halo, n)
        win = src_ref[lo:hi, :].astype(F32)
        size = hi - lo
        row = lax.broadcasted_iota(jnp.int32, win.shape, 0)
        prev = pltpu.roll(win, 1, 0)
        nxt = pltpu.roll(win, size - 1, 0)
        if lo == 0:
            prev = jnp.where(row == 0, 0.0, prev)
        if hi == n:
            nxt = jnp.where(row == size - 1, 0.0, nxt)
        out = w_ref[0:1, :] * prev + w_ref[1:2, :] * win + w_ref[2:3, :] * nxt
        return out[r0 - lo:r0 - lo + blk, :]

    def long_conv(hr_ref, hi_ref, fb, gate_ref, gate_w_ref, dst_ref):
        for i in range(nblk):
            rows = slice(i * blk, (i + 1) * blk)
            p_ref[rows, :] = _dot(c3_ref[...], zb_ref[rows, :]).astype(BF16)
            q_ref[rows, :] = _dot(s3_ref[...], zb_ref[rows, :]).astype(BF16)

        for j in range(nblk):
            for r0 in range(0, blk, sub):
                yr = yi = None
                for i in range(nblk):
                    rows = slice(i * blk + r0, i * blk + r0 + sub)
                    p, q = p_ref[rows, :], q_ref[rows, :]
                    gr = hr_ref[nblk - 1 + j - i, r0:r0 + sub, :]
                    gi = hi_ref[nblk - 1 + j - i, r0:r0 + sub, :]
                    tr, ti = p * gr + q * gi, q * gr - p * gi
                    yr, yi = (tr, ti) if yr is None else (yr + tr, yi + ti)
                yr_ref[j * blk + r0:j * blk + r0 + sub, :] = yr
                yi_ref[j * blk + r0:j * blk + r0 + sub, :] = yi
            rows = slice(j * blk, (j + 1) * blk)
            y = _dot(c3_ref[...], yr_ref[rows, :]) + _dot(s3_ref[...], yi_ref[rows, :])
            dst_ref[rows, :] = conv3(gate_ref, gate_w_ref, j * blk) * (y + z_ref[rows, :] * fb)

    for r0 in range(0, n, blk):
        z_ref[r0:r0 + blk, :] = conv3(v_ref, wv_ref, r0)
    zb_ref[...] = z_ref[...].astype(BF16)
    long_conv(h0r_ref, h0i_ref, fb_ref[0:1, :], x1_ref, w1_ref, z_ref)
    zb_ref[...] = z_ref[...].astype(BF16)
    long_conv(h1r_ref, h1i_ref, fb_ref[1:2, :], x2_ref, w2_ref, o_ref)


def _hyena(p3, col_blk0, conv_w, fbias, c3, s3, hr, hi, d_h, ch):
    b, n, _ = p3.shape
    nh, blk, _ = hr.shape
    per = d_h // ch
    zspec = lambda k: pl.BlockSpec((None, n, ch), lambda h, bi: (bi, 0, col_blk0 + k * per + h))
    wspec = lambda k: pl.BlockSpec((3, ch), lambda h, bi: (0, k * per + h))
    hspec = lambda o: pl.BlockSpec((nh, blk, ch), lambda h, bi: (0, 0, o * per + h), pipeline_mode=pl.Buffered(1))
    const = pl.BlockSpec((blk, blk), lambda h, bi: (0, 0))
    return pl.pallas_call(
        _hyena_body,
        grid=(per, b),
        in_specs=[zspec(0), zspec(1), zspec(2), wspec(0), wspec(1), wspec(2),
                  pl.BlockSpec((2, ch), lambda h, bi: (0, h)), const, const,
                  hspec(0), hspec(0), hspec(1), hspec(1)],
        out_specs=pl.BlockSpec((None, n, ch), lambda h, bi: (bi, 0, h)),
        out_shape=jax.ShapeDtypeStruct((b, n, d_h), F32),
        scratch_shapes=[pltpu.VMEM((n, ch), F32), pltpu.VMEM((n, ch), BF16),
                        pltpu.VMEM((n, ch), BF16), pltpu.VMEM((n, ch), BF16),
                        pltpu.VMEM((n, ch), BF16), pltpu.VMEM((n, ch), BF16)],
        compiler_params=_cparams("arbitrary", "arbitrary"),
        name="hyena_mix",
    )(p3, p3, p3, conv_w, conv_w, conv_w, fbias, c3, s3, hr, hi, hr, hi)


def _merge_body(ya_ref, yb_ref, x_ref, pe_ref, mod_ref, ga_ref, gb_ref, wo_ref, g2_ref, wr_ref, br_ref,
                x1_ref, h2_ref, rt_ref, cnt_ref, prow_ref, wob_ref, tri_ref):
    @pl.when((pl.program_id(0) == 0) & (pl.program_id(1) == 0))
    def _():
        wob_ref[...] = wo_ref[...].astype(BF16)
        earlier = lax.broadcasted_iota(jnp.int32, tri_ref.shape, 1) < lax.broadcasted_iota(jnp.int32, tri_ref.shape, 0)
        tri_ref[...] = jnp.where(earlier, 1.0, 0.0).astype(BF16)

    da = ya_ref.shape[1]
    na = _rms(ya_ref[...], ga_ref[...]).astype(BF16)
    nb = _rms(yb_ref[...], gb_ref[...]).astype(BF16)
    y = _dot(na, wob_ref[0:da, :]) + _dot(nb, wob_ref[da:, :])
    _route(y, x_ref, pe_ref, mod_ref, g2_ref, wr_ref, br_ref, x1_ref, h2_ref, rt_ref, cnt_ref, prow_ref, tri_ref)


def _route(y, x_ref, pe_ref, mod_ref, g2_ref, wr_ref, br_ref, x1_ref, h2_ref, rt_ref, cnt_ref, prow_ref, tri_ref):
    x1 = x_ref[...] + pe_ref[...] + mod_ref[2:3, :] * y
    x1_ref[...] = x1
    h2 = _rms(x1, g2_ref[...]) * (1.0 + mod_ref[4:5, :]) + mod_ref[3:4, :]
    h2_ref[...] = h2.astype(BF16)

    logits = _dot3(h2, wr_ref[...]) + br_ref[...]
    lane = lax.broadcasted_iota(jnp.int32, logits.shape, 1).astype(F32)
    neg = -jnp.inf
    big = jnp.float32(1 << 20)
    gl = jnp.where((lane >= N_EXPERTS) & (lane < N_EXPERTS + N_GROUPS), logits, neg)
    gmax = jnp.max(gl, axis=1, keepdims=True)
    g_p = 1.0 / jnp.sum(jnp.exp(gl - gmax), axis=1, keepdims=True)
    g_i = jnp.min(jnp.where(gl == gmax, lane, big), axis=1, keepdims=True) - N_EXPERTS
    lo = g_i * EXPERTS_PER_GROUP
    el = jnp.where((lane >= lo) & (lane < lo + EXPERTS_PER_GROUP), logits, neg)
    m1 = jnp.max(el, axis=1, keepdims=True)
    i1 = jnp.min(jnp.where(el == m1, lane, big), axis=1, keepdims=True)
    el2 = jnp.where(lane == i1, neg, el)
    m2 = jnp.max(el2, axis=1, keepdims=True)
    i2 = jnp.min(jnp.where(el2 == m2, lane, big), axis=1, keepdims=True)
    e2 = jnp.exp(m2 - m1)
    w1 = g_p / (1.0 + e2)
    w2 = g_p * e2 / (1.0 + e2)
    o1 = jnp.where(lane == i1, 1.0, 0.0)
    o2 = jnp.where(lane == i2, 1.0, 0.0)
    nsub = cnt_ref.shape[0]
    sub = o1.shape[0] // nsub
    sq = lambda shape, d: lax.broadcasted_iota(jnp.int32, shape, d)
    earlier = tri_ref[...]
    below = jnp.where(sq((LANES, LANES), 0) < sq((LANES, LANES), 1), 1.0, 0.0).astype(BF16)
    pick_row = jnp.where(sq((SUBLANES, LANES), 0) == sq((SUBLANES, LANES), 1), 1.0, 0.0).astype(BF16)
    row_id = sq((SUBLANES, sub), 0)
    pos1, pos2 = [], []
    for k in range(nsub):
        a1, a2 = o1[k * sub:(k + 1) * sub], o2[k * sub:(k + 1) * sub]
        both = a1 + a2
        count = jnp.sum(both, axis=0, keepdims=True)
        cnt_ref[k] = jnp.broadcast_to(count, cnt_ref.shape[1:])
        runs = jnp.broadcast_to(jnp.floor((count + (SEG - 1)) / SEG), (SUBLANES, LANES))
        start = _dot(runs.astype(BF16), below)[0:1, :] * SEG
        before = _dot(earlier, both.astype(BF16)) + start
        p1 = jnp.sum(a1 * before, axis=1, keepdims=True)
        p2 = jnp.sum(a2 * before, axis=1, keepdims=True)
        pos1.append(p1)
        pos2.append(p2)
        lane_k = sq((sub, LANES), 1).astype(F32)
        hi1, hi2 = jnp.floor(p1 / SLOT_SPLIT), jnp.floor(p2 / SLOT_SPLIT)
        parts = (jnp.where(lane_k == 0.0, hi1, 0.0) + jnp.where(lane_k == 1.0, p1 - SLOT_SPLIT * hi1, 0.0)
                 + jnp.where(lane_k == 2.0, hi2, 0.0) + jnp.where(lane_k == 3.0, p2 - SLOT_SPLIT * hi2, 0.0))
        rows = lax.dot_general(pick_row, parts.astype(BF16), (((1,), (1,)), ((), ())), preferred_element_type=F32)
        q1 = rows[0:1, :] * SLOT_SPLIT + rows[1:2, :]
        q2 = rows[2:3, :] * SLOT_SPLIT + rows[3:4, :]
        prow_ref[k] = jnp.where(row_id == 0, q1, jnp.where(row_id == 1, q2, 0.0))
    p1 = jnp.concatenate(pos1, axis=0)
    p2 = jnp.concatenate(pos2, axis=0)
    rt_ref[...] = (jnp.where(lane == 0.0, i1, 0.0) + jnp.where(lane == 1.0, i2, 0.0)
                   + jnp.where(lane == 2.0, w1, 0.0) + jnp.where(lane == 3.0, w2, 0.0)
                   + jnp.where(lane == 4.0, p1, 0.0) + jnp.where(lane == 5.0, p2, 0.0))


def _merge(ya, yb, x3, pe, mod3, ga, gb, w_out, g2n, w_r, b_r, tm, nsub):
    b, n, d = x3.shape
    da, db = ya.shape[2], yb.shape[2]
    per = n // tm
    tok = lambda w: pl.BlockSpec((None, tm, w), lambda i, bi: (bi, i, 0))
    const = lambda shape: pl.BlockSpec(shape, lambda i, bi: (0, 0))
    blk = lambda w: pl.BlockSpec((nsub, SUBLANES, w), lambda i, bi: (bi * per + i, 0, 0))
    return pl.pallas_call(
        _merge_body,
        grid=(per, b),
        in_specs=[tok(da), tok(db), tok(d), pl.BlockSpec((tm, d), lambda i, bi: (i, 0)),
                  pl.BlockSpec((None, N_MOD, d), lambda i, bi: (bi, 0, 0)),
                  const((1, da)), const((1, db)), const((da + db, d)), const((1, d)),
                  const((d, LANES)), const((1, LANES))],
        out_specs=[tok(d), tok(d), tok(LANES), blk(LANES), blk(tm // nsub)],
        out_shape=[jax.ShapeDtypeStruct((b, n, d), F32), jax.ShapeDtypeStruct((b, n, d), BF16),
                   jax.ShapeDtypeStruct((b, n, LANES), F32),
                   jax.ShapeDtypeStruct((b * per * nsub, SUBLANES, LANES), F32),
                   jax.ShapeDtypeStruct((b * per * nsub, SUBLANES, tm // nsub), F32)],
        scratch_shapes=[pltpu.VMEM((da + db, d), BF16), pltpu.VMEM((tm // nsub, tm // nsub), BF16)],
        compiler_params=_cparams("arbitrary", "arbitrary"),
        name="merge_route",
    )(ya, yb, x3, pe, mod3, ga.reshape(1, da), gb.reshape(1, db), w_out, g2n.reshape(1, d), w_r, b_r)


def _segment_copy(local_ref, lrow, global_ref, grow, rows, sem, to_global):
    lrows = pl.ds(pl.multiple_of(lrow, SEG), rows)
    grows = pl.ds(pl.multiple_of(grow, SEG), rows)
    if to_global:
        return pltpu.make_async_copy(local_ref.at[lrows, :], global_ref.at[grows, :], sem)
    return pltpu.make_async_copy(global_ref.at[grows, :], local_ref.at[lrows, :], sem)


def _start_segments(tab_ref, blk, local_ref, global_ref, sem, to_global):
    def per_expert(e, priority):
        k = (blk * N_EXPERTS + e) * 3
        ls, gs, nchunk = tab_ref[k], tab_ref[k + 1], tab_ref[k + 2]

        def per_chunk(ci, carry):
            cp = _segment_copy(local_ref, ls + ci * SEG, global_ref, gs + ci * SEG, SEG, sem, to_global)
            cp.start(priority=priority)
            return carry

        lax.fori_loop(0, nchunk, per_chunk, 0)

    def per_pair(j, carry):
        per_expert(2 * j, 0)
        per_expert(2 * j + 1, 1)
        return carry

    lax.fori_loop(0, N_EXPERTS // 2, per_pair, 0)


def _wait_segments(rows, local_ref, global_ref, sem, to_global):
    chunks = rows // SEG
    nbits = (local_ref.shape[0] // SEG).bit_length()
    for k in range(nbits):
        @pl.when(((chunks >> k) & 1) == 1)
        def _():
            _segment_copy(local_ref, 0, global_ref, 0, SEG << k, sem, to_global).wait()


def _dispatch_body(tab_ref, ends_ref, ltot_ref, prow_ref, h2_ref, xs_ref, loc_ref, zero_ref, sem, zsem):
    blk = pl.program_id(0)
    nblk = pl.num_programs(0)
    tb = h2_ref.shape[0]
    lp = loc_ref.shape[1]
    tile = zero_ref.shape[0]
    slot = blk % 2

    @pl.when(blk == 0)
    def _():
        zero_ref[...] = jnp.zeros_like(zero_ref)

        def zero_fill(first_row, rows):
            dst = xs_ref.at[pl.ds(pl.multiple_of(first_row, SEG), rows), :]
            return pltpu.make_async_copy(zero_ref.at[pl.ds(0, rows), :], dst, zsem)

        pad = 64

        def fill(e, carry, wait):
            end, real = ends_ref[e], ends_ref[N_EXPERTS + e]
            first = real // pad * pad

            def one(j, carry2):
                cp = zero_fill(first + j * pad, pad)
                cp.wait() if wait else cp.start()
                return carry2

            return lax.fori_loop(0, (end - first) // pad, one, carry)

        lax.fori_loop(0, N_EXPERTS, functools.partial(fill, wait=False), 0)
        used = ends_ref[N_EXPERTS - 1]
        spare = (xs_ref.shape[0] - used) // tile

        def fill_spare(j, carry, wait):
            cp = zero_fill(used + j * tile, tile)
            cp.wait() if wait else cp.start()
            return carry

        lax.fori_loop(0, spare, functools.partial(fill_spare, wait=False), 0)
        lax.fori_loop(0, N_EXPERTS, functools.partial(fill, wait=True), 0)
        lax.fori_loop(0, spare, functools.partial(fill_spare, wait=True), 0)

    pos = lax.broadcasted_iota(jnp.int32, (lp, tb), 0).astype(F32)
    onehot = jnp.where((pos == prow_ref[0:1, :]) | (pos == prow_ref[1:2, :]), 1.0, 0.0).astype(BF16)
    loc_ref[slot] = _dot(onehot, h2_ref[...]).astype(BF16)

    _start_segments(tab_ref, blk, loc_ref.at[slot], xs_ref, sem.at[slot], True)

    @pl.when(blk > 0)
    def _():
        _wait_segments(ltot_ref[blk - 1], loc_ref.at[1 - slot], xs_ref, sem.at[1 - slot], True)

    @pl.when(blk == nblk - 1)
    def _():
        _wait_segments(ltot_ref[blk], loc_ref.at[slot], xs_ref, sem.at[slot], True)


def _dispatch(tab, ends, ltot, prow, h2, slots, tb, lp, tile):
    t, d = h2.shape
    return pl.pallas_call(
        _dispatch_body,
        grid_spec=pltpu.PrefetchScalarGridSpec(
            num_scalar_prefetch=3,
            grid=(t // tb,),
            in_specs=[pl.BlockSpec((None, SUBLANES, tb), lambda i, *_: (i, 0, 0)),
                      pl.BlockSpec((tb, d), lambda i, *_: (i, 0))],
            out_specs=pl.BlockSpec(memory_space=pl.ANY),
            scratch_shapes=[pltpu.VMEM((2, lp, d), BF16), pltpu.VMEM((tile, d), BF16),
                            pltpu.SemaphoreType.DMA((2,)), pltpu.SemaphoreType.DMA]),
        out_shape=jax.ShapeDtypeStruct((slots, d), BF16),
        compiler_params=_cparams("arbitrary"),
        name="moe_dispatch",
    )(tab, ends, ltot, prow, h2)


def _ffn_body(te_ref, nu_ref, nxt_ref, par_ref, rows_ref, xs_ref, wg_hbm, wu_hbm, wd_hbm, ys_ref,
              wg_buf, wu_buf, wd_buf, wgb_ref, wub_ref, wdb_ref, sem):
    i = pl.program_id(0)
    e = te_ref[i]
    first = (i == 0) | (e != te_ref[jnp.maximum(i - 1, 0)])
    active = i < nu_ref[0]

    def fetch(expert, slot):
        return [pltpu.make_async_copy(w.at[expert], buf.at[slot], sem.at[slot])
                for w, buf in ((wg_hbm, wg_buf), (wu_hbm, wu_buf), (wd_hbm, wd_buf))]

    @pl.when(i == 0)
    def _():
        for cp in fetch(e, par_ref[e]):
            cp.start()

    @pl.when(first & active)
    def _():
        slot = par_ref[e]
        for cp in fetch(e, slot):
            cp.wait()
        nxt = nxt_ref[e]

        @pl.when(nxt >= 0)
        def _():
            for cp in fetch(nxt, 1 - slot):
                cp.start()

        wgb_ref[...] = wg_buf[slot].astype(BF16)
        wub_ref[...] = wu_buf[slot].astype(BF16)
        wdb_ref[...] = wd_buf[slot].astype(BF16)

    def expert(rows):
        xb = xs_ref[rows, :]
        act = _dot(xb, wgb_ref[...])
        act = act * _sigmoid(act) * _dot(xb, wub_ref[...])
        ys_ref[rows, :] = _dot(act.astype(BF16), wdb_ref[...]).astype(BF16)

    parts = 4
    quarter = xs_ref.shape[0] // parts
    filled = (rows_ref[i] + quarter - 1) // quarter
    for q in range(1, parts + 1):
        @pl.when(active & (filled == q))
        def _():
            expert(slice(0, q * quarter))
            if q < parts:
                ys_ref[q * quarter:, :] = jnp.zeros(((parts - q) * quarter, ys_ref.shape[1]), BF16)


def _ffn(tile_expert, n_used, nxt, par, tile_rows, xs, w_gate, w_up, w_down, tile):
    slots, d = xs.shape
    _, _, de = w_gate.shape
    last = lambda i, te, nu, *_: (jnp.minimum(i, nu[0] - 1), 0)
    hbm = pl.BlockSpec(memory_space=pl.ANY)
    return pl.pallas_call(
        _ffn_body,
        grid_spec=pltpu.PrefetchScalarGridSpec(
            num_scalar_prefetch=5,
            grid=(slots // tile,),
            in_specs=[pl.BlockSpec((tile, d), last), hbm, hbm, hbm],
            out_specs=pl.BlockSpec((tile, d), last),
            scratch_shapes=[pltpu.VMEM((2, d, de), F32), pltpu.VMEM((2, d, de), F32), pltpu.VMEM((2, de, d), F32),
                            pltpu.VMEM((d, de), BF16), pltpu.VMEM((d, de), BF16), pltpu.VMEM((de, d), BF16),
                            pltpu.SemaphoreType.DMA((2,))]),
        out_shape=jax.ShapeDtypeStruct((slots, d), BF16),
        input_output_aliases={5: 0},
        compiler_params=_cparams("arbitrary"),
        name="moe_ffn",
    )(tile_expert, n_used, nxt, par, tile_rows, xs, w_gate, w_up, w_down)


def _combine_body(tab_ref, ltot_ref, ys_ref, rt_ref, x1_ref, mod_ref, fg_ref, o_ref, loc_ref, sem):
    blk = pl.program_id(0)
    nblk = pl.num_programs(0)
    _, lp, d = loc_ref.shape
    slot = blk % 2

    @pl.when(blk == 0)
    def _():
        _start_segments(tab_ref, blk, loc_ref.at[slot], ys_ref, sem.at[slot], False)

    @pl.when(blk + 1 < nblk)
    def _():
        _start_segments(tab_ref, blk + 1, loc_ref.at[1 - slot], ys_ref, sem.at[1 - slot], False)

    rt = rt_ref[...]
    lane = lax.broadcasted_iota(jnp.int32, rt.shape, 1)
    col = lambda k: jnp.sum(jnp.where(lane == k, rt, 0.0), axis=1, keepdims=True)
    w1, w2, p1, p2 = col(2), col(3), col(4), col(5)
    pos = lax.broadcasted_iota(jnp.int32, (rt.shape[0], lp), 1).astype(F32)
    weights = (jnp.where(pos == p1, w1, 0.0) + jnp.where(pos == p2, w2, 0.0)).astype(BF16)
    _wait_segments(ltot_ref[blk], loc_ref.at[slot], ys_ref, sem.at[slot], False)

    def clear(ci, carry):
        loc_ref[slot, pl.ds(pl.multiple_of(ci * SEG, SEG), SEG), :] = jnp.zeros((SEG, d), BF16)
        return carry

    lax.fori_loop(ltot_ref[blk] // SEG, lp // SEG, clear, 0)
    moe = _dot(weights, loc_ref[slot])
    o_ref[...] = _rms(x1_ref[...] + mod_ref[5:6, :] * moe, fg_ref[...])


def _combine(tab, ltot, ys, rt2, x1, mod3, final_g, tb, lp, per_batch):
    t, d = x1.shape
    return pl.pallas_call(
        _combine_body,
        grid_spec=pltpu.PrefetchScalarGridSpec(
            num_scalar_prefetch=2,
            grid=(t // tb,),
            in_specs=[pl.BlockSpec(memory_space=pl.ANY),
                      pl.BlockSpec((tb, LANES), lambda i, *_: (i, 0)),
                      pl.BlockSpec((tb, d), lambda i, *_: (i, 0)),
                      pl.BlockSpec((None, N_MOD, d), lambda i, *_: (i // per_batch, 0, 0)),
                      pl.BlockSpec((1, d), lambda i, *_: (0, 0))],
            out_specs=pl.BlockSpec((tb, d), lambda i, *_: (i, 0)),
            scratch_shapes=[pltpu.VMEM((2, lp, d), BF16), pltpu.SemaphoreType.DMA((2,))]),
        out_shape=jax.ShapeDtypeStruct((t, d), F32),
        compiler_params=_cparams("arbitrary"),
        name="moe_combine",
    )(tab, ltot, ys, rt2, x1, mod3, final_g.reshape(1, d))


def _moe_layout(cnt_blocks, tile, n_tiles):
    lcnt = (cnt_blocks + SEG - 1) // SEG * SEG
    lstart = jnp.cumsum(lcnt, axis=1) - lcnt
    ltot = jnp.sum(lcnt, axis=1)
    per_expert = jnp.sum(lcnt, axis=0)
    tiles_per = (per_expert + tile - 1) // tile
    tile_ends = jnp.cumsum(tiles_per)
    row_ends = tile_ends * tile
    goff = row_ends - tiles_per * tile
    gstart = goff[None, :] + jnp.cumsum(lcnt, axis=0) - lcnt
    tab = jnp.stack([lstart, gstart, lcnt // SEG], axis=-1).reshape(-1).astype(jnp.int32)
    n_used = tile_ends[-1:]
    tile_ids = jnp.arange(n_tiles, dtype=jnp.int32)
    tile_expert = jnp.sum((tile_ends[None, :] <= jnp.minimum(tile_ids, n_used - 1)[:, None]).astype(jnp.int32), axis=1)
    ids = jnp.arange(N_EXPERTS, dtype=jnp.int32)
    used = tiles_per > 0
    par = (jnp.cumsum(used) - used) % 2
    later = jnp.where(used[None, :] & (ids[None, :] > ids[:, None]), ids[None, :], N_EXPERTS)
    nxt = jnp.min(later, axis=1)
    nxt = jnp.where(nxt == N_EXPERTS, -1, nxt)
    seg_ends = goff + per_expert
    ends = jnp.concatenate([row_ends, seg_ends])
    tile_rows = jnp.clip(seg_ends[tile_expert] - tile_ids * tile, 0, tile)
    return (tab, ends.astype(jnp.int32), ltot.astype(jnp.int32), tile_expert.astype(jnp.int32),
            n_used.astype(jnp.int32), nxt.astype(jnp.int32), par.astype(jnp.int32), tile_rows.astype(jnp.int32))


def _sincos_table(rows, cols, dim):
    quarter = dim // 4
    omega = 1.0 / (10000.0 ** (np.arange(quarter, dtype=np.float64) / quarter))
    ang_r = np.arange(rows, dtype=np.float64)[:, None] * omega
    ang_c = np.arange(cols, dtype=np.float64)[:, None] * omega
    emb_r = np.concatenate([np.sin(ang_r), np.cos(ang_r)], axis=-1)
    emb_c = np.concatenate([np.sin(ang_c), np.cos(ang_c)], axis=-1)
    pe = np.concatenate([np.broadcast_to(emb_r[:, None, :], (rows, cols, 2 * quarter)),
                         np.broadcast_to(emb_c[None, :, :], (rows, cols, 2 * quarter))], axis=-1)
    return pe.reshape(rows * cols, 4 * quarter).astype(np.float32)


def _filter_features(n, width):
    pos = np.arange(n, dtype=np.float64)
    t = pos / max(n - 1, 1)
    ang = (2.0 * math.pi * pos / n)[:, None] * np.linspace(1e-4, FILTER_BANDS - 1, FILTER_BANDS)[None, :]
    z = np.concatenate([t[:, None], np.cos(ang), -np.sin(ang)], axis=-1)
    return np.pad(z, ((0, 0), (0, width - z.shape[1]))).astype(np.float32)


def _decay_rates(d_h):
    min_decay = math.log(DECAY_TARGET) / DECAY_SLOW_PCT
    max_decay = math.log(DECAY_TARGET) / DECAY_FAST_PCT
    return np.abs(np.linspace(min_decay, max_decay, d_h)).astype(np.float32).reshape(1, d_h)


def _dft_tables(blk):
    odd = 2 * np.arange(blk, dtype=np.int64) + 1
    shifted = ((odd[:, None] * odd[None, :]) % (8 * blk)).astype(np.float64) * (2.0 * math.pi / (8 * blk))
    plain = ((odd[:, None] * np.arange(blk, dtype=np.int64)[None, :]) % (4 * blk)).astype(np.float64) * (2.0 * math.pi / (4 * blk))
    sgn = (1.0 - 2.0 * (np.arange(blk) % 2)).astype(np.float32).reshape(blk, 1)
    f32 = lambda a: a.astype(np.float32)
    return f32(np.cos(shifted)), f32(np.sin(shifted)), f32(np.cos(plain)), f32(np.sin(plain)), sgn


def kernel(x, c, ctx, c_ctx, w_ada, b_ada, norm1_g, w_in, conv_a_w, conv_a_b, lru_wa, lru_ba, lru_wx, lru_bx, lru_lambda, conv_b_w, filt_w1, filt_b1, filt_freq1, filt_w2, filt_b2, filt_freq2, filt_w3, filt_b3, filt_bias, out_norm_a, out_norm_b, w_out, norm2_g, w_rg, b_rg, w_re, b_re, w_gate, w_up, w_down, final_g):
    bsz, n, d = x.shape
    n_ctx = ctx.shape[1]
    d_rnn = conv_a_w.shape[2]
    d_h = filt_bias.shape[2]
    order = filt_bias.shape[1]
    assert w_ada.shape[0] == 1, "single-layer block"
    l = 0
    ch = 256

    mod_rows = 16
    cc = jnp.concatenate([c, c_ctx[None, :], jnp.zeros((mod_rows - bsz - 1, d), F32)], axis=0)
    mod3 = _ada(cc, w_ada[l], b_ada[l]).reshape(mod_rows, N_MOD, d)

    pe = jnp.asarray(_sincos_table(n // GRID_W, GRID_W, d))
    p3 = _inproj_lat(x, pe, mod3, norm1_g[l], w_in[l], 1024)
    pc3 = _inproj_ctx(ctx.reshape(bsz * n_ctx, d), mod3, bsz, norm1_g[l], w_in[l], 1, d_rnn,
                      1024).reshape(bsz, n_ctx, d_rnn)

    heads_per_blk = ch // (d_rnn // RNN_HEADS)
    nblk = d_rnn // ch
    eye = jnp.eye(heads_per_blk, dtype=F32)

    def blockdiag(w):
        w5 = w.reshape(2, nblk, heads_per_blk, w.shape[2], w.shape[3])
        return jnp.einsum("dnkij,kl->dnkilj", w5, eye).reshape(2, nblk, ch, ch)

    wa_bd, wx_bd = blockdiag(lru_wa[l]), blockdiag(lru_wx[l])
    w_gates = jnp.concatenate([wa_bd[0], wx_bd[0], wa_bd[1], wx_bd[1]], axis=-1).astype(BF16)
    ba, bx = lru_ba[l].reshape(2, nblk, ch), lru_bx[l].reshape(2, nblk, ch)
    bias = jnp.concatenate([ba[0], bx[0], ba[1], bx[1]], axis=-1).reshape(nblk, 1, 4 * ch)
    sp = jax.nn.softplus(-lru_lambda[l])
    cb = conv_a_b[l].reshape(1, d_rnn)
    zeros_state = jnp.zeros((bsz, 1, d_rnn), F32)
    hf_ctx, hb_ctx = _rglru(pc3, 0, 0, conv_a_w[l], cb, w_gates, bias, sp, zeros_state, zeros_state, False, ch)
    ya, _, _ = _rglru(p3, d_rnn // ch, 0, conv_a_w[l], cb, w_gates, bias, sp, hf_ctx, hb_ctx, True, ch)

    as_bf16 = lambda a: jnp.asarray(a).astype(BF16)
    c3, s3, cu, su, sgn = _dft_tables(CONV_BLOCK)
    fe = 64
    z = jnp.asarray(_filter_features(n, fe))
    w1 = jnp.pad(filt_w1[l], ((0, fe - filt_w1.shape[1]), (0, 0)))
    hid = filt_w2.shape[1]
    hr, hi = _filters(z, w1, filt_b1[l].reshape(1, hid), filt_freq1[l].reshape(1, hid), filt_w2[l],
                      filt_b2[l].reshape(1, hid), filt_freq2[l].reshape(1, hid), filt_w3[l],
                      filt_b3[l].reshape(1, -1), jnp.asarray(_decay_rates(d_h)), as_bf16(cu), as_bf16(su),
                      jnp.asarray(sgn), d_h, order, ch)
    yb = _hyena(p3, 2 * d_rnn // ch, conv_b_w[l], filt_bias[l], as_bf16(c3), as_bf16(s3), hr, hi, d_h, ch)

    w_r = jnp.concatenate([w_re[l], w_rg[l], jnp.zeros((d, LANES - N_EXPERTS - N_GROUPS), F32)], axis=1)
    b_r = jnp.concatenate([b_re[l], b_rg[l], jnp.zeros((LANES - N_EXPERTS - N_GROUPS,), F32)]).reshape(1, LANES)
    tb = 512
    tm = 1024
    x1, h2, rt, cnt, prow = _merge(ya, yb, x, pe, mod3, out_norm_a[l], out_norm_b[l], w_out[l], norm2_g[l],
                                   w_r, b_r, tm, tm // tb)

    tile = 512
    t_all = bsz * n
    nb = t_all // tb
    lp = 2 * tb + N_EXPERTS * SEG
    n_tiles = (2 * t_all + nb * N_EXPERTS * (SEG - 1) + N_EXPERTS * (tile - 1)) // tile
    cnt_blocks = cnt[:, 0, :N_EXPERTS].astype(jnp.int32)
    tab, ends, ltot, tile_expert, n_used, nxt, par, tile_rows = _moe_layout(cnt_blocks, tile, n_tiles)
    xs = _dispatch(tab, ends, ltot, prow, h2.reshape(t_all, d), n_tiles * tile, tb, lp, tile)
    ys = _ffn(tile_expert, n_used, nxt, par, tile_rows, xs, w_gate[l], w_up[l], w_down[l], tile)
    out = _combine(tab, ltot, ys, rt.reshape(t_all, LANES), x1.reshape(t_all, d), mod3, final_g, tb, lp, n // tb)
    return out.reshape(bsz, n, d)
```
